```python
import jax, jax.numpy as jnp
from jax import lax
import numpy as np

D_MODEL = 2048
BATCH = 8
SEQ = 4096
DEPTH = 4

GRID_W = 64
CTX_LEN = 256
N_MIXERS = 3
MIX_CHUNK_MLP = 0
MIX_CONV = 1
MIX_ATTN = 2
EXPAND = 2
BRANCH_W = EXPAND * D_MODEL
CHUNK = 128
SGU_GROUPS = 16
SGU_GROUP_W = BRANCH_W // SGU_GROUPS
CONV_W = 31
HEAD_DIM = 128
N_HEADS = D_MODEL // HEAD_DIM
N_KV_HEADS = 4
GQA_GROUP = N_HEADS // N_KV_HEADS
ATTN_W = N_HEADS * HEAD_DIM
KV_W = N_KV_HEADS * HEAD_DIM
Q_BLOCK = 128
ROPE_THETA = 10000.0
ROPE_AXIS_DIM = HEAD_DIM // 2
DEEPNORM_ALPHA = (2 * DEPTH) ** 0.25
DEEPNORM_BETA = (8 * DEPTH) ** -0.25
LN_EPS = 1e-6

kernel_name = "hybrid_dit_interleaved_chunkmlp_conformer_gqa"


def _layer_norm(x, g, b):
    xf = x.astype(jnp.float32)
    mu = jnp.mean(xf, axis=-1, keepdims=True)
    var = jnp.mean(jnp.square(xf - mu), axis=-1, keepdims=True)
    y = (xf - mu) * lax.rsqrt(var + LN_EPS) * g.astype(jnp.float32) + b.astype(jnp.float32)
    return y.astype(x.dtype)


def _rms_norm(x, g):
    xf = x.astype(jnp.float32)
    y = xf * lax.rsqrt(jnp.mean(jnp.square(xf), axis=-1, keepdims=True) + LN_EPS) * g.astype(jnp.float32)
    return y.astype(x.dtype)


def _modulation(w, b, cond):
    m = jax.nn.silu(cond) @ w + b
    return jnp.split(m, 3, axis=-1)


def _axial_rope_tables(n_tokens):
    rows = n_tokens // GRID_W
    row = jnp.repeat(jnp.arange(rows, dtype=jnp.float32), GRID_W)
    col = jnp.tile(jnp.arange(GRID_W, dtype=jnp.float32), rows)
    inv = 1.0 / (ROPE_THETA ** (jnp.arange(0, ROPE_AXIS_DIM, 2, dtype=jnp.float32) / ROPE_AXIS_DIM))
    ang = jnp.concatenate([row[:, None] * inv, col[:, None] * inv], axis=-1)
    return jnp.cos(ang), jnp.sin(ang)


def _apply_rope(t, cos, sin):
    tf = t.astype(jnp.float32).reshape(t.shape[:-1] + (HEAD_DIM // 2, 2))
    t0, t1 = tf[..., 0], tf[..., 1]
    cs = cos[None, :, None, :]
    sn = sin[None, :, None, :]
    out = jnp.stack([t0 * cs - t1 * sn, t0 * sn + t1 * cs], axis=-1).reshape(t.shape)
    return out.astype(t.dtype)


def _gqa_softmax(q, k, v):
    s = jnp.einsum('bqkgd,bskd->bkgqs', q, k, preferred_element_type=jnp.float32) * (HEAD_DIM ** -0.5)
    p = jax.nn.softmax(s, axis=-1).astype(v.dtype)
    return jnp.einsum('bkgqs,bskd->bqkgd', p, v)


def _chunk_mlp(h, w_in, ln_g, ln_b, w_s, b_s, w_out):
    bsz, n, _ = h.shape
    u, v, g = jnp.split(h @ w_in, 3, axis=-1)
    v = _layer_norm(v, ln_g, ln_b)
    vb = v.reshape(bsz, n // CHUNK, CHUNK, SGU_GROUPS, SGU_GROUP_W)
    s = jnp.einsum('hpq,bcqhe->bcphe', w_s, vb) + b_s.T[:, :, None]
    s = s.reshape(bsz, n, BRANCH_W)
    return (u * s * jax.nn.silu(g)) @ w_out


def _conv_module(h, w_in, conv_w, conv_b, ln_g, ln_b, w_out):
    a, b, g = jnp.split(h @ w_in, 3, axis=-1)
    y = a * jax.nn.sigmoid(b)
    y = lax.conv_general_dilated(
        y, conv_w[:, None, :].astype(y.dtype), window_strides=(1,),
        padding=[(CONV_W // 2, CONV_W // 2)], dimension_numbers=('NWC', 'WIO', 'NWC'),
        feature_group_count=BRANCH_W) + conv_b
    y = jax.nn.silu(_layer_norm(y, ln_g, ln_b))
    return (y * jax.nn.silu(g)) @ w_out


def _attn_qkvg(h, w_in, q_g, k_g):
    bsz, n, _ = h.shape
    q, k, v, g = jnp.split(h @ w_in, [ATTN_W, ATTN_W + KV_W, ATTN_W + 2 * KV_W], axis=-1)
    q = _rms_norm(q.reshape(bsz, n, N_HEADS, HEAD_DIM), q_g)
    k = _rms_norm(k.reshape(bsz, n, N_KV_HEADS, HEAD_DIM), k_g)
    v = v.reshape(bsz, n, N_KV_HEADS, HEAD_DIM)
    return q, k, v, g


def _ctx_kv(hc, w_in, k_g):
    bsz, n, _ = hc.shape
    k, v = jnp.split(hc @ w_in[:, ATTN_W:ATTN_W + 2 * KV_W], 2, axis=-1)
    k = _rms_norm(k.reshape(bsz, n, N_KV_HEADS, HEAD_DIM), k_g)
    return k, v.reshape(bsz, n, N_KV_HEADS, HEAD_DIM)


def _attention_layer(h, hc, w_in, q_g, k_g, w_out, cos, sin, need_ctx_out):
    bsz, n, _ = h.shape
    q, k, v, g = _attn_qkvg(h, w_in, q_g, k_g)
    q = _apply_rope(q, cos, sin)
    k = _apply_rope(k, cos, sin)
    if need_ctx_out:
        qc, kc, vc, gc = _attn_qkvg(hc, w_in, q_g, k_g)
    else:
        kc, vc = _ctx_kv(hc, w_in, k_g)
    k_all = jnp.concatenate([k, kc], axis=1)
    v_all = jnp.concatenate([v, vc], axis=1)
    n_blk = n // Q_BLOCK
    qb = q.reshape(bsz, n_blk, Q_BLOCK, N_KV_HEADS, GQA_GROUP, HEAD_DIM).transpose(1, 0, 2, 3, 4, 5)
    ob = lax.map(lambda qi: _gqa_softmax(qi, k_all, v_all), qb)
    o = ob.transpose(1, 0, 2, 3, 4, 5).reshape(bsz, n, ATTN_W)
    y = (o * jax.nn.silu(g)) @ w_out
    yc = None
    if need_ctx_out:
        lc = hc.shape[1]
        qc5 = qc.reshape(bsz, lc, N_KV_HEADS, GQA_GROUP, HEAD_DIM)
        oc = _gqa_softmax(qc5, kc, vc).reshape(bsz, lc, ATTN_W)
        yc = (oc * jax.nn.silu(gc)) @ w_out
    return y, yc


def _fwd_setup_inputs(seed: int = 0) -> dict:
    key = jax.random.key(seed)
    ks = jax.random.split(key, 32)
    kinds = [i % N_MIXERS for i in range(DEPTH)]
    n_a = kinds.count(MIX_CHUNK_MLP)
    n_b = kinds.count(MIX_CONV)
    n_c = kinds.count(MIX_ATTN)
    D = D_MODEL
    E = BRANCH_W
    f32 = jnp.float32

    def nrm(k, shape, scale):
        return jax.random.normal(k, shape, f32) * scale

    c_w_in = jnp.concatenate([
        nrm(ks[20], (n_c, D, ATTN_W), D ** -0.5),
        nrm(ks[21], (n_c, D, KV_W), D ** -0.5),
        nrm(ks[22], (n_c, D, KV_W), D ** -0.5 * DEEPNORM_BETA),
        nrm(ks[23], (n_c, D, ATTN_W), D ** -0.5)], axis=-1)
    return {
        "x": nrm(ks[0], (BATCH, SEQ, D), 1.0),
        "c": nrm(ks[1], (BATCH, D), 1.0),
        "ctx": nrm(ks[2], (BATCH, CTX_LEN, D), 1.0),
        "c_ctx": nrm(ks[3], (D,), 1.0),
        "mod_w": nrm(ks[4], (DEPTH, D, 3 * D), 0.5 * D ** -0.5),
        "mod_b": nrm(ks[5], (DEPTH, 3 * D), 0.02),
        "post_g": 1.0 + nrm(ks[6], (DEPTH, D), 0.02),
        "post_b": nrm(ks[7], (DEPTH, D), 0.02),
        "a_w_in": nrm(ks[8], (n_a, D, 3 * E), D ** -0.5),
        "a_ln_g": 1.0 + nrm(ks[9], (n_a, E), 0.02),
        "a_ln_b": nrm(ks[10], (n_a, E), 0.02),
        "a_w_s": nrm(ks[11], (n_a, SGU_GROUPS, CHUNK, CHUNK), CHUNK ** -0.5),
        "a_b_s": 1.0 + nrm(ks[12], (n_a, SGU_GROUPS, CHUNK), 0.02),
        "a_w_out": nrm(ks[13], (n_a, E, D), E ** -0.5 * DEEPNORM_BETA),
        "b_w_in": nrm(ks[14], (n_b, D, 3 * E), D ** -0.5),
        "b_conv_w": nrm(ks[15], (n_b, CONV_W, E), CONV_W ** -0.5),
        "b_conv_b": nrm(ks[16], (n_b, E), 0.02),
        "b_ln_g": 1.0 + nrm(ks[17], (n_b, E), 0.02),
        "b_ln_b": nrm(ks[18], (n_b, E), 0.02),
        "b_w_out": nrm(ks[19], (n_b, E, D), E ** -0.5 * DEEPNORM_BETA),
        "c_w_in": c_w_in,
        "c_q_g": 1.0 + nrm(ks[24], (n_c, HEAD_DIM), 0.02),
        "c_k_g": 1.0 + nrm(ks[25], (n_c, HEAD_DIM), 0.02),
        "c_w_out": nrm(ks[26], (n_c, ATTN_W, D), ATTN_W ** -0.5 * DEEPNORM_BETA),
    }


def _fwd_reference(x, c, ctx, c_ctx, mod_w, mod_b, post_g, post_b,
              a_w_in, a_ln_g, a_ln_b, a_w_s, a_b_s, a_w_out,
              b_w_in, b_conv_w, b_conv_b, b_ln_g, b_ln_b, b_w_out,
              c_w_in, c_q_g, c_k_g, c_w_out):
    n_tokens = x.shape[1]
    cos, sin = _axial_rope_tables(n_tokens)
    kinds = [i % N_MIXERS for i in range(DEPTH)]
    for i in range(DEPTH):
        kind = kinds[i]
        slot = kinds[:i].count(kind)
        ctx_read_later = MIX_ATTN in kinds[i + 1:]
        shift, scale, gate = _modulation(mod_w[i], mod_b[i], c)
        h = x * (1.0 + scale[:, None, :]) + shift[:, None, :]
        hc = None
        gate_c = None
        if ctx_read_later or kind == MIX_ATTN:
            shift_c, scale_c, gate_c = _modulation(mod_w[i], mod_b[i], c_ctx)
            hc = ctx * (1.0 + scale_c) + shift_c
        yc = None
        if kind == MIX_CHUNK_MLP:
            prm = (a_w_in[slot], a_ln_g[slot], a_ln_b[slot], a_w_s[slot], a_b_s[slot], a_w_out[slot])
            y = _chunk_mlp(h, *prm)
            if ctx_read_later:
                yc = _chunk_mlp(hc, *prm)
        elif kind == MIX_CONV:
            prm = (b_w_in[slot], b_conv_w[slot], b_conv_b[slot], b_ln_g[slot], b_ln_b[slot], b_w_out[slot])
            y = _conv_module(h, *prm)
            if ctx_read_later:
                yc = _conv_module(hc, *prm)
        else:
            y, yc = _attention_layer(h, hc, c_w_in[slot], c_q_g[slot], c_k_g[slot], c_w_out[slot],
                                     cos, sin, ctx_read_later)
        x = _layer_norm(DEEPNORM_ALPHA * x + gate[:, None, :] * y, post_g[i], post_b[i])
        if ctx_read_later:
            ctx = _layer_norm(DEEPNORM_ALPHA * ctx + gate_c * yc, post_g[i], post_b[i])
    return x


import jax as _jax
import jax.numpy as _jnp

TWIN_FORMAT = 'train_step'
FWD_PARAMS = ['x', 'c', 'ctx', 'c_ctx', 'mod_w', 'mod_b', 'post_g', 'post_b', 'a_w_in', 'a_ln_g', 'a_ln_b', 'a_w_s', 'a_b_s', 'a_w_out', 'b_w_in', 'b_conv_w', 'b_conv_b', 'b_ln_g', 'b_ln_b', 'b_w_out', 'c_w_in', 'c_q_g', 'c_k_g', 'c_w_out']
TWIN_WEIGHTS = ['c_ctx', 'mod_w', 'mod_b', 'post_g', 'post_b', 'a_w_in', 'a_ln_g', 'a_ln_b', 'a_w_s', 'a_b_s', 'a_w_out', 'b_w_in', 'b_conv_w', 'b_conv_b', 'b_ln_g', 'b_ln_b', 'b_w_out', 'c_w_in', 'c_q_g', 'c_k_g', 'c_w_out']
TWIN_DIFF_INPUT = 'x'
TWIN_INPUTS = ['x', 'c', 'ctx', 'c_ctx', 'mod_w', 'mod_b', 'post_g', 'post_b', 'a_w_in', 'a_ln_g', 'a_ln_b', 'a_w_s', 'a_b_s', 'a_w_out', 'b_w_in', 'b_conv_w', 'b_conv_b', 'b_ln_g', 'b_ln_b', 'b_w_out', 'c_w_in', 'c_q_g', 'c_k_g', 'c_w_out', 'loss_target', 'm_c_ctx', 'm_mod_w', 'm_mod_b', 'm_post_g', 'm_post_b', 'm_a_w_in', 'm_a_ln_g', 'm_a_ln_b', 'm_a_w_s', 'm_a_b_s', 'm_a_w_out', 'm_b_w_in', 'm_b_conv_w', 'm_b_conv_b', 'm_b_ln_g', 'm_b_ln_b', 'm_b_w_out', 'm_c_w_in', 'm_c_q_g', 'm_c_k_g', 'm_c_w_out', 'v_c_ctx', 'v_mod_w', 'v_mod_b', 'v_post_g', 'v_post_b', 'v_a_w_in', 'v_a_ln_g', 'v_a_ln_b', 'v_a_w_s', 'v_a_b_s', 'v_a_w_out', 'v_b_w_in', 'v_b_conv_w', 'v_b_conv_b', 'v_b_ln_g', 'v_b_ln_b', 'v_b_w_out', 'v_c_w_in', 'v_c_q_g', 'v_c_k_g', 'v_c_w_out']
TWIN_OUTPUTS = ['loss', 'grad_x', 'grad_c_ctx', 'grad_mod_w', 'grad_mod_b', 'grad_post_g', 'grad_post_b', 'grad_a_w_in', 'grad_a_ln_g', 'grad_a_ln_b', 'grad_a_w_s', 'grad_a_b_s', 'grad_a_w_out', 'grad_b_w_in', 'grad_b_conv_w', 'grad_b_conv_b', 'grad_b_ln_g', 'grad_b_ln_b', 'grad_b_w_out', 'grad_c_w_in', 'grad_c_q_g', 'grad_c_k_g', 'grad_c_w_out', 'delta_c_ctx', 'delta_mod_w', 'delta_mod_b', 'delta_post_g', 'delta_post_b', 'delta_a_w_in', 'delta_a_ln_g', 'delta_a_ln_b', 'delta_a_w_s', 'delta_a_b_s', 'delta_a_w_out', 'delta_b_w_in', 'delta_b_conv_w', 'delta_b_conv_b', 'delta_b_ln_g', 'delta_b_ln_b', 'delta_b_w_out', 'delta_c_w_in', 'delta_c_q_g', 'delta_c_k_g', 'delta_c_w_out', 'new_m_c_ctx', 'new_m_mod_w', 'new_m_mod_b', 'new_m_post_g', 'new_m_post_b', 'new_m_a_w_in', 'new_m_a_ln_g', 'new_m_a_ln_b', 'new_m_a_w_s', 'new_m_a_b_s', 'new_m_a_w_out', 'new_m_b_w_in', 'new_m_b_conv_w', 'new_m_b_conv_b', 'new_m_b_ln_g', 'new_m_b_ln_b', 'new_m_b_w_out', 'new_m_c_w_in', 'new_m_c_q_g', 'new_m_c_k_g', 'new_m_c_w_out', 'new_v_c_ctx', 'new_v_mod_w', 'new_v_mod_b', 'new_v_post_g', 'new_v_post_b', 'new_v_a_w_in', 'new_v_a_ln_g', 'new_v_a_ln_b', 'new_v_a_w_s', 'new_v_a_b_s', 'new_v_a_w_out', 'new_v_b_w_in', 'new_v_b_conv_w', 'new_v_b_conv_b', 'new_v_b_ln_g', 'new_v_b_ln_b', 'new_v_b_w_out', 'new_v_c_w_in', 'new_v_c_q_g', 'new_v_c_k_g', 'new_v_c_w_out']
TWIN_LEAF_KINDS = {'loss': 'loss', 'grad_x': 'grad_x', 'grad_c_ctx': 'grad_w', 'grad_mod_w': 'grad_w', 'grad_mod_b': 'grad_w', 'grad_post_g': 'grad_w', 'grad_post_b': 'grad_w', 'grad_a_w_in': 'grad_w', 'grad_a_ln_g': 'grad_w', 'grad_a_ln_b': 'grad_w', 'grad_a_w_s': 'grad_w', 'grad_a_b_s': 'grad_w', 'grad_a_w_out': 'grad_w', 'grad_b_w_in': 'grad_w', 'grad_b_conv_w': 'grad_w', 'grad_b_conv_b': 'grad_w', 'grad_b_ln_g': 'grad_w', 'grad_b_ln_b': 'grad_w', 'grad_b_w_out': 'grad_w', 'grad_c_w_in': 'grad_w', 'grad_c_q_g': 'grad_w', 'grad_c_k_g': 'grad_w', 'grad_c_w_out': 'grad_w', 'delta_c_ctx': 'delta_w', 'delta_mod_w': 'delta_w', 'delta_mod_b': 'delta_w', 'delta_post_g': 'delta_w', 'delta_post_b': 'delta_w', 'delta_a_w_in': 'delta_w', 'delta_a_ln_g': 'delta_w', 'delta_a_ln_b': 'delta_w', 'delta_a_w_s': 'delta_w', 'delta_a_b_s': 'delta_w', 'delta_a_w_out': 'delta_w', 'delta_b_w_in': 'delta_w', 'delta_b_conv_w': 'delta_w', 'delta_b_conv_b': 'delta_w', 'delta_b_ln_g': 'delta_w', 'delta_b_ln_b': 'delta_w', 'delta_b_w_out': 'delta_w', 'delta_c_w_in': 'delta_w', 'delta_c_q_g': 'delta_w', 'delta_c_k_g': 'delta_w', 'delta_c_w_out': 'delta_w', 'new_m_c_ctx': 'new_m', 'new_m_mod_w': 'new_m', 'new_m_mod_b': 'new_m', 'new_m_post_g': 'new_m', 'new_m_post_b': 'new_m', 'new_m_a_w_in': 'new_m', 'new_m_a_ln_g': 'new_m', 'new_m_a_ln_b': 'new_m', 'new_m_a_w_s': 'new_m', 'new_m_a_b_s': 'new_m', 'new_m_a_w_out': 'new_m', 'new_m_b_w_in': 'new_m', 'new_m_b_conv_w': 'new_m', 'new_m_b_conv_b': 'new_m', 'new_m_b_ln_g': 'new_m', 'new_m_b_ln_b': 'new_m', 'new_m_b_w_out': 'new_m', 'new_m_c_w_in': 'new_m', 'new_m_c_q_g': 'new_m', 'new_m_c_k_g': 'new_m', 'new_m_c_w_out': 'new_m', 'new_v_c_ctx': 'new_v', 'new_v_mod_w': 'new_v', 'new_v_mod_b': 'new_v', 'new_v_post_g': 'new_v', 'new_v_post_b': 'new_v', 'new_v_a_w_in': 'new_v', 'new_v_a_ln_g': 'new_v', 'new_v_a_ln_b': 'new_v', 'new_v_a_w_s': 'new_v', 'new_v_a_b_s': 'new_v', 'new_v_a_w_out': 'new_v', 'new_v_b_w_in': 'new_v', 'new_v_b_conv_w': 'new_v', 'new_v_b_conv_b': 'new_v', 'new_v_b_ln_g': 'new_v', 'new_v_b_ln_b': 'new_v', 'new_v_b_w_out': 'new_v', 'new_v_c_w_in': 'new_v', 'new_v_c_q_g': 'new_v', 'new_v_c_k_g': 'new_v', 'new_v_c_w_out': 'new_v'}


def _forward(args):
    return _fwd_reference(*[args[k] for k in FWD_PARAMS])


def _output_shape():
    def fwd():
        inp = _fwd_setup_inputs(0)
        return _fwd_reference(*[inp[k] for k in FWD_PARAMS])
    out = _jax.eval_shape(fwd)
    return out.shape, out.dtype

N_MICROBATCH = 1
ADAM_LR = 0.001
ADAM_B1 = 0.9
ADAM_B2 = 0.999
ADAM_EPS = 1e-08
ADAM_WD = 0.01
ADAM_STEP = 10
PER_EXAMPLE_BATCH_AXIS = {'x': 0, 'c': 0, 'ctx': 0, 'loss_target': 0}
SHARED_INPUTS = []
_WEIGHT_DTYPES = {'c_ctx': _jnp.float32, 'mod_w': _jnp.float32, 'mod_b': _jnp.float32, 'post_g': _jnp.float32, 'post_b': _jnp.float32, 'a_w_in': _jnp.float32, 'a_ln_g': _jnp.float32, 'a_ln_b': _jnp.float32, 'a_w_s': _jnp.float32, 'a_b_s': _jnp.float32, 'a_w_out': _jnp.float32, 'b_w_in': _jnp.float32, 'b_conv_w': _jnp.float32, 'b_conv_b': _jnp.float32, 'b_ln_g': _jnp.float32, 'b_ln_b': _jnp.float32, 'b_w_out': _jnp.float32, 'c_w_in': _jnp.float32, 'c_q_g': _jnp.float32, 'c_k_g': _jnp.float32, 'c_w_out': _jnp.float32}
MOMENT_SCALE = {'c_ctx': 2.867489e-04, 'mod_w': 6.788633e-03, 'mod_b': 1.174739e-02, 'post_g': 8.028719e+00, 'post_b': 3.746886e-01, 'a_w_in': 4.402722e-03, 'a_ln_g': 3.342270e-03, 'a_ln_b': 3.399953e-03, 'a_w_s': 4.743934e-03, 'a_b_s': 4.942026e-03, 'a_w_out': 1.603552e-02, 'b_w_in': 1.626963e-03, 'b_conv_w': 1.915739e-03, 'b_conv_b': 3.394221e-03, 'b_ln_g': 2.306770e-03, 'b_ln_b': 2.029094e-03, 'b_w_out': 6.301152e-03, 'c_w_in': 9.625439e-04, 'c_q_g': 5.947136e-04, 'c_k_g': 6.029421e-04, 'c_w_out': 1.346713e-03}


def _to_microbatches(a, axis):
    t = _jnp.moveaxis(a, axis, 0)
    t = t.reshape((N_MICROBATCH, t.shape[0] // N_MICROBATCH) + t.shape[1:])
    return _jnp.moveaxis(t, 1, axis + 1)


def setup_inputs(seed: int = 0) -> dict:
    inp = _fwd_setup_inputs(seed)
    key = _jax.random.fold_in(_jax.random.key(seed), 7919)
    shape, _ = _output_shape()
    out = dict(inp)
    out["loss_target"] = _jax.random.normal(_jax.random.fold_in(key, 0), shape, _jnp.float32)
    for i, name in enumerate(TWIN_WEIGHTS):
        w = inp[name].astype(_jnp.float32)
        if MOMENT_SCALE is None:
            s = _jnp.sqrt(_jnp.mean(_jnp.square(w)) + 1e-30)
        else:
            s = MOMENT_SCALE[name]
        km, kv = _jax.random.split(_jax.random.fold_in(key, i + 1))
        out[name] = w
        out["m_" + name] = s * _jax.random.normal(km, w.shape, _jnp.float32)
        out["v_" + name] = (s * s) * _jax.random.uniform(kv, w.shape, _jnp.float32, 0.5, 1.5)
    if N_MICROBATCH > 1:
        for name, axis in PER_EXAMPLE_BATCH_AXIS.items():
            out[name] = _to_microbatches(out[name], axis)
    return {'x': out['x'], 'c': out['c'], 'ctx': out['ctx'], 'c_ctx': out['c_ctx'], 'mod_w': out['mod_w'], 'mod_b': out['mod_b'], 'post_g': out['post_g'], 'post_b': out['post_b'], 'a_w_in': out['a_w_in'], 'a_ln_g': out['a_ln_g'], 'a_ln_b': out['a_ln_b'], 'a_w_s': out['a_w_s'], 'a_b_s': out['a_b_s'], 'a_w_out': out['a_w_out'], 'b_w_in': out['b_w_in'], 'b_conv_w': out['b_conv_w'], 'b_conv_b': out['b_conv_b'], 'b_ln_g': out['b_ln_g'], 'b_ln_b': out['b_ln_b'], 'b_w_out': out['b_w_out'], 'c_w_in': out['c_w_in'], 'c_q_g': out['c_q_g'], 'c_k_g': out['c_k_g'], 'c_w_out': out['c_w_out'], 'loss_target': out['loss_target'], 'm_c_ctx': out['m_c_ctx'], 'm_mod_w': out['m_mod_w'], 'm_mod_b': out['m_mod_b'], 'm_post_g': out['m_post_g'], 'm_post_b': out['m_post_b'], 'm_a_w_in': out['m_a_w_in'], 'm_a_ln_g': out['m_a_ln_g'], 'm_a_ln_b': out['m_a_ln_b'], 'm_a_w_s': out['m_a_w_s'], 'm_a_b_s': out['m_a_b_s'], 'm_a_w_out': out['m_a_w_out'], 'm_b_w_in': out['m_b_w_in'], 'm_b_conv_w': out['m_b_conv_w'], 'm_b_conv_b': out['m_b_conv_b'], 'm_b_ln_g': out['m_b_ln_g'], 'm_b_ln_b': out['m_b_ln_b'], 'm_b_w_out': out['m_b_w_out'], 'm_c_w_in': out['m_c_w_in'], 'm_c_q_g': out['m_c_q_g'], 'm_c_k_g': out['m_c_k_g'], 'm_c_w_out': out['m_c_w_out'], 'v_c_ctx': out['v_c_ctx'], 'v_mod_w': out['v_mod_w'], 'v_mod_b': out['v_mod_b'], 'v_post_g': out['v_post_g'], 'v_post_b': out['v_post_b'], 'v_a_w_in': out['v_a_w_in'], 'v_a_ln_g': out['v_a_ln_g'], 'v_a_ln_b': out['v_a_ln_b'], 'v_a_w_s': out['v_a_w_s'], 'v_a_b_s': out['v_a_b_s'], 'v_a_w_out': out['v_a_w_out'], 'v_b_w_in': out['v_b_w_in'], 'v_b_conv_w': out['v_b_conv_w'], 'v_b_conv_b': out['v_b_conv_b'], 'v_b_ln_g': out['v_b_ln_g'], 'v_b_ln_b': out['v_b_ln_b'], 'v_b_w_out': out['v_b_w_out'], 'v_c_w_in': out['v_c_w_in'], 'v_c_q_g': out['v_c_q_g'], 'v_c_k_g': out['v_c_k_g'], 'v_c_w_out': out['v_c_w_out']}


def _loss(weights, diff, rest, loss_target):
    with _jax.named_scope("forward"):
        args = {**rest, TWIN_DIFF_INPUT: diff, **{k: w.astype(_WEIGHT_DTYPES[k]) for k, w in weights.items()}}
        y = _forward(args)
    with _jax.named_scope("loss_head"):
        err = _jnp.square(y.astype(_jnp.float32) - loss_target)
        return 0.5 * _jnp.sum(_jnp.mean(err, axis=-1)) if err.ndim else 0.5 * err


def _adamw(w, g, m, v):
    m = ADAM_B1 * m + (1.0 - ADAM_B1) * g
    v = ADAM_B2 * v + (1.0 - ADAM_B2) * _jnp.square(g)
    m_hat = m / (1.0 - ADAM_B1 ** ADAM_STEP)
    v_hat = v / (1.0 - ADAM_B2 ** ADAM_STEP)
    delta = -ADAM_LR * (m_hat / (_jnp.sqrt(v_hat) + ADAM_EPS) + ADAM_WD * w)
    return delta, m, v


def reference(x, c, ctx, c_ctx, mod_w, mod_b, post_g, post_b, a_w_in, a_ln_g, a_ln_b, a_w_s, a_b_s, a_w_out, b_w_in, b_conv_w, b_conv_b, b_ln_g, b_ln_b, b_w_out, c_w_in, c_q_g, c_k_g, c_w_out, loss_target, m_c_ctx, m_mod_w, m_mod_b, m_post_g, m_post_b, m_a_w_in, m_a_ln_g, m_a_ln_b, m_a_w_s, m_a_b_s, m_a_w_out, m_b_w_in, m_b_conv_w, m_b_conv_b, m_b_ln_g, m_b_ln_b, m_b_w_out, m_c_w_in, m_c_q_g, m_c_k_g, m_c_w_out, v_c_ctx, v_mod_w, v_mod_b, v_post_g, v_post_b, v_a_w_in, v_a_ln_g, v_a_ln_b, v_a_w_s, v_a_b_s, v_a_w_out, v_b_w_in, v_b_conv_w, v_b_conv_b, v_b_ln_g, v_b_ln_b, v_b_w_out, v_c_w_in, v_c_q_g, v_c_k_g, v_c_w_out):
    given = dict(x=x, c=c, ctx=ctx, c_ctx=c_ctx, mod_w=mod_w, mod_b=mod_b, post_g=post_g, post_b=post_b, a_w_in=a_w_in, a_ln_g=a_ln_g, a_ln_b=a_ln_b, a_w_s=a_w_s, a_b_s=a_b_s, a_w_out=a_w_out, b_w_in=b_w_in, b_conv_w=b_conv_w, b_conv_b=b_conv_b, b_ln_g=b_ln_g, b_ln_b=b_ln_b, b_w_out=b_w_out, c_w_in=c_w_in, c_q_g=c_q_g, c_k_g=c_k_g, c_w_out=c_w_out, loss_target=loss_target, m_c_ctx=m_c_ctx, m_mod_w=m_mod_w, m_mod_b=m_mod_b, m_post_g=m_post_g, m_post_b=m_post_b, m_a_w_in=m_a_w_in, m_a_ln_g=m_a_ln_g, m_a_ln_b=m_a_ln_b, m_a_w_s=m_a_w_s, m_a_b_s=m_a_b_s, m_a_w_out=m_a_w_out, m_b_w_in=m_b_w_in, m_b_conv_w=m_b_conv_w, m_b_conv_b=m_b_conv_b, m_b_ln_g=m_b_ln_g, m_b_ln_b=m_b_ln_b, m_b_w_out=m_b_w_out, m_c_w_in=m_c_w_in, m_c_q_g=m_c_q_g, m_c_k_g=m_c_k_g, m_c_w_out=m_c_w_out, v_c_ctx=v_c_ctx, v_mod_w=v_mod_w, v_mod_b=v_mod_b, v_post_g=v_post_g, v_post_b=v_post_b, v_a_w_in=v_a_w_in, v_a_ln_g=v_a_ln_g, v_a_ln_b=v_a_ln_b, v_a_w_s=v_a_w_s, v_a_b_s=v_a_b_s, v_a_w_out=v_a_w_out, v_b_w_in=v_b_w_in, v_b_conv_w=v_b_conv_w, v_b_conv_b=v_b_conv_b, v_b_ln_g=v_b_ln_g, v_b_ln_b=v_b_ln_b, v_b_w_out=v_b_w_out, v_c_w_in=v_c_w_in, v_c_q_g=v_c_q_g, v_c_k_g=v_c_k_g, v_c_w_out=v_c_w_out)
    weights = {n: given[n] for n in TWIN_WEIGHTS}
    shared = {n: given[n] for n in SHARED_INPUTS}
    per_example = {n: given[n] for n in ['x', 'c', 'ctx']}
    grad_fn = _jax.value_and_grad(_loss, argnums=(0, 1))

    def one_microbatch(ex, loss_target):
        ex = dict(ex)
        diff = ex.pop(TWIN_DIFF_INPUT)
        return grad_fn(weights, diff, {**shared, **ex}, loss_target)

    if N_MICROBATCH == 1:
        loss, (grad_w, grad_x) = one_microbatch(per_example, given["loss_target"])
    else:
        def body(carry, xs):
            loss_sum, grad_sum = carry
            l_k, (gw_k, gx_k) = one_microbatch(xs[0], xs[1])
            with _jax.named_scope("update"):
                return (loss_sum + l_k, _jax.tree.map(_jnp.add, grad_sum, gw_k)), gx_k

        init = (_jnp.zeros((), _jnp.float32), _jax.tree.map(_jnp.zeros_like, weights))
        (loss, grad_w), grad_x = _jax.lax.scan(body, init, (per_example, given["loss_target"]))
    with _jax.named_scope("update"):
        delta_w, new_m, new_v = {}, {}, {}
        for n in TWIN_WEIGHTS:
            delta_w[n], new_m[n], new_v[n] = _adamw(weights[n], grad_w[n], given["m_" + n], given["v_" + n])
    return (loss, grad_x, *[grad_w[n] for n in TWIN_WEIGHTS], *[delta_w[n] for n in TWIN_WEIGHTS],
            *[new_m[n] for n in TWIN_WEIGHTS], *[new_v[n] for n in TWIN_WEIGHTS])
```

```python
import functools
import math

import jax
import jax.numpy as jnp
from jax import lax
from jax.experimental import pallas as pl
from jax.experimental.pallas import tpu as pltpu

F32 = jnp.float32
BF16 = jnp.bfloat16
MXU_DTYPE = jnp.bfloat16

DEPTH = 4
GRID_W = 64
CHUNK = 128
SGU_GROUPS = 16
CONV_W = 31
CONV_HALO = 16
HEAD_DIM = 128
N_KV_HEADS = 4
ROPE_THETA = 10000.0
DEEPNORM_ALPHA = (2 * DEPTH) ** 0.25
LN_EPS = 1e-6
ADAM_LR, ADAM_B1, ADAM_B2, ADAM_EPS, ADAM_WD, ADAM_STEP = 0.001, 0.9, 0.999, 1e-08, 0.01, 10

N_DEV = 8
V7X_VMEM_BYTES = 64 * 1024 * 1024
LANES = 128

NN = ((1,), (0,))
NT = ((1,), (1,))
TN = ((0,), (0,))


def _dot(a, b, dims):
    return lax.dot_general(a.astype(MXU_DTYPE), b.astype(MXU_DTYPE), (dims, ((), ())), preferred_element_type=F32)


def _pick(n, prefs):
    for p in prefs:
        if n % p == 0:
            return p
    raise ValueError(f"no tile for {n} among {prefs}")


def _params(sem, vmem_bytes):
    limit = int(min(max(vmem_bytes, 16 * 1024 * 1024), V7X_VMEM_BYTES * 7 // 8))
    return pltpu.CompilerParams(dimension_semantics=sem, vmem_limit_bytes=limit)


def _nbytes(shape, dtype):
    return math.prod(shape) * jnp.dtype(dtype).itemsize


def _sigmoid(x):
    return jax.nn.sigmoid(x)


def _silu(x):
    return x * jax.nn.sigmoid(x)


def _dsilu(x):
    s = jax.nn.sigmoid(x)
    return s * (1.0 + x * (1.0 - s))


def _ln_stats(x):
    mu = jnp.mean(x, axis=-1, keepdims=True)
    xc = x - mu
    var = jnp.mean(xc * xc, axis=-1, keepdims=True)
    return xc, lax.rsqrt(var + LN_EPS)


def _ln(x, g, b):
    xc, rstd = _ln_stats(x)
    return xc * rstd * g + b


def _mesh_pos():
    return lax.axis_index("x"), lax.axis_index("y"), lax.axis_index("c")


def _dev_index(p):
    return 4 * p[0] + 2 * p[1] + p[2]


def _all_gather(xs, name):
    n = len(xs)

    def body(*refs):
        x_refs, o_refs = refs[:n], refs[n:2 * n]
        send_sems, recv_sems, local_sems = refs[2 * n:]
        x, y, c = _mesh_pos()
        me, sibling = (x, y, c), (x, y, 1 - c)
        chips = [(1 - x, y), (x, 1 - y), (1 - x, 1 - y)]

        def copy(t, k, block, to, src=None):
            dst = o_refs[t].at[_dev_index(block)]
            return pltpu.make_async_remote_copy(
                src_ref=dst if src is None else src, dst_ref=dst,
                send_sem=send_sems.at[t, k], recv_sem=recv_sems.at[t, k],
                device_id=to, device_id_type=pl.DeviceIdType.MESH)

        mine, first, passed = [], [], []
        for t in range(n):
            cp = pltpu.make_async_copy(x_refs[t], o_refs[t].at[_dev_index(me)], local_sems.at[t])
            cp.start()
            mine.append(cp)
            cps = [copy(t, 0, me, sibling, src=x_refs[t])]
            cps += [copy(t, 1 + j, me, (*chip, c), src=x_refs[t]) for j, chip in enumerate(chips)]
            for cp in cps:
                cp.start()
            first += cps
        for t in range(n):
            for j, chip in enumerate(chips):
                copy(t, 1 + j, (*chip, c), me).wait_recv()
                cp = copy(t, 4 + j, (*chip, c), sibling)
                cp.start()
                passed.append(cp)
        for t in range(n):
            copy(t, 0, sibling, me).wait_recv()
            for j, chip in enumerate(chips):
                copy(t, 4 + j, (*chip, 1 - c), me).wait_recv()
        for cp in first + passed:
            cp.wait_send()
        for cp in mine:
            cp.wait()

    hbm = pl.BlockSpec(memory_space=pl.ANY)
    return pl.pallas_call(
        body, name=name,
        out_shape=[jax.ShapeDtypeStruct((N_DEV, *a.shape), a.dtype) for a in xs],
        in_specs=[hbm] * n, out_specs=[hbm] * n,
        scratch_shapes=[pltpu.SemaphoreType.DMA((n, 7)), pltpu.SemaphoreType.DMA((n, 7)), pltpu.SemaphoreType.DMA((n,))],
    )(*xs)


def _all_to_all(gs, name):
    n = len(gs)

    def body(*refs):
        g_refs, r_refs = refs[:n], refs[n:2 * n]
        send_sems, recv_sems, local_sems = refs[2 * n:]
        x, y, c = _mesh_pos()
        me = _dev_index((x, y, c))
        copies = []
        for t in range(n):
            cp = pltpu.make_async_copy(g_refs[t].at[me], r_refs[t].at[me], local_sems.at[t])
            cp.start()
            copies.append(cp)
        for t in range(n):
            for k in range(1, N_DEV):
                fx, fy, fc = (k >> 2) & 1, (k >> 1) & 1, k & 1
                peer = (1 - x if fx else x, 1 - y if fy else y, 1 - c if fc else c)
                cp = pltpu.make_async_remote_copy(
                    src_ref=g_refs[t].at[_dev_index(peer)], dst_ref=r_refs[t].at[me],
                    send_sem=send_sems.at[t, k - 1], recv_sem=recv_sems.at[t, k - 1],
                    device_id=peer, device_id_type=pl.DeviceIdType.MESH)
                cp.start()
                copies.append(cp)
        for cp in copies:
            cp.wait()

    hbm = pl.BlockSpec(memory_space=pl.ANY)
    return pl.pallas_call(
        body, name=name,
        out_shape=[jax.ShapeDtypeStruct(g.shape, g.dtype) for g in gs],
        in_specs=[hbm] * n, out_specs=[hbm] * n,
        scratch_shapes=[pltpu.SemaphoreType.DMA((n, 7)), pltpu.SemaphoreType.DMA((n, 7)), pltpu.SemaphoreType.DMA((n,))],
    )(*gs)


ROW_TILES = (1088, 1024, 768, 544, 512, 384, 272, 256, 128)
TOKEN_K_TILES = (2176, 2048, 1088, 1024, 768, 512, 384, 256, 128)
COL_TILES = (1024, 768, 640, 512, 384, 256, 128)
DEEP_K = 2048


def _mm_nn(a, w, *, planes=1, name):
    m, k = a.shape
    if w.ndim == 3:
        nd_w = w.shape[2]
        n = w.shape[0] * nd_w
    else:
        nd_w = n = w.shape[1]
    npl = n // planes
    tm = _pick(m, ROW_TILES)
    tn = _pick(math.gcd(nd_w, npl), COL_TILES if k <= DEEP_K else COL_TILES[3:])
    r, rp = nd_w // tn, npl // tn
    if w.ndim == 3:
        w_spec = pl.BlockSpec((None, k, tn), lambda i, j: (j // r, 0, j % r))
    else:
        w_spec = pl.BlockSpec((k, tn), lambda i, j: (0, j))
    if planes > 1:
        o_spec = pl.BlockSpec((None, tm, tn), lambda i, j: (j // rp, i, j % rp))
        out_shape = jax.ShapeDtypeStruct((planes, m, npl), F32)
    else:
        o_spec = pl.BlockSpec((tm, tn), lambda i, j: (i, j))
        out_shape = jax.ShapeDtypeStruct((m, n), F32)

    def body(a_ref, w_ref, o_ref):
        o_ref[...] = _dot(a_ref[...], w_ref[...], NN)

    vmem = 2 * (_nbytes((tm, k), a.dtype) + _nbytes((k, tn), w.dtype) + _nbytes((tm, tn), F32)) + _nbytes((tm, tn), F32)
    return pl.pallas_call(
        body, name=name, grid=(m // tm, n // tn),
        in_specs=[pl.BlockSpec((tm, k), lambda i, j: (i, 0)), w_spec], out_specs=o_spec, out_shape=out_shape,
        compiler_params=_params(("parallel", "arbitrary"), vmem),
    )(a, w)


def _mm_nt(a, w, *, name):
    m, k = a.shape
    n = w.shape[0]
    tm = _pick(m, ROW_TILES)
    tn = _pick(n, COL_TILES)

    def body(a_ref, w_ref, o_ref):
        o_ref[...] = _dot(a_ref[...], w_ref[...], NT)

    vmem = 2 * (_nbytes((tm, k), a.dtype) + _nbytes((tn, k), w.dtype) + _nbytes((tm, tn), F32)) + _nbytes((tm, tn), F32)
    return pl.pallas_call(
        body, name=name, grid=(m // tm, n // tn),
        in_specs=[pl.BlockSpec((tm, k), lambda i, j: (i, 0)), pl.BlockSpec((tn, k), lambda i, j: (j, 0))],
        out_specs=pl.BlockSpec((tm, tn), lambda i, j: (i, j)), out_shape=jax.ShapeDtypeStruct((m, n), F32),
        compiler_params=_params(("parallel", "arbitrary"), vmem),
    )(a, w)


def _mm_nt_blocked(a, w, *, name):
    nd, n, kd = w.shape
    if a.ndim == 3:
        p, m, kp = a.shape
    else:
        (m, kp), p = a.shape, 1
    tk = _pick(math.gcd(kd, kp), COL_TILES)
    ra, rw = kp // tk, kd // tk
    nk = nd * rw
    tm = _pick(m, ROW_TILES)
    if a.ndim == 3:
        a_spec = pl.BlockSpec((None, tm, tk), lambda i, kk: (kk // ra, i, kk % ra))
    else:
        a_spec = pl.BlockSpec((tm, tk), lambda i, kk: (i, kk))

    def body(a_ref, w_ref, o_ref, acc_ref):
        kk = pl.program_id(1)

        @pl.when(kk == 0)
        def _():
            acc_ref[...] = jnp.zeros_like(acc_ref)

        acc_ref[...] += _dot(a_ref[...], w_ref[...], NT)

        @pl.when(kk == nk - 1)
        def _():
            o_ref[...] = acc_ref[...]

    vmem = 2 * (_nbytes((tm, tk), a.dtype) + _nbytes((n, tk), w.dtype) + _nbytes((tm, n), F32)) + 2 * _nbytes((tm, n), F32)
    return pl.pallas_call(
        body, name=name, grid=(m // tm, nk),
        in_specs=[a_spec, pl.BlockSpec((None, n, tk), lambda i, kk: (kk // rw, 0, kk % rw))],
        out_specs=pl.BlockSpec((tm, n), lambda i, kk: (i, 0)), out_shape=jax.ShapeDtypeStruct((m, n), F32),
        scratch_shapes=[pltpu.VMEM((tm, n), F32)],
        compiler_params=_params(("parallel", "arbitrary"), vmem),
    )(a, w)


def _mm_tn(a, b, *, blocked, out_dtype, name):
    rows, da = a.shape
    if b.ndim == 3:
        p, _, npl = b.shape
    else:
        p, npl = 1, b.shape[1]
    n = p * npl
    nd_w = n // N_DEV if blocked else n
    tk = _pick(rows, TOKEN_K_TILES)
    tm = _pick(da, COL_TILES)
    tn = _pick(math.gcd(nd_w, npl), COL_TILES)
    rb, ro = npl // tn, nd_w // tn
    nk = rows // tk
    if b.ndim == 3:
        b_spec = pl.BlockSpec((None, tk, tn), lambda i, j, kk: (j // rb, kk, j % rb))
    else:
        b_spec = pl.BlockSpec((tk, tn), lambda i, j, kk: (kk, j))
    if blocked:
        o_spec = pl.BlockSpec((None, tm, tn), lambda i, j, kk: (j // ro, i, j % ro))
        out_shape = jax.ShapeDtypeStruct((N_DEV, da, nd_w), out_dtype)
    else:
        o_spec = pl.BlockSpec((tm, tn), lambda i, j, kk: (i, j))
        out_shape = jax.ShapeDtypeStruct((da, n), out_dtype)

    def body(a_ref, b_ref, o_ref, acc_ref):
        kk = pl.program_id(2)

        @pl.when(kk == 0)
        def _():
            acc_ref[...] = jnp.zeros_like(acc_ref)

        acc_ref[...] += _dot(a_ref[...], b_ref[...], TN)

        @pl.when(kk == nk - 1)
        def _():
            o_ref[...] = acc_ref[...].astype(o_ref.dtype)

    vmem = (2 * (_nbytes((tk, tm), a.dtype) + _nbytes((tk, tn), b.dtype) + _nbytes((tm, tn), out_dtype))
            + 3 * _nbytes((tm, tn), F32) + _nbytes((tk, tm), F32))
    return pl.pallas_call(
        body, name=name, grid=(da // tm, n // tn, nk),
        in_specs=[pl.BlockSpec((tk, tm), lambda i, j, kk: (kk, i)), b_spec], out_specs=o_spec, out_shape=out_shape,
        scratch_shapes=[pltpu.VMEM((tm, tn), F32)],
        compiler_params=_params(("parallel", "parallel", "arbitrary"), vmem),
    )(a, b)


ROW_TILE = 256


class _Rows:
    def __init__(self, n_x, n_ctx, tile=ROW_TILE):
        assert n_x % tile == 0 and n_ctx % tile == 0
        self.n_x, self.n_ctx, self.tile = n_x, n_ctx, tile
        self.rows = n_x + n_ctx
        self.nt_x = n_x // tile
        self.nt = self.rows // tile
        self.n_seg = 2 if n_ctx else 1

    def seg(self, i):
        return jnp.where(i >= self.nt_x, 1, 0) if self.n_ctx else 0

    def first_of_seg(self, i):
        return (i == 0) | (i == self.nt_x) if self.n_ctx else i == 0

    def full(self, width):
        return pl.BlockSpec((self.tile, width), lambda i: (i, 0))

    def plane(self, p, width):
        return pl.BlockSpec((None, self.tile, width), lambda i: (p, i, 0))

    def modvec(self, layer, which, width):
        return pl.BlockSpec((None, 1, width), lambda i: ((layer * 2 + self.seg(i)) * 3 + which, 0, 0))

    def seg_acc(self, width):
        return pl.BlockSpec((None, 1, width), lambda i: (self.seg(i), 0, 0))


def _vec(width):
    return pl.BlockSpec((1, width), lambda i: (0, 0))


def _acc(ref, first, val):
    @pl.when(first)
    def _():
        ref[...] = jnp.zeros_like(ref)

    ref[...] += val


def _modulate(xs, modv, layer, rt, name):
    d = xs.shape[1]

    def body(x_ref, sh_ref, sc_ref, o_ref):
        o_ref[...] = (x_ref[...] * (1.0 + sc_ref[...]) + sh_ref[...]).astype(o_ref.dtype)

    return pl.pallas_call(
        body, name=name, grid=(rt.nt,),
        in_specs=[rt.full(d), rt.modvec(layer, 0, d), rt.modvec(layer, 1, d)],
        out_specs=rt.full(d), out_shape=jax.ShapeDtypeStruct(xs.shape, MXU_DTYPE),
        compiler_params=_params(("parallel",), 6 * _nbytes((rt.tile, d), F32)),
    )(xs, modv, modv)


def _post(x, y, gate, pg, pb):
    return _ln(DEEPNORM_ALPHA * x + gate * y, pg, pb)


def _post_fwd(xs, y, modv, layer, pg, pb, rt, name):
    d = xs.shape[1]

    def body(x_ref, y_ref, gate_ref, pg_ref, pb_ref, o_ref):
        o_ref[...] = _post(x_ref[...], y_ref[...], gate_ref[...], pg_ref[...], pb_ref[...])

    return pl.pallas_call(
        body, name=name, grid=(rt.nt,),
        in_specs=[rt.full(d), rt.full(d), rt.modvec(layer, 2, d), _vec(d), _vec(d)],
        out_specs=rt.full(d), out_shape=jax.ShapeDtypeStruct((rt.rows, d), F32),
        compiler_params=_params(("parallel",), 10 * _nbytes((rt.tile, d), F32)),
    )(xs, y, modv, pg, pb)


def _post_bwd(xs, y, dout, modv, layer, pg, pb, rt, name):
    d = xs.shape[1]

    def body(x_ref, y_ref, do_ref, gate_ref, pg_ref, pb_ref, dres_ref, dy_ref, dpg_ref, dpb_ref, dgate_ref):
        i = pl.program_id(0)
        _, vjp = jax.vjp(_post, x_ref[...], y_ref[...], gate_ref[...], pg_ref[...], pb_ref[...])
        dx, dy, dgate, dpg, dpb = vjp(do_ref[...])
        dres_ref[...] = dx
        dy_ref[...] = dy.astype(dy_ref.dtype)
        _acc(dpg_ref, i == 0, dpg)
        _acc(dpb_ref, i == 0, dpb)
        _acc(dgate_ref, rt.first_of_seg(i), dgate)

    return pl.pallas_call(
        body, name=name, grid=(rt.nt,),
        in_specs=[rt.full(d), rt.full(d), rt.full(d), rt.modvec(layer, 2, d), _vec(d), _vec(d)],
        out_specs=[rt.full(d), rt.full(d), _vec(d), _vec(d), rt.seg_acc(d)],
        out_shape=[jax.ShapeDtypeStruct((rt.rows, d), F32), jax.ShapeDtypeStruct((rt.rows, d), MXU_DTYPE),
                   jax.ShapeDtypeStruct((1, d), F32), jax.ShapeDtypeStruct((1, d), F32),
                   jax.ShapeDtypeStruct((rt.n_seg, 1, d), F32)],
        compiler_params=_params(("arbitrary",), 16 * _nbytes((rt.tile, d), F32)),
    )(xs, y, dout, modv, pg, pb)


def _mod_bwd(dres, dh, xs, modv, layer, rt, name):
    d = xs.shape[1]

    def body(dres_ref, dh_ref, x_ref, sc_ref, dx_ref, dshift_ref, dscale_ref):
        i = pl.program_id(0)
        dh = dh_ref[...]
        dx_ref[...] = dres_ref[...] + dh * (1.0 + sc_ref[...])
        first = rt.first_of_seg(i)
        _acc(dshift_ref, first, jnp.sum(dh, axis=0, keepdims=True))
        _acc(dscale_ref, first, jnp.sum(dh * x_ref[...], axis=0, keepdims=True))

    return pl.pallas_call(
        body, name=name, grid=(rt.nt,),
        in_specs=[rt.full(d), rt.full(d), rt.full(d), rt.modvec(layer, 1, d)],
        out_specs=[rt.full(d), rt.seg_acc(d), rt.seg_acc(d)],
        out_shape=[jax.ShapeDtypeStruct((rt.rows, d), F32), jax.ShapeDtypeStruct((rt.n_seg, 1, d), F32),
                   jax.ShapeDtypeStruct((rt.n_seg, 1, d), F32)],
        compiler_params=_params(("arbitrary",), 10 * _nbytes((rt.tile, d), F32)),
    )(dres, dh, xs, modv)


def _cm_mid_fwd(z3, ln_g, ln_b, w_s, b_s_t, name):
    _, rows, e = z3.shape
    groups = w_s.shape[0]
    gw = e // groups

    def body(z_ref, lg_ref, lb_ref, ws_ref, bs_ref, t_ref):
        vn = _ln(z_ref[1], lg_ref[...], lb_ref[...])
        for h in range(groups):
            cols = slice(h * gw, (h + 1) * gw)
            s = _dot(ws_ref[h], vn[:, cols], NN) + bs_ref[:, h:h + 1]
            t_ref[:, cols] = (z_ref[0, :, cols] * s * _silu(z_ref[2, :, cols])).astype(t_ref.dtype)

    return pl.pallas_call(
        body, name=name, grid=(rows // CHUNK,),
        in_specs=[pl.BlockSpec((3, CHUNK, e), lambda i: (0, i, 0)), _vec(e), _vec(e),
                  pl.BlockSpec(w_s.shape, lambda i: (0, 0, 0)), pl.BlockSpec(b_s_t.shape, lambda i: (0, 0))],
        out_specs=pl.BlockSpec((CHUNK, e), lambda i: (i, 0)), out_shape=jax.ShapeDtypeStruct((rows, e), MXU_DTYPE),
        compiler_params=_params(("parallel",), 12 * _nbytes((CHUNK, e), F32)),
    )(z3, ln_g, ln_b, w_s, b_s_t)


def _cm_mid_bwd(z3, dt, ln_g, ln_b, w_s, b_s_t, name):
    _, rows, e = z3.shape
    groups = w_s.shape[0]
    gw = e // groups

    def body(z_ref, dt_ref, lg_ref, lb_ref, ws_ref, bs_ref, dz_ref, dlg_ref, dlb_ref, dws_ref, dbs_ref, dvn_ref):
        i = pl.program_id(0)
        first = i == 0
        v = z_ref[1]
        vn, ln_vjp = jax.vjp(_ln, v, lg_ref[...], lb_ref[...])

        @pl.when(first)
        def _():
            dws_ref[...] = jnp.zeros_like(dws_ref)
            dbs_ref[...] = jnp.zeros_like(dbs_ref)

        for h in range(groups):
            cols = slice(h * gw, (h + 1) * gw)
            vn_h = vn[:, cols]
            s = _dot(ws_ref[h], vn_h, NN) + bs_ref[:, h:h + 1]
            u, g, dth = z_ref[0, :, cols], z_ref[2, :, cols], dt_ref[:, cols]
            sg = _silu(g)
            dz_ref[0, :, cols] = (dth * s * sg).astype(dz_ref.dtype)
            dz_ref[2, :, cols] = (dth * u * s * _dsilu(g)).astype(dz_ref.dtype)
            ds = dth * u * sg
            dvn_ref[:, cols] = _dot(ws_ref[h], ds, TN)
            dws_ref[h] += _dot(ds, vn_h, NT)
            dbs_ref[:, h:h + 1] += jnp.sum(ds, axis=1, keepdims=True)
        dv, dlg, dlb = ln_vjp(dvn_ref[...])
        dz_ref[1] = dv.astype(dz_ref.dtype)
        _acc(dlg_ref, first, dlg)
        _acc(dlb_ref, first, dlb)

    return pl.pallas_call(
        body, name=name, grid=(rows // CHUNK,),
        in_specs=[pl.BlockSpec((3, CHUNK, e), lambda i: (0, i, 0)), pl.BlockSpec((CHUNK, e), lambda i: (i, 0)), _vec(e), _vec(e),
                  pl.BlockSpec(w_s.shape, lambda i: (0, 0, 0)), pl.BlockSpec(b_s_t.shape, lambda i: (0, 0))],
        out_specs=[pl.BlockSpec((3, CHUNK, e), lambda i: (0, i, 0)), _vec(e), _vec(e),
                   pl.BlockSpec(w_s.shape, lambda i: (0, 0, 0)), pl.BlockSpec(b_s_t.shape, lambda i: (0, 0))],
        out_shape=[jax.ShapeDtypeStruct((3, rows, e), MXU_DTYPE), jax.ShapeDtypeStruct((1, e), F32),
                   jax.ShapeDtypeStruct((1, e), F32), jax.ShapeDtypeStruct(w_s.shape, F32),
                   jax.ShapeDtypeStruct(b_s_t.shape, F32)],
        scratch_shapes=[pltpu.VMEM((CHUNK, e), F32)],
        compiler_params=_params(("arbitrary",), 20 * _nbytes((CHUNK, e), F32)),
    )(z3, dt, ln_g, ln_b, w_s, b_s_t)


CONV_COL_TILE = 512


def _conv_specs(rt, tc, planes):
    per = rt.tile // CONV_HALO
    last = rt.rows // CONV_HALO - 1
    if planes:
        cur = pl.BlockSpec((planes, rt.tile, tc), lambda j, i: (0, i, j))
        prev = pl.BlockSpec((planes, CONV_HALO, tc), lambda j, i: (0, jnp.maximum(i * per - 1, 0), j))
        nxt = pl.BlockSpec((planes, CONV_HALO, tc), lambda j, i: (0, jnp.minimum((i + 1) * per, last), j))
    else:
        cur = pl.BlockSpec((rt.tile, tc), lambda j, i: (i, j))
        prev = pl.BlockSpec((CONV_HALO, tc), lambda j, i: (jnp.maximum(i * per - 1, 0), j))
        nxt = pl.BlockSpec((CONV_HALO, tc), lambda j, i: (jnp.minimum((i + 1) * per, last), j))
    return cur, prev, nxt


def _halo_ok(rt, i):
    prev_ok = (i != 0) & (i != rt.nt_x)
    next_ok = (i != rt.nt_x - 1) & (i != rt.nt - 1)
    return prev_ok, next_ok


def _glu(ref):
    return ref[0] * _sigmoid(ref[1])


def _padded(cur, prev, nxt, prev_ok, next_ok):
    return jnp.concatenate([jnp.where(prev_ok, prev, 0.0), cur, jnp.where(next_ok, nxt, 0.0)], axis=0)


def _conv_fwd(z3, conv_w, conv_b, rt, name):
    _, rows, e = z3.shape
    tc = _pick(e, (CONV_COL_TILE, 256, 128))
    tr = rt.tile

    def body(cur_ref, prev_ref, next_ref, w_ref, b_ref, o_ref, pad_ref):
        prev_ok, next_ok = _halo_ok(rt, pl.program_id(1))
        pad_ref[...] = _padded(_glu(cur_ref), _glu(prev_ref), _glu(next_ref), prev_ok, next_ok)
        acc = jnp.broadcast_to(b_ref[...], (tr, tc))
        for k in range(CONV_W):
            off = CONV_HALO - CONV_W // 2 + k
            acc = acc + w_ref[k:k + 1, :] * pad_ref[off:off + tr, :]
        o_ref[...] = acc

    cur, prev, nxt = _conv_specs(rt, tc, 2)
    return pl.pallas_call(
        body, name=name, grid=(e // tc, rt.nt),
        in_specs=[cur, prev, nxt, pl.BlockSpec((CONV_W, tc), lambda j, i: (0, j)), pl.BlockSpec((1, tc), lambda j, i: (0, j))],
        out_specs=pl.BlockSpec((tr, tc), lambda j, i: (i, j)), out_shape=jax.ShapeDtypeStruct((rows, e), F32),
        scratch_shapes=[pltpu.VMEM((tr + 2 * CONV_HALO, tc), F32)],
        compiler_params=_params(("parallel", "arbitrary"), 16 * _nbytes((tr, tc), F32)),
    )(z3, z3, z3, conv_w, conv_b)


def _conv_bwd(z3, dy1, dg, conv_w, rt, name):
    _, rows, e = z3.shape
    tc = _pick(e, (CONV_COL_TILE, 256, 128))
    tr = rt.tile

    def body(cur_ref, prev_ref, next_ref, dcur_ref, dprev_ref, dnext_ref, dg_ref, w_ref, dz_ref, dw_ref, db_ref, pad_ref, dpad_ref):
        i = pl.program_id(1)
        prev_ok, next_ok = _halo_ok(rt, i)
        pad_ref[...] = _padded(_glu(cur_ref), _glu(prev_ref), _glu(next_ref), prev_ok, next_ok)
        dcur = dcur_ref[...]
        dpad_ref[...] = _padded(dcur, dprev_ref[...], dnext_ref[...], prev_ok, next_ok)

        @pl.when(i == 0)
        def _():
            dw_ref[...] = jnp.zeros_like(dw_ref)
            db_ref[...] = jnp.zeros_like(db_ref)

        dy0 = jnp.zeros((tr, tc), F32)
        for k in range(CONV_W):
            off = CONV_HALO - CONV_W // 2 + k
            roff = CONV_HALO + CONV_W // 2 - k
            dy0 = dy0 + w_ref[k:k + 1, :] * dpad_ref[roff:roff + tr, :]
            dw_ref[k:k + 1, :] += jnp.sum(dcur * pad_ref[off:off + tr, :], axis=0, keepdims=True)
        db_ref[...] += jnp.sum(dcur, axis=0, keepdims=True)
        a, sb = cur_ref[0], _sigmoid(cur_ref[1])
        dz_ref[0] = (dy0 * sb).astype(dz_ref.dtype)
        dz_ref[1] = (dy0 * a * sb * (1.0 - sb)).astype(dz_ref.dtype)
        dz_ref[2] = dg_ref[...]

    cur, prev, nxt = _conv_specs(rt, tc, 2)
    dcur, dprev, dnxt = _conv_specs(rt, tc, 0)
    return pl.pallas_call(
        body, name=name, grid=(e // tc, rt.nt),
        in_specs=[cur, prev, nxt, dcur, dprev, dnxt, pl.BlockSpec((tr, tc), lambda j, i: (i, j)),
                  pl.BlockSpec((CONV_W, tc), lambda j, i: (0, j))],
        out_specs=[pl.BlockSpec((3, tr, tc), lambda j, i: (0, i, j)), pl.BlockSpec((CONV_W, tc), lambda j, i: (0, j)),
                   pl.BlockSpec((1, tc), lambda j, i: (0, j))],
        out_shape=[jax.ShapeDtypeStruct((3, rows, e), MXU_DTYPE), jax.ShapeDtypeStruct((CONV_W, e), F32),
                   jax.ShapeDtypeStruct((1, e), F32)],
        scratch_shapes=[pltpu.VMEM((tr + 2 * CONV_HALO, tc), F32), pltpu.VMEM((tr + 2 * CONV_HALO, tc), F32)],
        compiler_params=_params(("parallel", "arbitrary"), 24 * _nbytes((tr, tc), F32)),
    )(z3, z3, z3, dy1, dy1, dy1, dg, conv_w)


def _conv_mid(y1, g, ln_g, ln_b):
    return _silu(_ln(y1, ln_g, ln_b)) * _silu(g)


def _conv_mid_fwd(y1, z3, ln_g, ln_b, rt, name):
    e = y1.shape[1]
    tr = CHUNK

    def body(y_ref, g_ref, lg_ref, lb_ref, t_ref):
        t_ref[...] = _conv_mid(y_ref[...], g_ref[...], lg_ref[...], lb_ref[...]).astype(t_ref.dtype)

    return pl.pallas_call(
        body, name=name, grid=(rt.rows // tr,),
        in_specs=[pl.BlockSpec((tr, e), lambda i: (i, 0)), pl.BlockSpec((None, tr, e), lambda i: (2, i, 0)), _vec(e), _vec(e)],
        out_specs=pl.BlockSpec((tr, e), lambda i: (i, 0)), out_shape=jax.ShapeDtypeStruct((rt.rows, e), MXU_DTYPE),
        compiler_params=_params(("parallel",), 12 * _nbytes((tr, e), F32)),
    )(y1, z3, ln_g, ln_b)


def _conv_mid_bwd(y1, z3, dt, ln_g, ln_b, rt, name):
    e = y1.shape[1]
    tr = CHUNK

    def body(y_ref, g_ref, dt_ref, lg_ref, lb_ref, dy_ref, dg_ref, dlg_ref, dlb_ref):
        first = pl.program_id(0) == 0
        _, vjp = jax.vjp(_conv_mid, y_ref[...], g_ref[...], lg_ref[...], lb_ref[...])
        dy, dg, dlg, dlb = vjp(dt_ref[...])
        dy_ref[...] = dy
        dg_ref[...] = dg.astype(dg_ref.dtype)
        _acc(dlg_ref, first, dlg)
        _acc(dlb_ref, first, dlb)

    row = pl.BlockSpec((tr, e), lambda i: (i, 0))
    return pl.pallas_call(
        body, name=name, grid=(rt.rows // tr,),
        in_specs=[row, pl.BlockSpec((None, tr, e), lambda i: (2, i, 0)), row, _vec(e), _vec(e)],
        out_specs=[row, row, _vec(e), _vec(e)],
        out_shape=[jax.ShapeDtypeStruct((rt.rows, e), F32), jax.ShapeDtypeStruct((rt.rows, e), MXU_DTYPE),
                   jax.ShapeDtypeStruct((1, e), F32), jax.ShapeDtypeStruct((1, e), F32)],
        compiler_params=_params(("arbitrary",), 20 * _nbytes((tr, e), F32)),
    )(y1, z3, dt, ln_g, ln_b)


def _rms(x, g):
    return x * lax.rsqrt(jnp.mean(x * x, axis=-1, keepdims=True) + LN_EPS) * g


def _pair_swap(x):
    lane = lax.broadcasted_iota(jnp.int32, x.shape, x.ndim - 1)
    return jnp.where(lane % 2 == 0, pltpu.roll(x, x.shape[-1] - 1, x.ndim - 1), pltpu.roll(x, 1, x.ndim - 1))


def _rope(x, cos, sin):
    return x * cos + _pair_swap(x) * sin


def _rope_t(dy, cos, sin):
    return dy * cos + _pair_swap(dy * sin)


def _rope_tables(n_x, n_ctx):
    t = jnp.arange(n_x)
    row = (t // GRID_W).astype(F32)
    col = (t % GRID_W).astype(F32)
    axis_dim = HEAD_DIM // 2
    inv = 1.0 / (ROPE_THETA ** (jnp.arange(0, axis_dim, 2, dtype=F32) / axis_dim))
    ang = jnp.concatenate([row[:, None] * inv, col[:, None] * inv], axis=-1)
    cos, sin = jnp.cos(ang), jnp.sin(ang)
    cos2 = jnp.repeat(cos, 2, axis=-1)
    sin2 = jnp.stack([-sin, sin], axis=-1).reshape(n_x, HEAD_DIM)
    cos2 = jnp.concatenate([cos2, jnp.ones((n_ctx, HEAD_DIM), F32)], axis=0)
    sin2 = jnp.concatenate([sin2, jnp.zeros((n_ctx, HEAD_DIM), F32)], axis=0)
    return cos2, sin2


def _qkv_prep(z4, q_g, k_g, cos, sin, d, kvw, rt, name):
    hd = HEAD_DIM
    kb = d // kvw

    def body(q_ref, k_ref, v_ref, qg_ref, kg_ref, cos_ref, sin_ref, qo_ref, ko_ref, vo_ref):
        cos, sin = cos_ref[...], sin_ref[...]
        for h in range(d // hd):
            cols = slice(h * hd, (h + 1) * hd)
            qo_ref[:, cols] = _rope(_rms(q_ref[:, cols], qg_ref[...]), cos, sin).astype(qo_ref.dtype)
        for h in range(kvw // hd):
            cols = slice(h * hd, (h + 1) * hd)
            ko_ref[:, cols] = _rope(_rms(k_ref[:, cols], kg_ref[...]), cos, sin).astype(ko_ref.dtype)
        vo_ref[...] = v_ref[...].astype(vo_ref.dtype)

    tr = rt.tile
    return pl.pallas_call(
        body, name=name, grid=(rt.nt,),
        in_specs=[pl.BlockSpec((tr, d), lambda i: (i, 0)), pl.BlockSpec((tr, kvw), lambda i: (i, kb)),
                  pl.BlockSpec((tr, kvw), lambda i: (i, kb + 1)), _vec(hd), _vec(hd), rt.full(hd), rt.full(hd)],
        out_specs=[rt.full(d), rt.full(kvw), rt.full(kvw)],
        out_shape=[jax.ShapeDtypeStruct((rt.rows, d), MXU_DTYPE), jax.ShapeDtypeStruct((rt.rows, kvw), MXU_DTYPE),
                   jax.ShapeDtypeStruct((rt.rows, kvw), MXU_DTYPE)],
        compiler_params=_params(("parallel",), 8 * _nbytes((tr, d), F32)),
    )(z4, z4, z4, q_g, k_g, cos, sin)


ATTN_Q_TILE = 256


def _attn_fwd(qh, kh, vh, n_x, name):
    rows, d = qh.shape
    kvw = kh.shape[1]
    hd = HEAD_DIM
    n_kv = kvw // hd
    gqw = d // n_kv
    grp = gqw // hd
    tq = _pick(n_x, (ATTN_Q_TILE, 128))
    scale = hd ** -0.5

    def body(q_ref, k_ref, v_ref, o_ref, lse_ref):
        k, v = k_ref[...], v_ref[...]
        for g in range(grp):
            cols = slice(g * hd, (g + 1) * hd)
            s = _dot(q_ref[:, cols], k, NT) * scale
            m = jnp.max(s, axis=-1, keepdims=True)
            p = jnp.exp(s - m)
            l = jnp.sum(p, axis=-1, keepdims=True)
            o_ref[:, cols] = _dot(p / l, v, NN)
            lse_ref[:, g:g + 1] = m + jnp.log(l)

    vmem = 4 * _nbytes((rows, hd), MXU_DTYPE) + 4 * _nbytes((tq, rows), F32) + 6 * _nbytes((tq, gqw), F32)
    return pl.pallas_call(
        body, name=name, grid=(n_kv, n_x // tq),
        in_specs=[pl.BlockSpec((tq, gqw), lambda h, i: (i, h)), pl.BlockSpec((rows, hd), lambda h, i: (0, h)),
                  pl.BlockSpec((rows, hd), lambda h, i: (0, h))],
        out_specs=[pl.BlockSpec((tq, gqw), lambda h, i: (i, h)), pl.BlockSpec((None, tq, grp), lambda h, i: (h, i, 0))],
        out_shape=[jax.ShapeDtypeStruct((n_x, d), F32), jax.ShapeDtypeStruct((n_kv, n_x, grp), F32)],
        compiler_params=_params(("parallel", "arbitrary"), vmem),
    )(qh, kh, vh)


def _attn_bwd(qh, kh, vh, do, lse, n_x, name):
    rows, d = qh.shape
    kvw = kh.shape[1]
    hd = HEAD_DIM
    n_kv = kvw // hd
    gqw = d // n_kv
    grp = gqw // hd
    tq = _pick(n_x, (ATTN_Q_TILE, 128))
    scale = hd ** -0.5

    def body(q_ref, k_ref, v_ref, do_ref, lse_ref, dq_ref, dk_ref, dv_ref):
        @pl.when(pl.program_id(1) == 0)
        def _():
            dk_ref[...] = jnp.zeros_like(dk_ref)
            dv_ref[...] = jnp.zeros_like(dv_ref)

        k, v = k_ref[...], v_ref[...]
        for g in range(grp):
            cols = slice(g * hd, (g + 1) * hd)
            q, dog = q_ref[:, cols], do_ref[:, cols]
            p = jnp.exp(_dot(q, k, NT) * scale - lse_ref[:, g:g + 1])
            dp = _dot(dog, v, NT)
            ds = (p * (dp - jnp.sum(dp * p, axis=-1, keepdims=True)) * scale).astype(MXU_DTYPE)
            dq_ref[:, cols] = _dot(ds, k, NN)
            dk_ref[...] += _dot(ds, q, TN)
            dv_ref[...] += _dot(p, dog, TN)

    vmem = 4 * _nbytes((rows, hd), MXU_DTYPE) + 4 * _nbytes((rows, hd), F32) + 6 * _nbytes((tq, rows), F32) + 8 * _nbytes((tq, gqw), F32)
    qspec = pl.BlockSpec((tq, gqw), lambda h, i: (i, h))
    kspec = pl.BlockSpec((rows, hd), lambda h, i: (0, h))
    return pl.pallas_call(
        body, name=name, grid=(n_kv, n_x // tq),
        in_specs=[qspec, kspec, kspec, qspec, pl.BlockSpec((None, tq, grp), lambda h, i: (h, i, 0))],
        out_specs=[qspec, kspec, kspec],
        out_shape=[jax.ShapeDtypeStruct((n_x, d), F32), jax.ShapeDtypeStruct((rows, kvw), F32),
                   jax.ShapeDtypeStruct((rows, kvw), F32)],
        compiler_params=_params(("parallel", "arbitrary"), vmem),
    )(qh, kh, vh, do, lse)


def _attn_gate(o, z4, d, kvw, rt, name):
    g0 = (d + 2 * kvw) // kvw
    tr = rt.tile

    def body(o_ref, g_ref, t_ref):
        t_ref[...] = (o_ref[...] * _silu(g_ref[...])).astype(t_ref.dtype)

    tile = pl.BlockSpec((tr, kvw), lambda i, j: (i, j))
    return pl.pallas_call(
        body, name=name, grid=(rt.nt, d // kvw),
        in_specs=[tile, pl.BlockSpec((tr, kvw), lambda i, j: (i, g0 + j))],
        out_specs=tile, out_shape=jax.ShapeDtypeStruct((rt.rows, d), MXU_DTYPE),
        compiler_params=_params(("parallel", "parallel"), 8 * _nbytes((tr, kvw), F32)),
    )(o, z4)


def _attn_gate_bwd(dt, o, z4, d, kvw, rt, name):
    g0 = (d + 2 * kvw) // kvw
    tr = rt.tile

    def body(dt_ref, o_ref, g_ref, do_ref, dg_ref):
        dt_v, g = dt_ref[...], g_ref[...]
        do_ref[...] = (dt_v * _silu(g)).astype(do_ref.dtype)
        dg_ref[...] = (dt_v * o_ref[...] * _dsilu(g)).astype(dg_ref.dtype)

    tile = pl.BlockSpec((tr, kvw), lambda i, j: (i, j))
    return pl.pallas_call(
        body, name=name, grid=(rt.nt, d // kvw),
        in_specs=[tile, tile, pl.BlockSpec((tr, kvw), lambda i, j: (i, g0 + j))],
        out_specs=[tile, tile],
        out_shape=[jax.ShapeDtypeStruct((rt.rows, d), MXU_DTYPE), jax.ShapeDtypeStruct((rt.rows, d), MXU_DTYPE)],
        compiler_params=_params(("parallel", "parallel"), 12 * _nbytes((tr, kvw), F32)),
    )(dt, o, z4)


def _prep_bwd(dxh, z4, col_block, gain, cos, sin, rt, name):
    w = dxh.shape[1]
    hd = HEAD_DIM

    def body(dxh_ref, x_ref, g_ref, cos_ref, sin_ref, dx_ref, dg_ref):
        cos, sin = cos_ref[...], sin_ref[...]
        dg = jnp.zeros((1, hd), F32)
        for h in range(w // hd):
            cols = slice(h * hd, (h + 1) * hd)
            _, vjp = jax.vjp(_rms, x_ref[:, cols], g_ref[...])
            dx, dgh = vjp(_rope_t(dxh_ref[:, cols], cos, sin))
            dx_ref[:, cols] = dx.astype(dx_ref.dtype)
            dg = dg + dgh
        _acc(dg_ref, pl.program_id(0) == 0, dg)

    tr = rt.tile
    return pl.pallas_call(
        body, name=name, grid=(rt.nt,),
        in_specs=[rt.full(w), pl.BlockSpec((tr, w), lambda i: (i, col_block)), _vec(hd), rt.full(hd), rt.full(hd)],
        out_specs=[rt.full(w), _vec(hd)],
        out_shape=[jax.ShapeDtypeStruct((rt.rows, w), MXU_DTYPE), jax.ShapeDtypeStruct((1, hd), F32)],
        compiler_params=_params(("arbitrary",), 12 * _nbytes((tr, w), F32)),
    )(dxh, z4, gain, cos, sin)


def _loss_head(x, target, rt, name):
    d = x.shape[1]

    def body(x_ref, t_ref, dx_ref, l_ref):
        err = x_ref[...] - t_ref[...]
        dx_ref[...] = err / d
        row = jnp.mean(err * err, axis=-1, keepdims=True)
        _acc(l_ref, pl.program_id(0) == 0, jnp.sum(row, axis=0, keepdims=True))

    return pl.pallas_call(
        body, name=name, grid=(rt.nt,),
        in_specs=[rt.full(d), rt.full(d)],
        out_specs=[rt.full(d), pl.BlockSpec((1, 1), lambda i: (0, 0))],
        out_shape=[jax.ShapeDtypeStruct(x.shape, F32), jax.ShapeDtypeStruct((1, 1), F32)],
        compiler_params=_params(("arbitrary",), 8 * _nbytes((rt.tile, d), F32)),
    )(x, target)


def _adamw(w, g, m, v):
    m = ADAM_B1 * m + (1.0 - ADAM_B1) * g
    v = ADAM_B2 * v + (1.0 - ADAM_B2) * (g * g)
    m_hat = m / (1.0 - ADAM_B1 ** ADAM_STEP)
    v_hat = v / (1.0 - ADAM_B2 ** ADAM_STEP)
    delta = -ADAM_LR * (m_hat / (jnp.sqrt(v_hat) + ADAM_EPS) + ADAM_WD * w)
    return delta, m, v


ADAM_TILE_BYTES = 1 << 20


def _adam_tile(rows, cols):
    tr = rows
    while tr % 16 == 0 and tr * cols * 4 > ADAM_TILE_BYTES:
        tr //= 2
    return tr


def _adam_reduce(parts, w, m, v, name):
    rows, cols = w.shape
    tr = _adam_tile(rows, cols)

    def body(p_ref, w_ref, m_ref, v_ref, g_ref, d_ref, mo_ref, vo_ref):
        g = p_ref[0].astype(F32)
        for k in range(1, N_DEV):
            g = g + p_ref[k].astype(F32)
        g_ref[...] = g
        d_ref[...], mo_ref[...], vo_ref[...] = _adamw(w_ref[...], g, m_ref[...], v_ref[...])

    row = pl.BlockSpec((tr, cols), lambda i: (i, 0))
    sds = jax.ShapeDtypeStruct((rows, cols), F32)
    return pl.pallas_call(
        body, name=name, grid=(rows // tr,),
        in_specs=[pl.BlockSpec((N_DEV, tr, cols), lambda i: (0, i, 0)), row, row, row],
        out_specs=[row] * 4, out_shape=[sds] * 4,
        compiler_params=_params(("parallel",), 40 * _nbytes((tr, cols), F32)),
    )(parts, w, m, v)


def _adam_plain(g, w, m, v, name):
    rows, cols = w.shape
    tr = _adam_tile(rows, cols)

    def body(g_ref, w_ref, m_ref, v_ref, d_ref, mo_ref, vo_ref):
        d_ref[...], mo_ref[...], vo_ref[...] = _adamw(w_ref[...], g_ref[...], m_ref[...], v_ref[...])

    row = pl.BlockSpec((tr, cols), lambda i: (i, 0))
    sds = jax.ShapeDtypeStruct((rows, cols), F32)
    return pl.pallas_call(
        body, name=name, grid=(rows // tr,),
        in_specs=[row] * 4, out_specs=[row] * 3, out_shape=[sds] * 3,
        compiler_params=_params(("parallel",), 32 * _nbytes((tr, cols), F32)),
    )(g, w, m, v)


def _sum_devices(parts, name):
    _, rows, cols = parts.shape
    tr = _adam_tile(rows, cols)

    def body(p_ref, o_ref):
        g = p_ref[0]
        for k in range(1, N_DEV):
            g = g + p_ref[k]
        o_ref[...] = g

    return pl.pallas_call(
        body, name=name, grid=(rows // tr,),
        in_specs=[pl.BlockSpec((N_DEV, tr, cols), lambda i: (0, i, 0))],
        out_specs=pl.BlockSpec((tr, cols), lambda i: (i, 0)), out_shape=jax.ShapeDtypeStruct((rows, cols), F32),
        compiler_params=_params(("parallel",), 24 * _nbytes((tr, cols), F32)),
    )(parts)


COND_ROWS = 16


def _mod_fwd_mm(cond, mod_w, mod_b, name):
    layers, d, w = mod_w.shape

    def body(c_ref, w_ref, b_ref, o_ref):
        o_ref[...] = _dot(_silu(c_ref[...]), w_ref[...], NN) + b_ref[...]

    return pl.pallas_call(
        body, name=name, grid=(layers,),
        in_specs=[pl.BlockSpec((COND_ROWS, d), lambda l: (0, 0)), pl.BlockSpec((None, d, w), lambda l: (l, 0, 0)),
                  pl.BlockSpec((None, 1, w), lambda l: (l, 0, 0))],
        out_specs=pl.BlockSpec((None, COND_ROWS, w), lambda l: (l, 0, 0)),
        out_shape=jax.ShapeDtypeStruct((layers, COND_ROWS, w), F32),
        compiler_params=_params(("parallel",), 4 * _nbytes((d, w), F32)),
    )(cond, mod_w, mod_b)


def _mod_bwd_mm(cond, dm, mod_w, name):
    layers, d, w = mod_w.shape

    def body(c_ref, dm_ref, w_ref, dw_ref, dc_ref):
        dmv = dm_ref[...]
        dw_ref[...] = _dot(_silu(c_ref[...]), dmv, TN)
        _acc(dc_ref, pl.program_id(0) == 0, _dot(dmv, w_ref[...], NT))

    return pl.pallas_call(
        body, name=name, grid=(layers,),
        in_specs=[pl.BlockSpec((COND_ROWS, d), lambda l: (0, 0)), pl.BlockSpec((None, COND_ROWS, w), lambda l: (l, 0, 0)),
                  pl.BlockSpec((None, d, w), lambda l: (l, 0, 0))],
        out_specs=[pl.BlockSpec((None, d, w), lambda l: (l, 0, 0)), pl.BlockSpec((COND_ROWS, d), lambda l: (0, 0))],
        out_shape=[jax.ShapeDtypeStruct((layers, d, w), F32), jax.ShapeDtypeStruct((COND_ROWS, d), F32)],
        compiler_params=_params(("arbitrary",), 6 * _nbytes((d, w), F32)),
    )(cond, dm, mod_w)


PACK_ROWS = 256


def _pack(arrs):
    flat = jnp.concatenate([a.reshape(-1).astype(F32) for a in arrs])
    pad = (-flat.shape[0]) % (PACK_ROWS * LANES)
    return jnp.pad(flat, (0, pad)).reshape(-1, LANES)


def _unpack(flat2d, shapes):
    flat = flat2d.reshape(-1)
    out, off = [], 0
    for s in shapes:
        n = math.prod(s)
        out.append(flat[off:off + n].reshape(s))
        off += n
    return out


def _unpack_dev(g2d, shapes):
    flat = g2d.reshape(N_DEV, -1)
    out, off = [], 0
    for s in shapes:
        n = math.prod(s)
        out.append(flat[:, off:off + n].reshape((N_DEV, *s)))
        off += n
    return out


def kernel(x, c, ctx, c_ctx, mod_w, mod_b, post_g, post_b, a_w_in, a_ln_g, a_ln_b, a_w_s, a_b_s, a_w_out, b_w_in, b_conv_w, b_conv_b, b_ln_g, b_ln_b, b_w_out, c_w_in, c_q_g, c_k_g, c_w_out, loss_target, m_c_ctx, m_mod_w, m_mod_b, m_post_g, m_post_b, m_a_w_in, m_a_ln_g, m_a_ln_b, m_a_w_s, m_a_b_s, m_a_w_out, m_b_w_in, m_b_conv_w, m_b_conv_b, m_b_ln_g, m_b_ln_b, m_b_w_out, m_c_w_in, m_c_q_g, m_c_k_g, m_c_w_out, v_c_ctx, v_mod_w, v_mod_b, v_post_g, v_post_b, v_a_w_in, v_a_ln_g, v_a_ln_b, v_a_w_s, v_a_b_s, v_a_w_out, v_b_w_in, v_b_conv_w, v_b_conv_b, v_b_ln_g, v_b_ln_b, v_b_w_out, v_c_w_in, v_c_q_g, v_c_k_g, v_c_w_out):
    n_x, d = x.shape[1], x.shape[2]
    n_ctx = ctx.shape[1]
    e = a_w_out.shape[1] * N_DEV
    kvw = N_KV_HEADS * HEAD_DIM
    me = _dev_index(_mesh_pos())
    rt_all = _Rows(n_x, n_ctx)
    rt_x = _Rows(n_x, 0)

    small_in = [c[0], a_ln_g, a_ln_b, b_conv_w[0]]
    (g_small,) = _all_gather([_pack(small_in)], "ag_small_params")
    conds, ln_g_all, ln_b_all, conv_w_all = _unpack_dev(g_small, [a.shape for a in small_in])
    a_ln_g_f = jnp.moveaxis(ln_g_all, 0, 1).reshape(a_ln_g.shape[0], 1, e)
    a_ln_b_f = jnp.moveaxis(ln_b_all, 0, 1).reshape(a_ln_b.shape[0], 1, e)
    conv_w_f = jnp.moveaxis(conv_w_all, 0, 1).reshape(CONV_W, e)
    cond = jnp.zeros((COND_ROWS, d), F32).at[:N_DEV].set(conds).at[N_DEV].set(c_ctx)

    wm = mod_w.shape[2]
    mod_b_mine = lax.dynamic_slice_in_dim(mod_b, me * wm, wm, axis=1).reshape(DEPTH, 1, wm)
    (mods_g,) = _all_gather([_mod_fwd_mm(cond, mod_w, mod_b_mine, "mod_fwd")], "ag_mod")
    mods = jnp.moveaxis(mods_g, 0, 2).reshape(DEPTH, COND_ROWS, 3 * d)
    mine = lax.dynamic_index_in_dim(mods, me, axis=1, keepdims=False)
    modv = jnp.stack([mine, mods[:, N_DEV]], axis=1).reshape(DEPTH * 2 * 3, 1, d)

    big = [a_w_in[0], a_w_out[0], b_w_in[0], b_w_out[0], c_w_in[0], c_w_out[0], a_w_in[1], a_w_out[1]]
    wa_in0, wa_out0, wb_in, wb_out, wc_in, wc_out, wa_in1, wa_out1 = [
        _all_gather([wt.astype(MXU_DTYPE)], f"ag_w{k}")[0] for k, wt in enumerate(big)]
    wa_out0, wb_out, wc_out, wa_out1 = [wt.reshape(-1, d) for wt in (wa_out0, wb_out, wc_out, wa_out1)]

    ws_op = a_w_s.astype(MXU_DTYPE)
    bs_t = jnp.swapaxes(a_b_s, 1, 2)
    pg = post_g.reshape(DEPTH, 1, d)
    pb = post_b.reshape(DEPTH, 1, d)

    xs0 = jnp.concatenate([x[0], ctx[0]], axis=0)
    h0 = _modulate(xs0, modv, 0, rt_all, "mod0")
    z0 = _mm_nn(h0, wa_in0, planes=3, name="l0_in")
    t0 = _cm_mid_fwd(z0, a_ln_g_f[0], a_ln_b_f[0], ws_op[0], bs_t[0], "l0_mid")
    y0 = _mm_nn(t0, wa_out0, name="l0_out")
    xs1 = _post_fwd(xs0, y0, modv, 0, pg[0], pb[0], rt_all, "l0_post")
    h1 = _modulate(xs1, modv, 1, rt_all, "mod1")
    z1 = _mm_nn(h1, wb_in, planes=3, name="l1_in")
    cy1 = _conv_fwd(z1, conv_w_f, b_conv_b, rt_all, "l1_conv")
    t1 = _conv_mid_fwd(cy1, z1, b_ln_g, b_ln_b, rt_all, "l1_mid")
    y1 = _mm_nn(t1, wb_out, name="l1_out")
    xs2 = _post_fwd(xs1, y1, modv, 1, pg[1], pb[1], rt_all, "l1_post")
    cos, sin = _rope_tables(n_x, n_ctx)
    h2 = _modulate(xs2, modv, 2, rt_all, "mod2")
    z2 = _mm_nn(h2, wc_in, name="l2_in")
    qh, kh, vh = _qkv_prep(z2, c_q_g, c_k_g, cos, sin, d, kvw, rt_all, "l2_prep")
    o2, lse = _attn_fwd(qh, kh, vh, n_x, "l2_attn")
    t2 = _attn_gate(o2, z2, d, kvw, rt_x, "l2_gate")
    y2 = _mm_nn(t2, wc_out, name="l2_out")
    x2 = xs2[:n_x]
    x3 = _post_fwd(x2, y2, modv, 2, pg[2], pb[2], rt_x, "l2_post")
    h3 = _modulate(x3, modv, 3, rt_x, "mod3")
    z3 = _mm_nn(h3, wa_in1, planes=3, name="l3_in")
    t3 = _cm_mid_fwd(z3, a_ln_g_f[1], a_ln_b_f[1], ws_op[1], bs_t[1], "l3_mid")
    y3 = _mm_nn(t3, wa_out1, name="l3_out")
    x4 = _post_fwd(x3, y3, modv, 3, pg[3], pb[3], rt_x, "l3_post")

    dx4, loss_sum = _loss_head(x4, loss_target[0], rt_x, "loss")
    loss = lax.psum(0.5 * loss_sum[0, 0], ("x", "y", "c"))

    gdt = MXU_DTYPE
    zeros_ctx = jnp.zeros((n_ctx, d), F32)
    dres3, dy3, dpg3, dpb3, dgate3 = _post_bwd(x3, y3, dx4, modv, 3, pg[3], pb[3], rt_x, "l3_post_b")
    dt3 = _mm_nt(dy3, wa_out1, name="l3_dt")
    gw_a_out1 = _mm_tn(t3, dy3, blocked=False, out_dtype=gdt, name="l3_dwout")
    dz3, dlg3, dlb3, dws3, dbs3 = _cm_mid_bwd(z3, dt3, a_ln_g_f[1], a_ln_b_f[1], ws_op[1], bs_t[1], "l3_mid_b")
    dh3 = _mm_nt_blocked(dz3, wa_in1, name="l3_dh")
    gw_a_in1 = _mm_tn(h3, dz3, blocked=True, out_dtype=gdt, name="l3_dwin")
    dx3, dshift3, dscale3 = _mod_bwd(dres3, dh3, x3, modv, 3, rt_x, "l3_mod_b")
    dres2, dy2, dpg2, dpb2, dgate2 = _post_bwd(x2, y2, dx3, modv, 2, pg[2], pb[2], rt_x, "l2_post_b")
    dt2 = _mm_nt(dy2, wc_out, name="l2_dt")
    gw_c_out = _mm_tn(t2, dy2, blocked=False, out_dtype=gdt, name="l2_dwout")
    do2, dg2 = _attn_gate_bwd(dt2, o2, z2, d, kvw, rt_x, "l2_gate_b")
    dqh, dkh, dvh = _attn_bwd(qh, kh, vh, do2, lse, n_x, "l2_attn_b")
    dq2, dqg = _prep_bwd(dqh, z2, 0, c_q_g, cos, sin, rt_x, "l2_qprep_b")
    dk2, dkg = _prep_bwd(dkh, z2, d // kvw, c_k_g, cos, sin, rt_all, "l2_kprep_b")
    zpad = jnp.zeros((n_ctx, d), MXU_DTYPE)
    dz2 = jnp.concatenate([jnp.concatenate([dq2, zpad], axis=0), dk2, dvh.astype(MXU_DTYPE),
                           jnp.concatenate([dg2, zpad], axis=0)], axis=1)
    dh2 = _mm_nt_blocked(dz2, wc_in, name="l2_dh")
    gw_c_in = _mm_tn(h2, dz2, blocked=True, out_dtype=gdt, name="l2_dwin")
    dres2 = jnp.concatenate([dres2, zeros_ctx], axis=0)
    dxs2, dshift2, dscale2 = _mod_bwd(dres2, dh2, xs2, modv, 2, rt_all, "l2_mod_b")
    dres1, dy1, dpg1, dpb1, dgate1 = _post_bwd(xs1, y1, dxs2, modv, 1, pg[1], pb[1], rt_all, "l1_post_b")
    dt1 = _mm_nt(dy1, wb_out, name="l1_dt")
    gw_b_out = _mm_tn(t1, dy1, blocked=False, out_dtype=gdt, name="l1_dwout")
    dcy1, dgc1, dblg, dblb = _conv_mid_bwd(cy1, z1, dt1, b_ln_g, b_ln_b, rt_all, "l1_mid_b")
    dz1, dconv_w, dconv_b = _conv_bwd(z1, dcy1, dgc1, conv_w_f, rt_all, "l1_conv_b")
    dh1 = _mm_nt_blocked(dz1, wb_in, name="l1_dh")
    gw_b_in = _mm_tn(h1, dz1, blocked=True, out_dtype=gdt, name="l1_dwin")
    dxs1, dshift1, dscale1 = _mod_bwd(dres1, dh1, xs1, modv, 1, rt_all, "l1_mod_b")
    dres0, dy0, dpg0, dpb0, dgate0 = _post_bwd(xs0, y0, dxs1, modv, 0, pg[0], pb[0], rt_all, "l0_post_b")
    dt0 = _mm_nt(dy0, wa_out0, name="l0_dt")
    gw_a_out0 = _mm_tn(t0, dy0, blocked=False, out_dtype=gdt, name="l0_dwout")
    dz0, dlg0, dlb0, dws0, dbs0 = _cm_mid_bwd(z0, dt0, a_ln_g_f[0], a_ln_b_f[0], ws_op[0], bs_t[0], "l0_mid_b")
    dh0 = _mm_nt_blocked(dz0, wa_in0, name="l0_dh")
    gw_a_in0 = _mm_tn(h0, dz0, blocked=True, out_dtype=gdt, name="l0_dwin")
    dxs0, dshift0, dscale0 = _mod_bwd(dres0, dh0, xs0, modv, 0, rt_all, "l0_mod_b")
    grad_x = dxs0[:n_x][None]

    def seg2(a):
        a = a[:, 0]
        return a if a.shape[0] == 2 else jnp.concatenate([a, jnp.zeros_like(a)], axis=0)

    gate2 = jnp.concatenate([dgate2[:, 0], jnp.zeros((1, d), F32)], axis=0)
    dmod = jnp.stack([
        jnp.concatenate([seg2(dshift0), seg2(dscale0), seg2(dgate0)], axis=1),
        jnp.concatenate([seg2(dshift1), seg2(dscale1), seg2(dgate1)], axis=1),
        jnp.concatenate([seg2(dshift2), seg2(dscale2), gate2], axis=1),
        jnp.concatenate([seg2(dshift3), seg2(dscale3), seg2(dgate3)], axis=1)])

    g_post_g = jnp.concatenate([dpg0, dpg1, dpg2, dpg3], axis=0)
    g_post_b = jnp.concatenate([dpb0, dpb1, dpb2, dpb3], axis=0)
    g_a_ln_g = jnp.concatenate([dlg0, dlg3], axis=0)
    g_a_ln_b = jnp.concatenate([dlb0, dlb3], axis=0)
    g_a_w_s = jnp.stack([dws0, dws3])
    g_a_b_s = jnp.swapaxes(jnp.stack([dbs0, dbs3]), 1, 2)
    small_g = [g_post_g, g_post_b, g_a_w_s, g_a_b_s, dconv_b, dblg, dblb, dqg, dkg, g_a_ln_g, g_a_ln_b, dconv_w,
               dmod[:, 0], dmod[:, 1]]
    small_shapes = [a.shape for a in small_g]
    (gs_all,) = _all_gather([_pack(small_g)], "ag_small_grads")
    sums = _unpack(_sum_devices(gs_all, "sum_small"), small_shapes)
    (s_post_g, s_post_b, s_a_w_s, s_a_b_s, s_conv_b, s_b_ln_g, s_b_ln_b, s_q_g, s_k_g, s_a_ln_g, s_a_ln_b, s_conv_w,
     s_dmod_own, s_dmod_ctx) = sums
    grad_mod_b = s_dmod_own + s_dmod_ctx
    wl = a_ln_g.shape[1]
    wcv = b_conv_w.shape[2]
    grad_a_ln_g = lax.dynamic_slice_in_dim(s_a_ln_g, me * wl, wl, axis=1)
    grad_a_ln_b = lax.dynamic_slice_in_dim(s_a_ln_b, me * wl, wl, axis=1)
    grad_b_conv_w = lax.dynamic_slice_in_dim(s_conv_w, me * wcv, wcv, axis=1)[None]

    dmod_dev = _unpack_dev(gs_all, small_shapes)[12]
    dm_rows = jnp.concatenate([jnp.moveaxis(dmod_dev, 0, 1), s_dmod_ctx[:, None],
                               jnp.zeros((DEPTH, COND_ROWS - N_DEV - 1, 3 * d), F32)], axis=1)
    dm_mine = lax.dynamic_slice_in_dim(dm_rows, me * wm, wm, axis=2)
    grad_mod_w, dcond_part = _mod_bwd_mm(cond, dm_mine, mod_w, "mod_bwd")
    (dcond_all,) = _all_gather([dcond_part], "ag_dcond")
    dcond = _sum_devices(dcond_all, "sum_dcond")
    grad_c_ctx = dcond[N_DEV] * _dsilu(c_ctx)

    gparts = [gw_a_in0, gw_a_in1, gw_a_out0.reshape(N_DEV, -1, d), gw_a_out1.reshape(N_DEV, -1, d), gw_b_in,
              gw_b_out.reshape(N_DEV, -1, d), gw_c_in, gw_c_out.reshape(N_DEV, -1, d)]
    recv = [_all_to_all([g], f"a2a_g{k}")[0] for k, g in enumerate(gparts)]
    r_a_in0, r_a_in1, r_a_out0, r_a_out1, r_b_in, r_b_out, r_c_in, r_c_out = recv

    def upd(parts, w, m, v, name):
        return _adam_reduce(parts, w, m, v, name)

    def stack2(a, b):
        return [jnp.stack([p, q]) for p, q in zip(a, b)]

    o_a_w_in = stack2(upd(r_a_in0, a_w_in[0], m_a_w_in[0], v_a_w_in[0], "adam_a_in0"),
                      upd(r_a_in1, a_w_in[1], m_a_w_in[1], v_a_w_in[1], "adam_a_in1"))
    o_a_w_out = stack2(upd(r_a_out0, a_w_out[0], m_a_w_out[0], v_a_w_out[0], "adam_a_out0"),
                       upd(r_a_out1, a_w_out[1], m_a_w_out[1], v_a_w_out[1], "adam_a_out1"))
    o_b_w_in = [a[None] for a in upd(r_b_in, b_w_in[0], m_b_w_in[0], v_b_w_in[0], "adam_b_in")]
    o_b_w_out = [a[None] for a in upd(r_b_out, b_w_out[0], m_b_w_out[0], v_b_w_out[0], "adam_b_out")]
    o_c_w_in = [a[None] for a in upd(r_c_in, c_w_in[0], m_c_w_in[0], v_c_w_in[0], "adam_c_in")]
    o_c_w_out = [a[None] for a in upd(r_c_out, c_w_out[0], m_c_w_out[0], v_c_w_out[0], "adam_c_out")]
    mw_shape = mod_w.shape
    o_mod_w = [grad_mod_w] + [a.reshape(mw_shape) for a in _adam_plain(
        grad_mod_w.reshape(-1, wm), mod_w.reshape(-1, wm), m_mod_w.reshape(-1, wm), v_mod_w.reshape(-1, wm), "adam_mod_w")]

    sg = [grad_c_ctx, grad_mod_b, s_post_g, s_post_b, grad_a_ln_g, grad_a_ln_b, s_a_w_s, s_a_b_s, grad_b_conv_w, s_conv_b,
          s_b_ln_g, s_b_ln_b, s_q_g, s_k_g]
    sw = [c_ctx, mod_b, post_g, post_b, a_ln_g, a_ln_b, a_w_s, a_b_s, b_conv_w, b_conv_b, b_ln_g, b_ln_b, c_q_g, c_k_g]
    sm = [m_c_ctx, m_mod_b, m_post_g, m_post_b, m_a_ln_g, m_a_ln_b, m_a_w_s, m_a_b_s, m_b_conv_w, m_b_conv_b, m_b_ln_g,
          m_b_ln_b, m_c_q_g, m_c_k_g]
    sv = [v_c_ctx, v_mod_b, v_post_g, v_post_b, v_a_ln_g, v_a_ln_b, v_a_w_s, v_a_b_s, v_b_conv_w, v_b_conv_b, v_b_ln_g,
          v_b_ln_b, v_c_q_g, v_c_k_g]
    shapes = [a.shape for a in sw]
    sg = [g.reshape(s) for g, s in zip(sg, shapes)]
    sd, snm, snv = [_unpack(a, shapes) for a in _adam_plain(_pack(sg), _pack(sw), _pack(sm), _pack(sv), "adam_small")]

    def small(k):
        return [sg[k], sd[k], snm[k], snv[k]]

    per_weight = [small(0), o_mod_w, small(1), small(2), small(3), o_a_w_in, small(4), small(5), small(6), small(7),
                  o_a_w_out, o_b_w_in, small(8), small(9), small(10), small(11), o_b_w_out, o_c_w_in, small(12), small(13),
                  o_c_w_out]
    outs = [loss, grad_x]
    for kind in range(4):
        outs += [pw[kind] for pw in per_weight]
    return tuple(outs)
```

```python
import functools
import math

import jax
import jax.numpy as jnp
from jax import lax
from jax.experimental import pallas as pl
from jax.experimental.pallas import tpu as pltpu

F32 = jnp.float32
BF16 = jnp.bfloat16
MXU_DTYPE = jnp.bfloat16

DEPTH = 4
GRID_W = 64
CHUNK = 128
SGU_GROUPS = 16
CONV_W = 31
CONV_HALO = 16
HEAD_DIM = 128
N_KV_HEADS = 4
ROPE_THETA = 10000.0
DEEPNORM_ALPHA = (2 * DEPTH) ** 0.25
LN_EPS = 1e-6
ADAM_LR, ADAM_B1, ADAM_B2, ADAM_EPS, ADAM_WD, ADAM_STEP = 0.001, 0.9, 0.999, 1e-08, 0.01, 10

N_DEV = 8
V7X_VMEM_BYTES = 64 * 1024 * 1024
LANES = 128

NN = ((1,), (0,))
NT = ((1,), (1,))
TN = ((0,), (0,))


def _dot(a, b, dims):
    return lax.dot_general(a.astype(MXU_DTYPE), b.astype(MXU_DTYPE), (dims, ((), ())), preferred_element_type=F32)


def _pick(n, prefs):
    for p in prefs:
        if n % p == 0:
            return p
    raise ValueError(f"no tile for {n} among {prefs}")


def _params(sem, vmem_bytes):
    limit = int(min(max(vmem_bytes, 16 * 1024 * 1024), V7X_VMEM_BYTES * 7 // 8))
    return pltpu.CompilerParams(dimension_semantics=sem, vmem_limit_bytes=limit)


def _nbytes(shape, dtype):
    return math.prod(shape) * jnp.dtype(dtype).itemsize


def _sigmoid(x):
    return jax.nn.sigmoid(x)


def _silu(x):
    return x * jax.nn.sigmoid(x)


def _dsilu(x):
    s = jax.nn.sigmoid(x)
    return s * (1.0 + x * (1.0 - s))


def _ln_stats(x):
    mu = jnp.mean(x, axis=-1, keepdims=True)
    xc = x - mu
    var = jnp.mean(xc * xc, axis=-1, keepdims=True)
    return xc, lax.rsqrt(var + LN_EPS)


def _ln(x, g, b):
    xc, rstd = _ln_stats(x)
    return xc * rstd * g + b


def _mesh_pos():
    return lax.axis_index("x"), lax.axis_index("y"), lax.axis_index("c")


def _dev_index(p):
    return 4 * p[0] + 2 * p[1] + p[2]


class _Comm:
    def __init__(self, inputs, out_shapes, sems, start, finish):
        self.inputs, self.out_shapes, self.sems, self.start, self.finish = inputs, out_shapes, sems, start, finish


def _gather_comm(xs):
    n = len(xs)

    def place():
        x, y, c = _mesh_pos()
        return (x, y, c), (x, y, 1 - c), [(1 - x, y), (x, 1 - y), (1 - x, 1 - y)], c

    def copier(x_refs, o_refs, sems):
        send_sems, recv_sems, _ = sems

        def copy(t, k, block, to, from_input=False):
            dst = o_refs[t].at[_dev_index(block)]
            return pltpu.make_async_remote_copy(
                src_ref=x_refs[t] if from_input else dst, dst_ref=dst,
                send_sem=send_sems.at[t, k], recv_sem=recv_sems.at[t, k],
                device_id=to, device_id_type=pl.DeviceIdType.MESH)

        return copy

    def own(x_refs, o_refs, sems, t, me):
        return pltpu.make_async_copy(x_refs[t], o_refs[t].at[_dev_index(me)], sems[2].at[t])

    def first_copies(copy, t, me, sibling, chips, c):
        return [copy(t, 0, me, sibling, True)] + [copy(t, 1 + j, me, (*chip, c), True) for j, chip in enumerate(chips)]

    def start(x_refs, o_refs, sems):
        me, sibling, chips, c = place()
        copy = copier(x_refs, o_refs, sems)
        for t in range(n):
            own(x_refs, o_refs, sems, t, me).start()
            for cp in first_copies(copy, t, me, sibling, chips, c):
                cp.start()

    def finish(x_refs, o_refs, sems):
        me, sibling, chips, c = place()
        copy = copier(x_refs, o_refs, sems)
        passed = []
        for t in range(n):
            for j, chip in enumerate(chips):
                copy(t, 1 + j, (*chip, c), me).wait_recv()
                cp = copy(t, 4 + j, (*chip, c), sibling)
                cp.start()
                passed.append(cp)
        for t in range(n):
            copy(t, 0, sibling, me).wait_recv()
            for j, chip in enumerate(chips):
                copy(t, 4 + j, (*chip, 1 - c), me).wait_recv()
        for t in range(n):
            for cp in first_copies(copy, t, me, sibling, chips, c):
                cp.wait_send()
        for cp in passed:
            cp.wait_send()
        for t in range(n):
            own(x_refs, o_refs, sems, t, me).wait()

    sems = [pltpu.SemaphoreType.DMA((n, 7)), pltpu.SemaphoreType.DMA((n, 7)), pltpu.SemaphoreType.DMA((n,))]
    return _Comm(list(xs), [jax.ShapeDtypeStruct((N_DEV, *a.shape), a.dtype) for a in xs], sems, start, finish)


def _exchange_comm(gs):
    n = len(gs)

    def copies(g_refs, r_refs, sems):
        send_sems, recv_sems, local_sems = sems
        x, y, c = _mesh_pos()
        me = _dev_index((x, y, c))
        out = []
        for t in range(n):
            out.append(pltpu.make_async_copy(g_refs[t].at[me], r_refs[t].at[me], local_sems.at[t]))
            for k in range(1, N_DEV):
                fx, fy, fc = (k >> 2) & 1, (k >> 1) & 1, k & 1
                peer = (1 - x if fx else x, 1 - y if fy else y, 1 - c if fc else c)
                out.append(pltpu.make_async_remote_copy(
                    src_ref=g_refs[t].at[_dev_index(peer)], dst_ref=r_refs[t].at[me],
                    send_sem=send_sems.at[t, k - 1], recv_sem=recv_sems.at[t, k - 1],
                    device_id=peer, device_id_type=pl.DeviceIdType.MESH))
        return out

    def start(g_refs, r_refs, sems):
        for cp in copies(g_refs, r_refs, sems):
            cp.start()

    def finish(g_refs, r_refs, sems):
        for cp in copies(g_refs, r_refs, sems):
            cp.wait()

    sems = [pltpu.SemaphoreType.DMA((n, 7)), pltpu.SemaphoreType.DMA((n, 7)), pltpu.SemaphoreType.DMA((n,))]
    return _Comm(list(gs), [jax.ShapeDtypeStruct(g.shape, g.dtype) for g in gs], sems, start, finish)


def _comm_call(comm, name):
    n_in, n_out = len(comm.inputs), len(comm.out_shapes)

    def body(*refs):
        ins, outs, sems = refs[:n_in], refs[n_in:n_in + n_out], refs[n_in + n_out:]
        comm.start(ins, outs, sems)
        comm.finish(ins, outs, sems)

    hbm = pl.BlockSpec(memory_space=pl.ANY)
    return pl.pallas_call(
        body, name=name, out_shape=comm.out_shapes, in_specs=[hbm] * n_in, out_specs=[hbm] * n_out,
        scratch_shapes=comm.sems)(*comm.inputs)


def _call(body, operands, *, name, grid, in_specs, out_specs, out_shape, scratch_shapes=(), compiler_params, comm=None):
    single = not isinstance(out_shape, (list, tuple))
    out_shape = [out_shape] if single else list(out_shape)
    out_specs = [out_specs] if single else list(out_specs)
    scratch_shapes = list(scratch_shapes)
    if comm is None:
        res = pl.pallas_call(
            body, name=name, grid=grid, in_specs=list(in_specs), out_specs=out_specs, out_shape=out_shape,
            scratch_shapes=scratch_shapes, compiler_params=compiler_params)(*operands)
        return res[0] if single else res
    n_in, n_out, n_scr = len(in_specs), len(out_specs), len(scratch_shapes)
    c_in, c_out = len(comm.inputs), len(comm.out_shapes)

    def with_comm(*refs):
        ins, c_ins = refs[:n_in], refs[n_in:n_in + c_in]
        o0 = n_in + c_in
        outs, c_outs = refs[o0:o0 + n_out], refs[o0 + n_out:o0 + n_out + c_out]
        s0 = o0 + n_out + c_out
        scr, sems = refs[s0:s0 + n_scr], refs[s0 + n_scr:]
        ids = [pl.program_id(a) for a in range(len(grid))]
        first = functools.reduce(jnp.logical_and, [i == 0 for i in ids])
        last = functools.reduce(jnp.logical_and, [i == g - 1 for i, g in zip(ids, grid)])

        @pl.when(first)
        def _():
            comm.start(c_ins, c_outs, sems)

        body(*ins, *outs, *scr)

        @pl.when(last)
        def _():
            comm.finish(c_ins, c_outs, sems)

    hbm = pl.BlockSpec(memory_space=pl.ANY)
    params = pltpu.CompilerParams(dimension_semantics=("arbitrary",) * len(grid),
                                  vmem_limit_bytes=compiler_params.vmem_limit_bytes)
    res = pl.pallas_call(
        with_comm, name=name, grid=grid, in_specs=list(in_specs) + [hbm] * c_in, out_specs=out_specs + [hbm] * c_out,
        out_shape=out_shape + list(comm.out_shapes), scratch_shapes=scratch_shapes + list(comm.sems),
        compiler_params=params)(*operands, *comm.inputs)
    return (res[0] if single else res[:n_out]), res[n_out:]


def _all_gather(xs, name):
    return _comm_call(_gather_comm(xs), name)


ROW_TILES = (1088, 1024, 768, 544, 512, 384, 272, 256, 128)
TOKEN_K_TILES = (2176, 2048, 1088, 1024, 768, 512, 384, 256, 128)
COL_TILES = (1024, 768, 640, 512, 384, 256, 128)
DEEP_K = 2048


def _mm_nn(a, w, *, planes=1, name, comm=None):
    m, k = a.shape
    if w.ndim == 3:
        nd_w = w.shape[2]
        n = w.shape[0] * nd_w
    else:
        nd_w = n = w.shape[1]
    npl = n // planes
    tm = _pick(m, ROW_TILES)
    tn = _pick(math.gcd(nd_w, npl), COL_TILES if k <= DEEP_K else COL_TILES[3:])
    r, rp = nd_w // tn, npl // tn
    if w.ndim == 3:
        w_spec = pl.BlockSpec((None, k, tn), lambda i, j: (j // r, 0, j % r))
    else:
        w_spec = pl.BlockSpec((k, tn), lambda i, j: (0, j))
    if planes > 1:
        o_spec = pl.BlockSpec((None, tm, tn), lambda i, j: (j // rp, i, j % rp))
        out_shape = jax.ShapeDtypeStruct((planes, m, npl), F32)
    else:
        o_spec = pl.BlockSpec((tm, tn), lambda i, j: (i, j))
        out_shape = jax.ShapeDtypeStruct((m, n), F32)

    def body(a_ref, w_ref, o_ref):
        o_ref[...] = _dot(a_ref[...], w_ref[...], NN)

    vmem = 2 * (_nbytes((tm, k), a.dtype) + _nbytes((k, tn), w.dtype) + _nbytes((tm, tn), F32)) + _nbytes((tm, tn), F32)
    return _call(
        body, (a, w), name=name, grid=(m // tm, n // tn),
        in_specs=[pl.BlockSpec((tm, k), lambda i, j: (i, 0)), w_spec], out_specs=o_spec, out_shape=out_shape,
        compiler_params=_params(("parallel", "arbitrary"), vmem), comm=comm)


def _mm_nt(a, w, *, name):
    m, k = a.shape
    n = w.shape[0]
    tm = _pick(m, ROW_TILES)
    tn = _pick(n, COL_TILES)

    def body(a_ref, w_ref, o_ref):
        o_ref[...] = _dot(a_ref[...], w_ref[...], NT)

    vmem = 2 * (_nbytes((tm, k), a.dtype) + _nbytes((tn, k), w.dtype) + _nbytes((tm, tn), F32)) + _nbytes((tm, tn), F32)
    return pl.pallas_call(
        body, name=name, grid=(m // tm, n // tn),
        in_specs=[pl.BlockSpec((tm, k), lambda i, j: (i, 0)), pl.BlockSpec((tn, k), lambda i, j: (j, 0))],
        out_specs=pl.BlockSpec((tm, tn), lambda i, j: (i, j)), out_shape=jax.ShapeDtypeStruct((m, n), F32),
        compiler_params=_params(("parallel", "arbitrary"), vmem),
    )(a, w)


def _mm_nt_blocked(a, w, *, name, comm=None):
    nd, n, kd = w.shape
    if a.ndim == 3:
        p, m, kp = a.shape
    else:
        (m, kp), p = a.shape, 1
    tk = _pick(math.gcd(kd, kp), COL_TILES)
    ra, rw = kp // tk, kd // tk
    nk = nd * rw
    tm = _pick(m, ROW_TILES)
    if a.ndim == 3:
        a_spec = pl.BlockSpec((None, tm, tk), lambda i, kk: (kk // ra, i, kk % ra))
    else:
        a_spec = pl.BlockSpec((tm, tk), lambda i, kk: (i, kk))

    def body(a_ref, w_ref, o_ref, acc_ref):
        kk = pl.program_id(1)

        @pl.when(kk == 0)
        def _():
            acc_ref[...] = jnp.zeros_like(acc_ref)

        acc_ref[...] += _dot(a_ref[...], w_ref[...], NT)

        @pl.when(kk == nk - 1)
        def _():
            o_ref[...] = acc_ref[...]

    vmem = 2 * (_nbytes((tm, tk), a.dtype) + _nbytes((n, tk), w.dtype) + _nbytes((tm, n), F32)) + 2 * _nbytes((tm, n), F32)
    return _call(
        body, (a, w), name=name, grid=(m // tm, nk),
        in_specs=[a_spec, pl.BlockSpec((None, n, tk), lambda i, kk: (kk // rw, 0, kk % rw))],
        out_specs=pl.BlockSpec((tm, n), lambda i, kk: (i, 0)), out_shape=jax.ShapeDtypeStruct((m, n), F32),
        scratch_shapes=[pltpu.VMEM((tm, n), F32)],
        compiler_params=_params(("parallel", "arbitrary"), vmem), comm=comm)


def _mm_tn(a, b, *, blocked, out_dtype, name, comm=None):
    rows, da = a.shape
    if b.ndim == 3:
        p, _, npl = b.shape
    else:
        p, npl = 1, b.shape[1]
    n = p * npl
    nd_w = n // N_DEV if blocked else n
    tk = _pick(rows, TOKEN_K_TILES)
    tm = _pick(da, COL_TILES)
    tn = _pick(math.gcd(nd_w, npl), COL_TILES)
    rb, ro = npl // tn, nd_w // tn
    nk = rows // tk
    if b.ndim == 3:
        b_spec = pl.BlockSpec((None, tk, tn), lambda i, j, kk: (j // rb, kk, j % rb))
    else:
        b_spec = pl.BlockSpec((tk, tn), lambda i, j, kk: (kk, j))
    if blocked:
        o_spec = pl.BlockSpec((None, tm, tn), lambda i, j, kk: (j // ro, i, j % ro))
        out_shape = jax.ShapeDtypeStruct((N_DEV, da, nd_w), out_dtype)
    else:
        o_spec = pl.BlockSpec((tm, tn), lambda i, j, kk: (i, j))
        out_shape = jax.ShapeDtypeStruct((da, n), out_dtype)

    def body(a_ref, b_ref, o_ref, acc_ref):
        kk = pl.program_id(2)

        @pl.when(kk == 0)
        def _():
            acc_ref[...] = jnp.zeros_like(acc_ref)

        acc_ref[...] += _dot(a_ref[...], b_ref[...], TN)

        @pl.when(kk == nk - 1)
        def _():
            o_ref[...] = acc_ref[...].astype(o_ref.dtype)

    vmem = (2 * (_nbytes((tk, tm), a.dtype) + _nbytes((tk, tn), b.dtype) + _nbytes((tm, tn), out_dtype))
            + 3 * _nbytes((tm, tn), F32) + _nbytes((tk, tm), F32))
    return _call(
        body, (a, b), name=name, grid=(da // tm, n // tn, nk),
        in_specs=[pl.BlockSpec((tk, tm), lambda i, j, kk: (kk, i)), b_spec], out_specs=o_spec, out_shape=out_shape,
        scratch_shapes=[pltpu.VMEM((tm, tn), F32)],
        compiler_params=_params(("parallel", "parallel", "arbitrary"), vmem), comm=comm)


ROW_TILE = 256


class _Rows:
    def __init__(self, n_x, n_ctx, tile=ROW_TILE):
        assert n_x % tile == 0 and n_ctx % tile == 0
        self.n_x, self.n_ctx, self.tile = n_x, n_ctx, tile
        self.rows = n_x + n_ctx
        self.nt_x = n_x // tile
        self.nt = self.rows // tile
        self.n_seg = 2 if n_ctx else 1

    def seg(self, i):
        return jnp.where(i >= self.nt_x, 1, 0) if self.n_ctx else 0

    def first_of_seg(self, i):
        return (i == 0) | (i == self.nt_x) if self.n_ctx else i == 0

    def full(self, width):
        return pl.BlockSpec((self.tile, width), lambda i: (i, 0))

    def plane(self, p, width):
        return pl.BlockSpec((None, self.tile, width), lambda i: (p, i, 0))

    def modvec(self, layer, which, width):
        return pl.BlockSpec((None, 1, width), lambda i: ((layer * 2 + self.seg(i)) * 3 + which, 0, 0))

    def seg_acc(self, width):
        return pl.BlockSpec((None, 1, width), lambda i: (self.seg(i), 0, 0))


def _vec(width):
    return pl.BlockSpec((1, width), lambda i: (0, 0))


def _acc(ref, first, val):
    @pl.when(first)
    def _():
        ref[...] = jnp.zeros_like(ref)

    ref[...] += val


def _modulate(xs, modv, layer, rt, name):
    d = xs.shape[1]

    def body(x_ref, sh_ref, sc_ref, o_ref):
        o_ref[...] = (x_ref[...] * (1.0 + sc_ref[...]) + sh_ref[...]).astype(o_ref.dtype)

    return pl.pallas_call(
        body, name=name, grid=(rt.nt,),
        in_specs=[rt.full(d), rt.modvec(layer, 0, d), rt.modvec(layer, 1, d)],
        out_specs=rt.full(d), out_shape=jax.ShapeDtypeStruct(xs.shape, MXU_DTYPE),
        compiler_params=_params(("parallel",), 6 * _nbytes((rt.tile, d), F32)),
    )(xs, modv, modv)


def _post(x, y, gate, pg, pb):
    return _ln(DEEPNORM_ALPHA * x + gate * y, pg, pb)


def _post_fwd(xs, y, modv, layer, pg, pb, rt, name):
    d = xs.shape[1]

    def body(x_ref, y_ref, gate_ref, pg_ref, pb_ref, o_ref):
        o_ref[...] = _post(x_ref[...], y_ref[...], gate_ref[...], pg_ref[...], pb_ref[...])

    return pl.pallas_call(
        body, name=name, grid=(rt.nt,),
        in_specs=[rt.full(d), rt.full(d), rt.modvec(layer, 2, d), _vec(d), _vec(d)],
        out_specs=rt.full(d), out_shape=jax.ShapeDtypeStruct((rt.rows, d), F32),
        compiler_params=_params(("parallel",), 10 * _nbytes((rt.tile, d), F32)),
    )(xs, y, modv, pg, pb)


def _post_bwd(xs, y, dout, modv, layer, pg, pb, rt, name):
    d = xs.shape[1]

    def body(x_ref, y_ref, do_ref, gate_ref, pg_ref, pb_ref, dres_ref, dy_ref, dpg_ref, dpb_ref, dgate_ref):
        i = pl.program_id(0)
        _, vjp = jax.vjp(_post, x_ref[...], y_ref[...], gate_ref[...], pg_ref[...], pb_ref[...])
        dx, dy, dgate, dpg, dpb = vjp(do_ref[...])
        dres_ref[...] = dx
        dy_ref[...] = dy.astype(dy_ref.dtype)
        _acc(dpg_ref, i == 0, dpg)
        _acc(dpb_ref, i == 0, dpb)
        _acc(dgate_ref, rt.first_of_seg(i), dgate)

    return pl.pallas_call(
        body, name=name, grid=(rt.nt,),
        in_specs=[rt.full(d), rt.full(d), rt.full(d), rt.modvec(layer, 2, d), _vec(d), _vec(d)],
        out_specs=[rt.full(d), rt.full(d), _vec(d), _vec(d), rt.seg_acc(d)],
        out_shape=[jax.ShapeDtypeStruct((rt.rows, d), F32), jax.ShapeDtypeStruct((rt.rows, d), MXU_DTYPE),
                   jax.ShapeDtypeStruct((1, d), F32), jax.ShapeDtypeStruct((1, d), F32),
                   jax.ShapeDtypeStruct((rt.n_seg, 1, d), F32)],
        compiler_params=_params(("arbitrary",), 16 * _nbytes((rt.tile, d), F32)),
    )(xs, y, dout, modv, pg, pb)


def _mod_bwd(dres, dh, xs, modv, layer, rt, name):
    d = xs.shape[1]

    def body(dres_ref, dh_ref, x_ref, sc_ref, dx_ref, dshift_ref, dscale_ref):
        i = pl.program_id(0)
        dh = dh_ref[...]
        dx_ref[...] = dres_ref[...] + dh * (1.0 + sc_ref[...])
        first = rt.first_of_seg(i)
        _acc(dshift_ref, first, jnp.sum(dh, axis=0, keepdims=True))
        _acc(dscale_ref, first, jnp.sum(dh * x_ref[...], axis=0, keepdims=True))

    return pl.pallas_call(
        body, name=name, grid=(rt.nt,),
        in_specs=[rt.full(d), rt.full(d), rt.full(d), rt.modvec(layer, 1, d)],
        out_specs=[rt.full(d), rt.seg_acc(d), rt.seg_acc(d)],
        out_shape=[jax.ShapeDtypeStruct((rt.rows, d), F32), jax.ShapeDtypeStruct((rt.n_seg, 1, d), F32),
                   jax.ShapeDtypeStruct((rt.n_seg, 1, d), F32)],
        compiler_params=_params(("arbitrary",), 10 * _nbytes((rt.tile, d), F32)),
    )(dres, dh, xs, modv)


def _cm_mid_fwd(z3, ln_g, ln_b, w_s, b_s_t, name):
    _, rows, e = z3.shape
    groups = w_s.shape[0]
    gw = e // groups

    def body(z_ref, lg_ref, lb_ref, ws_ref, bs_ref, t_ref):
        vn = _ln(z_ref[1], lg_ref[...], lb_ref[...])
        for h in range(groups):
            cols = slice(h * gw, (h + 1) * gw)
            s = _dot(ws_ref[h], vn[:, cols], NN) + bs_ref[:, h:h + 1]
            t_ref[:, cols] = (z_ref[0, :, cols] * s * _silu(z_ref[2, :, cols])).astype(t_ref.dtype)

    return pl.pallas_call(
        body, name=name, grid=(rows // CHUNK,),
        in_specs=[pl.BlockSpec((3, CHUNK, e), lambda i: (0, i, 0)), _vec(e), _vec(e),
                  pl.BlockSpec(w_s.shape, lambda i: (0, 0, 0)), pl.BlockSpec(b_s_t.shape, lambda i: (0, 0))],
        out_specs=pl.BlockSpec((CHUNK, e), lambda i: (i, 0)), out_shape=jax.ShapeDtypeStruct((rows, e), MXU_DTYPE),
        compiler_params=_params(("parallel",), 12 * _nbytes((CHUNK, e), F32)),
    )(z3, ln_g, ln_b, w_s, b_s_t)


def _cm_mid_bwd(z3, dt, ln_g, ln_b, w_s, b_s_t, name, comm=None):
    _, rows, e = z3.shape
    groups = w_s.shape[0]
    gw = e // groups

    def body(z_ref, dt_ref, lg_ref, lb_ref, ws_ref, bs_ref, dz_ref, dlg_ref, dlb_ref, dws_ref, dbs_ref, dvn_ref):
        i = pl.program_id(0)
        first = i == 0
        v = z_ref[1]
        vn, ln_vjp = jax.vjp(_ln, v, lg_ref[...], lb_ref[...])

        @pl.when(first)
        def _():
            dws_ref[...] = jnp.zeros_like(dws_ref)
            dbs_ref[...] = jnp.zeros_like(dbs_ref)

        for h in range(groups):
            cols = slice(h * gw, (h + 1) * gw)
            vn_h = vn[:, cols]
            s = _dot(ws_ref[h], vn_h, NN) + bs_ref[:, h:h + 1]
            u, g, dth = z_ref[0, :, cols], z_ref[2, :, cols], dt_ref[:, cols]
            sg = _silu(g)
            dz_ref[0, :, cols] = (dth * s * sg).astype(dz_ref.dtype)
            dz_ref[2, :, cols] = (dth * u * s * _dsilu(g)).astype(dz_ref.dtype)
            ds = dth * u * sg
            dvn_ref[:, cols] = _dot(ws_ref[h], ds, TN)
            dws_ref[h] += _dot(ds, vn_h, NT)
            dbs_ref[:, h:h + 1] += jnp.sum(ds, axis=1, keepdims=True)
        dv, dlg, dlb = ln_vjp(dvn_ref[...])
        dz_ref[1] = dv.astype(dz_ref.dtype)
        _acc(dlg_ref, first, dlg)
        _acc(dlb_ref, first, dlb)

    return _call(
        body, (z3, dt, ln_g, ln_b, w_s, b_s_t), name=name, grid=(rows // CHUNK,),
        in_specs=[pl.BlockSpec((3, CHUNK, e), lambda i: (0, i, 0)), pl.BlockSpec((CHUNK, e), lambda i: (i, 0)), _vec(e), _vec(e),
                  pl.BlockSpec(w_s.shape, lambda i: (0, 0, 0)), pl.BlockSpec(b_s_t.shape, lambda i: (0, 0))],
        out_specs=[pl.BlockSpec((3, CHUNK, e), lambda i: (0, i, 0)), _vec(e), _vec(e),
                   pl.BlockSpec(w_s.shape, lambda i: (0, 0, 0)), pl.BlockSpec(b_s_t.shape, lambda i: (0, 0))],
        out_shape=[jax.ShapeDtypeStruct((3, rows, e), MXU_DTYPE), jax.ShapeDtypeStruct((1, e), F32),
                   jax.ShapeDtypeStruct((1, e), F32), jax.ShapeDtypeStruct(w_s.shape, F32),
                   jax.ShapeDtypeStruct(b_s_t.shape, F32)],
        scratch_shapes=[pltpu.VMEM((CHUNK, e), F32)],
        compiler_params=_params(("arbitrary",), 20 * _nbytes((CHUNK, e), F32)), comm=comm)


CONV_COL_TILE = 512


def _conv_specs(rt, tc, planes):
    per = rt.tile // CONV_HALO
    last = rt.rows // CONV_HALO - 1
    if planes:
        cur = pl.BlockSpec((planes, rt.tile, tc), lambda j, i: (0, i, j))
        prev = pl.BlockSpec((planes, CONV_HALO, tc), lambda j, i: (0, jnp.maximum(i * per - 1, 0), j))
        nxt = pl.BlockSpec((planes, CONV_HALO, tc), lambda j, i: (0, jnp.minimum((i + 1) * per, last), j))
    else:
        cur = pl.BlockSpec((rt.tile, tc), lambda j, i: (i, j))
        prev = pl.BlockSpec((CONV_HALO, tc), lambda j, i: (jnp.maximum(i * per - 1, 0), j))
        nxt = pl.BlockSpec((CONV_HALO, tc), lambda j, i: (jnp.minimum((i + 1) * per, last), j))
    return cur, prev, nxt


def _halo_ok(rt, i):
    prev_ok = (i != 0) & (i != rt.nt_x)
    next_ok = (i != rt.nt_x - 1) & (i != rt.nt - 1)
    return prev_ok, next_ok


def _glu(ref):
    return ref[0] * _sigmoid(ref[1])


def _padded(cur, prev, nxt, prev_ok, next_ok):
    return jnp.concatenate([jnp.where(prev_ok, prev, 0.0), cur, jnp.where(next_ok, nxt, 0.0)], axis=0)


def _conv_fwd(z3, conv_w, conv_b, rt, name, comm=None):
    _, rows, e = z3.shape
    tc = _pick(e, (CONV_COL_TILE, 256, 128))
    tr = rt.tile

    def body(cur_ref, prev_ref, next_ref, w_ref, b_ref, o_ref, pad_ref):
        prev_ok, next_ok = _halo_ok(rt, pl.program_id(1))
        pad_ref[...] = _padded(_glu(cur_ref), _glu(prev_ref), _glu(next_ref), prev_ok, next_ok)
        acc = jnp.broadcast_to(b_ref[...], (tr, tc))
        for k in range(CONV_W):
            off = CONV_HALO - CONV_W // 2 + k
            acc = acc + w_ref[k:k + 1, :] * pad_ref[off:off + tr, :]
        o_ref[...] = acc

    cur, prev, nxt = _conv_specs(rt, tc, 2)
    return _call(
        body, (z3, z3, z3, conv_w, conv_b), name=name, grid=(e // tc, rt.nt),
        in_specs=[cur, prev, nxt, pl.BlockSpec((CONV_W, tc), lambda j, i: (0, j)), pl.BlockSpec((1, tc), lambda j, i: (0, j))],
        out_specs=pl.BlockSpec((tr, tc), lambda j, i: (i, j)), out_shape=jax.ShapeDtypeStruct((rows, e), F32),
        scratch_shapes=[pltpu.VMEM((tr + 2 * CONV_HALO, tc), F32)],
        compiler_params=_params(("parallel", "arbitrary"), 16 * _nbytes((tr, tc), F32)), comm=comm)


def _conv_bwd(z3, dy1, dg, conv_w, rt, name, comm=None):
    _, rows, e = z3.shape
    tc = _pick(e, (CONV_COL_TILE, 256, 128))
    tr = rt.tile

    def body(cur_ref, prev_ref, next_ref, dcur_ref, dprev_ref, dnext_ref, dg_ref, w_ref, dz_ref, dw_ref, db_ref, pad_ref, dpad_ref):
        i = pl.program_id(1)
        prev_ok, next_ok = _halo_ok(rt, i)
        pad_ref[...] = _padded(_glu(cur_ref), _glu(prev_ref), _glu(next_ref), prev_ok, next_ok)
        dcur = dcur_ref[...]
        dpad_ref[...] = _padded(dcur, dprev_ref[...], dnext_ref[...], prev_ok, next_ok)

        @pl.when(i == 0)
        def _():
            dw_ref[...] = jnp.zeros_like(dw_ref)
            db_ref[...] = jnp.zeros_like(db_ref)

        dy0 = jnp.zeros((tr, tc), F32)
        for k in range(CONV_W):
            off = CONV_HALO - CONV_W // 2 + k
            roff = CONV_HALO + CONV_W // 2 - k
            dy0 = dy0 + w_ref[k:k + 1, :] * dpad_ref[roff:roff + tr, :]
            dw_ref[k:k + 1, :] += jnp.sum(dcur * pad_ref[off:off + tr, :], axis=0, keepdims=True)
        db_ref[...] += jnp.sum(dcur, axis=0, keepdims=True)
        a, sb = cur_ref[0], _sigmoid(cur_ref[1])
        dz_ref[0] = (dy0 * sb).astype(dz_ref.dtype)
        dz_ref[1] = (dy0 * a * sb * (1.0 - sb)).astype(dz_ref.dtype)
        dz_ref[2] = dg_ref[...]

    cur, prev, nxt = _conv_specs(rt, tc, 2)
    dcur, dprev, dnxt = _conv_specs(rt, tc, 0)
    return _call(
        body, (z3, z3, z3, dy1, dy1, dy1, dg, conv_w), name=name, grid=(e // tc, rt.nt),
        in_specs=[cur, prev, nxt, dcur, dprev, dnxt, pl.BlockSpec((tr, tc), lambda j, i: (i, j)),
                  pl.BlockSpec((CONV_W, tc), lambda j, i: (0, j))],
        out_specs=[pl.BlockSpec((3, tr, tc), lambda j, i: (0, i, j)), pl.BlockSpec((CONV_W, tc), lambda j, i: (0, j)),
                   pl.BlockSpec((1, tc), lambda j, i: (0, j))],
        out_shape=[jax.ShapeDtypeStruct((3, rows, e), MXU_DTYPE), jax.ShapeDtypeStruct((CONV_W, e), F32),
                   jax.ShapeDtypeStruct((1, e), F32)],
        scratch_shapes=[pltpu.VMEM((tr + 2 * CONV_HALO, tc), F32), pltpu.VMEM((tr + 2 * CONV_HALO, tc), F32)],
        compiler_params=_params(("parallel", "arbitrary"), 24 * _nbytes((tr, tc), F32)), comm=comm)


def _conv_mid(y1, g, ln_g, ln_b):
    return _silu(_ln(y1, ln_g, ln_b)) * _silu(g)


def _conv_mid_fwd(y1, z3, ln_g, ln_b, rt, name):
    e = y1.shape[1]
    tr = CHUNK

    def body(y_ref, g_ref, lg_ref, lb_ref, t_ref):
        t_ref[...] = _conv_mid(y_ref[...], g_ref[...], lg_ref[...], lb_ref[...]).astype(t_ref.dtype)

    return pl.pallas_call(
        body, name=name, grid=(rt.rows // tr,),
        in_specs=[pl.BlockSpec((tr, e), lambda i: (i, 0)), pl.BlockSpec((None, tr, e), lambda i: (2, i, 0)), _vec(e), _vec(e)],
        out_specs=pl.BlockSpec((tr, e), lambda i: (i, 0)), out_shape=jax.ShapeDtypeStruct((rt.rows, e), MXU_DTYPE),
        compiler_params=_params(("parallel",), 12 * _nbytes((tr, e), F32)),
    )(y1, z3, ln_g, ln_b)


def _conv_mid_bwd(y1, z3, dt, ln_g, ln_b, rt, name):
    e = y1.shape[1]
    tr = CHUNK

    def body(y_ref, g_ref, dt_ref, lg_ref, lb_ref, dy_ref, dg_ref, dlg_ref, dlb_ref):
        first = pl.program_id(0) == 0
        _, vjp = jax.vjp(_conv_mid, y_ref[...], g_ref[...], lg_ref[...], lb_ref[...])
        dy, dg, dlg, dlb = vjp(dt_ref[...])
        dy_ref[...] = dy
        dg_ref[...] = dg.astype(dg_ref.dtype)
        _acc(dlg_ref, first, dlg)
        _acc(dlb_ref, first, dlb)

    row = pl.BlockSpec((tr, e), lambda i: (i, 0))
    return pl.pallas_call(
        body, name=name, grid=(rt.rows // tr,),
        in_specs=[row, pl.BlockSpec((None, tr, e), lambda i: (2, i, 0)), row, _vec(e), _vec(e)],
        out_specs=[row, row, _vec(e), _vec(e)],
        out_shape=[jax.ShapeDtypeStruct((rt.rows, e), F32), jax.ShapeDtypeStruct((rt.rows, e), MXU_DTYPE),
                   jax.ShapeDtypeStruct((1, e), F32), jax.ShapeDtypeStruct((1, e), F32)],
        compiler_params=_params(("arbitrary",), 20 * _nbytes((tr, e), F32)),
    )(y1, z3, dt, ln_g, ln_b)


def _rms(x, g):
    return x * lax.rsqrt(jnp.mean(x * x, axis=-1, keepdims=True) + LN_EPS) * g


def _pair_swap(x):
    lane = lax.broadcasted_iota(jnp.int32, x.shape, x.ndim - 1)
    return jnp.where(lane % 2 == 0, pltpu.roll(x, x.shape[-1] - 1, x.ndim - 1), pltpu.roll(x, 1, x.ndim - 1))


def _rope(x, cos, sin):
    return x * cos + _pair_swap(x) * sin


def _rope_t(dy, cos, sin):
    return dy * cos + _pair_swap(dy * sin)


def _rope_tables(n_x, n_ctx):
    t = jnp.arange(n_x)
    row = (t // GRID_W).astype(F32)
    col = (t % GRID_W).astype(F32)
    axis_dim = HEAD_DIM // 2
    inv = 1.0 / (ROPE_THETA ** (jnp.arange(0, axis_dim, 2, dtype=F32) / axis_dim))
    ang = jnp.concatenate([row[:, None] * inv, col[:, None] * inv], axis=-1)
    cos, sin = jnp.cos(ang), jnp.sin(ang)
    cos2 = jnp.repeat(cos, 2, axis=-1)
    sin2 = jnp.stack([-sin, sin], axis=-1).reshape(n_x, HEAD_DIM)
    cos2 = jnp.concatenate([cos2, jnp.ones((n_ctx, HEAD_DIM), F32)], axis=0)
    sin2 = jnp.concatenate([sin2, jnp.zeros((n_ctx, HEAD_DIM), F32)], axis=0)
    return cos2, sin2


def _qkv_prep(z4, q_g, k_g, cos, sin, d, kvw, rt, name):
    hd = HEAD_DIM
    kb = d // kvw

    def body(q_ref, k_ref, v_ref, qg_ref, kg_ref, cos_ref, sin_ref, qo_ref, ko_ref, vo_ref):
        cos, sin = cos_ref[...], sin_ref[...]
        for h in range(d // hd):
            cols = slice(h * hd, (h + 1) * hd)
            qo_ref[:, cols] = _rope(_rms(q_ref[:, cols], qg_ref[...]), cos, sin).astype(qo_ref.dtype)
        for h in range(kvw // hd):
            cols = slice(h * hd, (h + 1) * hd)
            ko_ref[:, cols] = _rope(_rms(k_ref[:, cols], kg_ref[...]), cos, sin).astype(ko_ref.dtype)
        vo_ref[...] = v_ref[...].astype(vo_ref.dtype)

    tr = rt.tile
    return pl.pallas_call(
        body, name=name, grid=(rt.nt,),
        in_specs=[pl.BlockSpec((tr, d), lambda i: (i, 0)), pl.BlockSpec((tr, kvw), lambda i: (i, kb)),
                  pl.BlockSpec((tr, kvw), lambda i: (i, kb + 1)), _vec(hd), _vec(hd), rt.full(hd), rt.full(hd)],
        out_specs=[rt.full(d), rt.full(kvw), rt.full(kvw)],
        out_shape=[jax.ShapeDtypeStruct((rt.rows, d), MXU_DTYPE), jax.ShapeDtypeStruct((rt.rows, kvw), MXU_DTYPE),
                   jax.ShapeDtypeStruct((rt.rows, kvw), MXU_DTYPE)],
        compiler_params=_params(("parallel",), 8 * _nbytes((tr, d), F32)),
    )(z4, z4, z4, q_g, k_g, cos, sin)


ATTN_Q_TILE = 256


def _attn_fwd(qh, kh, vh, n_x, name, comm=None):
    rows, d = qh.shape
    kvw = kh.shape[1]
    hd = HEAD_DIM
    n_kv = kvw // hd
    gqw = d // n_kv
    grp = gqw // hd
    tq = _pick(n_x, (ATTN_Q_TILE, 128))
    scale = hd ** -0.5

    def body(q_ref, k_ref, v_ref, o_ref, lse_ref):
        k, v = k_ref[...], v_ref[...]
        for g in range(grp):
            cols = slice(g * hd, (g + 1) * hd)
            s = _dot(q_ref[:, cols], k, NT) * scale
            m = jnp.max(s, axis=-1, keepdims=True)
            p = jnp.exp(s - m)
            l = jnp.sum(p, axis=-1, keepdims=True)
            o_ref[:, cols] = _dot(p / l, v, NN)
            lse_ref[:, g:g + 1] = m + jnp.log(l)

    vmem = 4 * _nbytes((rows, hd), MXU_DTYPE) + 4 * _nbytes((tq, rows), F32) + 6 * _nbytes((tq, gqw), F32)
    return _call(
        body, (qh, kh, vh), name=name, grid=(n_kv, n_x // tq),
        in_specs=[pl.BlockSpec((tq, gqw), lambda h, i: (i, h)), pl.BlockSpec((rows, hd), lambda h, i: (0, h)),
                  pl.BlockSpec((rows, hd), lambda h, i: (0, h))],
        out_specs=[pl.BlockSpec((tq, gqw), lambda h, i: (i, h)), pl.BlockSpec((None, tq, grp), lambda h, i: (h, i, 0))],
        out_shape=[jax.ShapeDtypeStruct((n_x, d), F32), jax.ShapeDtypeStruct((n_kv, n_x, grp), F32)],
        compiler_params=_params(("parallel", "arbitrary"), vmem), comm=comm)


def _attn_bwd(qh, kh, vh, do, lse, n_x, name, comm=None):
    rows, d = qh.shape
    kvw = kh.shape[1]
    hd = HEAD_DIM
    n_kv = kvw // hd
    gqw = d // n_kv
    grp = gqw // hd
    tq = _pick(n_x, (ATTN_Q_TILE, 128))
    scale = hd ** -0.5

    def body(q_ref, k_ref, v_ref, do_ref, lse_ref, dq_ref, dk_ref, dv_ref):
        @pl.when(pl.program_id(1) == 0)
        def _():
            dk_ref[...] = jnp.zeros_like(dk_ref)
            dv_ref[...] = jnp.zeros_like(dv_ref)

        k, v = k_ref[...], v_ref[...]
        for g in range(grp):
            cols = slice(g * hd, (g + 1) * hd)
            q, dog = q_ref[:, cols], do_ref[:, cols]
            p = jnp.exp(_dot(q, k, NT) * scale - lse_ref[:, g:g + 1])
            dp = _dot(dog, v, NT)
            ds = (p * (dp - jnp.sum(dp * p, axis=-1, keepdims=True)) * scale).astype(MXU_DTYPE)
            dq_ref[:, cols] = _dot(ds, k, NN)
            dk_ref[...] += _dot(ds, q, TN)
            dv_ref[...] += _dot(p, dog, TN)

    vmem = 4 * _nbytes((rows, hd), MXU_DTYPE) + 4 * _nbytes((rows, hd), F32) + 6 * _nbytes((tq, rows), F32) + 8 * _nbytes((tq, gqw), F32)
    qspec = pl.BlockSpec((tq, gqw), lambda h, i: (i, h))
    kspec = pl.BlockSpec((rows, hd), lambda h, i: (0, h))
    return _call(
        body, (qh, kh, vh, do, lse), name=name, grid=(n_kv, n_x // tq),
        in_specs=[qspec, kspec, kspec, qspec, pl.BlockSpec((None, tq, grp), lambda h, i: (h, i, 0))],
        out_specs=[qspec, kspec, kspec],
        out_shape=[jax.ShapeDtypeStruct((n_x, d), F32), jax.ShapeDtypeStruct((rows, kvw), F32),
                   jax.ShapeDtypeStruct((rows, kvw), F32)],
        compiler_params=_params(("parallel", "arbitrary"), vmem), comm=comm)


def _attn_gate(o, z4, d, kvw, rt, name):
    g0 = (d + 2 * kvw) // kvw
    tr = rt.tile

    def body(o_ref, g_ref, t_ref):
        t_ref[...] = (o_ref[...] * _silu(g_ref[...])).astype(t_ref.dtype)

    tile = pl.BlockSpec((tr, kvw), lambda i, j: (i, j))
    return pl.pallas_call(
        body, name=name, grid=(rt.nt, d // kvw),
        in_specs=[tile, pl.BlockSpec((tr, kvw), lambda i, j: (i, g0 + j))],
        out_specs=tile, out_shape=jax.ShapeDtypeStruct((rt.rows, d), MXU_DTYPE),
        compiler_params=_params(("parallel", "parallel"), 8 * _nbytes((tr, kvw), F32)),
    )(o, z4)


def _attn_gate_bwd(dt, o, z4, d, kvw, rt, name):
    g0 = (d + 2 * kvw) // kvw
    tr = rt.tile

    def body(dt_ref, o_ref, g_ref, do_ref, dg_ref):
        dt_v, g = dt_ref[...], g_ref[...]
        do_ref[...] = (dt_v * _silu(g)).astype(do_ref.dtype)
        dg_ref[...] = (dt_v * o_ref[...] * _dsilu(g)).astype(dg_ref.dtype)

    tile = pl.BlockSpec((tr, kvw), lambda i, j: (i, j))
    return pl.pallas_call(
        body, name=name, grid=(rt.nt, d // kvw),
        in_specs=[tile, tile, pl.BlockSpec((tr, kvw), lambda i, j: (i, g0 + j))],
        out_specs=[tile, tile],
        out_shape=[jax.ShapeDtypeStruct((rt.rows, d), MXU_DTYPE), jax.ShapeDtypeStruct((rt.rows, d), MXU_DTYPE)],
        compiler_params=_params(("parallel", "parallel"), 12 * _nbytes((tr, kvw), F32)),
    )(dt, o, z4)


def _prep_bwd(dxh, z4, col_block, gain, cos, sin, rt, name):
    w = dxh.shape[1]
    hd = HEAD_DIM

    def body(dxh_ref, x_ref, g_ref, cos_ref, sin_ref, dx_ref, dg_ref):
        cos, sin = cos_ref[...], sin_ref[...]
        dg = jnp.zeros((1, hd), F32)
        for h in range(w // hd):
            cols = slice(h * hd, (h + 1) * hd)
            _, vjp = jax.vjp(_rms, x_ref[:, cols], g_ref[...])
            dx, dgh = vjp(_rope_t(dxh_ref[:, cols], cos, sin))
            dx_ref[:, cols] = dx.astype(dx_ref.dtype)
            dg = dg + dgh
        _acc(dg_ref, pl.program_id(0) == 0, dg)

    tr = rt.tile
    return pl.pallas_call(
        body, name=name, grid=(rt.nt,),
        in_specs=[rt.full(w), pl.BlockSpec((tr, w), lambda i: (i, col_block)), _vec(hd), rt.full(hd), rt.full(hd)],
        out_specs=[rt.full(w), _vec(hd)],
        out_shape=[jax.ShapeDtypeStruct((rt.rows, w), MXU_DTYPE), jax.ShapeDtypeStruct((1, hd), F32)],
        compiler_params=_params(("arbitrary",), 12 * _nbytes((tr, w), F32)),
    )(dxh, z4, gain, cos, sin)


def _loss_head(x, target, rt, name):
    d = x.shape[1]

    def body(x_ref, t_ref, dx_ref, l_ref):
        err = x_ref[...] - t_ref[...]
        dx_ref[...] = err / d
        row = jnp.mean(err * err, axis=-1, keepdims=True)
        _acc(l_ref, pl.program_id(0) == 0, jnp.sum(row, axis=0, keepdims=True))

    return pl.pallas_call(
        body, name=name, grid=(rt.nt,),
        in_specs=[rt.full(d), rt.full(d)],
        out_specs=[rt.full(d), pl.BlockSpec((1, 1), lambda i: (0, 0))],
        out_shape=[jax.ShapeDtypeStruct(x.shape, F32), jax.ShapeDtypeStruct((1, 1), F32)],
        compiler_params=_params(("arbitrary",), 8 * _nbytes((rt.tile, d), F32)),
    )(x, target)


def _adamw(w, g, m, v):
    m = ADAM_B1 * m + (1.0 - ADAM_B1) * g
    v = ADAM_B2 * v + (1.0 - ADAM_B2) * (g * g)
    m_hat = m / (1.0 - ADAM_B1 ** ADAM_STEP)
    v_hat = v / (1.0 - ADAM_B2 ** ADAM_STEP)
    delta = -ADAM_LR * (m_hat / (jnp.sqrt(v_hat) + ADAM_EPS) + ADAM_WD * w)
    return delta, m, v


ADAM_TILE_BYTES = 1 << 20


def _adam_tile(rows, cols):
    tr = rows
    while tr % 16 == 0 and tr * cols * 4 > ADAM_TILE_BYTES:
        tr //= 2
    return tr


def _adam_reduce(parts, w, m, v, name):
    rows, cols = w.shape
    tr = _adam_tile(rows, cols)

    def body(p_ref, w_ref, m_ref, v_ref, g_ref, d_ref, mo_ref, vo_ref):
        g = p_ref[0].astype(F32)
        for k in range(1, N_DEV):
            g = g + p_ref[k].astype(F32)
        g_ref[...] = g
        d_ref[...], mo_ref[...], vo_ref[...] = _adamw(w_ref[...], g, m_ref[...], v_ref[...])

    row = pl.BlockSpec((tr, cols), lambda i: (i, 0))
    sds = jax.ShapeDtypeStruct((rows, cols), F32)
    return pl.pallas_call(
        body, name=name, grid=(rows // tr,),
        in_specs=[pl.BlockSpec((N_DEV, tr, cols), lambda i: (0, i, 0)), row, row, row],
        out_specs=[row] * 4, out_shape=[sds] * 4,
        compiler_params=_params(("parallel",), 40 * _nbytes((tr, cols), F32)),
    )(parts, w, m, v)


def _adam_plain(g, w, m, v, name):
    rows, cols = w.shape
    tr = _adam_tile(rows, cols)

    def body(g_ref, w_ref, m_ref, v_ref, d_ref, mo_ref, vo_ref):
        d_ref[...], mo_ref[...], vo_ref[...] = _adamw(w_ref[...], g_ref[...], m_ref[...], v_ref[...])

    row = pl.BlockSpec((tr, cols), lambda i: (i, 0))
    sds = jax.ShapeDtypeStruct((rows, cols), F32)
    return pl.pallas_call(
        body, name=name, grid=(rows // tr,),
        in_specs=[row] * 4, out_specs=[row] * 3, out_shape=[sds] * 3,
        compiler_params=_params(("parallel",), 32 * _nbytes((tr, cols), F32)),
    )(g, w, m, v)


def _sum_devices(parts, name):
    _, rows, cols = parts.shape
    tr = _adam_tile(rows, cols)

    def body(p_ref, o_ref):
        g = p_ref[0]
        for k in range(1, N_DEV):
            g = g + p_ref[k]
        o_ref[...] = g

    return pl.pallas_call(
        body, name=name, grid=(rows // tr,),
        in_specs=[pl.BlockSpec((N_DEV, tr, cols), lambda i: (0, i, 0))],
        out_specs=pl.BlockSpec((tr, cols), lambda i: (i, 0)), out_shape=jax.ShapeDtypeStruct((rows, cols), F32),
        compiler_params=_params(("parallel",), 24 * _nbytes((tr, cols), F32)),
    )(parts)


COND_ROWS = 16


def _mod_fwd_mm(cond, mod_w, mod_b, name):
    layers, d, w = mod_w.shape

    def body(c_ref, w_ref, b_ref, o_ref):
        o_ref[...] = _dot(_silu(c_ref[...]), w_ref[...], NN) + b_ref[...]

    return pl.pallas_call(
        body, name=name, grid=(layers,),
        in_specs=[pl.BlockSpec((COND_ROWS, d), lambda l: (0, 0)), pl.BlockSpec((None, d, w), lambda l: (l, 0, 0)),
                  pl.BlockSpec((None, 1, w), lambda l: (l, 0, 0))],
        out_specs=pl.BlockSpec((None, COND_ROWS, w), lambda l: (l, 0, 0)),
        out_shape=jax.ShapeDtypeStruct((layers, COND_ROWS, w), F32),
        compiler_params=_params(("parallel",), 4 * _nbytes((d, w), F32)),
    )(cond, mod_w, mod_b)


def _mod_bwd_mm(cond, dm, mod_w, name):
    layers, d, w = mod_w.shape

    def body(c_ref, dm_ref, w_ref, dw_ref, dc_ref):
        dmv = dm_ref[...]
        dw_ref[...] = _dot(_silu(c_ref[...]), dmv, TN)
        _acc(dc_ref, pl.program_id(0) == 0, _dot(dmv, w_ref[...], NT))

    return pl.pallas_call(
        body, name=name, grid=(layers,),
        in_specs=[pl.BlockSpec((COND_ROWS, d), lambda l: (0, 0)), pl.BlockSpec((None, COND_ROWS, w), lambda l: (l, 0, 0)),
                  pl.BlockSpec((None, d, w), lambda l: (l, 0, 0))],
        out_specs=[pl.BlockSpec((None, d, w), lambda l: (l, 0, 0)), pl.BlockSpec((COND_ROWS, d), lambda l: (0, 0))],
        out_shape=[jax.ShapeDtypeStruct((layers, d, w), F32), jax.ShapeDtypeStruct((COND_ROWS, d), F32)],
        compiler_params=_params(("arbitrary",), 6 * _nbytes((d, w), F32)),
    )(cond, dm, mod_w)


PACK_ROWS = 256


def _pack(arrs):
    flat = jnp.concatenate([a.reshape(-1).astype(F32) for a in arrs])
    pad = (-flat.shape[0]) % (PACK_ROWS * LANES)
    return jnp.pad(flat, (0, pad)).reshape(-1, LANES)


def _unpack(flat2d, shapes):
    flat = flat2d.reshape(-1)
    out, off = [], 0
    for s in shapes:
        n = math.prod(s)
        out.append(flat[off:off + n].reshape(s))
        off += n
    return out


def _unpack_dev(g2d, shapes):
    flat = g2d.reshape(N_DEV, -1)
    out, off = [], 0
    for s in shapes:
        n = math.prod(s)
        out.append(flat[:, off:off + n].reshape((N_DEV, *s)))
        off += n
    return out


def kernel(x, c, ctx, c_ctx, mod_w, mod_b, post_g, post_b, a_w_in, a_ln_g, a_ln_b, a_w_s, a_b_s, a_w_out, b_w_in, b_conv_w, b_conv_b, b_ln_g, b_ln_b, b_w_out, c_w_in, c_q_g, c_k_g, c_w_out, loss_target, m_c_ctx, m_mod_w, m_mod_b, m_post_g, m_post_b, m_a_w_in, m_a_ln_g, m_a_ln_b, m_a_w_s, m_a_b_s, m_a_w_out, m_b_w_in, m_b_conv_w, m_b_conv_b, m_b_ln_g, m_b_ln_b, m_b_w_out, m_c_w_in, m_c_q_g, m_c_k_g, m_c_w_out, v_c_ctx, v_mod_w, v_mod_b, v_post_g, v_post_b, v_a_w_in, v_a_ln_g, v_a_ln_b, v_a_w_s, v_a_b_s, v_a_w_out, v_b_w_in, v_b_conv_w, v_b_conv_b, v_b_ln_g, v_b_ln_b, v_b_w_out, v_c_w_in, v_c_q_g, v_c_k_g, v_c_w_out):
    n_x, d = x.shape[1], x.shape[2]
    n_ctx = ctx.shape[1]
    e = a_w_out.shape[1] * N_DEV
    kvw = N_KV_HEADS * HEAD_DIM
    me = _dev_index(_mesh_pos())
    rt_all = _Rows(n_x, n_ctx)
    rt_x = _Rows(n_x, 0)

    small_in = [c[0], a_ln_g, a_ln_b, b_conv_w[0]]
    (g_small,) = _all_gather([_pack(small_in)], "ag_small_params")
    conds, ln_g_all, ln_b_all, conv_w_all = _unpack_dev(g_small, [a.shape for a in small_in])
    a_ln_g_f = jnp.moveaxis(ln_g_all, 0, 1).reshape(a_ln_g.shape[0], 1, e)
    a_ln_b_f = jnp.moveaxis(ln_b_all, 0, 1).reshape(a_ln_b.shape[0], 1, e)
    conv_w_f = jnp.moveaxis(conv_w_all, 0, 1).reshape(CONV_W, e)
    cond = jnp.zeros((COND_ROWS, d), F32).at[:N_DEV].set(conds).at[N_DEV].set(c_ctx)

    wm = mod_w.shape[2]
    mod_b_mine = lax.dynamic_slice_in_dim(mod_b, me * wm, wm, axis=1).reshape(DEPTH, 1, wm)
    (mods_g,) = _all_gather([_mod_fwd_mm(cond, mod_w, mod_b_mine, "mod_fwd")], "ag_mod")
    mods = jnp.moveaxis(mods_g, 0, 2).reshape(DEPTH, COND_ROWS, 3 * d)
    mine = lax.dynamic_index_in_dim(mods, me, axis=1, keepdims=False)
    modv = jnp.stack([mine, mods[:, N_DEV]], axis=1).reshape(DEPTH * 2 * 3, 1, d)

    def gather_of(wt):
        return _gather_comm([wt.astype(MXU_DTYPE)])

    def exchange_of(*gs):
        return _exchange_comm([g if g.ndim == 3 else g.reshape(N_DEV, -1, g.shape[-1]) for g in gs])

    wa_in0, wa_out0 = _all_gather([a_w_in[0].astype(MXU_DTYPE), a_w_out[0].astype(MXU_DTYPE)], "ag_w_l0")
    wa_out0 = wa_out0.reshape(-1, d)

    ws_op = a_w_s.astype(MXU_DTYPE)
    bs_t = jnp.swapaxes(a_b_s, 1, 2)
    pg = post_g.reshape(DEPTH, 1, d)
    pb = post_b.reshape(DEPTH, 1, d)

    xs0 = jnp.concatenate([x[0], ctx[0]], axis=0)
    h0 = _modulate(xs0, modv, 0, rt_all, "mod0")
    z0, (wb_in,) = _mm_nn(h0, wa_in0, planes=3, name="l0_in", comm=gather_of(b_w_in[0]))
    t0 = _cm_mid_fwd(z0, a_ln_g_f[0], a_ln_b_f[0], ws_op[0], bs_t[0], "l0_mid")
    y0, (wb_out,) = _mm_nn(t0, wa_out0, name="l0_out", comm=gather_of(b_w_out[0]))
    xs1 = _post_fwd(xs0, y0, modv, 0, pg[0], pb[0], rt_all, "l0_post")
    h1 = _modulate(xs1, modv, 1, rt_all, "mod1")
    z1, (wc_in,) = _mm_nn(h1, wb_in, planes=3, name="l1_in", comm=gather_of(c_w_in[0]))
    cy1, (wa_in1,) = _conv_fwd(z1, conv_w_f, b_conv_b, rt_all, "l1_conv", comm=gather_of(a_w_in[1]))
    t1 = _conv_mid_fwd(cy1, z1, b_ln_g, b_ln_b, rt_all, "l1_mid")
    y1, (wc_out,) = _mm_nn(t1, wb_out.reshape(-1, d), name="l1_out", comm=gather_of(c_w_out[0]))
    xs2 = _post_fwd(xs1, y1, modv, 1, pg[1], pb[1], rt_all, "l1_post")
    cos, sin = _rope_tables(n_x, n_ctx)
    h2 = _modulate(xs2, modv, 2, rt_all, "mod2")
    z2 = _mm_nn(h2, wc_in, name="l2_in")
    qh, kh, vh = _qkv_prep(z2, c_q_g, c_k_g, cos, sin, d, kvw, rt_all, "l2_prep")
    (o2, lse), (wa_out1,) = _attn_fwd(qh, kh, vh, n_x, "l2_attn", comm=gather_of(a_w_out[1]))
    wb_out, wc_out, wa_out1 = [wt.reshape(-1, d) for wt in (wb_out, wc_out, wa_out1)]
    t2 = _attn_gate(o2, z2, d, kvw, rt_x, "l2_gate")
    y2 = _mm_nn(t2, wc_out, name="l2_out")
    x2 = xs2[:n_x]
    x3 = _post_fwd(x2, y2, modv, 2, pg[2], pb[2], rt_x, "l2_post")
    h3 = _modulate(x3, modv, 3, rt_x, "mod3")
    z3 = _mm_nn(h3, wa_in1, planes=3, name="l3_in")
    t3 = _cm_mid_fwd(z3, a_ln_g_f[1], a_ln_b_f[1], ws_op[1], bs_t[1], "l3_mid")
    y3 = _mm_nn(t3, wa_out1, name="l3_out")
    x4 = _post_fwd(x3, y3, modv, 3, pg[3], pb[3], rt_x, "l3_post")

    dx4, loss_sum = _loss_head(x4, loss_target[0], rt_x, "loss")
    loss = lax.psum(0.5 * loss_sum[0, 0], ("x", "y", "c"))

    gdt = MXU_DTYPE
    zeros_ctx = jnp.zeros((n_ctx, d), F32)
    dres3, dy3, dpg3, dpb3, dgate3 = _post_bwd(x3, y3, dx4, modv, 3, pg[3], pb[3], rt_x, "l3_post_b")
    dt3 = _mm_nt(dy3, wa_out1, name="l3_dt")
    gw_a_out1 = _mm_tn(t3, dy3, blocked=False, out_dtype=gdt, name="l3_dwout")
    dz3, dlg3, dlb3, dws3, dbs3 = _cm_mid_bwd(z3, dt3, a_ln_g_f[1], a_ln_b_f[1], ws_op[1], bs_t[1], "l3_mid_b")
    gw_a_in1 = _mm_tn(h3, dz3, blocked=True, out_dtype=gdt, name="l3_dwin")
    dh3, (r_a_out1,) = _mm_nt_blocked(dz3, wa_in1, name="l3_dh", comm=exchange_of(gw_a_out1))
    dx3, dshift3, dscale3 = _mod_bwd(dres3, dh3, x3, modv, 3, rt_x, "l3_mod_b")
    dres2, dy2, dpg2, dpb2, dgate2 = _post_bwd(x2, y2, dx3, modv, 2, pg[2], pb[2], rt_x, "l2_post_b")
    dt2 = _mm_nt(dy2, wc_out, name="l2_dt")
    gw_c_out = _mm_tn(t2, dy2, blocked=False, out_dtype=gdt, name="l2_dwout")
    do2, dg2 = _attn_gate_bwd(dt2, o2, z2, d, kvw, rt_x, "l2_gate_b")
    (dqh, dkh, dvh), (r_a_in1,) = _attn_bwd(qh, kh, vh, do2, lse, n_x, "l2_attn_b", comm=exchange_of(gw_a_in1))
    dq2, dqg = _prep_bwd(dqh, z2, 0, c_q_g, cos, sin, rt_x, "l2_qprep_b")
    dk2, dkg = _prep_bwd(dkh, z2, d // kvw, c_k_g, cos, sin, rt_all, "l2_kprep_b")
    zpad = jnp.zeros((n_ctx, d), MXU_DTYPE)
    dz2 = jnp.concatenate([jnp.concatenate([dq2, zpad], axis=0), dk2, dvh.astype(MXU_DTYPE),
                           jnp.concatenate([dg2, zpad], axis=0)], axis=1)
    gw_c_in = _mm_tn(h2, dz2, blocked=True, out_dtype=gdt, name="l2_dwin")
    dh2, (r_c_out,) = _mm_nt_blocked(dz2, wc_in, name="l2_dh", comm=exchange_of(gw_c_out))
    dres2 = jnp.concatenate([dres2, zeros_ctx], axis=0)
    dxs2, dshift2, dscale2 = _mod_bwd(dres2, dh2, xs2, modv, 2, rt_all, "l2_mod_b")
    dres1, dy1, dpg1, dpb1, dgate1 = _post_bwd(xs1, y1, dxs2, modv, 1, pg[1], pb[1], rt_all, "l1_post_b")
    dt1 = _mm_nt(dy1, wb_out, name="l1_dt")
    gw_b_out = _mm_tn(t1, dy1, blocked=False, out_dtype=gdt, name="l1_dwout")
    dcy1, dgc1, dblg, dblb = _conv_mid_bwd(cy1, z1, dt1, b_ln_g, b_ln_b, rt_all, "l1_mid_b")
    (dz1, dconv_w, dconv_b), (r_c_in, r_b_out) = _conv_bwd(z1, dcy1, dgc1, conv_w_f, rt_all, "l1_conv_b",
                                                           comm=exchange_of(gw_c_in, gw_b_out))
    gw_b_in = _mm_tn(h1, dz1, blocked=True, out_dtype=gdt, name="l1_dwin")
    dh1, (r_b_in,) = _mm_nt_blocked(dz1, wb_in, name="l1_dh", comm=exchange_of(gw_b_in))
    dxs1, dshift1, dscale1 = _mod_bwd(dres1, dh1, xs1, modv, 1, rt_all, "l1_mod_b")
    dres0, dy0, dpg0, dpb0, dgate0 = _post_bwd(xs0, y0, dxs1, modv, 0, pg[0], pb[0], rt_all, "l0_post_b")
    dt0 = _mm_nt(dy0, wa_out0, name="l0_dt")
    gw_a_out0 = _mm_tn(t0, dy0, blocked=False, out_dtype=gdt, name="l0_dwout")
    (dz0, dlg0, dlb0, dws0, dbs0), (r_a_out0,) = _cm_mid_bwd(z0, dt0, a_ln_g_f[0], a_ln_b_f[0], ws_op[0], bs_t[0], "l0_mid_b",
                                                             comm=exchange_of(gw_a_out0))
    gw_a_in0 = _mm_tn(h0, dz0, blocked=True, out_dtype=gdt, name="l0_dwin")
    dh0, (r_a_in0,) = _mm_nt_blocked(dz0, wa_in0, name="l0_dh", comm=exchange_of(gw_a_in0))
    dxs0, dshift0, dscale0 = _mod_bwd(dres0, dh0, xs0, modv, 0, rt_all, "l0_mod_b")
    grad_x = dxs0[:n_x][None]

    def seg2(a):
        a = a[:, 0]
        return a if a.shape[0] == 2 else jnp.concatenate([a, jnp.zeros_like(a)], axis=0)

    gate2 = jnp.concatenate([dgate2[:, 0], jnp.zeros((1, d), F32)], axis=0)
    dmod = jnp.stack([
        jnp.concatenate([seg2(dshift0), seg2(dscale0), seg2(dgate0)], axis=1),
        jnp.concatenate([seg2(dshift1), seg2(dscale1), seg2(dgate1)], axis=1),
        jnp.concatenate([seg2(dshift2), seg2(dscale2), gate2], axis=1),
        jnp.concatenate([seg2(dshift3), seg2(dscale3), seg2(dgate3)], axis=1)])

    g_post_g = jnp.concatenate([dpg0, dpg1, dpg2, dpg3], axis=0)
    g_post_b = jnp.concatenate([dpb0, dpb1, dpb2, dpb3], axis=0)
    g_a_ln_g = jnp.concatenate([dlg0, dlg3], axis=0)
    g_a_ln_b = jnp.concatenate([dlb0, dlb3], axis=0)
    g_a_w_s = jnp.stack([dws0, dws3])
    g_a_b_s = jnp.swapaxes(jnp.stack([dbs0, dbs3]), 1, 2)
    small_g = [g_post_g, g_post_b, g_a_w_s, g_a_b_s, dconv_b, dblg, dblb, dqg, dkg, g_a_ln_g, g_a_ln_b, dconv_w,
               dmod[:, 0], dmod[:, 1]]
    small_shapes = [a.shape for a in small_g]
    (gs_all,) = _all_gather([_pack(small_g)], "ag_small_grads")
    sums = _unpack(_sum_devices(gs_all, "sum_small"), small_shapes)
    (s_post_g, s_post_b, s_a_w_s, s_a_b_s, s_conv_b, s_b_ln_g, s_b_ln_b, s_q_g, s_k_g, s_a_ln_g, s_a_ln_b, s_conv_w,
     s_dmod_own, s_dmod_ctx) = sums
    grad_mod_b = s_dmod_own + s_dmod_ctx
    wl = a_ln_g.shape[1]
    wcv = b_conv_w.shape[2]
    grad_a_ln_g = lax.dynamic_slice_in_dim(s_a_ln_g, me * wl, wl, axis=1)
    grad_a_ln_b = lax.dynamic_slice_in_dim(s_a_ln_b, me * wl, wl, axis=1)
    grad_b_conv_w = lax.dynamic_slice_in_dim(s_conv_w, me * wcv, wcv, axis=1)[None]

    dmod_dev = _unpack_dev(gs_all, small_shapes)[12]
    dm_rows = jnp.concatenate([jnp.moveaxis(dmod_dev, 0, 1), s_dmod_ctx[:, None],
                               jnp.zeros((DEPTH, COND_ROWS - N_DEV - 1, 3 * d), F32)], axis=1)
    dm_mine = lax.dynamic_slice_in_dim(dm_rows, me * wm, wm, axis=2)
    grad_mod_w, dcond_part = _mod_bwd_mm(cond, dm_mine, mod_w, "mod_bwd")
    (dcond_all,) = _all_gather([dcond_part], "ag_dcond")
    dcond = _sum_devices(dcond_all, "sum_dcond")
    grad_c_ctx = dcond[N_DEV] * _dsilu(c_ctx)

    def upd(parts, w, m, v, name):
        return _adam_reduce(parts, w, m, v, name)

    def stack2(a, b):
        return [jnp.stack([p, q]) for p, q in zip(a, b)]

    o_a_w_in = stack2(upd(r_a_in0, a_w_in[0], m_a_w_in[0], v_a_w_in[0], "adam_a_in0"),
                      upd(r_a_in1, a_w_in[1], m_a_w_in[1], v_a_w_in[1], "adam_a_in1"))
    o_a_w_out = stack2(upd(r_a_out0, a_w_out[0], m_a_w_out[0], v_a_w_out[0], "adam_a_out0"),
                       upd(r_a_out1, a_w_out[1], m_a_w_out[1], v_a_w_out[1], "adam_a_out1"))
    o_b_w_in = [a[None] for a in upd(r_b_in, b_w_in[0], m_b_w_in[0], v_b_w_in[0], "adam_b_in")]
    o_b_w_out = [a[None] for a in upd(r_b_out, b_w_out[0], m_b_w_out[0], v_b_w_out[0], "adam_b_out")]
    o_c_w_in = [a[None] for a in upd(r_c_in, c_w_in[0], m_c_w_in[0], v_c_w_in[0], "adam_c_in")]
    o_c_w_out = [a[None] for a in upd(r_c_out, c_w_out[0], m_c_w_out[0], v_c_w_out[0], "adam_c_out")]
    mw_shape = mod_w.shape
    o_mod_w = [grad_mod_w] + [a.reshape(mw_shape) for a in _adam_plain(
        grad_mod_w.reshape(-1, wm), mod_w.reshape(-1, wm), m_mod_w.reshape(-1, wm), v_mod_w.reshape(-1, wm), "adam_mod_w")]

    sg = [grad_c_ctx, grad_mod_b, s_post_g, s_post_b, grad_a_ln_g, grad_a_ln_b, s_a_w_s, s_a_b_s, grad_b_conv_w, s_conv_b,
          s_b_ln_g, s_b_ln_b, s_q_g, s_k_g]
    sw = [c_ctx, mod_b, post_g, post_b, a_ln_g, a_ln_b, a_w_s, a_b_s, b_conv_w, b_conv_b, b_ln_g, b_ln_b, c_q_g, c_k_g]
    sm = [m_c_ctx, m_mod_b, m_post_g, m_post_b, m_a_ln_g, m_a_ln_b, m_a_w_s, m_a_b_s, m_b_conv_w, m_b_conv_b, m_b_ln_g,
          m_b_ln_b, m_c_q_g, m_c_k_g]
    sv = [v_c_ctx, v_mod_b, v_post_g, v_post_b, v_a_ln_g, v_a_ln_b, v_a_w_s, v_a_b_s, v_b_conv_w, v_b_conv_b, v_b_ln_g,
          v_b_ln_b, v_c_q_g, v_c_k_g]
    shapes = [a.shape for a in sw]
    sg = [g.reshape(s) for g, s in zip(sg, shapes)]
    sd, snm, snv = [_unpack(a, shapes) for a in _adam_plain(_pack(sg), _pack(sw), _pack(sm), _pack(sv), "adam_small")]

    def small(k):
        return [sg[k], sd[k], snm[k], snv[k]]

    per_weight = [small(0), o_mod_w, small(1), small(2), small(3), o_a_w_in, small(4), small(5), small(6), small(7),
                  o_a_w_out, o_b_w_in, small(8), small(9), small(10), small(11), o_b_w_out, o_c_w_in, small(12), small(13),
                  o_c_w_out]
    outs = [loss, grad_x]
    for kind in range(4):
        outs += [pw[kind] for pw in per_weight]
    return tuple(outs)
```

```python
import functools
import math

import jax
import jax.numpy as jnp
from jax import lax
from jax.experimental import pallas as pl
from jax.experimental.pallas import tpu as pltpu

F32 = jnp.float32
BF16 = jnp.bfloat16
MXU_DTYPE = jnp.bfloat16

DEPTH = 4
GRID_W = 64
CHUNK = 128
SGU_GROUPS = 16
CONV_W = 31
CONV_HALO = 16
HEAD_DIM = 128
N_KV_HEADS = 4
ROPE_THETA = 10000.0
DEEPNORM_ALPHA = (2 * DEPTH) ** 0.25
LN_EPS = 1e-6
ADAM_LR, ADAM_B1, ADAM_B2, ADAM_EPS, ADAM_WD, ADAM_STEP = 0.001, 0.9, 0.999, 1e-08, 0.01, 10

N_DEV = 8
V7X_VMEM_BYTES = 64 * 1024 * 1024
V7X_VMEM_CLAIM = V7X_VMEM_BYTES * 7 // 8
LANES = 128

NN = ((1,), (0,))
NT = ((1,), (1,))
TN = ((0,), (0,))


def _dot(a, b, dims):
    return lax.dot_general(a.astype(MXU_DTYPE), b.astype(MXU_DTYPE), (dims, ((), ())), preferred_element_type=F32)


def _pick(n, prefs):
    for p in prefs:
        if n % p == 0:
            return p
    raise ValueError(f"no tile for {n} among {prefs}")


def _params(sem, vmem_bytes):
    assert vmem_bytes <= V7X_VMEM_CLAIM, (vmem_bytes, V7X_VMEM_CLAIM)
    return pltpu.CompilerParams(dimension_semantics=sem, vmem_limit_bytes=V7X_VMEM_CLAIM)


def _nbytes(shape, dtype):
    return math.prod(shape) * jnp.dtype(dtype).itemsize


def _sigmoid(x):
    return jax.nn.sigmoid(x)


def _silu(x):
    return x * jax.nn.sigmoid(x)


def _dsilu(x):
    s = jax.nn.sigmoid(x)
    return s * (1.0 + x * (1.0 - s))


def _ln_stats(x):
    mu = jnp.mean(x, axis=-1, keepdims=True)
    xc = x - mu
    var = jnp.mean(xc * xc, axis=-1, keepdims=True)
    return xc, lax.rsqrt(var + LN_EPS)


def _ln(x, g, b):
    xc, rstd = _ln_stats(x)
    return xc * rstd * g + b


def _mesh_pos():
    return lax.axis_index("x"), lax.axis_index("y"), lax.axis_index("c")


def _dev_index(p):
    return 4 * p[0] + 2 * p[1] + p[2]


class _Comm:
    def __init__(self, inputs, out_shapes, sems, start, finish):
        self.inputs, self.out_shapes, self.sems, self.start, self.finish = inputs, out_shapes, sems, start, finish


def _gather_comm(xs):
    n = len(xs)

    def place():
        x, y, c = _mesh_pos()
        return (x, y, c), (x, y, 1 - c), [(1 - x, y), (x, 1 - y), (1 - x, 1 - y)], c

    def copier(x_refs, o_refs, sems):
        send_sems, recv_sems, _ = sems

        def copy(t, k, block, to, from_input=False):
            dst = o_refs[t].at[_dev_index(block)]
            return pltpu.make_async_remote_copy(
                src_ref=x_refs[t] if from_input else dst, dst_ref=dst,
                send_sem=send_sems.at[t, k], recv_sem=recv_sems.at[t, k],
                device_id=to, device_id_type=pl.DeviceIdType.MESH)

        return copy

    def own(x_refs, o_refs, sems, t, me):
        return pltpu.make_async_copy(x_refs[t], o_refs[t].at[_dev_index(me)], sems[2].at[t])

    def first_copies(copy, t, me, sibling, chips, c):
        return [copy(t, 0, me, sibling, True)] + [copy(t, 1 + j, me, (*chip, c), True) for j, chip in enumerate(chips)]

    def start(x_refs, o_refs, sems):
        me, sibling, chips, c = place()
        copy = copier(x_refs, o_refs, sems)
        for t in range(n):
            own(x_refs, o_refs, sems, t, me).start()
            for cp in first_copies(copy, t, me, sibling, chips, c):
                cp.start()

    def finish(x_refs, o_refs, sems):
        me, sibling, chips, c = place()
        copy = copier(x_refs, o_refs, sems)
        passed = []
        for t in range(n):
            for j, chip in enumerate(chips):
                copy(t, 1 + j, (*chip, c), me).wait_recv()
                cp = copy(t, 4 + j, (*chip, c), sibling)
                cp.start()
                passed.append(cp)
        for t in range(n):
            copy(t, 0, sibling, me).wait_recv()
            for j, chip in enumerate(chips):
                copy(t, 4 + j, (*chip, 1 - c), me).wait_recv()
        for t in range(n):
            for cp in first_copies(copy, t, me, sibling, chips, c):
                cp.wait_send()
        for cp in passed:
            cp.wait_send()
        for t in range(n):
            own(x_refs, o_refs, sems, t, me).wait()

    sems = [pltpu.SemaphoreType.DMA((n, 7)), pltpu.SemaphoreType.DMA((n, 7)), pltpu.SemaphoreType.DMA((n,))]
    return _Comm(list(xs), [jax.ShapeDtypeStruct((N_DEV, *a.shape), a.dtype) for a in xs], sems, start, finish)


def _exchange_comm(gs):
    n = len(gs)

    def copies(g_refs, r_refs, sems):
        send_sems, recv_sems, local_sems = sems
        x, y, c = _mesh_pos()
        me = _dev_index((x, y, c))
        out = []
        for t in range(n):
            out.append(pltpu.make_async_copy(g_refs[t].at[me], r_refs[t].at[me], local_sems.at[t]))
            for k in range(1, N_DEV):
                fx, fy, fc = (k >> 2) & 1, (k >> 1) & 1, k & 1
                peer = (1 - x if fx else x, 1 - y if fy else y, 1 - c if fc else c)
                out.append(pltpu.make_async_remote_copy(
                    src_ref=g_refs[t].at[_dev_index(peer)], dst_ref=r_refs[t].at[me],
                    send_sem=send_sems.at[t, k - 1], recv_sem=recv_sems.at[t, k - 1],
                    device_id=peer, device_id_type=pl.DeviceIdType.MESH))
        return out

    def start(g_refs, r_refs, sems):
        for cp in copies(g_refs, r_refs, sems):
            cp.start()

    def finish(g_refs, r_refs, sems):
        for cp in copies(g_refs, r_refs, sems):
            cp.wait()

    sems = [pltpu.SemaphoreType.DMA((n, 7)), pltpu.SemaphoreType.DMA((n, 7)), pltpu.SemaphoreType.DMA((n,))]
    return _Comm(list(gs), [jax.ShapeDtypeStruct(g.shape, g.dtype) for g in gs], sems, start, finish)


def _comm_call(comm, name):
    n_in, n_out = len(comm.inputs), len(comm.out_shapes)

    def body(*refs):
        ins, outs, sems = refs[:n_in], refs[n_in:n_in + n_out], refs[n_in + n_out:]
        comm.start(ins, outs, sems)
        comm.finish(ins, outs, sems)

    hbm = pl.BlockSpec(memory_space=pl.ANY)
    return pl.pallas_call(
        body, name=name, out_shape=comm.out_shapes, in_specs=[hbm] * n_in, out_specs=[hbm] * n_out,
        scratch_shapes=comm.sems)(*comm.inputs)


def _call(body, operands, *, name, grid, in_specs, out_specs, out_shape, scratch_shapes=(), compiler_params, comm=None):
    single = not isinstance(out_shape, (list, tuple))
    out_shape = [out_shape] if single else list(out_shape)
    out_specs = [out_specs] if single else list(out_specs)
    scratch_shapes = list(scratch_shapes)
    if comm is None:
        res = pl.pallas_call(
            body, name=name, grid=grid, in_specs=list(in_specs), out_specs=out_specs, out_shape=out_shape,
            scratch_shapes=scratch_shapes, compiler_params=compiler_params)(*operands)
        return res[0] if single else res
    n_in, n_out, n_scr = len(in_specs), len(out_specs), len(scratch_shapes)
    c_in, c_out = len(comm.inputs), len(comm.out_shapes)

    def with_comm(*refs):
        ins, c_ins = refs[:n_in], refs[n_in:n_in + c_in]
        o0 = n_in + c_in
        outs, c_outs = refs[o0:o0 + n_out], refs[o0 + n_out:o0 + n_out + c_out]
        s0 = o0 + n_out + c_out
        scr, sems = refs[s0:s0 + n_scr], refs[s0 + n_scr:]
        ids = [pl.program_id(a) for a in range(len(grid))]
        first = functools.reduce(jnp.logical_and, [i == 0 for i in ids])
        last = functools.reduce(jnp.logical_and, [i == g - 1 for i, g in zip(ids, grid)])

        @pl.when(first)
        def _():
            comm.start(c_ins, c_outs, sems)

        body(*ins, *outs, *scr)

        @pl.when(last)
        def _():
            comm.finish(c_ins, c_outs, sems)

    hbm = pl.BlockSpec(memory_space=pl.ANY)
    params = pltpu.CompilerParams(dimension_semantics=("arbitrary",) * len(grid),
                                  vmem_limit_bytes=compiler_params.vmem_limit_bytes)
    res = pl.pallas_call(
        with_comm, name=name, grid=grid, in_specs=list(in_specs) + [hbm] * c_in, out_specs=out_specs + [hbm] * c_out,
        out_shape=out_shape + list(comm.out_shapes), scratch_shapes=scratch_shapes + list(comm.sems),
        compiler_params=params)(*operands, *comm.inputs)
    return (res[0] if single else res[:n_out]), res[n_out:]


def _all_gather(xs, name):
    return _comm_call(_gather_comm(xs), name)


ROW_TILES = (1088, 1024, 768, 544, 512, 384, 272, 256, 128)
TOKEN_K_TILES = (2176, 2048, 1088, 1024, 768, 512, 384, 256, 128)
COL_TILES = (1024, 768, 640, 512, 384, 256, 128)
DEEP_K = 2048


def _mm_nn(a, w, *, planes=1, name, comm=None):
    m, k = a.shape
    if w.ndim == 3:
        nd_w = w.shape[2]
        n = w.shape[0] * nd_w
    else:
        nd_w = n = w.shape[1]
    npl = n // planes
    tm = _pick(m, ROW_TILES)
    tn = _pick(math.gcd(nd_w, npl), COL_TILES if k <= DEEP_K else COL_TILES[3:])
    r, rp = nd_w // tn, npl // tn
    if w.ndim == 3:
        w_spec = pl.BlockSpec((None, k, tn), lambda i, j: (j // r, 0, j % r))
    else:
        w_spec = pl.BlockSpec((k, tn), lambda i, j: (0, j))
    if planes > 1:
        o_spec = pl.BlockSpec((None, tm, tn), lambda i, j: (j // rp, i, j % rp))
        out_shape = jax.ShapeDtypeStruct((planes, m, npl), F32)
    else:
        o_spec = pl.BlockSpec((tm, tn), lambda i, j: (i, j))
        out_shape = jax.ShapeDtypeStruct((m, n), F32)

    def body(a_ref, w_ref, o_ref):
        o_ref[...] = _dot(a_ref[...], w_ref[...], NN)

    vmem = 2 * (_nbytes((tm, k), a.dtype) + _nbytes((k, tn), w.dtype) + _nbytes((tm, tn), F32)) + _nbytes((tm, tn), F32)
    return _call(
        body, (a, w), name=name, grid=(m // tm, n // tn),
        in_specs=[pl.BlockSpec((tm, k), lambda i, j: (i, 0)), w_spec], out_specs=o_spec, out_shape=out_shape,
        compiler_params=_params(("parallel", "arbitrary"), vmem), comm=comm)


def _mm_nt(a, w, *, name):
    m, k = a.shape
    n = w.shape[0]
    tm = _pick(m, ROW_TILES)
    tn = _pick(n, COL_TILES)

    def body(a_ref, w_ref, o_ref):
        o_ref[...] = _dot(a_ref[...], w_ref[...], NT)

    vmem = 2 * (_nbytes((tm, k), a.dtype) + _nbytes((tn, k), w.dtype) + _nbytes((tm, tn), F32)) + _nbytes((tm, tn), F32)
    return pl.pallas_call(
        body, name=name, grid=(m // tm, n // tn),
        in_specs=[pl.BlockSpec((tm, k), lambda i, j: (i, 0)), pl.BlockSpec((tn, k), lambda i, j: (j, 0))],
        out_specs=pl.BlockSpec((tm, tn), lambda i, j: (i, j)), out_shape=jax.ShapeDtypeStruct((m, n), F32),
        compiler_params=_params(("parallel", "arbitrary"), vmem),
    )(a, w)


def _mm_nt_blocked(a, w, *, name, comm=None):
    nd, n, kd = w.shape
    if a.ndim == 3:
        p, m, kp = a.shape
    else:
        (m, kp), p = a.shape, 1
    tk = _pick(math.gcd(kd, kp), COL_TILES)
    ra, rw = kp // tk, kd // tk
    nk = nd * rw
    tm = _pick(m, ROW_TILES)
    if a.ndim == 3:
        a_spec = pl.BlockSpec((None, tm, tk), lambda i, kk: (kk // ra, i, kk % ra))
    else:
        a_spec = pl.BlockSpec((tm, tk), lambda i, kk: (i, kk))

    def body(a_ref, w_ref, o_ref, acc_ref):
        kk = pl.program_id(1)

        @pl.when(kk == 0)
        def _():
            acc_ref[...] = jnp.zeros_like(acc_ref)

        acc_ref[...] += _dot(a_ref[...], w_ref[...], NT)

        @pl.when(kk == nk - 1)
        def _():
            o_ref[...] = acc_ref[...]

    vmem = 2 * (_nbytes((tm, tk), a.dtype) + _nbytes((n, tk), w.dtype) + _nbytes((tm, n), F32)) + 2 * _nbytes((tm, n), F32)
    return _call(
        body, (a, w), name=name, grid=(m // tm, nk),
        in_specs=[a_spec, pl.BlockSpec((None, n, tk), lambda i, kk: (kk // rw, 0, kk % rw))],
        out_specs=pl.BlockSpec((tm, n), lambda i, kk: (i, 0)), out_shape=jax.ShapeDtypeStruct((m, n), F32),
        scratch_shapes=[pltpu.VMEM((tm, n), F32)],
        compiler_params=_params(("parallel", "arbitrary"), vmem), comm=comm)


def _mm_tn(a, b, *, blocked, out_dtype, name, comm=None):
    rows, da = a.shape
    if b.ndim == 3:
        p, _, npl = b.shape
    else:
        p, npl = 1, b.shape[1]
    n = p * npl
    nd_w = n // N_DEV if blocked else n
    tk = _pick(rows, TOKEN_K_TILES)
    tm = _pick(da, COL_TILES)
    tn = _pick(math.gcd(nd_w, npl), COL_TILES)
    rb, ro = npl // tn, nd_w // tn
    nk = rows // tk
    if b.ndim == 3:
        b_spec = pl.BlockSpec((None, tk, tn), lambda i, j, kk: (j // rb, kk, j % rb))
    else:
        b_spec = pl.BlockSpec((tk, tn), lambda i, j, kk: (kk, j))
    if blocked:
        o_spec = pl.BlockSpec((None, tm, tn), lambda i, j, kk: (j // ro, i, j % ro))
        out_shape = jax.ShapeDtypeStruct((N_DEV, da, nd_w), out_dtype)
    else:
        o_spec = pl.BlockSpec((tm, tn), lambda i, j, kk: (i, j))
        out_shape = jax.ShapeDtypeStruct((da, n), out_dtype)

    def body(a_ref, b_ref, o_ref, acc_ref):
        kk = pl.program_id(2)

        @pl.when(kk == 0)
        def _():
            acc_ref[...] = jnp.zeros_like(acc_ref)

        acc_ref[...] += _dot(a_ref[...], b_ref[...], TN)

        @pl.when(kk == nk - 1)
        def _():
            o_ref[...] = acc_ref[...].astype(o_ref.dtype)

    vmem = (2 * (_nbytes((tk, tm), a.dtype) + _nbytes((tk, tn), b.dtype) + _nbytes((tm, tn), out_dtype))
            + 3 * _nbytes((tm, tn), F32) + _nbytes((tk, tm), F32))
    return _call(
        body, (a, b), name=name, grid=(da // tm, n // tn, nk),
        in_specs=[pl.BlockSpec((tk, tm), lambda i, j, kk: (kk, i)), b_spec], out_specs=o_spec, out_shape=out_shape,
        scratch_shapes=[pltpu.VMEM((tm, tn), F32)],
        compiler_params=_params(("parallel", "parallel", "arbitrary"), vmem), comm=comm)


ROW_TILE = 256


class _Rows:
    def __init__(self, n_x, n_ctx, tile=ROW_TILE):
        assert n_x % tile == 0 and n_ctx % tile == 0
        self.n_x, self.n_ctx, self.tile = n_x, n_ctx, tile
        self.rows = n_x + n_ctx
        self.nt_x = n_x // tile
        self.nt = self.rows // tile
        self.n_seg = 2 if n_ctx else 1

    def seg(self, i):
        return jnp.where(i >= self.nt_x, 1, 0) if self.n_ctx else 0

    def first_of_seg(self, i):
        return (i == 0) | (i == self.nt_x) if self.n_ctx else i == 0

    def full(self, width):
        return pl.BlockSpec((self.tile, width), lambda i: (i, 0))

    def plane(self, p, width):
        return pl.BlockSpec((None, self.tile, width), lambda i: (p, i, 0))

    def modvec(self, layer, which, width):
        return pl.BlockSpec((None, 1, width), lambda i: ((layer * 2 + self.seg(i)) * 3 + which, 0, 0))

    def seg_acc(self, width):
        return pl.BlockSpec((None, 1, width), lambda i: (self.seg(i), 0, 0))


def _vec(width):
    return pl.BlockSpec((1, width), lambda i: (0, 0))


def _acc(ref, first, val):
    @pl.when(first)
    def _():
        ref[...] = jnp.zeros_like(ref)

    ref[...] += val


def _modulate(xs, modv, layer, rt, name):
    d = xs.shape[1]

    def body(x_ref, sh_ref, sc_ref, o_ref):
        o_ref[...] = (x_ref[...] * (1.0 + sc_ref[...]) + sh_ref[...]).astype(o_ref.dtype)

    return pl.pallas_call(
        body, name=name, grid=(rt.nt,),
        in_specs=[rt.full(d), rt.modvec(layer, 0, d), rt.modvec(layer, 1, d)],
        out_specs=rt.full(d), out_shape=jax.ShapeDtypeStruct(xs.shape, MXU_DTYPE),
        compiler_params=_params(("parallel",), 6 * _nbytes((rt.tile, d), F32)),
    )(xs, modv, modv)


def _post(x, y, gate, pg, pb):
    return _ln(DEEPNORM_ALPHA * x + gate * y, pg, pb)


def _post_fwd(xs, y, modv, layer, pg, pb, rt, name):
    d = xs.shape[1]

    def body(x_ref, y_ref, gate_ref, pg_ref, pb_ref, o_ref):
        o_ref[...] = _post(x_ref[...], y_ref[...], gate_ref[...], pg_ref[...], pb_ref[...])

    return pl.pallas_call(
        body, name=name, grid=(rt.nt,),
        in_specs=[rt.full(d), rt.full(d), rt.modvec(layer, 2, d), _vec(d), _vec(d)],
        out_specs=rt.full(d), out_shape=jax.ShapeDtypeStruct((rt.rows, d), F32),
        compiler_params=_params(("parallel",), 10 * _nbytes((rt.tile, d), F32)),
    )(xs, y, modv, pg, pb)


def _post_bwd(xs, y, dout, modv, layer, pg, pb, rt, name):
    d = xs.shape[1]

    def body(x_ref, y_ref, do_ref, gate_ref, pg_ref, pb_ref, dres_ref, dy_ref, dpg_ref, dpb_ref, dgate_ref):
        i = pl.program_id(0)
        _, vjp = jax.vjp(_post, x_ref[...], y_ref[...], gate_ref[...], pg_ref[...], pb_ref[...])
        dx, dy, dgate, dpg, dpb = vjp(do_ref[...])
        dres_ref[...] = dx
        dy_ref[...] = dy.astype(dy_ref.dtype)
        _acc(dpg_ref, i == 0, dpg)
        _acc(dpb_ref, i == 0, dpb)
        _acc(dgate_ref, rt.first_of_seg(i), dgate)

    return pl.pallas_call(
        body, name=name, grid=(rt.nt,),
        in_specs=[rt.full(d), rt.full(d), rt.full(d), rt.modvec(layer, 2, d), _vec(d), _vec(d)],
        out_specs=[rt.full(d), rt.full(d), _vec(d), _vec(d), rt.seg_acc(d)],
        out_shape=[jax.ShapeDtypeStruct((rt.rows, d), F32), jax.ShapeDtypeStruct((rt.rows, d), MXU_DTYPE),
                   jax.ShapeDtypeStruct((1, d), F32), jax.ShapeDtypeStruct((1, d), F32),
                   jax.ShapeDtypeStruct((rt.n_seg, 1, d), F32)],
        compiler_params=_params(("arbitrary",), 16 * _nbytes((rt.tile, d), F32)),
    )(xs, y, dout, modv, pg, pb)


def _mod_bwd(dres, dh, xs, modv, layer, rt, name, dx_rows=None):
    d = xs.shape[1]
    nt_res = dres.shape[0] // rt.tile
    nt_dx = rt.nt if dx_rows is None else dx_rows // rt.tile

    def body(dres_ref, dh_ref, x_ref, sc_ref, dx_ref, dshift_ref, dscale_ref):
        i = pl.program_id(0)
        dh = dh_ref[...]

        @pl.when(i < nt_dx)
        def _():
            dx_ref[...] = jnp.where(i < nt_res, dres_ref[...], 0.0) + dh * (1.0 + sc_ref[...])

        first = rt.first_of_seg(i)
        _acc(dshift_ref, first, jnp.sum(dh, axis=0, keepdims=True))
        _acc(dscale_ref, first, jnp.sum(dh * x_ref[...], axis=0, keepdims=True))

    def clamped(nt):
        return pl.BlockSpec((rt.tile, d), lambda i: (jnp.minimum(i, nt - 1), 0))

    return pl.pallas_call(
        body, name=name, grid=(rt.nt,),
        in_specs=[clamped(nt_res), rt.full(d), rt.full(d), rt.modvec(layer, 1, d)],
        out_specs=[clamped(nt_dx), rt.seg_acc(d), rt.seg_acc(d)],
        out_shape=[jax.ShapeDtypeStruct((nt_dx * rt.tile, d), F32), jax.ShapeDtypeStruct((rt.n_seg, 1, d), F32),
                   jax.ShapeDtypeStruct((rt.n_seg, 1, d), F32)],
        compiler_params=_params(("arbitrary",), 10 * _nbytes((rt.tile, d), F32)),
    )(dres, dh, xs, modv)


def _cm_mid_fwd(z3, ln_g, ln_b, w_s, b_s_t, name):
    _, rows, e = z3.shape
    groups = w_s.shape[0]
    gw = e // groups

    def body(z_ref, lg_ref, lb_ref, ws_ref, bs_ref, t_ref):
        vn = _ln(z_ref[1], lg_ref[...], lb_ref[...])
        for h in range(groups):
            cols = slice(h * gw, (h + 1) * gw)
            s = _dot(ws_ref[h], vn[:, cols], NN) + bs_ref[:, h:h + 1]
            t_ref[:, cols] = (z_ref[0, :, cols] * s * _silu(z_ref[2, :, cols])).astype(t_ref.dtype)

    return pl.pallas_call(
        body, name=name, grid=(rows // CHUNK,),
        in_specs=[pl.BlockSpec((3, CHUNK, e), lambda i: (0, i, 0)), _vec(e), _vec(e),
                  pl.BlockSpec(w_s.shape, lambda i: (0, 0, 0)), pl.BlockSpec(b_s_t.shape, lambda i: (0, 0))],
        out_specs=pl.BlockSpec((CHUNK, e), lambda i: (i, 0)), out_shape=jax.ShapeDtypeStruct((rows, e), MXU_DTYPE),
        compiler_params=_params(("parallel",), 12 * _nbytes((CHUNK, e), F32)),
    )(z3, ln_g, ln_b, w_s, b_s_t)


def _cm_mid_bwd(z3, dt, ln_g, ln_b, w_s, b_s_t, name, comm=None):
    _, rows, e = z3.shape
    groups = w_s.shape[0]
    gw = e // groups

    def body(z_ref, dt_ref, lg_ref, lb_ref, ws_ref, bs_ref, dz_ref, dlg_ref, dlb_ref, dws_ref, dbs_ref, dvn_ref):
        i = pl.program_id(0)
        first = i == 0
        v = z_ref[1]
        vn, ln_vjp = jax.vjp(_ln, v, lg_ref[...], lb_ref[...])

        @pl.when(first)
        def _():
            dws_ref[...] = jnp.zeros_like(dws_ref)
            dbs_ref[...] = jnp.zeros_like(dbs_ref)

        for h in range(groups):
            cols = slice(h * gw, (h + 1) * gw)
            vn_h = vn[:, cols]
            s = _dot(ws_ref[h], vn_h, NN) + bs_ref[:, h:h + 1]
            u, g, dth = z_ref[0, :, cols], z_ref[2, :, cols], dt_ref[:, cols]
            sg = _silu(g)
            dz_ref[0, :, cols] = (dth * s * sg).astype(dz_ref.dtype)
            dz_ref[2, :, cols] = (dth * u * s * _dsilu(g)).astype(dz_ref.dtype)
            ds = dth * u * sg
            dvn_ref[:, cols] = _dot(ws_ref[h], ds, TN)
            dws_ref[h] += _dot(ds, vn_h, NT)
            dbs_ref[:, h:h + 1] += jnp.sum(ds, axis=1, keepdims=True)
        dv, dlg, dlb = ln_vjp(dvn_ref[...])
        dz_ref[1] = dv.astype(dz_ref.dtype)
        _acc(dlg_ref, first, dlg)
        _acc(dlb_ref, first, dlb)

    return _call(
        body, (z3, dt, ln_g, ln_b, w_s, b_s_t), name=name, grid=(rows // CHUNK,),
        in_specs=[pl.BlockSpec((3, CHUNK, e), lambda i: (0, i, 0)), pl.BlockSpec((CHUNK, e), lambda i: (i, 0)), _vec(e), _vec(e),
                  pl.BlockSpec(w_s.shape, lambda i: (0, 0, 0)), pl.BlockSpec(b_s_t.shape, lambda i: (0, 0))],
        out_specs=[pl.BlockSpec((3, CHUNK, e), lambda i: (0, i, 0)), _vec(e), _vec(e),
                   pl.BlockSpec(w_s.shape, lambda i: (0, 0, 0)), pl.BlockSpec(b_s_t.shape, lambda i: (0, 0))],
        out_shape=[jax.ShapeDtypeStruct((3, rows, e), MXU_DTYPE), jax.ShapeDtypeStruct((1, e), F32),
                   jax.ShapeDtypeStruct((1, e), F32), jax.ShapeDtypeStruct(w_s.shape, F32),
                   jax.ShapeDtypeStruct(b_s_t.shape, F32)],
        scratch_shapes=[pltpu.VMEM((CHUNK, e), F32)],
        compiler_params=_params(("arbitrary",), 20 * _nbytes((CHUNK, e), F32)), comm=comm)


CONV_COL_TILE = 512


def _conv_specs(rt, tc, planes):
    per = rt.tile // CONV_HALO
    last = rt.rows // CONV_HALO - 1
    if planes:
        cur = pl.BlockSpec((planes, rt.tile, tc), lambda j, i: (0, i, j))
        prev = pl.BlockSpec((planes, CONV_HALO, tc), lambda j, i: (0, jnp.maximum(i * per - 1, 0), j))
        nxt = pl.BlockSpec((planes, CONV_HALO, tc), lambda j, i: (0, jnp.minimum((i + 1) * per, last), j))
    else:
        cur = pl.BlockSpec((rt.tile, tc), lambda j, i: (i, j))
        prev = pl.BlockSpec((CONV_HALO, tc), lambda j, i: (jnp.maximum(i * per - 1, 0), j))
        nxt = pl.BlockSpec((CONV_HALO, tc), lambda j, i: (jnp.minimum((i + 1) * per, last), j))
    return cur, prev, nxt


def _halo_ok(rt, i):
    prev_ok = (i != 0) & (i != rt.nt_x)
    next_ok = (i != rt.nt_x - 1) & (i != rt.nt - 1)
    return prev_ok, next_ok


def _glu(ref):
    return ref[0] * _sigmoid(ref[1])


def _padded(cur, prev, nxt, prev_ok, next_ok):
    return jnp.concatenate([jnp.where(prev_ok, prev, 0.0), cur, jnp.where(next_ok, nxt, 0.0)], axis=0)


SUBLANES = 8
CONV_ROW_BLOCK = 32


def _phase_scratch(tr, tc):
    return pltpu.VMEM((SUBLANES, tr + 2 * CONV_HALO - SUBLANES, tc), F32)


def _store_phases(rot_ref, pad):
    rows = rot_ref.shape[1]
    for b in range(SUBLANES):
        rot_ref[b] = pad[b:b + rows, :]


def _shifted(rot_ref, r0, off):
    a, b = divmod(off, SUBLANES)
    return rot_ref[b, pl.ds(r0 + SUBLANES * a, CONV_ROW_BLOCK), :]


def _conv_fwd(z3, conv_w, conv_b, rt, name, comm=None):
    _, rows, e = z3.shape
    tc = _pick(e, (CONV_COL_TILE, 256, 128))
    tr = rt.tile

    def body(cur_ref, prev_ref, next_ref, w_ref, b_ref, o_ref, rot_ref):
        prev_ok, next_ok = _halo_ok(rt, pl.program_id(1))
        _store_phases(rot_ref, _padded(_glu(cur_ref), _glu(prev_ref), _glu(next_ref), prev_ok, next_ok))
        bias = jnp.broadcast_to(b_ref[...], (CONV_ROW_BLOCK, tc))

        def rows_block(rb, carry):
            r0 = pl.multiple_of(rb * CONV_ROW_BLOCK, CONV_ROW_BLOCK)
            acc = bias
            for k in range(CONV_W):
                acc = acc + w_ref[k:k + 1, :] * _shifted(rot_ref, r0, CONV_HALO - CONV_W // 2 + k)
            o_ref[pl.ds(r0, CONV_ROW_BLOCK), :] = acc
            return carry

        lax.fori_loop(0, tr // CONV_ROW_BLOCK, rows_block, 0)

    cur, prev, nxt = _conv_specs(rt, tc, 2)
    return _call(
        body, (z3, z3, z3, conv_w, conv_b), name=name, grid=(e // tc, rt.nt),
        in_specs=[cur, prev, nxt, pl.BlockSpec((CONV_W, tc), lambda j, i: (0, j)), pl.BlockSpec((1, tc), lambda j, i: (0, j))],
        out_specs=pl.BlockSpec((tr, tc), lambda j, i: (i, j)), out_shape=jax.ShapeDtypeStruct((rows, e), F32),
        scratch_shapes=[_phase_scratch(tr, tc)],
        compiler_params=_params(("parallel", "arbitrary"), 32 * _nbytes((tr, tc), F32)), comm=comm)


def _conv_bwd(z3, dy1, dg, conv_w, rt, name, comm=None):
    _, rows, e = z3.shape
    tc = _pick(e, (CONV_COL_TILE, 256, 128))
    tr = rt.tile

    def body(cur_ref, prev_ref, next_ref, dcur_ref, dprev_ref, dnext_ref, dg_ref, w_ref, dz_ref, dw_ref, db_ref, rot_ref, drot_ref):
        i = pl.program_id(1)
        prev_ok, next_ok = _halo_ok(rt, i)
        _store_phases(rot_ref, _padded(_glu(cur_ref), _glu(prev_ref), _glu(next_ref), prev_ok, next_ok))
        _store_phases(drot_ref, _padded(dcur_ref[...], dprev_ref[...], dnext_ref[...], prev_ok, next_ok))
        n_blocks = tr // CONV_ROW_BLOCK

        @pl.when(i == 0)
        def _():
            dw_ref[...] = jnp.zeros_like(dw_ref)
            db_ref[...] = jnp.zeros_like(db_ref)

        def block_rows(rb):
            return pl.ds(pl.multiple_of(rb * CONV_ROW_BLOCK, CONV_ROW_BLOCK), CONV_ROW_BLOCK)

        def dgate_block(rb, carry):
            r0 = pl.multiple_of(rb * CONV_ROW_BLOCK, CONV_ROW_BLOCK)
            dy0 = jnp.zeros((CONV_ROW_BLOCK, tc), F32)
            for k in range(CONV_W):
                dy0 = dy0 + w_ref[k:k + 1, :] * _shifted(drot_ref, r0, CONV_HALO + CONV_W // 2 - k)
            rws = block_rows(rb)
            a, sb = cur_ref[0, rws, :], _sigmoid(cur_ref[1, rws, :])
            dz_ref[0, rws, :] = (dy0 * sb).astype(dz_ref.dtype)
            dz_ref[1, rws, :] = (dy0 * a * sb * (1.0 - sb)).astype(dz_ref.dtype)
            return carry

        lax.fori_loop(0, n_blocks, dgate_block, 0)
        dz_ref[2] = dg_ref[...]

        for k in range(CONV_W):
            def tap_block(rb, acc, off=CONV_HALO - CONV_W // 2 + k):
                r0 = pl.multiple_of(rb * CONV_ROW_BLOCK, CONV_ROW_BLOCK)
                return acc + dcur_ref[block_rows(rb), :] * _shifted(rot_ref, r0, off)

            acc = lax.fori_loop(0, n_blocks, tap_block, jnp.zeros((CONV_ROW_BLOCK, tc), F32))
            dw_ref[k:k + 1, :] += jnp.sum(acc, axis=0, keepdims=True)
        db_ref[...] += jnp.sum(dcur_ref[...], axis=0, keepdims=True)

    cur, prev, nxt = _conv_specs(rt, tc, 2)
    dcur, dprev, dnxt = _conv_specs(rt, tc, 0)
    return _call(
        body, (z3, z3, z3, dy1, dy1, dy1, dg, conv_w), name=name, grid=(e // tc, rt.nt),
        in_specs=[cur, prev, nxt, dcur, dprev, dnxt, pl.BlockSpec((tr, tc), lambda j, i: (i, j)),
                  pl.BlockSpec((CONV_W, tc), lambda j, i: (0, j))],
        out_specs=[pl.BlockSpec((3, tr, tc), lambda j, i: (0, i, j)), pl.BlockSpec((CONV_W, tc), lambda j, i: (0, j)),
                   pl.BlockSpec((1, tc), lambda j, i: (0, j))],
        out_shape=[jax.ShapeDtypeStruct((3, rows, e), MXU_DTYPE), jax.ShapeDtypeStruct((CONV_W, e), F32),
                   jax.ShapeDtypeStruct((1, e), F32)],
        scratch_shapes=[_phase_scratch(tr, tc), _phase_scratch(tr, tc)],
        compiler_params=_params(("parallel", "arbitrary"), 48 * _nbytes((tr, tc), F32)), comm=comm)


def _conv_mid(y1, g, ln_g, ln_b):
    return _silu(_ln(y1, ln_g, ln_b)) * _silu(g)


def _conv_mid_fwd(y1, z3, ln_g, ln_b, rt, name):
    e = y1.shape[1]
    tr = CHUNK

    def body(y_ref, g_ref, lg_ref, lb_ref, t_ref):
        t_ref[...] = _conv_mid(y_ref[...], g_ref[...], lg_ref[...], lb_ref[...]).astype(t_ref.dtype)

    return pl.pallas_call(
        body, name=name, grid=(rt.rows // tr,),
        in_specs=[pl.BlockSpec((tr, e), lambda i: (i, 0)), pl.BlockSpec((None, tr, e), lambda i: (2, i, 0)), _vec(e), _vec(e)],
        out_specs=pl.BlockSpec((tr, e), lambda i: (i, 0)), out_shape=jax.ShapeDtypeStruct((rt.rows, e), MXU_DTYPE),
        compiler_params=_params(("parallel",), 12 * _nbytes((tr, e), F32)),
    )(y1, z3, ln_g, ln_b)


def _conv_mid_bwd(y1, z3, dt, ln_g, ln_b, rt, name):
    e = y1.shape[1]
    tr = CHUNK

    def body(y_ref, g_ref, dt_ref, lg_ref, lb_ref, dy_ref, dg_ref, dlg_ref, dlb_ref):
        first = pl.program_id(0) == 0
        _, vjp = jax.vjp(_conv_mid, y_ref[...], g_ref[...], lg_ref[...], lb_ref[...])
        dy, dg, dlg, dlb = vjp(dt_ref[...])
        dy_ref[...] = dy
        dg_ref[...] = dg.astype(dg_ref.dtype)
        _acc(dlg_ref, first, dlg)
        _acc(dlb_ref, first, dlb)

    row = pl.BlockSpec((tr, e), lambda i: (i, 0))
    return pl.pallas_call(
        body, name=name, grid=(rt.rows // tr,),
        in_specs=[row, pl.BlockSpec((None, tr, e), lambda i: (2, i, 0)), row, _vec(e), _vec(e)],
        out_specs=[row, row, _vec(e), _vec(e)],
        out_shape=[jax.ShapeDtypeStruct((rt.rows, e), F32), jax.ShapeDtypeStruct((rt.rows, e), MXU_DTYPE),
                   jax.ShapeDtypeStruct((1, e), F32), jax.ShapeDtypeStruct((1, e), F32)],
        compiler_params=_params(("arbitrary",), 20 * _nbytes((tr, e), F32)),
    )(y1, z3, dt, ln_g, ln_b)


def _rms(x, g):
    return x * lax.rsqrt(jnp.mean(x * x, axis=-1, keepdims=True) + LN_EPS) * g


def _pair_swap(x):
    lane = lax.broadcasted_iota(jnp.int32, x.shape, x.ndim - 1)
    return jnp.where(lane % 2 == 0, pltpu.roll(x, x.shape[-1] - 1, x.ndim - 1), pltpu.roll(x, 1, x.ndim - 1))


def _rope(x, cos, sin):
    return x * cos + _pair_swap(x) * sin


def _rope_t(dy, cos, sin):
    return dy * cos + _pair_swap(dy * sin)


def _rope_tables(n_x, n_ctx):
    t = jnp.arange(n_x)
    row = (t // GRID_W).astype(F32)
    col = (t % GRID_W).astype(F32)
    axis_dim = HEAD_DIM // 2
    inv = 1.0 / (ROPE_THETA ** (jnp.arange(0, axis_dim, 2, dtype=F32) / axis_dim))
    ang = jnp.concatenate([row[:, None] * inv, col[:, None] * inv], axis=-1)
    cos, sin = jnp.cos(ang), jnp.sin(ang)
    cos2 = jnp.repeat(cos, 2, axis=-1)
    sin2 = jnp.stack([-sin, sin], axis=-1).reshape(n_x, HEAD_DIM)
    cos2 = jnp.concatenate([cos2, jnp.ones((n_ctx, HEAD_DIM), F32)], axis=0)
    sin2 = jnp.concatenate([sin2, jnp.zeros((n_ctx, HEAD_DIM), F32)], axis=0)
    return cos2, sin2


def _qkv_prep(z4, q_g, k_g, cos, sin, d, kvw, rt, name):
    hd = HEAD_DIM
    kb = d // kvw

    def body(q_ref, k_ref, v_ref, qg_ref, kg_ref, cos_ref, sin_ref, qo_ref, ko_ref, vo_ref):
        cos, sin = cos_ref[...], sin_ref[...]
        for h in range(d // hd):
            cols = slice(h * hd, (h + 1) * hd)
            qo_ref[:, cols] = _rope(_rms(q_ref[:, cols], qg_ref[...]), cos, sin).astype(qo_ref.dtype)
        for h in range(kvw // hd):
            cols = slice(h * hd, (h + 1) * hd)
            ko_ref[:, cols] = _rope(_rms(k_ref[:, cols], kg_ref[...]), cos, sin).astype(ko_ref.dtype)
        vo_ref[...] = v_ref[...].astype(vo_ref.dtype)

    tr = rt.tile
    return pl.pallas_call(
        body, name=name, grid=(rt.nt,),
        in_specs=[pl.BlockSpec((tr, d), lambda i: (i, 0)), pl.BlockSpec((tr, kvw), lambda i: (i, kb)),
                  pl.BlockSpec((tr, kvw), lambda i: (i, kb + 1)), _vec(hd), _vec(hd), rt.full(hd), rt.full(hd)],
        out_specs=[rt.full(d), rt.full(kvw), rt.full(kvw)],
        out_shape=[jax.ShapeDtypeStruct((rt.rows, d), MXU_DTYPE), jax.ShapeDtypeStruct((rt.rows, kvw), MXU_DTYPE),
                   jax.ShapeDtypeStruct((rt.rows, kvw), MXU_DTYPE)],
        compiler_params=_params(("parallel",), 8 * _nbytes((tr, d), F32)),
    )(z4, z4, z4, q_g, k_g, cos, sin)


ATTN_Q_TILE = 256


def _attn_fwd(qh, kh, vh, n_x, name, comm=None):
    rows, d = qh.shape
    kvw = kh.shape[1]
    hd = HEAD_DIM
    n_kv = kvw // hd
    gqw = d // n_kv
    grp = gqw // hd
    tq = _pick(n_x, (ATTN_Q_TILE, 128))
    scale = hd ** -0.5

    def body(q_ref, k_ref, v_ref, o_ref, lse_ref):
        k, v = k_ref[...], v_ref[...]
        for g in range(grp):
            cols = slice(g * hd, (g + 1) * hd)
            s = _dot(q_ref[:, cols], k, NT) * scale
            m = jnp.max(s, axis=-1, keepdims=True)
            p = jnp.exp(s - m)
            l = jnp.sum(p, axis=-1, keepdims=True)
            o_ref[:, cols] = _dot(p / l, v, NN)
            lse_ref[:, g:g + 1] = m + jnp.log(l)

    vmem = 4 * _nbytes((rows, hd), MXU_DTYPE) + 4 * _nbytes((tq, rows), F32) + 6 * _nbytes((tq, gqw), F32)
    return _call(
        body, (qh, kh, vh), name=name, grid=(n_kv, n_x // tq),
        in_specs=[pl.BlockSpec((tq, gqw), lambda h, i: (i, h)), pl.BlockSpec((rows, hd), lambda h, i: (0, h)),
                  pl.BlockSpec((rows, hd), lambda h, i: (0, h))],
        out_specs=[pl.BlockSpec((tq, gqw), lambda h, i: (i, h)), pl.BlockSpec((None, tq, grp), lambda h, i: (h, i, 0))],
        out_shape=[jax.ShapeDtypeStruct((n_x, d), F32), jax.ShapeDtypeStruct((n_kv, n_x, grp), F32)],
        compiler_params=_params(("parallel", "arbitrary"), vmem), comm=comm)


def _attn_bwd(qh, kh, vh, do, lse, n_x, name, comm=None):
    rows, d = qh.shape
    kvw = kh.shape[1]
    hd = HEAD_DIM
    n_kv = kvw // hd
    gqw = d // n_kv
    grp = gqw // hd
    tq = _pick(n_x, (ATTN_Q_TILE, 128))
    scale = hd ** -0.5

    def body(q_ref, k_ref, v_ref, do_ref, lse_ref, dq_ref, dk_ref, dv_ref):
        @pl.when(pl.program_id(1) == 0)
        def _():
            dk_ref[...] = jnp.zeros_like(dk_ref)
            dv_ref[...] = jnp.zeros_like(dv_ref)

        k, v = k_ref[...], v_ref[...]
        for g in range(grp):
            cols = slice(g * hd, (g + 1) * hd)
            q, dog = q_ref[:, cols], do_ref[:, cols]
            p = jnp.exp(_dot(q, k, NT) * scale - lse_ref[:, g:g + 1])
            dp = _dot(dog, v, NT)
            ds = (p * (dp - jnp.sum(dp * p, axis=-1, keepdims=True)) * scale).astype(MXU_DTYPE)
            dq_ref[:, cols] = _dot(ds, k, NN)
            dk_ref[...] += _dot(ds, q, TN)
            dv_ref[...] += _dot(p, dog, TN)

    vmem = 4 * _nbytes((rows, hd), MXU_DTYPE) + 4 * _nbytes((rows, hd), F32) + 6 * _nbytes((tq, rows), F32) + 8 * _nbytes((tq, gqw), F32)
    qspec = pl.BlockSpec((tq, gqw), lambda h, i: (i, h))
    kspec = pl.BlockSpec((rows, hd), lambda h, i: (0, h))
    return _call(
        body, (qh, kh, vh, do, lse), name=name, grid=(n_kv, n_x // tq),
        in_specs=[qspec, kspec, kspec, qspec, pl.BlockSpec((None, tq, grp), lambda h, i: (h, i, 0))],
        out_specs=[qspec, kspec, kspec],
        out_shape=[jax.ShapeDtypeStruct((n_x, d), F32), jax.ShapeDtypeStruct((rows, kvw), F32),
                   jax.ShapeDtypeStruct((rows, kvw), F32)],
        compiler_params=_params(("parallel", "arbitrary"), vmem), comm=comm)


def _attn_gate(o, z4, d, kvw, rt, name):
    g0 = (d + 2 * kvw) // kvw
    tr = rt.tile

    def body(o_ref, g_ref, t_ref):
        t_ref[...] = (o_ref[...] * _silu(g_ref[...])).astype(t_ref.dtype)

    tile = pl.BlockSpec((tr, kvw), lambda i, j: (i, j))
    return pl.pallas_call(
        body, name=name, grid=(rt.nt, d // kvw),
        in_specs=[tile, pl.BlockSpec((tr, kvw), lambda i, j: (i, g0 + j))],
        out_specs=tile, out_shape=jax.ShapeDtypeStruct((rt.rows, d), MXU_DTYPE),
        compiler_params=_params(("parallel", "parallel"), 8 * _nbytes((tr, kvw), F32)),
    )(o, z4)


def _attn_gate_bwd(dt, o, z4, d, kvw, rt, name):
    g0 = (d + 2 * kvw) // kvw
    tr = rt.tile

    def body(dt_ref, o_ref, g_ref, do_ref, dg_ref):
        dt_v, g = dt_ref[...], g_ref[...]
        do_ref[...] = (dt_v * _silu(g)).astype(do_ref.dtype)
        dg_ref[...] = (dt_v * o_ref[...] * _dsilu(g)).astype(dg_ref.dtype)

    tile = pl.BlockSpec((tr, kvw), lambda i, j: (i, j))
    return pl.pallas_call(
        body, name=name, grid=(rt.nt, d // kvw),
        in_specs=[tile, tile, pl.BlockSpec((tr, kvw), lambda i, j: (i, g0 + j))],
        out_specs=[tile, tile],
        out_shape=[jax.ShapeDtypeStruct((rt.rows, d), MXU_DTYPE), jax.ShapeDtypeStruct((rt.rows, d), MXU_DTYPE)],
        compiler_params=_params(("parallel", "parallel"), 12 * _nbytes((tr, kvw), F32)),
    )(dt, o, z4)


def _prep_bwd(dxh, z4, col_block, gain, cos, sin, rt, name):
    w = dxh.shape[1]
    hd = HEAD_DIM

    def body(dxh_ref, x_ref, g_ref, cos_ref, sin_ref, dx_ref, dg_ref):
        cos, sin = cos_ref[...], sin_ref[...]
        dg = jnp.zeros((1, hd), F32)
        for h in range(w // hd):
            cols = slice(h * hd, (h + 1) * hd)
            _, vjp = jax.vjp(_rms, x_ref[:, cols], g_ref[...])
            dx, dgh = vjp(_rope_t(dxh_ref[:, cols], cos, sin))
            dx_ref[:, cols] = dx.astype(dx_ref.dtype)
            dg = dg + dgh
        _acc(dg_ref, pl.program_id(0) == 0, dg)

    tr = rt.tile
    return pl.pallas_call(
        body, name=name, grid=(rt.nt,),
        in_specs=[rt.full(w), pl.BlockSpec((tr, w), lambda i: (i, col_block)), _vec(hd), rt.full(hd), rt.full(hd)],
        out_specs=[rt.full(w), _vec(hd)],
        out_shape=[jax.ShapeDtypeStruct((rt.rows, w), MXU_DTYPE), jax.ShapeDtypeStruct((1, hd), F32)],
        compiler_params=_params(("arbitrary",), 12 * _nbytes((tr, w), F32)),
    )(dxh, z4, gain, cos, sin)


def _loss_head(x, target, rt, name):
    d = x.shape[1]

    def body(x_ref, t_ref, dx_ref, l_ref):
        err = x_ref[...] - t_ref[...]
        dx_ref[...] = err / d
        row = jnp.mean(err * err, axis=-1, keepdims=True)
        _acc(l_ref, pl.program_id(0) == 0, jnp.sum(row, axis=0, keepdims=True))

    return pl.pallas_call(
        body, name=name, grid=(rt.nt,),
        in_specs=[rt.full(d), rt.full(d)],
        out_specs=[rt.full(d), pl.BlockSpec((1, 1), lambda i: (0, 0))],
        out_shape=[jax.ShapeDtypeStruct(x.shape, F32), jax.ShapeDtypeStruct((1, 1), F32)],
        compiler_params=_params(("arbitrary",), 8 * _nbytes((rt.tile, d), F32)),
    )(x, target)


def _adamw(w, g, m, v):
    m = ADAM_B1 * m + (1.0 - ADAM_B1) * g
    v = ADAM_B2 * v + (1.0 - ADAM_B2) * (g * g)
    m_hat = m / (1.0 - ADAM_B1 ** ADAM_STEP)
    v_hat = v / (1.0 - ADAM_B2 ** ADAM_STEP)
    delta = -ADAM_LR * (m_hat / (jnp.sqrt(v_hat) + ADAM_EPS) + ADAM_WD * w)
    return delta, m, v


ADAM_TILE_BYTES = 1 << 20


def _adam_tile(rows, cols):
    tr = rows
    while tr % 16 == 0 and tr * cols * 4 > ADAM_TILE_BYTES:
        tr //= 2
    return tr


def _adam_reduce(parts, w, m, v, name):
    slots, rows, cols = w.shape
    tr = _adam_tile(rows, cols)
    nt = rows // tr

    def body(*refs):
        p_refs = refs[:slots]
        w_ref, m_ref, v_ref, g_ref, d_ref, mo_ref, vo_ref = refs[slots:]
        for k in range(slots):
            @pl.when(pl.program_id(0) == k)
            def _(p_ref=p_refs[k]):
                g = p_ref[0].astype(F32)
                for dev in range(1, N_DEV):
                    g = g + p_ref[dev].astype(F32)
                g_ref[...] = g
                d_ref[...], mo_ref[...], vo_ref[...] = _adamw(w_ref[...], g, m_ref[...], v_ref[...])

    def part_spec(k):
        return pl.BlockSpec((N_DEV, tr, cols), lambda s, i: (0, jnp.where(s < k, 0, jnp.where(s == k, i, nt - 1)), 0))

    row = pl.BlockSpec((None, tr, cols), lambda s, i: (s, i, 0))
    sds = jax.ShapeDtypeStruct((slots, rows, cols), F32)
    return pl.pallas_call(
        body, name=name, grid=(slots, nt),
        in_specs=[part_spec(k) for k in range(slots)] + [row, row, row],
        out_specs=[row] * 4, out_shape=[sds] * 4,
        compiler_params=_params(("arbitrary", "arbitrary"), 40 * _nbytes((tr, cols), F32)),
    )(*parts, w, m, v)


def _adam_plain(g, w, m, v, name):
    rows, cols = w.shape
    tr = _adam_tile(rows, cols)

    def body(g_ref, w_ref, m_ref, v_ref, d_ref, mo_ref, vo_ref):
        d_ref[...], mo_ref[...], vo_ref[...] = _adamw(w_ref[...], g_ref[...], m_ref[...], v_ref[...])

    row = pl.BlockSpec((tr, cols), lambda i: (i, 0))
    sds = jax.ShapeDtypeStruct((rows, cols), F32)
    return pl.pallas_call(
        body, name=name, grid=(rows // tr,),
        in_specs=[row] * 4, out_specs=[row] * 3, out_shape=[sds] * 3,
        compiler_params=_params(("parallel",), 32 * _nbytes((tr, cols), F32)),
    )(g, w, m, v)


def _sum_devices(parts, name):
    _, rows, cols = parts.shape
    tr = _adam_tile(rows, cols)

    def body(p_ref, o_ref):
        g = p_ref[0]
        for k in range(1, N_DEV):
            g = g + p_ref[k]
        o_ref[...] = g

    return pl.pallas_call(
        body, name=name, grid=(rows // tr,),
        in_specs=[pl.BlockSpec((N_DEV, tr, cols), lambda i: (0, i, 0))],
        out_specs=pl.BlockSpec((tr, cols), lambda i: (i, 0)), out_shape=jax.ShapeDtypeStruct((rows, cols), F32),
        compiler_params=_params(("parallel",), 24 * _nbytes((tr, cols), F32)),
    )(parts)


COND_ROWS = 16


def _mod_fwd_mm(cond, mod_w, mod_b, name):
    layers, d, w = mod_w.shape

    def body(c_ref, w_ref, b_ref, o_ref):
        o_ref[...] = _dot(_silu(c_ref[...]), w_ref[...], NN) + b_ref[...]

    return pl.pallas_call(
        body, name=name, grid=(layers,),
        in_specs=[pl.BlockSpec((COND_ROWS, d), lambda l: (0, 0)), pl.BlockSpec((None, d, w), lambda l: (l, 0, 0)),
                  pl.BlockSpec((None, 1, w), lambda l: (l, 0, 0))],
        out_specs=pl.BlockSpec((None, COND_ROWS, w), lambda l: (l, 0, 0)),
        out_shape=jax.ShapeDtypeStruct((layers, COND_ROWS, w), F32),
        compiler_params=_params(("parallel",), 4 * _nbytes((d, w), F32)),
    )(cond, mod_w, mod_b)


def _mod_bwd_mm(cond, dm, mod_w, name):
    layers, d, w = mod_w.shape

    def body(c_ref, dm_ref, w_ref, dw_ref, dc_ref):
        dmv = dm_ref[...]
        dw_ref[...] = _dot(_silu(c_ref[...]), dmv, TN)
        _acc(dc_ref, pl.program_id(0) == 0, _dot(dmv, w_ref[...], NT))

    return pl.pallas_call(
        body, name=name, grid=(layers,),
        in_specs=[pl.BlockSpec((COND_ROWS, d), lambda l: (0, 0)), pl.BlockSpec((None, COND_ROWS, w), lambda l: (l, 0, 0)),
                  pl.BlockSpec((None, d, w), lambda l: (l, 0, 0))],
        out_specs=[pl.BlockSpec((None, d, w), lambda l: (l, 0, 0)), pl.BlockSpec((COND_ROWS, d), lambda l: (0, 0))],
        out_shape=[jax.ShapeDtypeStruct((layers, d, w), F32), jax.ShapeDtypeStruct((COND_ROWS, d), F32)],
        compiler_params=_params(("arbitrary",), 6 * _nbytes((d, w), F32)),
    )(cond, dm, mod_w)


PACK_ROWS = 256


def _pack(arrs):
    flat = jnp.concatenate([a.reshape(-1).astype(F32) for a in arrs])
    pad = (-flat.shape[0]) % (PACK_ROWS * LANES)
    return jnp.pad(flat, (0, pad)).reshape(-1, LANES)


def _unpack(flat2d, shapes):
    flat = flat2d.reshape(-1)
    out, off = [], 0
    for s in shapes:
        n = math.prod(s)
        out.append(flat[off:off + n].reshape(s))
        off += n
    return out


def _unpack_dev(g2d, shapes):
    flat = g2d.reshape(N_DEV, -1)
    out, off = [], 0
    for s in shapes:
        n = math.prod(s)
        out.append(flat[:, off:off + n].reshape((N_DEV, *s)))
        off += n
    return out


def kernel(x, c, ctx, c_ctx, mod_w, mod_b, post_g, post_b, a_w_in, a_ln_g, a_ln_b, a_w_s, a_b_s, a_w_out, b_w_in, b_conv_w, b_conv_b, b_ln_g, b_ln_b, b_w_out, c_w_in, c_q_g, c_k_g, c_w_out, loss_target, m_c_ctx, m_mod_w, m_mod_b, m_post_g, m_post_b, m_a_w_in, m_a_ln_g, m_a_ln_b, m_a_w_s, m_a_b_s, m_a_w_out, m_b_w_in, m_b_conv_w, m_b_conv_b, m_b_ln_g, m_b_ln_b, m_b_w_out, m_c_w_in, m_c_q_g, m_c_k_g, m_c_w_out, v_c_ctx, v_mod_w, v_mod_b, v_post_g, v_post_b, v_a_w_in, v_a_ln_g, v_a_ln_b, v_a_w_s, v_a_b_s, v_a_w_out, v_b_w_in, v_b_conv_w, v_b_conv_b, v_b_ln_g, v_b_ln_b, v_b_w_out, v_c_w_in, v_c_q_g, v_c_k_g, v_c_w_out):
    n_x, d = x.shape[1], x.shape[2]
    n_ctx = ctx.shape[1]
    e = a_w_out.shape[1] * N_DEV
    kvw = N_KV_HEADS * HEAD_DIM
    me = _dev_index(_mesh_pos())
    rt_all = _Rows(n_x, n_ctx)
    rt_x = _Rows(n_x, 0)

    small_in = [c[0], a_ln_g, a_ln_b, b_conv_w[0]]
    (g_small,) = _all_gather([_pack(small_in)], "ag_small_params")
    conds, ln_g_all, ln_b_all, conv_w_all = _unpack_dev(g_small, [a.shape for a in small_in])
    a_ln_g_f = jnp.moveaxis(ln_g_all, 0, 1).reshape(a_ln_g.shape[0], 1, e)
    a_ln_b_f = jnp.moveaxis(ln_b_all, 0, 1).reshape(a_ln_b.shape[0], 1, e)
    conv_w_f = jnp.moveaxis(conv_w_all, 0, 1).reshape(CONV_W, e)
    cond = jnp.zeros((COND_ROWS, d), F32).at[:N_DEV].set(conds).at[N_DEV].set(c_ctx)

    wm = mod_w.shape[2]
    mod_b_mine = lax.dynamic_slice_in_dim(mod_b, me * wm, wm, axis=1).reshape(DEPTH, 1, wm)
    (mods_g,) = _all_gather([_mod_fwd_mm(cond, mod_w, mod_b_mine, "mod_fwd")], "ag_mod")
    mods = jnp.moveaxis(mods_g, 0, 2).reshape(DEPTH, COND_ROWS, 3 * d)
    mine = lax.dynamic_index_in_dim(mods, me, axis=1, keepdims=False)
    modv = jnp.stack([mine, mods[:, N_DEV]], axis=1).reshape(DEPTH * 2 * 3, 1, d)

    def gather_of(wt):
        return _gather_comm([wt.astype(MXU_DTYPE)])

    def exchange_of(*gs):
        return _exchange_comm([g if g.ndim == 3 else g.reshape(N_DEV, -1, g.shape[-1]) for g in gs])

    wa_in0, wa_out0 = _all_gather([a_w_in[0].astype(MXU_DTYPE), a_w_out[0].astype(MXU_DTYPE)], "ag_w_l0")
    wa_out0 = wa_out0.reshape(-1, d)

    ws_op = a_w_s.astype(MXU_DTYPE)
    bs_t = jnp.swapaxes(a_b_s, 1, 2)
    pg = post_g.reshape(DEPTH, 1, d)
    pb = post_b.reshape(DEPTH, 1, d)

    xs0 = jnp.concatenate([x[0], ctx[0]], axis=0)
    h0 = _modulate(xs0, modv, 0, rt_all, "mod0")
    z0, (wb_in,) = _mm_nn(h0, wa_in0, planes=3, name="l0_in", comm=gather_of(b_w_in[0]))
    t0 = _cm_mid_fwd(z0, a_ln_g_f[0], a_ln_b_f[0], ws_op[0], bs_t[0], "l0_mid")
    y0, (wb_out,) = _mm_nn(t0, wa_out0, name="l0_out", comm=gather_of(b_w_out[0]))
    xs1 = _post_fwd(xs0, y0, modv, 0, pg[0], pb[0], rt_all, "l0_post")
    h1 = _modulate(xs1, modv, 1, rt_all, "mod1")
    z1, (wc_in,) = _mm_nn(h1, wb_in, planes=3, name="l1_in", comm=gather_of(c_w_in[0]))
    cy1, (wa_in1,) = _conv_fwd(z1, conv_w_f, b_conv_b, rt_all, "l1_conv", comm=gather_of(a_w_in[1]))
    t1 = _conv_mid_fwd(cy1, z1, b_ln_g, b_ln_b, rt_all, "l1_mid")
    y1, (wc_out,) = _mm_nn(t1, wb_out.reshape(-1, d), name="l1_out", comm=gather_of(c_w_out[0]))
    xs2 = _post_fwd(xs1, y1, modv, 1, pg[1], pb[1], rt_all, "l1_post")
    cos, sin = _rope_tables(n_x, n_ctx)
    h2 = _modulate(xs2, modv, 2, rt_all, "mod2")
    z2 = _mm_nn(h2, wc_in, name="l2_in")
    qh, kh, vh = _qkv_prep(z2, c_q_g, c_k_g, cos, sin, d, kvw, rt_all, "l2_prep")
    (o2, lse), (wa_out1,) = _attn_fwd(qh, kh, vh, n_x, "l2_attn", comm=gather_of(a_w_out[1]))
    wb_out, wc_out, wa_out1 = [wt.reshape(-1, d) for wt in (wb_out, wc_out, wa_out1)]
    t2 = _attn_gate(o2, z2, d, kvw, rt_x, "l2_gate")
    y2 = _mm_nn(t2, wc_out, name="l2_out")
    x2 = xs2
    x3 = _post_fwd(x2, y2, modv, 2, pg[2], pb[2], rt_x, "l2_post")
    h3 = _modulate(x3, modv, 3, rt_x, "mod3")
    z3 = _mm_nn(h3, wa_in1, planes=3, name="l3_in")
    t3 = _cm_mid_fwd(z3, a_ln_g_f[1], a_ln_b_f[1], ws_op[1], bs_t[1], "l3_mid")
    y3 = _mm_nn(t3, wa_out1, name="l3_out")
    x4 = _post_fwd(x3, y3, modv, 3, pg[3], pb[3], rt_x, "l3_post")

    dx4, loss_sum = _loss_head(x4, loss_target[0], rt_x, "loss")
    loss = lax.psum(0.5 * loss_sum[0, 0], ("x", "y", "c"))

    gdt = MXU_DTYPE
    dres3, dy3, dpg3, dpb3, dgate3 = _post_bwd(x3, y3, dx4, modv, 3, pg[3], pb[3], rt_x, "l3_post_b")
    dt3 = _mm_nt(dy3, wa_out1, name="l3_dt")
    gw_a_out1 = _mm_tn(t3, dy3, blocked=False, out_dtype=gdt, name="l3_dwout")
    dz3, dlg3, dlb3, dws3, dbs3 = _cm_mid_bwd(z3, dt3, a_ln_g_f[1], a_ln_b_f[1], ws_op[1], bs_t[1], "l3_mid_b")
    gw_a_in1 = _mm_tn(h3, dz3, blocked=True, out_dtype=gdt, name="l3_dwin")
    dh3, (r_a_out1,) = _mm_nt_blocked(dz3, wa_in1, name="l3_dh", comm=exchange_of(gw_a_out1))
    dx3, dshift3, dscale3 = _mod_bwd(dres3, dh3, x3, modv, 3, rt_x, "l3_mod_b")
    dres2, dy2, dpg2, dpb2, dgate2 = _post_bwd(x2, y2, dx3, modv, 2, pg[2], pb[2], rt_x, "l2_post_b")
    dt2 = _mm_nt(dy2, wc_out, name="l2_dt")
    gw_c_out = _mm_tn(t2, dy2, blocked=False, out_dtype=gdt, name="l2_dwout")
    do2, dg2 = _attn_gate_bwd(dt2, o2, z2, d, kvw, rt_x, "l2_gate_b")
    (dqh, dkh, dvh), (r_a_in1,) = _attn_bwd(qh, kh, vh, do2, lse, n_x, "l2_attn_b", comm=exchange_of(gw_a_in1))
    dq2, dqg = _prep_bwd(dqh, z2, 0, c_q_g, cos, sin, rt_x, "l2_qprep_b")
    dk2, dkg = _prep_bwd(dkh, z2, d // kvw, c_k_g, cos, sin, rt_all, "l2_kprep_b")
    zpad = jnp.zeros((n_ctx, d), MXU_DTYPE)
    dz2 = jnp.concatenate([jnp.concatenate([dq2, zpad], axis=0), dk2, dvh.astype(MXU_DTYPE),
                           jnp.concatenate([dg2, zpad], axis=0)], axis=1)
    gw_c_in = _mm_tn(h2, dz2, blocked=True, out_dtype=gdt, name="l2_dwin")
    dh2, (r_c_out,) = _mm_nt_blocked(dz2, wc_in, name="l2_dh", comm=exchange_of(gw_c_out))
    dxs2, dshift2, dscale2 = _mod_bwd(dres2, dh2, xs2, modv, 2, rt_all, "l2_mod_b")
    dres1, dy1, dpg1, dpb1, dgate1 = _post_bwd(xs1, y1, dxs2, modv, 1, pg[1], pb[1], rt_all, "l1_post_b")
    dt1 = _mm_nt(dy1, wb_out, name="l1_dt")
    gw_b_out = _mm_tn(t1, dy1, blocked=False, out_dtype=gdt, name="l1_dwout")
    dcy1, dgc1, dblg, dblb = _conv_mid_bwd(cy1, z1, dt1, b_ln_g, b_ln_b, rt_all, "l1_mid_b")
    (dz1, dconv_w, dconv_b), (r_c_in, r_b_out) = _conv_bwd(z1, dcy1, dgc1, conv_w_f, rt_all, "l1_conv_b",
                                                           comm=exchange_of(gw_c_in, gw_b_out))
    gw_b_in = _mm_tn(h1, dz1, blocked=True, out_dtype=gdt, name="l1_dwin")
    dh1, (r_b_in,) = _mm_nt_blocked(dz1, wb_in, name="l1_dh", comm=exchange_of(gw_b_in))
    dxs1, dshift1, dscale1 = _mod_bwd(dres1, dh1, xs1, modv, 1, rt_all, "l1_mod_b")
    dres0, dy0, dpg0, dpb0, dgate0 = _post_bwd(xs0, y0, dxs1, modv, 0, pg[0], pb[0], rt_all, "l0_post_b")
    dt0 = _mm_nt(dy0, wa_out0, name="l0_dt")
    gw_a_out0 = _mm_tn(t0, dy0, blocked=False, out_dtype=gdt, name="l0_dwout")
    (dz0, dlg0, dlb0, dws0, dbs0), (r_a_out0,) = _cm_mid_bwd(z0, dt0, a_ln_g_f[0], a_ln_b_f[0], ws_op[0], bs_t[0], "l0_mid_b",
                                                             comm=exchange_of(gw_a_out0))
    gw_a_in0 = _mm_tn(h0, dz0, blocked=True, out_dtype=gdt, name="l0_dwin")
    dh0, (r_a_in0,) = _mm_nt_blocked(dz0, wa_in0, name="l0_dh", comm=exchange_of(gw_a_in0))
    dx0, dshift0, dscale0 = _mod_bwd(dres0, dh0, xs0, modv, 0, rt_all, "l0_mod_b", dx_rows=n_x)
    grad_x = dx0[None]

    def seg2(a):
        a = a[:, 0]
        return a if a.shape[0] == 2 else jnp.concatenate([a, jnp.zeros_like(a)], axis=0)

    gate2 = jnp.concatenate([dgate2[:, 0], jnp.zeros((1, d), F32)], axis=0)
    dmod = jnp.stack([
        jnp.concatenate([seg2(dshift0), seg2(dscale0), seg2(dgate0)], axis=1),
        jnp.concatenate([seg2(dshift1), seg2(dscale1), seg2(dgate1)], axis=1),
        jnp.concatenate([seg2(dshift2), seg2(dscale2), gate2], axis=1),
        jnp.concatenate([seg2(dshift3), seg2(dscale3), seg2(dgate3)], axis=1)])

    g_post_g = jnp.concatenate([dpg0, dpg1, dpg2, dpg3], axis=0)
    g_post_b = jnp.concatenate([dpb0, dpb1, dpb2, dpb3], axis=0)
    g_a_ln_g = jnp.concatenate([dlg0, dlg3], axis=0)
    g_a_ln_b = jnp.concatenate([dlb0, dlb3], axis=0)
    g_a_w_s = jnp.stack([dws0, dws3])
    g_a_b_s = jnp.swapaxes(jnp.stack([dbs0, dbs3]), 1, 2)
    small_g = [g_post_g, g_post_b, g_a_w_s, g_a_b_s, dconv_b, dblg, dblb, dqg, dkg, g_a_ln_g, g_a_ln_b, dconv_w,
               dmod[:, 0], dmod[:, 1]]
    small_shapes = [a.shape for a in small_g]
    (gs_all,) = _all_gather([_pack(small_g)], "ag_small_grads")
    sums = _unpack(_sum_devices(gs_all, "sum_small"), small_shapes)
    (s_post_g, s_post_b, s_a_w_s, s_a_b_s, s_conv_b, s_b_ln_g, s_b_ln_b, s_q_g, s_k_g, s_a_ln_g, s_a_ln_b, s_conv_w,
     s_dmod_own, s_dmod_ctx) = sums
    grad_mod_b = s_dmod_own + s_dmod_ctx
    wl = a_ln_g.shape[1]
    wcv = b_conv_w.shape[2]
    grad_a_ln_g = lax.dynamic_slice_in_dim(s_a_ln_g, me * wl, wl, axis=1)
    grad_a_ln_b = lax.dynamic_slice_in_dim(s_a_ln_b, me * wl, wl, axis=1)
    grad_b_conv_w = lax.dynamic_slice_in_dim(s_conv_w, me * wcv, wcv, axis=1)[None]

    dmod_dev = _unpack_dev(gs_all, small_shapes)[12]
    dm_rows = jnp.concatenate([jnp.moveaxis(dmod_dev, 0, 1), s_dmod_ctx[:, None],
                               jnp.zeros((DEPTH, COND_ROWS - N_DEV - 1, 3 * d), F32)], axis=1)
    dm_mine = lax.dynamic_slice_in_dim(dm_rows, me * wm, wm, axis=2)
    grad_mod_w, dcond_part = _mod_bwd_mm(cond, dm_mine, mod_w, "mod_bwd")
    (dcond_all,) = _all_gather([dcond_part], "ag_dcond")
    dcond = _sum_devices(dcond_all, "sum_dcond")
    grad_c_ctx = dcond[N_DEV] * _dsilu(c_ctx)

    o_a_w_in = _adam_reduce([r_a_in0, r_a_in1], a_w_in, m_a_w_in, v_a_w_in, "adam_a_in")
    o_a_w_out = _adam_reduce([r_a_out0, r_a_out1], a_w_out, m_a_w_out, v_a_w_out, "adam_a_out")
    o_b_w_in = _adam_reduce([r_b_in], b_w_in, m_b_w_in, v_b_w_in, "adam_b_in")
    o_b_w_out = _adam_reduce([r_b_out], b_w_out, m_b_w_out, v_b_w_out, "adam_b_out")
    o_c_w_in = _adam_reduce([r_c_in], c_w_in, m_c_w_in, v_c_w_in, "adam_c_in")
    o_c_w_out = _adam_reduce([r_c_out], c_w_out, m_c_w_out, v_c_w_out, "adam_c_out")
    mw_shape = mod_w.shape
    o_mod_w = [grad_mod_w] + [a.reshape(mw_shape) for a in _adam_plain(
        grad_mod_w.reshape(-1, wm), mod_w.reshape(-1, wm), m_mod_w.reshape(-1, wm), v_mod_w.reshape(-1, wm), "adam_mod_w")]

    sg = [grad_c_ctx, grad_mod_b, s_post_g, s_post_b, grad_a_ln_g, grad_a_ln_b, s_a_w_s, s_a_b_s, grad_b_conv_w, s_conv_b,
          s_b_ln_g, s_b_ln_b, s_q_g, s_k_g]
    sw = [c_ctx, mod_b, post_g, post_b, a_ln_g, a_ln_b, a_w_s, a_b_s, b_conv_w, b_conv_b, b_ln_g, b_ln_b, c_q_g, c_k_g]
    sm = [m_c_ctx, m_mod_b, m_post_g, m_post_b, m_a_ln_g, m_a_ln_b, m_a_w_s, m_a_b_s, m_b_conv_w, m_b_conv_b, m_b_ln_g,
          m_b_ln_b, m_c_q_g, m_c_k_g]
    sv = [v_c_ctx, v_mod_b, v_post_g, v_post_b, v_a_ln_g, v_a_ln_b, v_a_w_s, v_a_b_s, v_b_conv_w, v_b_conv_b, v_b_ln_g,
          v_b_ln_b, v_c_q_g, v_c_k_g]
    shapes = [a.shape for a in sw]
    sg = [g.reshape(s) for g, s in zip(sg, shapes)]
    sd, snm, snv = [_unpack(a, shapes) for a in _adam_plain(_pack(sg), _pack(sw), _pack(sm), _pack(sv), "adam_small")]

    def small(k):
        return [sg[k], sd[k], snm[k], snv[k]]

    per_weight = [small(0), o_mod_w, small(1), small(2), small(3), o_a_w_in, small(4), small(5), small(6), small(7),
                  o_a_w_out, o_b_w_in, small(8), small(9), small(10), small(11), o_b_w_out, o_c_w_in, small(12), small(13),
                  o_c_w_out]
    outs = [loss, grad_x]
    for kind in range(4):
        outs += [pw[kind] for pw in per_weight]
    return tuple(outs)
```

```python
import functools
import math

import jax
import jax.numpy as jnp
from jax import lax
from jax.experimental import pallas as pl
from jax.experimental.pallas import tpu as pltpu

F32 = jnp.float32
BF16 = jnp.bfloat16
MXU_DTYPE = jnp.bfloat16

DEPTH = 4
GRID_W = 64
CHUNK = 128
SGU_GROUPS = 16
CONV_W = 31
CONV_HALO = 16
HEAD_DIM = 128
N_KV_HEADS = 4
ROPE_THETA = 10000.0
DEEPNORM_ALPHA = (2 * DEPTH) ** 0.25
LN_EPS = 1e-6
ADAM_LR, ADAM_B1, ADAM_B2, ADAM_EPS, ADAM_WD, ADAM_STEP = 0.001, 0.9, 0.999, 1e-08, 0.01, 10

N_DEV = 8
V7X_VMEM_BYTES = 64 * 1024 * 1024
V7X_VMEM_CLAIM = V7X_VMEM_BYTES * 7 // 8
LANES = 128

NN = ((1,), (0,))
NT = ((1,), (1,))
TN = ((0,), (0,))


def _dot(a, b, dims):
    return lax.dot_general(a.astype(MXU_DTYPE), b.astype(MXU_DTYPE), (dims, ((), ())), preferred_element_type=F32)


def _pick(n, prefs):
    for p in prefs:
        if n % p == 0:
            return p
    raise ValueError(f"no tile for {n} among {prefs}")


def _params(sem, vmem_bytes):
    assert vmem_bytes <= V7X_VMEM_CLAIM, (vmem_bytes, V7X_VMEM_CLAIM)
    return pltpu.CompilerParams(dimension_semantics=sem, vmem_limit_bytes=V7X_VMEM_CLAIM)


def _nbytes(shape, dtype):
    return math.prod(shape) * jnp.dtype(dtype).itemsize


def _sigmoid(x):
    return jax.nn.sigmoid(x)


def _silu(x):
    return x * jax.nn.sigmoid(x)


def _dsilu(x):
    s = jax.nn.sigmoid(x)
    return s * (1.0 + x * (1.0 - s))


def _ln_stats(x):
    mu = jnp.mean(x, axis=-1, keepdims=True)
    xc = x - mu
    var = jnp.mean(xc * xc, axis=-1, keepdims=True)
    return xc, lax.rsqrt(var + LN_EPS)


def _ln(x, g, b):
    xc, rstd = _ln_stats(x)
    return xc * rstd * g + b


def _mesh_pos():
    return lax.axis_index("x"), lax.axis_index("y"), lax.axis_index("c")


def _dev_index(p):
    return 4 * p[0] + 2 * p[1] + p[2]


class _Comm:
    def __init__(self, inputs, out_shapes, sems, start, finish):
        self.inputs, self.out_shapes, self.sems, self.start, self.finish = inputs, out_shapes, sems, start, finish


def _gather_comm(xs):
    n = len(xs)

    def place():
        x, y, c = _mesh_pos()
        return (x, y, c), (x, y, 1 - c), [(1 - x, y), (x, 1 - y), (1 - x, 1 - y)], c

    def copier(x_refs, o_refs, sems):
        send_sems, recv_sems, _ = sems

        def copy(t, k, block, to, from_input=False):
            dst = o_refs[t].at[_dev_index(block)]
            return pltpu.make_async_remote_copy(
                src_ref=x_refs[t] if from_input else dst, dst_ref=dst,
                send_sem=send_sems.at[t, k], recv_sem=recv_sems.at[t, k],
                device_id=to, device_id_type=pl.DeviceIdType.MESH)

        return copy

    def own(x_refs, o_refs, sems, t, me):
        return pltpu.make_async_copy(x_refs[t], o_refs[t].at[_dev_index(me)], sems[2].at[t])

    def first_copies(copy, t, me, sibling, chips, c):
        return [copy(t, 0, me, sibling, True)] + [copy(t, 1 + j, me, (*chip, c), True) for j, chip in enumerate(chips)]

    def start(x_refs, o_refs, sems):
        me, sibling, chips, c = place()
        copy = copier(x_refs, o_refs, sems)
        for t in range(n):
            own(x_refs, o_refs, sems, t, me).start()
            for cp in first_copies(copy, t, me, sibling, chips, c):
                cp.start()

    def finish(x_refs, o_refs, sems):
        me, sibling, chips, c = place()
        copy = copier(x_refs, o_refs, sems)
        passed = []
        for t in range(n):
            for j, chip in enumerate(chips):
                copy(t, 1 + j, (*chip, c), me).wait_recv()
                cp = copy(t, 4 + j, (*chip, c), sibling)
                cp.start()
                passed.append(cp)
        for t in range(n):
            copy(t, 0, sibling, me).wait_recv()
            for j, chip in enumerate(chips):
                copy(t, 4 + j, (*chip, 1 - c), me).wait_recv()
        for t in range(n):
            for cp in first_copies(copy, t, me, sibling, chips, c):
                cp.wait_send()
        for cp in passed:
            cp.wait_send()
        for t in range(n):
            own(x_refs, o_refs, sems, t, me).wait()

    sems = [pltpu.SemaphoreType.DMA((n, 7)), pltpu.SemaphoreType.DMA((n, 7)), pltpu.SemaphoreType.DMA((n,))]
    return _Comm(list(xs), [jax.ShapeDtypeStruct((N_DEV, *a.shape), a.dtype) for a in xs], sems, start, finish)


def _exchange_comm(gs):
    n = len(gs)

    def copies(g_refs, r_refs, sems):
        send_sems, recv_sems, local_sems = sems
        x, y, c = _mesh_pos()
        me = _dev_index((x, y, c))
        out = []
        for t in range(n):
            out.append(pltpu.make_async_copy(g_refs[t].at[me], r_refs[t].at[me], local_sems.at[t]))
            for k in range(1, N_DEV):
                fx, fy, fc = (k >> 2) & 1, (k >> 1) & 1, k & 1
                peer = (1 - x if fx else x, 1 - y if fy else y, 1 - c if fc else c)
                out.append(pltpu.make_async_remote_copy(
                    src_ref=g_refs[t].at[_dev_index(peer)], dst_ref=r_refs[t].at[me],
                    send_sem=send_sems.at[t, k - 1], recv_sem=recv_sems.at[t, k - 1],
                    device_id=peer, device_id_type=pl.DeviceIdType.MESH))
        return out

    def start(g_refs, r_refs, sems):
        for cp in copies(g_refs, r_refs, sems):
            cp.start()

    def finish(g_refs, r_refs, sems):
        for cp in copies(g_refs, r_refs, sems):
            cp.wait()

    sems = [pltpu.SemaphoreType.DMA((n, 7)), pltpu.SemaphoreType.DMA((n, 7)), pltpu.SemaphoreType.DMA((n,))]
    return _Comm(list(gs), [jax.ShapeDtypeStruct(g.shape, g.dtype) for g in gs], sems, start, finish)


N_CHIPS = N_DEV // 2


def _started_and_waited(copies):
    def start(in_refs, out_refs, sems):
        for cp in copies(in_refs, out_refs, sems):
            cp.start()

    def finish(in_refs, out_refs, sems):
        for cp in copies(in_refs, out_refs, sems):
            cp.wait()

    return start, finish


def _pair_exchange_comm(gs):
    n = len(gs)

    def copies(g_refs, r_refs, sems):
        send_sems, recv_sems = sems
        x, y, c = _mesh_pos()
        return [pltpu.make_async_remote_copy(
            src_ref=g_refs[t].at[2 * q + 1 - c], dst_ref=r_refs[t].at[q],
            send_sem=send_sems.at[t, q], recv_sem=recv_sems.at[t, q],
            device_id=(x, y, 1 - c), device_id_type=pl.DeviceIdType.MESH) for t in range(n) for q in range(N_CHIPS)]

    sems = [pltpu.SemaphoreType.DMA((n, N_CHIPS)), pltpu.SemaphoreType.DMA((n, N_CHIPS))]
    return _Comm(list(gs), [jax.ShapeDtypeStruct((N_CHIPS, *g.shape[1:]), g.dtype) for g in gs], sems,
                 *_started_and_waited(copies))


def _chip_exchange_comm(hs):
    n = len(hs)

    def copies(h_refs, r_refs, sems):
        send_sems, recv_sems, local_sems = sems
        x, y, c = _mesh_pos()
        mine = 2 * x + y
        out = []
        for t in range(n):
            out.append(pltpu.make_async_copy(h_refs[t].at[mine], r_refs[t].at[mine], local_sems.at[t]))
            for k in range(1, N_CHIPS):
                px, py = (1 - x if (k >> 1) & 1 else x), (1 - y if k & 1 else y)
                out.append(pltpu.make_async_remote_copy(
                    src_ref=h_refs[t].at[2 * px + py], dst_ref=r_refs[t].at[mine],
                    send_sem=send_sems.at[t, k - 1], recv_sem=recv_sems.at[t, k - 1],
                    device_id=(px, py, c), device_id_type=pl.DeviceIdType.MESH))
        return out

    sems = [pltpu.SemaphoreType.DMA((n, N_CHIPS - 1)), pltpu.SemaphoreType.DMA((n, N_CHIPS - 1)), pltpu.SemaphoreType.DMA((n,))]
    return _Comm(list(hs), [jax.ShapeDtypeStruct(h.shape, h.dtype) for h in hs], sems, *_started_and_waited(copies))


def _pair_add(mine, theirs, name):
    q, rows, cols = mine.shape
    tr = _adam_tile(rows, cols)

    def body(a_ref, b_ref, o_ref):
        o_ref[...] = (a_ref[...].astype(F32) + b_ref[...].astype(F32)).astype(o_ref.dtype)

    blk = pl.BlockSpec((None, tr, cols), lambda s, i: (s, i, 0))
    return pl.pallas_call(
        body, name=name, grid=(q, rows // tr), in_specs=[blk, blk], out_specs=blk,
        out_shape=jax.ShapeDtypeStruct(mine.shape, mine.dtype),
        compiler_params=_params(("parallel", "parallel"), 16 * _nbytes((tr, cols), F32)),
    )(mine, theirs)


def _comm_call(comm, name):
    n_in, n_out = len(comm.inputs), len(comm.out_shapes)

    def body(*refs):
        ins, outs, sems = refs[:n_in], refs[n_in:n_in + n_out], refs[n_in + n_out:]
        comm.start(ins, outs, sems)
        comm.finish(ins, outs, sems)

    hbm = pl.BlockSpec(memory_space=pl.ANY)
    return pl.pallas_call(
        body, name=name, out_shape=comm.out_shapes, in_specs=[hbm] * n_in, out_specs=[hbm] * n_out,
        scratch_shapes=comm.sems)(*comm.inputs)


def _call(body, operands, *, name, grid, in_specs, out_specs, out_shape, scratch_shapes=(), compiler_params, comm=None):
    single = not isinstance(out_shape, (list, tuple))
    out_shape = [out_shape] if single else list(out_shape)
    out_specs = [out_specs] if single else list(out_specs)
    scratch_shapes = list(scratch_shapes)
    if comm is None:
        res = pl.pallas_call(
            body, name=name, grid=grid, in_specs=list(in_specs), out_specs=out_specs, out_shape=out_shape,
            scratch_shapes=scratch_shapes, compiler_params=compiler_params)(*operands)
        return res[0] if single else res
    n_in, n_out, n_scr = len(in_specs), len(out_specs), len(scratch_shapes)
    c_in, c_out = len(comm.inputs), len(comm.out_shapes)

    def with_comm(*refs):
        ins, c_ins = refs[:n_in], refs[n_in:n_in + c_in]
        o0 = n_in + c_in
        outs, c_outs = refs[o0:o0 + n_out], refs[o0 + n_out:o0 + n_out + c_out]
        s0 = o0 + n_out + c_out
        scr, sems = refs[s0:s0 + n_scr], refs[s0 + n_scr:]
        ids = [pl.program_id(a) for a in range(len(grid))]
        first = functools.reduce(jnp.logical_and, [i == 0 for i in ids])
        last = functools.reduce(jnp.logical_and, [i == g - 1 for i, g in zip(ids, grid)])

        @pl.when(first)
        def _():
            comm.start(c_ins, c_outs, sems)

        body(*ins, *outs, *scr)

        @pl.when(last)
        def _():
            comm.finish(c_ins, c_outs, sems)

    hbm = pl.BlockSpec(memory_space=pl.ANY)
    params = pltpu.CompilerParams(dimension_semantics=("arbitrary",) * len(grid),
                                  vmem_limit_bytes=compiler_params.vmem_limit_bytes)
    res = pl.pallas_call(
        with_comm, name=name, grid=grid, in_specs=list(in_specs) + [hbm] * c_in, out_specs=out_specs + [hbm] * c_out,
        out_shape=out_shape + list(comm.out_shapes), scratch_shapes=scratch_shapes + list(comm.sems),
        compiler_params=params)(*operands, *comm.inputs)
    return (res[0] if single else res[:n_out]), res[n_out:]


def _all_gather(xs, name):
    return _comm_call(_gather_comm(xs), name)


ROW_TILES = (1088, 1024, 768, 544, 512, 384, 272, 256, 128)
TOKEN_K_TILES = (2176, 2048, 1088, 1024, 768, 512, 384, 256, 128)
COL_TILES = (1024, 768, 640, 512, 384, 256, 128)
DEEP_K = 2048


def _mm_nn(a, w, *, planes=1, name, comm=None):
    m, k = a.shape
    if w.ndim == 3:
        nd_w = w.shape[2]
        n = w.shape[0] * nd_w
    else:
        nd_w = n = w.shape[1]
    npl = n // planes
    tm = _pick(m, ROW_TILES)
    tn = _pick(math.gcd(nd_w, npl), COL_TILES if k <= DEEP_K else COL_TILES[3:])
    r, rp = nd_w // tn, npl // tn
    if w.ndim == 3:
        w_spec = pl.BlockSpec((None, k, tn), lambda i, j: (j // r, 0, j % r))
    else:
        w_spec = pl.BlockSpec((k, tn), lambda i, j: (0, j))
    if planes > 1:
        o_spec = pl.BlockSpec((None, tm, tn), lambda i, j: (j // rp, i, j % rp))
        out_shape = jax.ShapeDtypeStruct((planes, m, npl), F32)
    else:
        o_spec = pl.BlockSpec((tm, tn), lambda i, j: (i, j))
        out_shape = jax.ShapeDtypeStruct((m, n), F32)

    def body(a_ref, w_ref, o_ref):
        o_ref[...] = _dot(a_ref[...], w_ref[...], NN)

    vmem = 2 * (_nbytes((tm, k), a.dtype) + _nbytes((k, tn), w.dtype) + _nbytes((tm, tn), F32)) + _nbytes((tm, tn), F32)
    return _call(
        body, (a, w), name=name, grid=(m // tm, n // tn),
        in_specs=[pl.BlockSpec((tm, k), lambda i, j: (i, 0)), w_spec], out_specs=o_spec, out_shape=out_shape,
        compiler_params=_params(("parallel", "arbitrary"), vmem), comm=comm)


def _mm_nt(a, w, *, name):
    m, k = a.shape
    n = w.shape[0]
    tm = _pick(m, ROW_TILES)
    tn = _pick(n, COL_TILES)

    def body(a_ref, w_ref, o_ref):
        o_ref[...] = _dot(a_ref[...], w_ref[...], NT)

    vmem = 2 * (_nbytes((tm, k), a.dtype) + _nbytes((tn, k), w.dtype) + _nbytes((tm, tn), F32)) + _nbytes((tm, tn), F32)
    return pl.pallas_call(
        body, name=name, grid=(m // tm, n // tn),
        in_specs=[pl.BlockSpec((tm, k), lambda i, j: (i, 0)), pl.BlockSpec((tn, k), lambda i, j: (j, 0))],
        out_specs=pl.BlockSpec((tm, tn), lambda i, j: (i, j)), out_shape=jax.ShapeDtypeStruct((m, n), F32),
        compiler_params=_params(("parallel", "arbitrary"), vmem),
    )(a, w)


def _mm_nt_blocked(a, w, *, name, comm=None):
    nd, n, kd = w.shape
    if a.ndim == 3:
        p, m, kp = a.shape
    else:
        (m, kp), p = a.shape, 1
    tk = _pick(math.gcd(kd, kp), COL_TILES)
    ra, rw = kp // tk, kd // tk
    nk = nd * rw
    tm = _pick(m, ROW_TILES)
    if a.ndim == 3:
        a_spec = pl.BlockSpec((None, tm, tk), lambda i, kk: (kk // ra, i, kk % ra))
    else:
        a_spec = pl.BlockSpec((tm, tk), lambda i, kk: (i, kk))

    def body(a_ref, w_ref, o_ref, acc_ref):
        kk = pl.program_id(1)

        @pl.when(kk == 0)
        def _():
            acc_ref[...] = jnp.zeros_like(acc_ref)

        acc_ref[...] += _dot(a_ref[...], w_ref[...], NT)

        @pl.when(kk == nk - 1)
        def _():
            o_ref[...] = acc_ref[...]

    vmem = 2 * (_nbytes((tm, tk), a.dtype) + _nbytes((n, tk), w.dtype) + _nbytes((tm, n), F32)) + 2 * _nbytes((tm, n), F32)
    return _call(
        body, (a, w), name=name, grid=(m // tm, nk),
        in_specs=[a_spec, pl.BlockSpec((None, n, tk), lambda i, kk: (kk // rw, 0, kk % rw))],
        out_specs=pl.BlockSpec((tm, n), lambda i, kk: (i, 0)), out_shape=jax.ShapeDtypeStruct((m, n), F32),
        scratch_shapes=[pltpu.VMEM((tm, n), F32)],
        compiler_params=_params(("parallel", "arbitrary"), vmem), comm=comm)


def _mm_tn(a, b, *, blocked, out_dtype, name, comm=None):
    rows, da = a.shape
    if b.ndim == 3:
        p, _, npl = b.shape
    else:
        p, npl = 1, b.shape[1]
    n = p * npl
    nd_w = n // N_DEV if blocked else n
    tk = _pick(rows, TOKEN_K_TILES)
    tm = _pick(da, COL_TILES)
    tn = _pick(math.gcd(nd_w, npl), COL_TILES)
    rb, ro = npl // tn, nd_w // tn
    nk = rows // tk
    if b.ndim == 3:
        b_spec = pl.BlockSpec((None, tk, tn), lambda i, j, kk: (j // rb, kk, j % rb))
    else:
        b_spec = pl.BlockSpec((tk, tn), lambda i, j, kk: (kk, j))
    if blocked:
        o_spec = pl.BlockSpec((None, tm, tn), lambda i, j, kk: (j // ro, i, j % ro))
        out_shape = jax.ShapeDtypeStruct((N_DEV, da, nd_w), out_dtype)
    else:
        o_spec = pl.BlockSpec((tm, tn), lambda i, j, kk: (i, j))
        out_shape = jax.ShapeDtypeStruct((da, n), out_dtype)

    def body(a_ref, b_ref, o_ref, acc_ref):
        kk = pl.program_id(2)

        @pl.when(kk == 0)
        def _():
            acc_ref[...] = jnp.zeros_like(acc_ref)

        acc_ref[...] += _dot(a_ref[...], b_ref[...], TN)

        @pl.when(kk == nk - 1)
        def _():
            o_ref[...] = acc_ref[...].astype(o_ref.dtype)

    vmem = (2 * (_nbytes((tk, tm), a.dtype) + _nbytes((tk, tn), b.dtype) + _nbytes((tm, tn), out_dtype))
            + 3 * _nbytes((tm, tn), F32) + _nbytes((tk, tm), F32))
    return _call(
        body, (a, b), name=name, grid=(da // tm, n // tn, nk),
        in_specs=[pl.BlockSpec((tk, tm), lambda i, j, kk: (kk, i)), b_spec], out_specs=o_spec, out_shape=out_shape,
        scratch_shapes=[pltpu.VMEM((tm, tn), F32)],
        compiler_params=_params(("parallel", "parallel", "arbitrary"), vmem), comm=comm)


ROW_TILE = 256


class _Rows:
    def __init__(self, n_x, n_ctx, tile=ROW_TILE):
        assert n_x % tile == 0 and n_ctx % tile == 0
        self.n_x, self.n_ctx, self.tile = n_x, n_ctx, tile
        self.rows = n_x + n_ctx
        self.nt_x = n_x // tile
        self.nt = self.rows // tile
        self.n_seg = 2 if n_ctx else 1

    def seg(self, i):
        return jnp.where(i >= self.nt_x, 1, 0) if self.n_ctx else 0

    def first_of_seg(self, i):
        return (i == 0) | (i == self.nt_x) if self.n_ctx else i == 0

    def full(self, width):
        return pl.BlockSpec((self.tile, width), lambda i: (i, 0))

    def plane(self, p, width):
        return pl.BlockSpec((None, self.tile, width), lambda i: (p, i, 0))

    def modvec(self, layer, which, width):
        return pl.BlockSpec((None, 1, width), lambda i: ((layer * 2 + self.seg(i)) * 3 + which, 0, 0))

    def seg_acc(self, width):
        return pl.BlockSpec((None, 1, width), lambda i: (self.seg(i), 0, 0))


def _vec(width):
    return pl.BlockSpec((1, width), lambda i: (0, 0))


def _acc(ref, first, val):
    @pl.when(first)
    def _():
        ref[...] = jnp.zeros_like(ref)

    ref[...] += val


def _modulate(xs, modv, layer, rt, name):
    d = xs.shape[1]

    def body(x_ref, sh_ref, sc_ref, o_ref):
        o_ref[...] = (x_ref[...] * (1.0 + sc_ref[...]) + sh_ref[...]).astype(o_ref.dtype)

    return pl.pallas_call(
        body, name=name, grid=(rt.nt,),
        in_specs=[rt.full(d), rt.modvec(layer, 0, d), rt.modvec(layer, 1, d)],
        out_specs=rt.full(d), out_shape=jax.ShapeDtypeStruct(xs.shape, MXU_DTYPE),
        compiler_params=_params(("parallel",), 6 * _nbytes((rt.tile, d), F32)),
    )(xs, modv, modv)


def _post(x, y, gate, pg, pb):
    return _ln(DEEPNORM_ALPHA * x + gate * y, pg, pb)


def _post_fwd(xs, y, modv, layer, pg, pb, rt, name):
    d = xs.shape[1]

    def body(x_ref, y_ref, gate_ref, pg_ref, pb_ref, o_ref):
        o_ref[...] = _post(x_ref[...], y_ref[...], gate_ref[...], pg_ref[...], pb_ref[...])

    return pl.pallas_call(
        body, name=name, grid=(rt.nt,),
        in_specs=[rt.full(d), rt.full(d), rt.modvec(layer, 2, d), _vec(d), _vec(d)],
        out_specs=rt.full(d), out_shape=jax.ShapeDtypeStruct((rt.rows, d), F32),
        compiler_params=_params(("parallel",), 10 * _nbytes((rt.tile, d), F32)),
    )(xs, y, modv, pg, pb)


def _post_bwd(xs, y, dout, modv, layer, pg, pb, rt, name):
    d = xs.shape[1]

    def body(x_ref, y_ref, do_ref, gate_ref, pg_ref, pb_ref, dres_ref, dy_ref, dpg_ref, dpb_ref, dgate_ref):
        i = pl.program_id(0)
        _, vjp = jax.vjp(_post, x_ref[...], y_ref[...], gate_ref[...], pg_ref[...], pb_ref[...])
        dx, dy, dgate, dpg, dpb = vjp(do_ref[...])
        dres_ref[...] = dx
        dy_ref[...] = dy.astype(dy_ref.dtype)
        _acc(dpg_ref, i == 0, dpg)
        _acc(dpb_ref, i == 0, dpb)
        _acc(dgate_ref, rt.first_of_seg(i), dgate)

    return pl.pallas_call(
        body, name=name, grid=(rt.nt,),
        in_specs=[rt.full(d), rt.full(d), rt.full(d), rt.modvec(layer, 2, d), _vec(d), _vec(d)],
        out_specs=[rt.full(d), rt.full(d), _vec(d), _vec(d), rt.seg_acc(d)],
        out_shape=[jax.ShapeDtypeStruct((rt.rows, d), F32), jax.ShapeDtypeStruct((rt.rows, d), MXU_DTYPE),
                   jax.ShapeDtypeStruct((1, d), F32), jax.ShapeDtypeStruct((1, d), F32),
                   jax.ShapeDtypeStruct((rt.n_seg, 1, d), F32)],
        compiler_params=_params(("arbitrary",), 16 * _nbytes((rt.tile, d), F32)),
    )(xs, y, dout, modv, pg, pb)


def _mod_bwd(dres, dh, xs, modv, layer, rt, name, dx_rows=None):
    d = xs.shape[1]
    nt_res = dres.shape[0] // rt.tile
    nt_dx = rt.nt if dx_rows is None else dx_rows // rt.tile

    def body(dres_ref, dh_ref, x_ref, sc_ref, dx_ref, dshift_ref, dscale_ref):
        i = pl.program_id(0)
        dh = dh_ref[...]

        @pl.when(i < nt_dx)
        def _():
            dx_ref[...] = jnp.where(i < nt_res, dres_ref[...], 0.0) + dh * (1.0 + sc_ref[...])

        first = rt.first_of_seg(i)
        _acc(dshift_ref, first, jnp.sum(dh, axis=0, keepdims=True))
        _acc(dscale_ref, first, jnp.sum(dh * x_ref[...], axis=0, keepdims=True))

    def clamped(nt):
        return pl.BlockSpec((rt.tile, d), lambda i: (jnp.minimum(i, nt - 1), 0))

    return pl.pallas_call(
        body, name=name, grid=(rt.nt,),
        in_specs=[clamped(nt_res), rt.full(d), rt.full(d), rt.modvec(layer, 1, d)],
        out_specs=[clamped(nt_dx), rt.seg_acc(d), rt.seg_acc(d)],
        out_shape=[jax.ShapeDtypeStruct((nt_dx * rt.tile, d), F32), jax.ShapeDtypeStruct((rt.n_seg, 1, d), F32),
                   jax.ShapeDtypeStruct((rt.n_seg, 1, d), F32)],
        compiler_params=_params(("arbitrary",), 10 * _nbytes((rt.tile, d), F32)),
    )(dres, dh, xs, modv)


def _cm_mid_fwd(z3, ln_g, ln_b, w_s, b_s_t, name, comm=None):
    _, rows, e = z3.shape
    groups = w_s.shape[0]
    gw = e // groups

    def body(z_ref, lg_ref, lb_ref, ws_ref, bs_ref, t_ref):
        vn = _ln(z_ref[1], lg_ref[...], lb_ref[...])
        for h in range(groups):
            cols = slice(h * gw, (h + 1) * gw)
            s = _dot(ws_ref[h], vn[:, cols], NN) + bs_ref[:, h:h + 1]
            t_ref[:, cols] = (z_ref[0, :, cols] * s * _silu(z_ref[2, :, cols])).astype(t_ref.dtype)

    return _call(
        body, (z3, ln_g, ln_b, w_s, b_s_t), name=name, grid=(rows // CHUNK,),
        in_specs=[pl.BlockSpec((3, CHUNK, e), lambda i: (0, i, 0)), _vec(e), _vec(e),
                  pl.BlockSpec(w_s.shape, lambda i: (0, 0, 0)), pl.BlockSpec(b_s_t.shape, lambda i: (0, 0))],
        out_specs=pl.BlockSpec((CHUNK, e), lambda i: (i, 0)), out_shape=jax.ShapeDtypeStruct((rows, e), MXU_DTYPE),
        compiler_params=_params(("parallel",), 12 * _nbytes((CHUNK, e), F32)), comm=comm)


def _cm_mid_bwd(z3, dt, ln_g, ln_b, w_s, b_s_t, name, comm=None):
    _, rows, e = z3.shape
    groups = w_s.shape[0]
    gw = e // groups

    def body(z_ref, dt_ref, lg_ref, lb_ref, ws_ref, bs_ref, dz_ref, dlg_ref, dlb_ref, dws_ref, dbs_ref, dvn_ref):
        i = pl.program_id(0)
        first = i == 0
        v = z_ref[1]
        vn, ln_vjp = jax.vjp(_ln, v, lg_ref[...], lb_ref[...])

        @pl.when(first)
        def _():
            dws_ref[...] = jnp.zeros_like(dws_ref)
            dbs_ref[...] = jnp.zeros_like(dbs_ref)

        for h in range(groups):
            cols = slice(h * gw, (h + 1) * gw)
            vn_h = vn[:, cols]
            s = _dot(ws_ref[h], vn_h, NN) + bs_ref[:, h:h + 1]
            u, g, dth = z_ref[0, :, cols], z_ref[2, :, cols], dt_ref[:, cols]
            sg = _silu(g)
            dz_ref[0, :, cols] = (dth * s * sg).astype(dz_ref.dtype)
            dz_ref[2, :, cols] = (dth * u * s * _dsilu(g)).astype(dz_ref.dtype)
            ds = dth * u * sg
            dvn_ref[:, cols] = _dot(ws_ref[h], ds, TN)
            dws_ref[h] += _dot(ds, vn_h, NT)
            dbs_ref[:, h:h + 1] += jnp.sum(ds, axis=1, keepdims=True)
        dv, dlg, dlb = ln_vjp(dvn_ref[...])
        dz_ref[1] = dv.astype(dz_ref.dtype)
        _acc(dlg_ref, first, dlg)
        _acc(dlb_ref, first, dlb)

    return _call(
        body, (z3, dt, ln_g, ln_b, w_s, b_s_t), name=name, grid=(rows // CHUNK,),
        in_specs=[pl.BlockSpec((3, CHUNK, e), lambda i: (0, i, 0)), pl.BlockSpec((CHUNK, e), lambda i: (i, 0)), _vec(e), _vec(e),
                  pl.BlockSpec(w_s.shape, lambda i: (0, 0, 0)), pl.BlockSpec(b_s_t.shape, lambda i: (0, 0))],
        out_specs=[pl.BlockSpec((3, CHUNK, e), lambda i: (0, i, 0)), _vec(e), _vec(e),
                   pl.BlockSpec(w_s.shape, lambda i: (0, 0, 0)), pl.BlockSpec(b_s_t.shape, lambda i: (0, 0))],
        out_shape=[jax.ShapeDtypeStruct((3, rows, e), MXU_DTYPE), jax.ShapeDtypeStruct((1, e), F32),
                   jax.ShapeDtypeStruct((1, e), F32), jax.ShapeDtypeStruct(w_s.shape, F32),
                   jax.ShapeDtypeStruct(b_s_t.shape, F32)],
        scratch_shapes=[pltpu.VMEM((CHUNK, e), F32)],
        compiler_params=_params(("arbitrary",), 20 * _nbytes((CHUNK, e), F32)), comm=comm)


CONV_COL_TILE = 512


def _conv_specs(rt, tc, planes):
    per = rt.tile // CONV_HALO
    last = rt.rows // CONV_HALO - 1
    if planes:
        cur = pl.BlockSpec((planes, rt.tile, tc), lambda j, i: (0, i, j))
        prev = pl.BlockSpec((planes, CONV_HALO, tc), lambda j, i: (0, jnp.maximum(i * per - 1, 0), j))
        nxt = pl.BlockSpec((planes, CONV_HALO, tc), lambda j, i: (0, jnp.minimum((i + 1) * per, last), j))
    else:
        cur = pl.BlockSpec((rt.tile, tc), lambda j, i: (i, j))
        prev = pl.BlockSpec((CONV_HALO, tc), lambda j, i: (jnp.maximum(i * per - 1, 0), j))
        nxt = pl.BlockSpec((CONV_HALO, tc), lambda j, i: (jnp.minimum((i + 1) * per, last), j))
    return cur, prev, nxt


def _halo_ok(rt, i):
    prev_ok = (i != 0) & (i != rt.nt_x)
    next_ok = (i != rt.nt_x - 1) & (i != rt.nt - 1)
    return prev_ok, next_ok


def _glu(ref):
    return ref[0] * _sigmoid(ref[1])


def _padded(cur, prev, nxt, prev_ok, next_ok):
    return jnp.concatenate([jnp.where(prev_ok, prev, 0.0), cur, jnp.where(next_ok, nxt, 0.0)], axis=0)


SUBLANES = 8
CONV_ROW_BLOCK = 32
CONV_DW_ROWS, CONV_DW_TAPS = 16, 4


def _phase_scratch(tr, tc):
    return pltpu.VMEM((SUBLANES, tr + 2 * CONV_HALO - SUBLANES, tc), F32)


def _store_phases(rot_ref, pad):
    rows = rot_ref.shape[1]
    for b in range(SUBLANES):
        rot_ref[b] = pad[b:b + rows, :]


def _tap_rows(rot_ref, r0, off, g):
    a, b = divmod(off, SUBLANES)
    return rot_ref[b, pl.ds(r0 + SUBLANES * (a + g), SUBLANES), :]


def _conv_rows(rot_ref, w_ref, r0, offs, init):
    tc = rot_ref.shape[2]
    accs = [init] * (CONV_ROW_BLOCK // SUBLANES)
    for k, off in enumerate(offs):
        wk = jnp.broadcast_to(w_ref[k:k + 1, :], (SUBLANES, tc))
        accs = [acc + wk * _tap_rows(rot_ref, r0, off, g) for g, acc in enumerate(accs)]
    return jnp.concatenate(accs, axis=0)


def _conv_fwd(z3, conv_w, conv_b, rt, name, comm=None):
    _, rows, e = z3.shape
    tc = _pick(e, (CONV_COL_TILE, 256, 128))
    tr = rt.tile

    def body(cur_ref, prev_ref, next_ref, w_ref, b_ref, o_ref, rot_ref):
        prev_ok, next_ok = _halo_ok(rt, pl.program_id(1))
        _store_phases(rot_ref, _padded(_glu(cur_ref), _glu(prev_ref), _glu(next_ref), prev_ok, next_ok))
        bias = jnp.broadcast_to(b_ref[...], (SUBLANES, tc))
        offs = [CONV_HALO - CONV_W // 2 + k for k in range(CONV_W)]

        def rows_block(rb, carry):
            r0 = pl.multiple_of(rb * CONV_ROW_BLOCK, CONV_ROW_BLOCK)
            o_ref[pl.ds(r0, CONV_ROW_BLOCK), :] = _conv_rows(rot_ref, w_ref, r0, offs, bias)
            return carry

        lax.fori_loop(0, tr // CONV_ROW_BLOCK, rows_block, 0)

    cur, prev, nxt = _conv_specs(rt, tc, 2)
    return _call(
        body, (z3, z3, z3, conv_w, conv_b), name=name, grid=(e // tc, rt.nt),
        in_specs=[cur, prev, nxt, pl.BlockSpec((CONV_W, tc), lambda j, i: (0, j)), pl.BlockSpec((1, tc), lambda j, i: (0, j))],
        out_specs=pl.BlockSpec((tr, tc), lambda j, i: (i, j)), out_shape=jax.ShapeDtypeStruct((rows, e), F32),
        scratch_shapes=[_phase_scratch(tr, tc)],
        compiler_params=_params(("parallel", "arbitrary"), 32 * _nbytes((tr, tc), F32)), comm=comm)


def _conv_bwd(z3, dy1, dg, conv_w, rt, name, comm=None):
    _, rows, e = z3.shape
    tc = _pick(e, (CONV_COL_TILE, 256, 128))
    tr = rt.tile

    def body(cur_ref, prev_ref, next_ref, dcur_ref, dprev_ref, dnext_ref, dg_ref, w_ref, dz_ref, dw_ref, db_ref, rot_ref, drot_ref):
        i = pl.program_id(1)
        prev_ok, next_ok = _halo_ok(rt, i)
        _store_phases(rot_ref, _padded(_glu(cur_ref), _glu(prev_ref), _glu(next_ref), prev_ok, next_ok))
        _store_phases(drot_ref, _padded(dcur_ref[...], dprev_ref[...], dnext_ref[...], prev_ok, next_ok))
        n_blocks = tr // CONV_ROW_BLOCK

        @pl.when(i == 0)
        def _():
            dw_ref[...] = jnp.zeros_like(dw_ref)
            db_ref[...] = jnp.zeros_like(db_ref)

        roffs = [CONV_HALO + CONV_W // 2 - k for k in range(CONV_W)]
        zero = jnp.zeros((SUBLANES, tc), F32)

        def dgate_block(rb, carry):
            r0 = pl.multiple_of(rb * CONV_ROW_BLOCK, CONV_ROW_BLOCK)
            dy0 = _conv_rows(drot_ref, w_ref, r0, roffs, zero)
            rws = pl.ds(r0, CONV_ROW_BLOCK)
            a, sb = cur_ref[0, rws, :], _sigmoid(cur_ref[1, rws, :])
            dz_ref[0, rws, :] = (dy0 * sb).astype(dz_ref.dtype)
            dz_ref[1, rws, :] = (dy0 * a * sb * (1.0 - sb)).astype(dz_ref.dtype)
            return carry

        lax.fori_loop(0, n_blocks, dgate_block, 0)
        dz_ref[2] = dg_ref[...]

        groups = CONV_DW_ROWS // SUBLANES
        for k0 in range(0, CONV_W, CONV_DW_TAPS):
            taps = list(range(k0, min(k0 + CONV_DW_TAPS, CONV_W)))

            def taps_block(rb, accs, taps=taps):
                r0 = pl.multiple_of(rb * CONV_DW_ROWS, CONV_DW_ROWS)
                dy = [dcur_ref[pl.ds(r0 + SUBLANES * g, SUBLANES), :] for g in range(groups)]
                return tuple(accs[t * groups + g] + dy[g] * _tap_rows(rot_ref, r0, CONV_HALO - CONV_W // 2 + k, g)
                             for t, k in enumerate(taps) for g in range(groups))

            accs = lax.fori_loop(0, tr // CONV_DW_ROWS, taps_block, (zero,) * (len(taps) * groups))
            for t, k in enumerate(taps):
                tot = functools.reduce(jnp.add, accs[t * groups:(t + 1) * groups])
                dw_ref[k:k + 1, :] += jnp.sum(tot, axis=0, keepdims=True)
        db_ref[...] += jnp.sum(dcur_ref[...], axis=0, keepdims=True)

    cur, prev, nxt = _conv_specs(rt, tc, 2)
    dcur, dprev, dnxt = _conv_specs(rt, tc, 0)
    return _call(
        body, (z3, z3, z3, dy1, dy1, dy1, dg, conv_w), name=name, grid=(e // tc, rt.nt),
        in_specs=[cur, prev, nxt, dcur, dprev, dnxt, pl.BlockSpec((tr, tc), lambda j, i: (i, j)),
                  pl.BlockSpec((CONV_W, tc), lambda j, i: (0, j))],
        out_specs=[pl.BlockSpec((3, tr, tc), lambda j, i: (0, i, j)), pl.BlockSpec((CONV_W, tc), lambda j, i: (0, j)),
                   pl.BlockSpec((1, tc), lambda j, i: (0, j))],
        out_shape=[jax.ShapeDtypeStruct((3, rows, e), MXU_DTYPE), jax.ShapeDtypeStruct((CONV_W, e), F32),
                   jax.ShapeDtypeStruct((1, e), F32)],
        scratch_shapes=[_phase_scratch(tr, tc), _phase_scratch(tr, tc)],
        compiler_params=_params(("parallel", "arbitrary"), 48 * _nbytes((tr, tc), F32)), comm=comm)


def _conv_mid(y1, g, ln_g, ln_b):
    return _silu(_ln(y1, ln_g, ln_b)) * _silu(g)


def _conv_mid_fwd(y1, z3, ln_g, ln_b, rt, name):
    e = y1.shape[1]
    tr = CHUNK

    def body(y_ref, g_ref, lg_ref, lb_ref, t_ref):
        t_ref[...] = _conv_mid(y_ref[...], g_ref[...], lg_ref[...], lb_ref[...]).astype(t_ref.dtype)

    return pl.pallas_call(
        body, name=name, grid=(rt.rows // tr,),
        in_specs=[pl.BlockSpec((tr, e), lambda i: (i, 0)), pl.BlockSpec((None, tr, e), lambda i: (2, i, 0)), _vec(e), _vec(e)],
        out_specs=pl.BlockSpec((tr, e), lambda i: (i, 0)), out_shape=jax.ShapeDtypeStruct((rt.rows, e), MXU_DTYPE),
        compiler_params=_params(("parallel",), 12 * _nbytes((tr, e), F32)),
    )(y1, z3, ln_g, ln_b)


def _conv_mid_bwd(y1, z3, dt, ln_g, ln_b, rt, name):
    e = y1.shape[1]
    tr = CHUNK

    def body(y_ref, g_ref, dt_ref, lg_ref, lb_ref, dy_ref, dg_ref, dlg_ref, dlb_ref):
        first = pl.program_id(0) == 0
        _, vjp = jax.vjp(_conv_mid, y_ref[...], g_ref[...], lg_ref[...], lb_ref[...])
        dy, dg, dlg, dlb = vjp(dt_ref[...])
        dy_ref[...] = dy
        dg_ref[...] = dg.astype(dg_ref.dtype)
        _acc(dlg_ref, first, dlg)
        _acc(dlb_ref, first, dlb)

    row = pl.BlockSpec((tr, e), lambda i: (i, 0))
    return pl.pallas_call(
        body, name=name, grid=(rt.rows // tr,),
        in_specs=[row, pl.BlockSpec((None, tr, e), lambda i: (2, i, 0)), row, _vec(e), _vec(e)],
        out_specs=[row, row, _vec(e), _vec(e)],
        out_shape=[jax.ShapeDtypeStruct((rt.rows, e), F32), jax.ShapeDtypeStruct((rt.rows, e), MXU_DTYPE),
                   jax.ShapeDtypeStruct((1, e), F32), jax.ShapeDtypeStruct((1, e), F32)],
        compiler_params=_params(("arbitrary",), 20 * _nbytes((tr, e), F32)),
    )(y1, z3, dt, ln_g, ln_b)


def _rms(x, g):
    return x * lax.rsqrt(jnp.mean(x * x, axis=-1, keepdims=True) + LN_EPS) * g


def _pair_swap(x):
    lane = lax.broadcasted_iota(jnp.int32, x.shape, x.ndim - 1)
    return jnp.where(lane % 2 == 0, pltpu.roll(x, x.shape[-1] - 1, x.ndim - 1), pltpu.roll(x, 1, x.ndim - 1))


def _rope(x, cos, sin):
    return x * cos + _pair_swap(x) * sin


def _rope_t(dy, cos, sin):
    return dy * cos + _pair_swap(dy * sin)


def _rope_tables(n_x, n_ctx):
    t = jnp.arange(n_x)
    row = (t // GRID_W).astype(F32)
    col = (t % GRID_W).astype(F32)
    axis_dim = HEAD_DIM // 2
    inv = 1.0 / (ROPE_THETA ** (jnp.arange(0, axis_dim, 2, dtype=F32) / axis_dim))
    ang = jnp.concatenate([row[:, None] * inv, col[:, None] * inv], axis=-1)
    cos, sin = jnp.cos(ang), jnp.sin(ang)
    cos2 = jnp.repeat(cos, 2, axis=-1)
    sin2 = jnp.stack([-sin, sin], axis=-1).reshape(n_x, HEAD_DIM)
    cos2 = jnp.concatenate([cos2, jnp.ones((n_ctx, HEAD_DIM), F32)], axis=0)
    sin2 = jnp.concatenate([sin2, jnp.zeros((n_ctx, HEAD_DIM), F32)], axis=0)
    return cos2, sin2


def _qkv_prep(z4, q_g, k_g, cos, sin, d, kvw, rt, name):
    hd = HEAD_DIM
    kb = d // kvw

    def body(q_ref, k_ref, v_ref, qg_ref, kg_ref, cos_ref, sin_ref, qo_ref, ko_ref, vo_ref):
        cos, sin = cos_ref[...], sin_ref[...]
        for h in range(d // hd):
            cols = slice(h * hd, (h + 1) * hd)
            qo_ref[:, cols] = _rope(_rms(q_ref[:, cols], qg_ref[...]), cos, sin).astype(qo_ref.dtype)
        for h in range(kvw // hd):
            cols = slice(h * hd, (h + 1) * hd)
            ko_ref[:, cols] = _rope(_rms(k_ref[:, cols], kg_ref[...]), cos, sin).astype(ko_ref.dtype)
        vo_ref[...] = v_ref[...].astype(vo_ref.dtype)

    tr = rt.tile
    return pl.pallas_call(
        body, name=name, grid=(rt.nt,),
        in_specs=[pl.BlockSpec((tr, d), lambda i: (i, 0)), pl.BlockSpec((tr, kvw), lambda i: (i, kb)),
                  pl.BlockSpec((tr, kvw), lambda i: (i, kb + 1)), _vec(hd), _vec(hd), rt.full(hd), rt.full(hd)],
        out_specs=[rt.full(d), rt.full(kvw), rt.full(kvw)],
        out_shape=[jax.ShapeDtypeStruct((rt.rows, d), MXU_DTYPE), jax.ShapeDtypeStruct((rt.rows, kvw), MXU_DTYPE),
                   jax.ShapeDtypeStruct((rt.rows, kvw), MXU_DTYPE)],
        compiler_params=_params(("parallel",), 8 * _nbytes((tr, d), F32)),
    )(z4, z4, z4, q_g, k_g, cos, sin)


ATTN_Q_TILE = 256


def _attn_fwd(qh, kh, vh, n_x, name, comm=None):
    rows, d = qh.shape
    kvw = kh.shape[1]
    hd = HEAD_DIM
    n_kv = kvw // hd
    gqw = d // n_kv
    grp = gqw // hd
    tq = _pick(n_x, (ATTN_Q_TILE, 128))
    scale = hd ** -0.5

    def body(q_ref, k_ref, v_ref, o_ref, lse_ref):
        k, v = k_ref[...], v_ref[...]
        for g in range(grp):
            cols = slice(g * hd, (g + 1) * hd)
            s = _dot(q_ref[:, cols], k, NT) * scale
            m = jnp.max(s, axis=-1, keepdims=True)
            p = jnp.exp(s - m)
            l = jnp.sum(p, axis=-1, keepdims=True)
            o_ref[:, cols] = _dot(p / l, v, NN)
            lse_ref[:, g:g + 1] = m + jnp.log(l)

    vmem = 4 * _nbytes((rows, hd), MXU_DTYPE) + 4 * _nbytes((tq, rows), F32) + 6 * _nbytes((tq, gqw), F32)
    return _call(
        body, (qh, kh, vh), name=name, grid=(n_kv, n_x // tq),
        in_specs=[pl.BlockSpec((tq, gqw), lambda h, i: (i, h)), pl.BlockSpec((rows, hd), lambda h, i: (0, h)),
                  pl.BlockSpec((rows, hd), lambda h, i: (0, h))],
        out_specs=[pl.BlockSpec((tq, gqw), lambda h, i: (i, h)), pl.BlockSpec((None, tq, grp), lambda h, i: (h, i, 0))],
        out_shape=[jax.ShapeDtypeStruct((n_x, d), F32), jax.ShapeDtypeStruct((n_kv, n_x, grp), F32)],
        compiler_params=_params(("parallel", "arbitrary"), vmem), comm=comm)


def _attn_bwd(qh, kh, vh, do, lse, n_x, name, comm=None):
    rows, d = qh.shape
    kvw = kh.shape[1]
    hd = HEAD_DIM
    n_kv = kvw // hd
    gqw = d // n_kv
    grp = gqw // hd
    tq = _pick(n_x, (ATTN_Q_TILE, 128))
    scale = hd ** -0.5

    def body(q_ref, k_ref, v_ref, do_ref, lse_ref, dq_ref, dk_ref, dv_ref):
        @pl.when(pl.program_id(1) == 0)
        def _():
            dk_ref[...] = jnp.zeros_like(dk_ref)
            dv_ref[...] = jnp.zeros_like(dv_ref)

        k, v = k_ref[...], v_ref[...]
        for g in range(grp):
            cols = slice(g * hd, (g + 1) * hd)
            q, dog = q_ref[:, cols], do_ref[:, cols]
            p = jnp.exp(_dot(q, k, NT) * scale - lse_ref[:, g:g + 1])
            dp = _dot(dog, v, NT)
            ds = (p * (dp - jnp.sum(dp * p, axis=-1, keepdims=True)) * scale).astype(MXU_DTYPE)
            dq_ref[:, cols] = _dot(ds, k, NN)
            dk_ref[...] += _dot(ds, q, TN)
            dv_ref[...] += _dot(p, dog, TN)

    vmem = 4 * _nbytes((rows, hd), MXU_DTYPE) + 4 * _nbytes((rows, hd), F32) + 6 * _nbytes((tq, rows), F32) + 8 * _nbytes((tq, gqw), F32)
    qspec = pl.BlockSpec((tq, gqw), lambda h, i: (i, h))
    kspec = pl.BlockSpec((rows, hd), lambda h, i: (0, h))
    return _call(
        body, (qh, kh, vh, do, lse), name=name, grid=(n_kv, n_x // tq),
        in_specs=[qspec, kspec, kspec, qspec, pl.BlockSpec((None, tq, grp), lambda h, i: (h, i, 0))],
        out_specs=[qspec, kspec, kspec],
        out_shape=[jax.ShapeDtypeStruct((n_x, d), F32), jax.ShapeDtypeStruct((rows, kvw), F32),
                   jax.ShapeDtypeStruct((rows, kvw), F32)],
        compiler_params=_params(("parallel", "arbitrary"), vmem), comm=comm)


def _attn_gate(o, z4, d, kvw, rt, name):
    g0 = (d + 2 * kvw) // kvw
    tr = rt.tile

    def body(o_ref, g_ref, t_ref):
        t_ref[...] = (o_ref[...] * _silu(g_ref[...])).astype(t_ref.dtype)

    tile = pl.BlockSpec((tr, kvw), lambda i, j: (i, j))
    return pl.pallas_call(
        body, name=name, grid=(rt.nt, d // kvw),
        in_specs=[tile, pl.BlockSpec((tr, kvw), lambda i, j: (i, g0 + j))],
        out_specs=tile, out_shape=jax.ShapeDtypeStruct((rt.rows, d), MXU_DTYPE),
        compiler_params=_params(("parallel", "parallel"), 8 * _nbytes((tr, kvw), F32)),
    )(o, z4)


def _attn_gate_bwd(dt, o, z4, d, kvw, rt, name):
    g0 = (d + 2 * kvw) // kvw
    tr = rt.tile

    def body(dt_ref, o_ref, g_ref, do_ref, dg_ref):
        dt_v, g = dt_ref[...], g_ref[...]
        do_ref[...] = (dt_v * _silu(g)).astype(do_ref.dtype)
        dg_ref[...] = (dt_v * o_ref[...] * _dsilu(g)).astype(dg_ref.dtype)

    tile = pl.BlockSpec((tr, kvw), lambda i, j: (i, j))
    return pl.pallas_call(
        body, name=name, grid=(rt.nt, d // kvw),
        in_specs=[tile, tile, pl.BlockSpec((tr, kvw), lambda i, j: (i, g0 + j))],
        out_specs=[tile, tile],
        out_shape=[jax.ShapeDtypeStruct((rt.rows, d), MXU_DTYPE), jax.ShapeDtypeStruct((rt.rows, d), MXU_DTYPE)],
        compiler_params=_params(("parallel", "parallel"), 12 * _nbytes((tr, kvw), F32)),
    )(dt, o, z4)


def _prep_bwd(dxh, z4, col_block, gain, cos, sin, rt, name):
    w = dxh.shape[1]
    hd = HEAD_DIM

    def body(dxh_ref, x_ref, g_ref, cos_ref, sin_ref, dx_ref, dg_ref):
        cos, sin = cos_ref[...], sin_ref[...]
        dg = jnp.zeros((1, hd), F32)
        for h in range(w // hd):
            cols = slice(h * hd, (h + 1) * hd)
            _, vjp = jax.vjp(_rms, x_ref[:, cols], g_ref[...])
            dx, dgh = vjp(_rope_t(dxh_ref[:, cols], cos, sin))
            dx_ref[:, cols] = dx.astype(dx_ref.dtype)
            dg = dg + dgh
        _acc(dg_ref, pl.program_id(0) == 0, dg)

    tr = rt.tile
    return pl.pallas_call(
        body, name=name, grid=(rt.nt,),
        in_specs=[rt.full(w), pl.BlockSpec((tr, w), lambda i: (i, col_block)), _vec(hd), rt.full(hd), rt.full(hd)],
        out_specs=[rt.full(w), _vec(hd)],
        out_shape=[jax.ShapeDtypeStruct((rt.rows, w), MXU_DTYPE), jax.ShapeDtypeStruct((1, hd), F32)],
        compiler_params=_params(("arbitrary",), 12 * _nbytes((tr, w), F32)),
    )(dxh, z4, gain, cos, sin)


def _loss_head(x, target, rt, name):
    d = x.shape[1]

    def body(x_ref, t_ref, dx_ref, l_ref):
        err = x_ref[...] - t_ref[...]
        dx_ref[...] = err / d
        row = jnp.mean(err * err, axis=-1, keepdims=True)
        _acc(l_ref, pl.program_id(0) == 0, jnp.sum(row, axis=0, keepdims=True))

    return pl.pallas_call(
        body, name=name, grid=(rt.nt,),
        in_specs=[rt.full(d), rt.full(d)],
        out_specs=[rt.full(d), pl.BlockSpec((1, 1), lambda i: (0, 0))],
        out_shape=[jax.ShapeDtypeStruct(x.shape, F32), jax.ShapeDtypeStruct((1, 1), F32)],
        compiler_params=_params(("arbitrary",), 8 * _nbytes((rt.tile, d), F32)),
    )(x, target)


def _adamw(w, g, m, v):
    m = ADAM_B1 * m + (1.0 - ADAM_B1) * g
    v = ADAM_B2 * v + (1.0 - ADAM_B2) * (g * g)
    m_hat = m / (1.0 - ADAM_B1 ** ADAM_STEP)
    v_hat = v / (1.0 - ADAM_B2 ** ADAM_STEP)
    delta = -ADAM_LR * (m_hat / (jnp.sqrt(v_hat) + ADAM_EPS) + ADAM_WD * w)
    return delta, m, v


ADAM_TILE_BYTES = 1 << 20


def _adam_tile(rows, cols):
    tr = rows
    while tr % 16 == 0 and tr * cols * 4 > ADAM_TILE_BYTES:
        tr //= 2
    return tr


def _adam_reduce(parts, w, m, v, name):
    slots, rows, cols = w.shape
    tr = _adam_tile(rows, cols)
    nt = rows // tr

    def body(*refs):
        p_refs = refs[:slots]
        w_ref, m_ref, v_ref, g_ref, d_ref, mo_ref, vo_ref = refs[slots:]
        for k in range(slots):
            @pl.when(pl.program_id(0) == k)
            def _(p_ref=p_refs[k]):
                g = p_ref[0].astype(F32)
                for part in range(1, p_ref.shape[0]):
                    g = g + p_ref[part].astype(F32)
                g_ref[...] = g
                d_ref[...], mo_ref[...], vo_ref[...] = _adamw(w_ref[...], g, m_ref[...], v_ref[...])

    def part_spec(k):
        return pl.BlockSpec((parts[k].shape[0], tr, cols),
                            lambda s, i: (0, jnp.where(s < k, 0, jnp.where(s == k, i, nt - 1)), 0))

    row = pl.BlockSpec((None, tr, cols), lambda s, i: (s, i, 0))
    sds = jax.ShapeDtypeStruct((slots, rows, cols), F32)
    return pl.pallas_call(
        body, name=name, grid=(slots, nt),
        in_specs=[part_spec(k) for k in range(slots)] + [row, row, row],
        out_specs=[row] * 4, out_shape=[sds] * 4,
        compiler_params=_params(("arbitrary", "arbitrary"), 40 * _nbytes((tr, cols), F32)),
    )(*parts, w, m, v)


def _adam_plain(g, w, m, v, name):
    rows, cols = w.shape
    tr = _adam_tile(rows, cols)

    def body(g_ref, w_ref, m_ref, v_ref, d_ref, mo_ref, vo_ref):
        d_ref[...], mo_ref[...], vo_ref[...] = _adamw(w_ref[...], g_ref[...], m_ref[...], v_ref[...])

    row = pl.BlockSpec((tr, cols), lambda i: (i, 0))
    sds = jax.ShapeDtypeStruct((rows, cols), F32)
    return pl.pallas_call(
        body, name=name, grid=(rows // tr,),
        in_specs=[row] * 4, out_specs=[row] * 3, out_shape=[sds] * 3,
        compiler_params=_params(("parallel",), 32 * _nbytes((tr, cols), F32)),
    )(g, w, m, v)


def _sum_devices(parts, name):
    _, rows, cols = parts.shape
    tr = _adam_tile(rows, cols)

    def body(p_ref, o_ref):
        g = p_ref[0]
        for k in range(1, N_DEV):
            g = g + p_ref[k]
        o_ref[...] = g

    return pl.pallas_call(
        body, name=name, grid=(rows // tr,),
        in_specs=[pl.BlockSpec((N_DEV, tr, cols), lambda i: (0, i, 0))],
        out_specs=pl.BlockSpec((tr, cols), lambda i: (i, 0)), out_shape=jax.ShapeDtypeStruct((rows, cols), F32),
        compiler_params=_params(("parallel",), 24 * _nbytes((tr, cols), F32)),
    )(parts)


COND_ROWS = 16


def _mod_fwd_mm(cond, mod_w, mod_b, name):
    layers, d, w = mod_w.shape

    def body(c_ref, w_ref, b_ref, o_ref):
        o_ref[...] = _dot(_silu(c_ref[...]), w_ref[...], NN) + b_ref[...]

    return pl.pallas_call(
        body, name=name, grid=(layers,),
        in_specs=[pl.BlockSpec((COND_ROWS, d), lambda l: (0, 0)), pl.BlockSpec((None, d, w), lambda l: (l, 0, 0)),
                  pl.BlockSpec((None, 1, w), lambda l: (l, 0, 0))],
        out_specs=pl.BlockSpec((None, COND_ROWS, w), lambda l: (l, 0, 0)),
        out_shape=jax.ShapeDtypeStruct((layers, COND_ROWS, w), F32),
        compiler_params=_params(("parallel",), 4 * _nbytes((d, w), F32)),
    )(cond, mod_w, mod_b)


def _mod_bwd_mm(cond, dm, mod_w, name):
    layers, d, w = mod_w.shape

    def body(c_ref, dm_ref, w_ref, dw_ref, dc_ref):
        dmv = dm_ref[...]
        dw_ref[...] = _dot(_silu(c_ref[...]), dmv, TN)
        _acc(dc_ref, pl.program_id(0) == 0, _dot(dmv, w_ref[...], NT))

    return pl.pallas_call(
        body, name=name, grid=(layers,),
        in_specs=[pl.BlockSpec((COND_ROWS, d), lambda l: (0, 0)), pl.BlockSpec((None, COND_ROWS, w), lambda l: (l, 0, 0)),
                  pl.BlockSpec((None, d, w), lambda l: (l, 0, 0))],
        out_specs=[pl.BlockSpec((None, d, w), lambda l: (l, 0, 0)), pl.BlockSpec((COND_ROWS, d), lambda l: (0, 0))],
        out_shape=[jax.ShapeDtypeStruct((layers, d, w), F32), jax.ShapeDtypeStruct((COND_ROWS, d), F32)],
        compiler_params=_params(("arbitrary",), 6 * _nbytes((d, w), F32)),
    )(cond, dm, mod_w)


PACK_ROWS = 256


def _pack(arrs):
    flat = jnp.concatenate([a.reshape(-1).astype(F32) for a in arrs])
    pad = (-flat.shape[0]) % (PACK_ROWS * LANES)
    return jnp.pad(flat, (0, pad)).reshape(-1, LANES)


def _unpack(flat2d, shapes):
    flat = flat2d.reshape(-1)
    out, off = [], 0
    for s in shapes:
        n = math.prod(s)
        out.append(flat[off:off + n].reshape(s))
        off += n
    return out


def _unpack_dev(g2d, shapes):
    flat = g2d.reshape(N_DEV, -1)
    out, off = [], 0
    for s in shapes:
        n = math.prod(s)
        out.append(flat[:, off:off + n].reshape((N_DEV, *s)))
        off += n
    return out


def kernel(x, c, ctx, c_ctx, mod_w, mod_b, post_g, post_b, a_w_in, a_ln_g, a_ln_b, a_w_s, a_b_s, a_w_out, b_w_in, b_conv_w, b_conv_b, b_ln_g, b_ln_b, b_w_out, c_w_in, c_q_g, c_k_g, c_w_out, loss_target, m_c_ctx, m_mod_w, m_mod_b, m_post_g, m_post_b, m_a_w_in, m_a_ln_g, m_a_ln_b, m_a_w_s, m_a_b_s, m_a_w_out, m_b_w_in, m_b_conv_w, m_b_conv_b, m_b_ln_g, m_b_ln_b, m_b_w_out, m_c_w_in, m_c_q_g, m_c_k_g, m_c_w_out, v_c_ctx, v_mod_w, v_mod_b, v_post_g, v_post_b, v_a_w_in, v_a_ln_g, v_a_ln_b, v_a_w_s, v_a_b_s, v_a_w_out, v_b_w_in, v_b_conv_w, v_b_conv_b, v_b_ln_g, v_b_ln_b, v_b_w_out, v_c_w_in, v_c_q_g, v_c_k_g, v_c_w_out):
    n_x, d = x.shape[1], x.shape[2]
    n_ctx = ctx.shape[1]
    e = a_w_out.shape[1] * N_DEV
    kvw = N_KV_HEADS * HEAD_DIM
    me = _dev_index(_mesh_pos())
    rt_all = _Rows(n_x, n_ctx)
    rt_x = _Rows(n_x, 0)

    small_in = [c[0], a_ln_g, a_ln_b, b_conv_w[0]]
    (g_small,) = _all_gather([_pack(small_in)], "ag_small_params")
    conds, ln_g_all, ln_b_all, conv_w_all = _unpack_dev(g_small, [a.shape for a in small_in])
    a_ln_g_f = jnp.moveaxis(ln_g_all, 0, 1).reshape(a_ln_g.shape[0], 1, e)
    a_ln_b_f = jnp.moveaxis(ln_b_all, 0, 1).reshape(a_ln_b.shape[0], 1, e)
    conv_w_f = jnp.moveaxis(conv_w_all, 0, 1).reshape(CONV_W, e)
    cond = jnp.zeros((COND_ROWS, d), F32).at[:N_DEV].set(conds).at[N_DEV].set(c_ctx)

    wm = mod_w.shape[2]
    mod_b_mine = lax.dynamic_slice_in_dim(mod_b, me * wm, wm, axis=1).reshape(DEPTH, 1, wm)
    (mods_g,) = _all_gather([_mod_fwd_mm(cond, mod_w, mod_b_mine, "mod_fwd")], "ag_mod")
    mods = jnp.moveaxis(mods_g, 0, 2).reshape(DEPTH, COND_ROWS, 3 * d)
    mine = lax.dynamic_index_in_dim(mods, me, axis=1, keepdims=False)
    modv = jnp.stack([mine, mods[:, N_DEV]], axis=1).reshape(DEPTH * 2 * 3, 1, d)

    def gather_of(wt):
        return _gather_comm([wt.astype(MXU_DTYPE)])

    def exchange_of(*gs):
        return _exchange_comm([g if g.ndim == 3 else g.reshape(N_DEV, -1, g.shape[-1]) for g in gs])

    def chip_sums_of(g, tag):
        (theirs,) = _comm_call(_pair_exchange_comm([g]), f"pair_{tag}")
        mine = lax.dynamic_index_in_dim(g.reshape(N_CHIPS, 2, *g.shape[1:]), lax.axis_index("c"), axis=1, keepdims=False)
        return _chip_exchange_comm([_pair_add(mine, theirs, f"pair_add_{tag}")])

    (wa_in0,) = _all_gather([a_w_in[0].astype(MXU_DTYPE)], "ag_w_l0")

    ws_op = a_w_s.astype(MXU_DTYPE)
    bs_t = jnp.swapaxes(a_b_s, 1, 2)
    pg = post_g.reshape(DEPTH, 1, d)
    pb = post_b.reshape(DEPTH, 1, d)

    xs0 = jnp.concatenate([x[0], ctx[0]], axis=0)
    h0 = _modulate(xs0, modv, 0, rt_all, "mod0")
    z0, (wb_in,) = _mm_nn(h0, wa_in0, planes=3, name="l0_in", comm=gather_of(b_w_in[0]))
    t0, (wa_out0,) = _cm_mid_fwd(z0, a_ln_g_f[0], a_ln_b_f[0], ws_op[0], bs_t[0], "l0_mid", comm=gather_of(a_w_out[0]))
    wa_out0 = wa_out0.reshape(-1, d)
    y0, (wb_out,) = _mm_nn(t0, wa_out0, name="l0_out", comm=gather_of(b_w_out[0]))
    xs1 = _post_fwd(xs0, y0, modv, 0, pg[0], pb[0], rt_all, "l0_post")
    h1 = _modulate(xs1, modv, 1, rt_all, "mod1")
    z1, (wc_in,) = _mm_nn(h1, wb_in, planes=3, name="l1_in", comm=gather_of(c_w_in[0]))
    cy1, (wa_in1,) = _conv_fwd(z1, conv_w_f, b_conv_b, rt_all, "l1_conv", comm=gather_of(a_w_in[1]))
    t1 = _conv_mid_fwd(cy1, z1, b_ln_g, b_ln_b, rt_all, "l1_mid")
    y1, (wc_out,) = _mm_nn(t1, wb_out.reshape(-1, d), name="l1_out", comm=gather_of(c_w_out[0]))
    xs2 = _post_fwd(xs1, y1, modv, 1, pg[1], pb[1], rt_all, "l1_post")
    cos, sin = _rope_tables(n_x, n_ctx)
    h2 = _modulate(xs2, modv, 2, rt_all, "mod2")
    z2 = _mm_nn(h2, wc_in, name="l2_in")
    qh, kh, vh = _qkv_prep(z2, c_q_g, c_k_g, cos, sin, d, kvw, rt_all, "l2_prep")
    (o2, lse), (wa_out1,) = _attn_fwd(qh, kh, vh, n_x, "l2_attn", comm=gather_of(a_w_out[1]))
    wb_out, wc_out, wa_out1 = [wt.reshape(-1, d) for wt in (wb_out, wc_out, wa_out1)]
    t2 = _attn_gate(o2, z2, d, kvw, rt_x, "l2_gate")
    y2 = _mm_nn(t2, wc_out, name="l2_out")
    x2 = xs2
    x3 = _post_fwd(x2, y2, modv, 2, pg[2], pb[2], rt_x, "l2_post")
    h3 = _modulate(x3, modv, 3, rt_x, "mod3")
    z3 = _mm_nn(h3, wa_in1, planes=3, name="l3_in")
    t3 = _cm_mid_fwd(z3, a_ln_g_f[1], a_ln_b_f[1], ws_op[1], bs_t[1], "l3_mid")
    y3 = _mm_nn(t3, wa_out1, name="l3_out")
    x4 = _post_fwd(x3, y3, modv, 3, pg[3], pb[3], rt_x, "l3_post")

    dx4, loss_sum = _loss_head(x4, loss_target[0], rt_x, "loss")
    loss = lax.psum(0.5 * loss_sum[0, 0], ("x", "y", "c"))

    gdt = MXU_DTYPE
    dres3, dy3, dpg3, dpb3, dgate3 = _post_bwd(x3, y3, dx4, modv, 3, pg[3], pb[3], rt_x, "l3_post_b")
    dt3 = _mm_nt(dy3, wa_out1, name="l3_dt")
    gw_a_out1 = _mm_tn(t3, dy3, blocked=False, out_dtype=gdt, name="l3_dwout")
    dz3, dlg3, dlb3, dws3, dbs3 = _cm_mid_bwd(z3, dt3, a_ln_g_f[1], a_ln_b_f[1], ws_op[1], bs_t[1], "l3_mid_b")
    gw_a_in1 = _mm_tn(h3, dz3, blocked=True, out_dtype=gdt, name="l3_dwin")
    dh3, (r_a_out1,) = _mm_nt_blocked(dz3, wa_in1, name="l3_dh", comm=exchange_of(gw_a_out1))
    dx3, dshift3, dscale3 = _mod_bwd(dres3, dh3, x3, modv, 3, rt_x, "l3_mod_b")
    dres2, dy2, dpg2, dpb2, dgate2 = _post_bwd(x2, y2, dx3, modv, 2, pg[2], pb[2], rt_x, "l2_post_b")
    dt2 = _mm_nt(dy2, wc_out, name="l2_dt")
    gw_c_out = _mm_tn(t2, dy2, blocked=False, out_dtype=gdt, name="l2_dwout")
    do2, dg2 = _attn_gate_bwd(dt2, o2, z2, d, kvw, rt_x, "l2_gate_b")
    (dqh, dkh, dvh), (r_a_in1,) = _attn_bwd(qh, kh, vh, do2, lse, n_x, "l2_attn_b", comm=exchange_of(gw_a_in1))
    dq2, dqg = _prep_bwd(dqh, z2, 0, c_q_g, cos, sin, rt_x, "l2_qprep_b")
    dk2, dkg = _prep_bwd(dkh, z2, d // kvw, c_k_g, cos, sin, rt_all, "l2_kprep_b")
    zpad = jnp.zeros((n_ctx, d), MXU_DTYPE)
    dz2 = jnp.concatenate([jnp.concatenate([dq2, zpad], axis=0), dk2, dvh.astype(MXU_DTYPE),
                           jnp.concatenate([dg2, zpad], axis=0)], axis=1)
    gw_c_in = _mm_tn(h2, dz2, blocked=True, out_dtype=gdt, name="l2_dwin")
    dh2, (r_c_out,) = _mm_nt_blocked(dz2, wc_in, name="l2_dh", comm=exchange_of(gw_c_out))
    dxs2, dshift2, dscale2 = _mod_bwd(dres2, dh2, xs2, modv, 2, rt_all, "l2_mod_b")
    dres1, dy1, dpg1, dpb1, dgate1 = _post_bwd(xs1, y1, dxs2, modv, 1, pg[1], pb[1], rt_all, "l1_post_b")
    dt1 = _mm_nt(dy1, wb_out, name="l1_dt")
    gw_b_out = _mm_tn(t1, dy1, blocked=False, out_dtype=gdt, name="l1_dwout")
    dcy1, dgc1, dblg, dblb = _conv_mid_bwd(cy1, z1, dt1, b_ln_g, b_ln_b, rt_all, "l1_mid_b")
    (dz1, dconv_w, dconv_b), (r_c_in, r_b_out) = _conv_bwd(z1, dcy1, dgc1, conv_w_f, rt_all, "l1_conv_b",
                                                           comm=exchange_of(gw_c_in, gw_b_out))
    gw_b_in = _mm_tn(h1, dz1, blocked=True, out_dtype=gdt, name="l1_dwin")
    dh1, (r_b_in,) = _mm_nt_blocked(dz1, wb_in, name="l1_dh", comm=chip_sums_of(gw_b_in, "b_in"))
    dxs1, dshift1, dscale1 = _mod_bwd(dres1, dh1, xs1, modv, 1, rt_all, "l1_mod_b")
    dres0, dy0, dpg0, dpb0, dgate0 = _post_bwd(xs0, y0, dxs1, modv, 0, pg[0], pb[0], rt_all, "l0_post_b")
    dt0 = _mm_nt(dy0, wa_out0, name="l0_dt")
    gw_a_out0 = _mm_tn(t0, dy0, blocked=False, out_dtype=gdt, name="l0_dwout")
    (dz0, dlg0, dlb0, dws0, dbs0), (r_a_out0,) = _cm_mid_bwd(z0, dt0, a_ln_g_f[0], a_ln_b_f[0], ws_op[0], bs_t[0], "l0_mid_b",
                                                             comm=exchange_of(gw_a_out0))
    gw_a_in0 = _mm_tn(h0, dz0, blocked=True, out_dtype=gdt, name="l0_dwin")
    dh0, (r_a_in0,) = _mm_nt_blocked(dz0, wa_in0, name="l0_dh", comm=chip_sums_of(gw_a_in0, "a_in0"))
    dx0, dshift0, dscale0 = _mod_bwd(dres0, dh0, xs0, modv, 0, rt_all, "l0_mod_b", dx_rows=n_x)
    grad_x = dx0[None]

    def seg2(a):
        a = a[:, 0]
        return a if a.shape[0] == 2 else jnp.concatenate([a, jnp.zeros_like(a)], axis=0)

    gate2 = jnp.concatenate([dgate2[:, 0], jnp.zeros((1, d), F32)], axis=0)
    dmod = jnp.stack([
        jnp.concatenate([seg2(dshift0), seg2(dscale0), seg2(dgate0)], axis=1),
        jnp.concatenate([seg2(dshift1), seg2(dscale1), seg2(dgate1)], axis=1),
        jnp.concatenate([seg2(dshift2), seg2(dscale2), gate2], axis=1),
        jnp.concatenate([seg2(dshift3), seg2(dscale3), seg2(dgate3)], axis=1)])

    g_post_g = jnp.concatenate([dpg0, dpg1, dpg2, dpg3], axis=0)
    g_post_b = jnp.concatenate([dpb0, dpb1, dpb2, dpb3], axis=0)
    g_a_ln_g = jnp.concatenate([dlg0, dlg3], axis=0)
    g_a_ln_b = jnp.concatenate([dlb0, dlb3], axis=0)
    g_a_w_s = jnp.stack([dws0, dws3])
    g_a_b_s = jnp.swapaxes(jnp.stack([dbs0, dbs3]), 1, 2)
    small_g = [g_post_g, g_post_b, g_a_w_s, g_a_b_s, dconv_b, dblg, dblb, dqg, dkg, g_a_ln_g, g_a_ln_b, dconv_w,
               dmod[:, 0], dmod[:, 1]]
    small_shapes = [a.shape for a in small_g]
    (gs_all,) = _all_gather([_pack(small_g)], "ag_small_grads")
    sums = _unpack(_sum_devices(gs_all, "sum_small"), small_shapes)
    (s_post_g, s_post_b, s_a_w_s, s_a_b_s, s_conv_b, s_b_ln_g, s_b_ln_b, s_q_g, s_k_g, s_a_ln_g, s_a_ln_b, s_conv_w,
     s_dmod_own, s_dmod_ctx) = sums
    grad_mod_b = s_dmod_own + s_dmod_ctx
    wl = a_ln_g.shape[1]
    wcv = b_conv_w.shape[2]
    grad_a_ln_g = lax.dynamic_slice_in_dim(s_a_ln_g, me * wl, wl, axis=1)
    grad_a_ln_b = lax.dynamic_slice_in_dim(s_a_ln_b, me * wl, wl, axis=1)
    grad_b_conv_w = lax.dynamic_slice_in_dim(s_conv_w, me * wcv, wcv, axis=1)[None]

    dmod_dev = _unpack_dev(gs_all, small_shapes)[12]
    dm_rows = jnp.concatenate([jnp.moveaxis(dmod_dev, 0, 1), s_dmod_ctx[:, None],
                               jnp.zeros((DEPTH, COND_ROWS - N_DEV - 1, 3 * d), F32)], axis=1)
    dm_mine = lax.dynamic_slice_in_dim(dm_rows, me * wm, wm, axis=2)
    grad_mod_w, dcond_part = _mod_bwd_mm(cond, dm_mine, mod_w, "mod_bwd")
    (dcond_all,) = _all_gather([dcond_part], "ag_dcond")
    dcond = _sum_devices(dcond_all, "sum_dcond")
    grad_c_ctx = dcond[N_DEV] * _dsilu(c_ctx)

    o_a_w_in = _adam_reduce([r_a_in0, r_a_in1], a_w_in, m_a_w_in, v_a_w_in, "adam_a_in")
    o_a_w_out = _adam_reduce([r_a_out0, r_a_out1], a_w_out, m_a_w_out, v_a_w_out, "adam_a_out")
    o_b_w_in = _adam_reduce([r_b_in], b_w_in, m_b_w_in, v_b_w_in, "adam_b_in")
    o_b_w_out = _adam_reduce([r_b_out], b_w_out, m_b_w_out, v_b_w_out, "adam_b_out")
    o_c_w_in = _adam_reduce([r_c_in], c_w_in, m_c_w_in, v_c_w_in, "adam_c_in")
    o_c_w_out = _adam_reduce([r_c_out], c_w_out, m_c_w_out, v_c_w_out, "adam_c_out")
    mw_shape = mod_w.shape
    o_mod_w = [grad_mod_w] + [a.reshape(mw_shape) for a in _adam_plain(
        grad_mod_w.reshape(-1, wm), mod_w.reshape(-1, wm), m_mod_w.reshape(-1, wm), v_mod_w.reshape(-1, wm), "adam_mod_w")]

    sg = [grad_c_ctx, grad_mod_b, s_post_g, s_post_b, grad_a_ln_g, grad_a_ln_b, s_a_w_s, s_a_b_s, grad_b_conv_w, s_conv_b,
          s_b_ln_g, s_b_ln_b, s_q_g, s_k_g]
    sw = [c_ctx, mod_b, post_g, post_b, a_ln_g, a_ln_b, a_w_s, a_b_s, b_conv_w, b_conv_b, b_ln_g, b_ln_b, c_q_g, c_k_g]
    sm = [m_c_ctx, m_mod_b, m_post_g, m_post_b, m_a_ln_g, m_a_ln_b, m_a_w_s, m_a_b_s, m_b_conv_w, m_b_conv_b, m_b_ln_g,
          m_b_ln_b, m_c_q_g, m_c_k_g]
    sv = [v_c_ctx, v_mod_b, v_post_g, v_post_b, v_a_ln_g, v_a_ln_b, v_a_w_s, v_a_b_s, v_b_conv_w, v_b_conv_b, v_b_ln_g,
          v_b_ln_b, v_c_q_g, v_c_k_g]
    shapes = [a.shape for a in sw]
    sg = [g.reshape(s) for g, s in zip(sg, shapes)]
    sd, snm, snv = [_unpack(a, shapes) for a in _adam_plain(_pack(sg), _pack(sw), _pack(sm), _pack(sv), "adam_small")]

    def small(k):
        return [sg[k], sd[k], snm[k], snv[k]]

    per_weight = [small(0), o_mod_w, small(1), small(2), small(3), o_a_w_in, small(4), small(5), small(6), small(7),
                  o_a_w_out, o_b_w_in, small(8), small(9), small(10), small(11), o_b_w_out, o_c_w_in, small(12), small(13),
                  o_c_w_out]
    outs = [loss, grad_x]
    for kind in range(4):
        outs += [pw[kind] for pw in per_weight]
    return tuple(outs)
```

```python
import functools
import math

import jax
import jax.numpy as jnp
from jax import lax
from jax.experimental import pallas as pl
from jax.experimental.pallas import tpu as pltpu

F32 = jnp.float32
BF16 = jnp.bfloat16
MXU_DTYPE = jnp.bfloat16

DEPTH = 4
GRID_W = 64
CHUNK = 128
SGU_GROUPS = 16
CONV_W = 31
CONV_HALO = 16
HEAD_DIM = 128
N_KV_HEADS = 4
ROPE_THETA = 10000.0
DEEPNORM_ALPHA = (2 * DEPTH) ** 0.25
LN_EPS = 1e-6
ADAM_LR, ADAM_B1, ADAM_B2, ADAM_EPS, ADAM_WD, ADAM_STEP = 0.001, 0.9, 0.999, 1e-08, 0.01, 10

N_DEV = 8
V7X_VMEM_BYTES = 64 * 1024 * 1024
V7X_VMEM_CLAIM = V7X_VMEM_BYTES * 7 // 8
LANES = 128

NN = ((1,), (0,))
NT = ((1,), (1,))
TN = ((0,), (0,))


def _dot(a, b, dims):
    return lax.dot_general(a.astype(MXU_DTYPE), b.astype(MXU_DTYPE), (dims, ((), ())), preferred_element_type=F32)


def _pick(n, prefs):
    for p in prefs:
        if n % p == 0:
            return p
    raise ValueError(f"no tile for {n} among {prefs}")


def _params(sem, vmem_bytes):
    assert vmem_bytes <= V7X_VMEM_CLAIM, (vmem_bytes, V7X_VMEM_CLAIM)
    return pltpu.CompilerParams(dimension_semantics=sem, vmem_limit_bytes=V7X_VMEM_CLAIM)


def _nbytes(shape, dtype):
    return math.prod(shape) * jnp.dtype(dtype).itemsize


def _sigmoid(x):
    return jax.nn.sigmoid(x)


def _silu(x):
    return x * jax.nn.sigmoid(x)


def _dsilu(x):
    s = jax.nn.sigmoid(x)
    return s * (1.0 + x * (1.0 - s))


def _ln_stats(x):
    mu = jnp.mean(x, axis=-1, keepdims=True)
    xc = x - mu
    var = jnp.mean(xc * xc, axis=-1, keepdims=True)
    return xc, lax.rsqrt(var + LN_EPS)


def _ln(x, g, b):
    xc, rstd = _ln_stats(x)
    return xc * rstd * g + b


def _mesh_pos():
    return lax.axis_index("x"), lax.axis_index("y"), lax.axis_index("c")


def _dev_index(p):
    return 4 * p[0] + 2 * p[1] + p[2]


class _Comm:
    def __init__(self, inputs, out_shapes, sems, start, finish):
        self.inputs, self.out_shapes, self.sems, self.start, self.finish = inputs, out_shapes, sems, start, finish


def _gather_comm(xs):
    n = len(xs)

    def place():
        x, y, c = _mesh_pos()
        return (x, y, c), (x, y, 1 - c), [(1 - x, y), (x, 1 - y), (1 - x, 1 - y)], c

    def copier(x_refs, o_refs, sems):
        send_sems, recv_sems, _ = sems

        def copy(t, k, block, to, from_input=False):
            dst = o_refs[t].at[_dev_index(block)]
            return pltpu.make_async_remote_copy(
                src_ref=x_refs[t] if from_input else dst, dst_ref=dst,
                send_sem=send_sems.at[t, k], recv_sem=recv_sems.at[t, k],
                device_id=to, device_id_type=pl.DeviceIdType.MESH)

        return copy

    def own(x_refs, o_refs, sems, t, me):
        return pltpu.make_async_copy(x_refs[t], o_refs[t].at[_dev_index(me)], sems[2].at[t])

    def first_copies(copy, t, me, sibling, chips, c):
        return [copy(t, 0, me, sibling, True)] + [copy(t, 1 + j, me, (*chip, c), True) for j, chip in enumerate(chips)]

    def start(x_refs, o_refs, sems):
        me, sibling, chips, c = place()
        copy = copier(x_refs, o_refs, sems)
        for t in range(n):
            own(x_refs, o_refs, sems, t, me).start()
            for cp in first_copies(copy, t, me, sibling, chips, c):
                cp.start()

    def finish(x_refs, o_refs, sems):
        me, sibling, chips, c = place()
        copy = copier(x_refs, o_refs, sems)
        passed = []
        for t in range(n):
            for j, chip in enumerate(chips):
                copy(t, 1 + j, (*chip, c), me).wait_recv()
                cp = copy(t, 4 + j, (*chip, c), sibling)
                cp.start()
                passed.append(cp)
        for t in range(n):
            copy(t, 0, sibling, me).wait_recv()
            for j, chip in enumerate(chips):
                copy(t, 4 + j, (*chip, 1 - c), me).wait_recv()
        for t in range(n):
            for cp in first_copies(copy, t, me, sibling, chips, c):
                cp.wait_send()
        for cp in passed:
            cp.wait_send()
        for t in range(n):
            own(x_refs, o_refs, sems, t, me).wait()

    sems = [pltpu.SemaphoreType.DMA((n, 7)), pltpu.SemaphoreType.DMA((n, 7)), pltpu.SemaphoreType.DMA((n,))]
    return _Comm(list(xs), [jax.ShapeDtypeStruct((N_DEV, *a.shape), a.dtype) for a in xs], sems, start, finish)


def _exchange_comm(gs):
    n = len(gs)

    def copies(g_refs, r_refs, sems):
        send_sems, recv_sems, local_sems = sems
        x, y, c = _mesh_pos()
        me = _dev_index((x, y, c))
        out = []
        for t in range(n):
            out.append(pltpu.make_async_copy(g_refs[t].at[me], r_refs[t].at[me], local_sems.at[t]))
            for k in range(1, N_DEV):
                fx, fy, fc = (k >> 2) & 1, (k >> 1) & 1, k & 1
                peer = (1 - x if fx else x, 1 - y if fy else y, 1 - c if fc else c)
                out.append(pltpu.make_async_remote_copy(
                    src_ref=g_refs[t].at[_dev_index(peer)], dst_ref=r_refs[t].at[me],
                    send_sem=send_sems.at[t, k - 1], recv_sem=recv_sems.at[t, k - 1],
                    device_id=peer, device_id_type=pl.DeviceIdType.MESH))
        return out

    def start(g_refs, r_refs, sems):
        for cp in copies(g_refs, r_refs, sems):
            cp.start()

    def finish(g_refs, r_refs, sems):
        for cp in copies(g_refs, r_refs, sems):
            cp.wait()

    sems = [pltpu.SemaphoreType.DMA((n, 7)), pltpu.SemaphoreType.DMA((n, 7)), pltpu.SemaphoreType.DMA((n,))]
    return _Comm(list(gs), [jax.ShapeDtypeStruct(g.shape, g.dtype) for g in gs], sems, start, finish)


N_CHIPS = N_DEV // 2


def _started_and_waited(copies):
    def start(in_refs, out_refs, sems):
        for cp in copies(in_refs, out_refs, sems):
            cp.start()

    def finish(in_refs, out_refs, sems):
        for cp in copies(in_refs, out_refs, sems):
            cp.wait()

    return start, finish


def _pair_exchange_comm(gs):
    n = len(gs)

    def copies(g_refs, r_refs, sems):
        send_sems, recv_sems = sems
        x, y, c = _mesh_pos()
        return [pltpu.make_async_remote_copy(
            src_ref=g_refs[t].at[2 * q + 1 - c], dst_ref=r_refs[t].at[q],
            send_sem=send_sems.at[t, q], recv_sem=recv_sems.at[t, q],
            device_id=(x, y, 1 - c), device_id_type=pl.DeviceIdType.MESH) for t in range(n) for q in range(N_CHIPS)]

    sems = [pltpu.SemaphoreType.DMA((n, N_CHIPS)), pltpu.SemaphoreType.DMA((n, N_CHIPS))]
    return _Comm(list(gs), [jax.ShapeDtypeStruct((N_CHIPS, *g.shape[1:]), g.dtype) for g in gs], sems,
                 *_started_and_waited(copies))


def _chip_exchange_comm(hs):
    n = len(hs)

    def copies(h_refs, r_refs, sems):
        send_sems, recv_sems, local_sems = sems
        x, y, c = _mesh_pos()
        mine = 2 * x + y
        out = []
        for t in range(n):
            out.append(pltpu.make_async_copy(h_refs[t].at[mine], r_refs[t].at[mine], local_sems.at[t]))
            for k in range(1, N_CHIPS):
                px, py = (1 - x if (k >> 1) & 1 else x), (1 - y if k & 1 else y)
                out.append(pltpu.make_async_remote_copy(
                    src_ref=h_refs[t].at[2 * px + py], dst_ref=r_refs[t].at[mine],
                    send_sem=send_sems.at[t, k - 1], recv_sem=recv_sems.at[t, k - 1],
                    device_id=(px, py, c), device_id_type=pl.DeviceIdType.MESH))
        return out

    sems = [pltpu.SemaphoreType.DMA((n, N_CHIPS - 1)), pltpu.SemaphoreType.DMA((n, N_CHIPS - 1)), pltpu.SemaphoreType.DMA((n,))]
    return _Comm(list(hs), [jax.ShapeDtypeStruct(h.shape, h.dtype) for h in hs], sems, *_started_and_waited(copies))


def _pair_add(mine, theirs, name):
    q, rows, cols = mine.shape
    tr = _pick(rows, (512, 256, 128, 64, 32, 16))

    def body(a_ref, b_ref, o_ref):
        o_ref[...] = (a_ref[...].astype(F32) + b_ref[...].astype(F32)).astype(o_ref.dtype)

    blk = pl.BlockSpec((None, tr, cols), lambda s, i: (s, i, 0))
    return pl.pallas_call(
        body, name=name, grid=(q, rows // tr), in_specs=[blk, blk], out_specs=blk,
        out_shape=jax.ShapeDtypeStruct(mine.shape, mine.dtype),
        compiler_params=_params(("parallel", "parallel"), 16 * _nbytes((tr, cols), F32)),
    )(mine, theirs)


def _comm_call(comm, name):
    n_in, n_out = len(comm.inputs), len(comm.out_shapes)

    def body(*refs):
        ins, outs, sems = refs[:n_in], refs[n_in:n_in + n_out], refs[n_in + n_out:]
        comm.start(ins, outs, sems)
        comm.finish(ins, outs, sems)

    hbm = pl.BlockSpec(memory_space=pl.ANY)
    return pl.pallas_call(
        body, name=name, out_shape=comm.out_shapes, in_specs=[hbm] * n_in, out_specs=[hbm] * n_out,
        scratch_shapes=comm.sems)(*comm.inputs)


def _call(body, operands, *, name, grid, in_specs, out_specs, out_shape, scratch_shapes=(), compiler_params, comm=None):
    single = not isinstance(out_shape, (list, tuple))
    out_shape = [out_shape] if single else list(out_shape)
    out_specs = [out_specs] if single else list(out_specs)
    scratch_shapes = list(scratch_shapes)
    if comm is None:
        res = pl.pallas_call(
            body, name=name, grid=grid, in_specs=list(in_specs), out_specs=out_specs, out_shape=out_shape,
            scratch_shapes=scratch_shapes, compiler_params=compiler_params)(*operands)
        return res[0] if single else res
    n_in, n_out, n_scr = len(in_specs), len(out_specs), len(scratch_shapes)
    c_in, c_out = len(comm.inputs), len(comm.out_shapes)

    def with_comm(*refs):
        ins, c_ins = refs[:n_in], refs[n_in:n_in + c_in]
        o0 = n_in + c_in
        outs, c_outs = refs[o0:o0 + n_out], refs[o0 + n_out:o0 + n_out + c_out]
        s0 = o0 + n_out + c_out
        scr, sems = refs[s0:s0 + n_scr], refs[s0 + n_scr:]
        ids = [pl.program_id(a) for a in range(len(grid))]
        first = functools.reduce(jnp.logical_and, [i == 0 for i in ids])
        last = functools.reduce(jnp.logical_and, [i == g - 1 for i, g in zip(ids, grid)])

        @pl.when(first)
        def _():
            comm.start(c_ins, c_outs, sems)

        body(*ins, *outs, *scr)

        @pl.when(last)
        def _():
            comm.finish(c_ins, c_outs, sems)

    hbm = pl.BlockSpec(memory_space=pl.ANY)
    params = pltpu.CompilerParams(dimension_semantics=("arbitrary",) * len(grid),
                                  vmem_limit_bytes=compiler_params.vmem_limit_bytes)
    res = pl.pallas_call(
        with_comm, name=name, grid=grid, in_specs=list(in_specs) + [hbm] * c_in, out_specs=out_specs + [hbm] * c_out,
        out_shape=out_shape + list(comm.out_shapes), scratch_shapes=scratch_shapes + list(comm.sems),
        compiler_params=params)(*operands, *comm.inputs)
    return (res[0] if single else res[:n_out]), res[n_out:]


def _all_gather(xs, name):
    return _comm_call(_gather_comm(xs), name)


ROW_TILES = (1088, 1024, 768, 544, 512, 384, 272, 256, 128)
TOKEN_K_TILES = (2176, 2048, 1088, 1024, 768, 512, 384, 256, 128)
COL_TILES = (1024, 768, 640, 512, 384, 256, 128)
DEEP_K = 2048


def _mm_nn(a, w, *, planes=1, name, comm=None):
    m, k = a.shape
    if w.ndim == 3:
        nd_w = w.shape[2]
        n = w.shape[0] * nd_w
    else:
        nd_w = n = w.shape[1]
    npl = n // planes
    tm = _pick(m, ROW_TILES)
    tn = _pick(math.gcd(nd_w, npl), COL_TILES if k <= DEEP_K else COL_TILES[3:])
    r, rp = nd_w // tn, npl // tn
    if w.ndim == 3:
        w_spec = pl.BlockSpec((None, k, tn), lambda i, j: (j // r, 0, j % r))
    else:
        w_spec = pl.BlockSpec((k, tn), lambda i, j: (0, j))
    if planes > 1:
        o_spec = pl.BlockSpec((None, tm, tn), lambda i, j: (j // rp, i, j % rp))
        out_shape = jax.ShapeDtypeStruct((planes, m, npl), F32)
    else:
        o_spec = pl.BlockSpec((tm, tn), lambda i, j: (i, j))
        out_shape = jax.ShapeDtypeStruct((m, n), F32)

    def body(a_ref, w_ref, o_ref):
        o_ref[...] = _dot(a_ref[...], w_ref[...], NN)

    vmem = 2 * (_nbytes((tm, k), a.dtype) + _nbytes((k, tn), w.dtype) + _nbytes((tm, tn), F32)) + _nbytes((tm, tn), F32)
    return _call(
        body, (a, w), name=name, grid=(m // tm, n // tn),
        in_specs=[pl.BlockSpec((tm, k), lambda i, j: (i, 0)), w_spec], out_specs=o_spec, out_shape=out_shape,
        compiler_params=_params(("parallel", "arbitrary"), vmem), comm=comm)


def _mm_nt(a, w, *, name):
    m, k = a.shape
    n = w.shape[0]
    tm = _pick(m, ROW_TILES)
    tn = _pick(n, COL_TILES)

    def body(a_ref, w_ref, o_ref):
        o_ref[...] = _dot(a_ref[...], w_ref[...], NT)

    vmem = 2 * (_nbytes((tm, k), a.dtype) + _nbytes((tn, k), w.dtype) + _nbytes((tm, tn), F32)) + _nbytes((tm, tn), F32)
    return pl.pallas_call(
        body, name=name, grid=(m // tm, n // tn),
        in_specs=[pl.BlockSpec((tm, k), lambda i, j: (i, 0)), pl.BlockSpec((tn, k), lambda i, j: (j, 0))],
        out_specs=pl.BlockSpec((tm, tn), lambda i, j: (i, j)), out_shape=jax.ShapeDtypeStruct((m, n), F32),
        compiler_params=_params(("parallel", "arbitrary"), vmem),
    )(a, w)


def _mm_nt_blocked(a, w, *, name, comm=None):
    nd, n, kd = w.shape
    if a.ndim == 3:
        p, m, kp = a.shape
    else:
        (m, kp), p = a.shape, 1
    tk = _pick(math.gcd(kd, kp), COL_TILES)
    ra, rw = kp // tk, kd // tk
    nk = nd * rw
    tm = _pick(m, ROW_TILES)
    if a.ndim == 3:
        a_spec = pl.BlockSpec((None, tm, tk), lambda i, kk: (kk // ra, i, kk % ra))
    else:
        a_spec = pl.BlockSpec((tm, tk), lambda i, kk: (i, kk))

    def body(a_ref, w_ref, o_ref, acc_ref):
        kk = pl.program_id(1)

        @pl.when(kk == 0)
        def _():
            acc_ref[...] = jnp.zeros_like(acc_ref)

        acc_ref[...] += _dot(a_ref[...], w_ref[...], NT)

        @pl.when(kk == nk - 1)
        def _():
            o_ref[...] = acc_ref[...]

    vmem = 2 * (_nbytes((tm, tk), a.dtype) + _nbytes((n, tk), w.dtype) + _nbytes((tm, n), F32)) + 2 * _nbytes((tm, n), F32)
    return _call(
        body, (a, w), name=name, grid=(m // tm, nk),
        in_specs=[a_spec, pl.BlockSpec((None, n, tk), lambda i, kk: (kk // rw, 0, kk % rw))],
        out_specs=pl.BlockSpec((tm, n), lambda i, kk: (i, 0)), out_shape=jax.ShapeDtypeStruct((m, n), F32),
        scratch_shapes=[pltpu.VMEM((tm, n), F32)],
        compiler_params=_params(("parallel", "arbitrary"), vmem), comm=comm)


def _mm_tn(a, b, *, blocked, out_dtype, name, comm=None):
    rows, da = a.shape
    if b.ndim == 3:
        p, _, npl = b.shape
    else:
        p, npl = 1, b.shape[1]
    n = p * npl
    nd_w = n // N_DEV if blocked else n
    tk = _pick(rows, TOKEN_K_TILES)
    tm = _pick(da, COL_TILES)
    tn = _pick(math.gcd(nd_w, npl), COL_TILES)
    rb, ro = npl // tn, nd_w // tn
    nk = rows // tk
    if b.ndim == 3:
        b_spec = pl.BlockSpec((None, tk, tn), lambda i, j, kk: (j // rb, kk, j % rb))
    else:
        b_spec = pl.BlockSpec((tk, tn), lambda i, j, kk: (kk, j))
    if blocked:
        o_spec = pl.BlockSpec((None, tm, tn), lambda i, j, kk: (j // ro, i, j % ro))
        out_shape = jax.ShapeDtypeStruct((N_DEV, da, nd_w), out_dtype)
    else:
        o_spec = pl.BlockSpec((tm, tn), lambda i, j, kk: (i, j))
        out_shape = jax.ShapeDtypeStruct((da, n), out_dtype)

    def body(a_ref, b_ref, o_ref, acc_ref):
        kk = pl.program_id(2)

        @pl.when(kk == 0)
        def _():
            acc_ref[...] = jnp.zeros_like(acc_ref)

        acc_ref[...] += _dot(a_ref[...], b_ref[...], TN)

        @pl.when(kk == nk - 1)
        def _():
            o_ref[...] = acc_ref[...].astype(o_ref.dtype)

    vmem = (2 * (_nbytes((tk, tm), a.dtype) + _nbytes((tk, tn), b.dtype) + _nbytes((tm, tn), out_dtype))
            + 3 * _nbytes((tm, tn), F32) + _nbytes((tk, tm), F32))
    return _call(
        body, (a, b), name=name, grid=(da // tm, n // tn, nk),
        in_specs=[pl.BlockSpec((tk, tm), lambda i, j, kk: (kk, i)), b_spec], out_specs=o_spec, out_shape=out_shape,
        scratch_shapes=[pltpu.VMEM((tm, tn), F32)],
        compiler_params=_params(("parallel", "parallel", "arbitrary"), vmem), comm=comm)


ROW_TILE = 256


class _Rows:
    def __init__(self, n_x, n_ctx, tile=ROW_TILE):
        assert n_x % tile == 0 and n_ctx % tile == 0
        self.n_x, self.n_ctx, self.tile = n_x, n_ctx, tile
        self.rows = n_x + n_ctx
        self.nt_x = n_x // tile
        self.nt = self.rows // tile
        self.n_seg = 2 if n_ctx else 1

    def seg(self, i):
        return jnp.where(i >= self.nt_x, 1, 0) if self.n_ctx else 0

    def first_of_seg(self, i):
        return (i == 0) | (i == self.nt_x) if self.n_ctx else i == 0

    def full(self, width):
        return pl.BlockSpec((self.tile, width), lambda i: (i, 0))

    def plane(self, p, width):
        return pl.BlockSpec((None, self.tile, width), lambda i: (p, i, 0))

    def modvec(self, layer, which, width):
        return pl.BlockSpec((None, 1, width), lambda i: ((layer * 2 + self.seg(i)) * 3 + which, 0, 0))

    def seg_acc(self, width):
        return pl.BlockSpec((None, 1, width), lambda i: (self.seg(i), 0, 0))


def _vec(width):
    return pl.BlockSpec((1, width), lambda i: (0, 0))


def _acc(ref, first, val):
    @pl.when(first)
    def _():
        ref[...] = jnp.zeros_like(ref)

    ref[...] += val


def _modulate(xs, modv, layer, rt, name):
    d = xs.shape[1]

    def body(x_ref, sh_ref, sc_ref, o_ref):
        o_ref[...] = (x_ref[...] * (1.0 + sc_ref[...]) + sh_ref[...]).astype(o_ref.dtype)

    return pl.pallas_call(
        body, name=name, grid=(rt.nt,),
        in_specs=[rt.full(d), rt.modvec(layer, 0, d), rt.modvec(layer, 1, d)],
        out_specs=rt.full(d), out_shape=jax.ShapeDtypeStruct(xs.shape, MXU_DTYPE),
        compiler_params=_params(("parallel",), 6 * _nbytes((rt.tile, d), F32)),
    )(xs, modv, modv)


def _post(x, y, gate, pg, pb):
    return _ln(DEEPNORM_ALPHA * x + gate * y, pg, pb)


def _post_fwd(xs, y, modv, layer, pg, pb, rt, name):
    d = xs.shape[1]

    def body(x_ref, y_ref, gate_ref, pg_ref, pb_ref, o_ref):
        o_ref[...] = _post(x_ref[...], y_ref[...], gate_ref[...], pg_ref[...], pb_ref[...])

    return pl.pallas_call(
        body, name=name, grid=(rt.nt,),
        in_specs=[rt.full(d), rt.full(d), rt.modvec(layer, 2, d), _vec(d), _vec(d)],
        out_specs=rt.full(d), out_shape=jax.ShapeDtypeStruct((rt.rows, d), F32),
        compiler_params=_params(("parallel",), 10 * _nbytes((rt.tile, d), F32)),
    )(xs, y, modv, pg, pb)


def _post_bwd(xs, y, dout, modv, layer, pg, pb, rt, name):
    d = xs.shape[1]

    def body(x_ref, y_ref, do_ref, gate_ref, pg_ref, pb_ref, dres_ref, dy_ref, dpg_ref, dpb_ref, dgate_ref):
        i = pl.program_id(0)
        _, vjp = jax.vjp(_post, x_ref[...], y_ref[...], gate_ref[...], pg_ref[...], pb_ref[...])
        dx, dy, dgate, dpg, dpb = vjp(do_ref[...])
        dres_ref[...] = dx
        dy_ref[...] = dy.astype(dy_ref.dtype)
        _acc(dpg_ref, i == 0, dpg)
        _acc(dpb_ref, i == 0, dpb)
        _acc(dgate_ref, rt.first_of_seg(i), dgate)

    return pl.pallas_call(
        body, name=name, grid=(rt.nt,),
        in_specs=[rt.full(d), rt.full(d), rt.full(d), rt.modvec(layer, 2, d), _vec(d), _vec(d)],
        out_specs=[rt.full(d), rt.full(d), _vec(d), _vec(d), rt.seg_acc(d)],
        out_shape=[jax.ShapeDtypeStruct((rt.rows, d), F32), jax.ShapeDtypeStruct((rt.rows, d), MXU_DTYPE),
                   jax.ShapeDtypeStruct((1, d), F32), jax.ShapeDtypeStruct((1, d), F32),
                   jax.ShapeDtypeStruct((rt.n_seg, 1, d), F32)],
        compiler_params=_params(("arbitrary",), 16 * _nbytes((rt.tile, d), F32)),
    )(xs, y, dout, modv, pg, pb)


def _mod_bwd(dres, dh, xs, modv, layer, rt, name, dx_rows=None):
    d = xs.shape[1]
    nt_res = dres.shape[0] // rt.tile
    nt_dx = rt.nt if dx_rows is None else dx_rows // rt.tile

    def body(dres_ref, dh_ref, x_ref, sc_ref, dx_ref, dshift_ref, dscale_ref):
        i = pl.program_id(0)
        dh = dh_ref[...]

        @pl.when(i < nt_dx)
        def _():
            dx_ref[...] = jnp.where(i < nt_res, dres_ref[...], 0.0) + dh * (1.0 + sc_ref[...])

        first = rt.first_of_seg(i)
        _acc(dshift_ref, first, jnp.sum(dh, axis=0, keepdims=True))
        _acc(dscale_ref, first, jnp.sum(dh * x_ref[...], axis=0, keepdims=True))

    def clamped(nt):
        return pl.BlockSpec((rt.tile, d), lambda i: (jnp.minimum(i, nt - 1), 0))

    return pl.pallas_call(
        body, name=name, grid=(rt.nt,),
        in_specs=[clamped(nt_res), rt.full(d), rt.full(d), rt.modvec(layer, 1, d)],
        out_specs=[clamped(nt_dx), rt.seg_acc(d), rt.seg_acc(d)],
        out_shape=[jax.ShapeDtypeStruct((nt_dx * rt.tile, d), F32), jax.ShapeDtypeStruct((rt.n_seg, 1, d), F32),
                   jax.ShapeDtypeStruct((rt.n_seg, 1, d), F32)],
        compiler_params=_params(("arbitrary",), 10 * _nbytes((rt.tile, d), F32)),
    )(dres, dh, xs, modv)


def _cm_mid_fwd(z3, ln_g, ln_b, w_s, b_s_t, name, comm=None):
    _, rows, e = z3.shape
    groups = w_s.shape[0]
    gw = e // groups

    def body(z_ref, lg_ref, lb_ref, ws_ref, bs_ref, t_ref):
        vn = _ln(z_ref[1], lg_ref[...], lb_ref[...])
        for h in range(groups):
            cols = slice(h * gw, (h + 1) * gw)
            s = _dot(ws_ref[h], vn[:, cols], NN) + bs_ref[:, h:h + 1]
            t_ref[:, cols] = (z_ref[0, :, cols] * s * _silu(z_ref[2, :, cols])).astype(t_ref.dtype)

    return _call(
        body, (z3, ln_g, ln_b, w_s, b_s_t), name=name, grid=(rows // CHUNK,),
        in_specs=[pl.BlockSpec((3, CHUNK, e), lambda i: (0, i, 0)), _vec(e), _vec(e),
                  pl.BlockSpec(w_s.shape, lambda i: (0, 0, 0)), pl.BlockSpec(b_s_t.shape, lambda i: (0, 0))],
        out_specs=pl.BlockSpec((CHUNK, e), lambda i: (i, 0)), out_shape=jax.ShapeDtypeStruct((rows, e), MXU_DTYPE),
        compiler_params=_params(("parallel",), 12 * _nbytes((CHUNK, e), F32)), comm=comm)


def _cm_mid_bwd(z3, dt, ln_g, ln_b, w_s, b_s_t, name, comm=None):
    _, rows, e = z3.shape
    groups = w_s.shape[0]
    gw = e // groups

    def body(z_ref, dt_ref, lg_ref, lb_ref, ws_ref, bs_ref, dz_ref, dlg_ref, dlb_ref, dws_ref, dbs_ref, dvn_ref):
        i = pl.program_id(0)
        first = i == 0
        v = z_ref[1]
        vn, ln_vjp = jax.vjp(_ln, v, lg_ref[...], lb_ref[...])

        @pl.when(first)
        def _():
            dws_ref[...] = jnp.zeros_like(dws_ref)
            dbs_ref[...] = jnp.zeros_like(dbs_ref)

        for h in range(groups):
            cols = slice(h * gw, (h + 1) * gw)
            vn_h = vn[:, cols]
            s = _dot(ws_ref[h], vn_h, NN) + bs_ref[:, h:h + 1]
            u, g, dth = z_ref[0, :, cols], z_ref[2, :, cols], dt_ref[:, cols]
            sg = _silu(g)
            dz_ref[0, :, cols] = (dth * s * sg).astype(dz_ref.dtype)
            dz_ref[2, :, cols] = (dth * u * s * _dsilu(g)).astype(dz_ref.dtype)
            ds = dth * u * sg
            dvn_ref[:, cols] = _dot(ws_ref[h], ds, TN)
            dws_ref[h] += _dot(ds, vn_h, NT)
            dbs_ref[:, h:h + 1] += jnp.sum(ds, axis=1, keepdims=True)
        dv, dlg, dlb = ln_vjp(dvn_ref[...])
        dz_ref[1] = dv.astype(dz_ref.dtype)
        _acc(dlg_ref, first, dlg)
        _acc(dlb_ref, first, dlb)

    return _call(
        body, (z3, dt, ln_g, ln_b, w_s, b_s_t), name=name, grid=(rows // CHUNK,),
        in_specs=[pl.BlockSpec((3, CHUNK, e), lambda i: (0, i, 0)), pl.BlockSpec((CHUNK, e), lambda i: (i, 0)), _vec(e), _vec(e),
                  pl.BlockSpec(w_s.shape, lambda i: (0, 0, 0)), pl.BlockSpec(b_s_t.shape, lambda i: (0, 0))],
        out_specs=[pl.BlockSpec((3, CHUNK, e), lambda i: (0, i, 0)), _vec(e), _vec(e),
                   pl.BlockSpec(w_s.shape, lambda i: (0, 0, 0)), pl.BlockSpec(b_s_t.shape, lambda i: (0, 0))],
        out_shape=[jax.ShapeDtypeStruct((3, rows, e), MXU_DTYPE), jax.ShapeDtypeStruct((1, e), F32),
                   jax.ShapeDtypeStruct((1, e), F32), jax.ShapeDtypeStruct(w_s.shape, F32),
                   jax.ShapeDtypeStruct(b_s_t.shape, F32)],
        scratch_shapes=[pltpu.VMEM((CHUNK, e), F32)],
        compiler_params=_params(("arbitrary",), 20 * _nbytes((CHUNK, e), F32)), comm=comm)


CONV_COL_TILE = 512


def _conv_specs(rt, tc, planes):
    per = rt.tile // CONV_HALO
    last = rt.rows // CONV_HALO - 1
    if planes:
        cur = pl.BlockSpec((planes, rt.tile, tc), lambda j, i: (0, i, j))
        prev = pl.BlockSpec((planes, CONV_HALO, tc), lambda j, i: (0, jnp.maximum(i * per - 1, 0), j))
        nxt = pl.BlockSpec((planes, CONV_HALO, tc), lambda j, i: (0, jnp.minimum((i + 1) * per, last), j))
    else:
        cur = pl.BlockSpec((rt.tile, tc), lambda j, i: (i, j))
        prev = pl.BlockSpec((CONV_HALO, tc), lambda j, i: (jnp.maximum(i * per - 1, 0), j))
        nxt = pl.BlockSpec((CONV_HALO, tc), lambda j, i: (jnp.minimum((i + 1) * per, last), j))
    return cur, prev, nxt


def _halo_ok(rt, i):
    prev_ok = (i != 0) & (i != rt.nt_x)
    next_ok = (i != rt.nt_x - 1) & (i != rt.nt - 1)
    return prev_ok, next_ok


def _glu(ref):
    return ref[0] * _sigmoid(ref[1])


def _padded(cur, prev, nxt, prev_ok, next_ok):
    return jnp.concatenate([jnp.where(prev_ok, prev, 0.0), cur, jnp.where(next_ok, nxt, 0.0)], axis=0)


SUBLANES = 8
CONV_ROW_BLOCK = 16
CONV_DW_ROWS, CONV_DW_TAPS = 16, 4


def _phase_scratch(tr, tc):
    return pltpu.VMEM((SUBLANES, tr + 2 * CONV_HALO - SUBLANES, tc), F32)


def _store_phases(rot_ref, pad):
    rows = rot_ref.shape[1]
    for b in range(SUBLANES):
        rot_ref[b] = pad[b:b + rows, :]


def _tap_rows(rot_ref, r0, off, g):
    a, b = divmod(off, SUBLANES)
    return rot_ref[b, pl.ds(r0 + SUBLANES * (a + g), SUBLANES), :]


def _conv_rows(rot_ref, w_ref, r0, offs, init):
    tc = rot_ref.shape[2]
    accs = [init] * (CONV_ROW_BLOCK // SUBLANES)
    for k, off in enumerate(offs):
        wk = jnp.broadcast_to(w_ref[k:k + 1, :], (SUBLANES, tc))
        accs = [acc + wk * _tap_rows(rot_ref, r0, off, g) for g, acc in enumerate(accs)]
    return jnp.concatenate(accs, axis=0)


def _conv_fwd(z3, conv_w, conv_b, rt, name, comm=None):
    _, rows, e = z3.shape
    tc = _pick(e, (CONV_COL_TILE, 256, 128))
    tr = rt.tile

    def body(cur_ref, prev_ref, next_ref, w_ref, b_ref, o_ref, rot_ref):
        prev_ok, next_ok = _halo_ok(rt, pl.program_id(1))
        _store_phases(rot_ref, _padded(_glu(cur_ref), _glu(prev_ref), _glu(next_ref), prev_ok, next_ok))
        bias = jnp.broadcast_to(b_ref[...], (SUBLANES, tc))
        offs = [CONV_HALO - CONV_W // 2 + k for k in range(CONV_W)]

        def rows_block(rb, carry):
            r0 = pl.multiple_of(rb * CONV_ROW_BLOCK, CONV_ROW_BLOCK)
            o_ref[pl.ds(r0, CONV_ROW_BLOCK), :] = _conv_rows(rot_ref, w_ref, r0, offs, bias)
            return carry

        lax.fori_loop(0, tr // CONV_ROW_BLOCK, rows_block, 0)

    cur, prev, nxt = _conv_specs(rt, tc, 2)
    return _call(
        body, (z3, z3, z3, conv_w, conv_b), name=name, grid=(e // tc, rt.nt),
        in_specs=[cur, prev, nxt, pl.BlockSpec((CONV_W, tc), lambda j, i: (0, j)), pl.BlockSpec((1, tc), lambda j, i: (0, j))],
        out_specs=pl.BlockSpec((tr, tc), lambda j, i: (i, j)), out_shape=jax.ShapeDtypeStruct((rows, e), F32),
        scratch_shapes=[_phase_scratch(tr, tc)],
        compiler_params=_params(("parallel", "arbitrary"), 32 * _nbytes((tr, tc), F32)), comm=comm)


def _conv_bwd(z3, dy1, dg, conv_w, rt, name, comm=None):
    _, rows, e = z3.shape
    tc = _pick(e, (CONV_COL_TILE, 256, 128))
    tr = rt.tile

    def body(cur_ref, prev_ref, next_ref, dcur_ref, dprev_ref, dnext_ref, dg_ref, w_ref, dz_ref, dw_ref, db_ref, rot_ref, drot_ref):
        i = pl.program_id(1)
        prev_ok, next_ok = _halo_ok(rt, i)
        _store_phases(rot_ref, _padded(_glu(cur_ref), _glu(prev_ref), _glu(next_ref), prev_ok, next_ok))
        _store_phases(drot_ref, _padded(dcur_ref[...], dprev_ref[...], dnext_ref[...], prev_ok, next_ok))
        n_blocks = tr // CONV_ROW_BLOCK

        @pl.when(i == 0)
        def _():
            dw_ref[...] = jnp.zeros_like(dw_ref)
            db_ref[...] = jnp.zeros_like(db_ref)

        roffs = [CONV_HALO + CONV_W // 2 - k for k in range(CONV_W)]
        zero = jnp.zeros((SUBLANES, tc), F32)

        def dgate_block(rb, carry):
            r0 = pl.multiple_of(rb * CONV_ROW_BLOCK, CONV_ROW_BLOCK)
            dy0 = _conv_rows(drot_ref, w_ref, r0, roffs, zero)
            rws = pl.ds(r0, CONV_ROW_BLOCK)
            a, sb = cur_ref[0, rws, :], _sigmoid(cur_ref[1, rws, :])
            dz_ref[0, rws, :] = (dy0 * sb).astype(dz_ref.dtype)
            dz_ref[1, rws, :] = (dy0 * a * sb * (1.0 - sb)).astype(dz_ref.dtype)
            return carry

        lax.fori_loop(0, n_blocks, dgate_block, 0)
        dz_ref[2] = dg_ref[...]

        groups = CONV_DW_ROWS // SUBLANES
        for k0 in range(0, CONV_W, CONV_DW_TAPS):
            taps = list(range(k0, min(k0 + CONV_DW_TAPS, CONV_W)))

            accs = [zero] * (len(taps) * groups)
            for r0 in range(0, tr, CONV_DW_ROWS):
                dy = [dcur_ref[r0 + SUBLANES * g:r0 + SUBLANES * (g + 1), :] for g in range(groups)]
                accs = [accs[t * groups + g] + dy[g] * _tap_rows(rot_ref, r0, CONV_HALO - CONV_W // 2 + k, g)
                        for t, k in enumerate(taps) for g in range(groups)]
            for t, k in enumerate(taps):
                tot = functools.reduce(jnp.add, accs[t * groups:(t + 1) * groups])
                dw_ref[k:k + 1, :] += jnp.sum(tot, axis=0, keepdims=True)
        db_ref[...] += jnp.sum(dcur_ref[...], axis=0, keepdims=True)

    cur, prev, nxt = _conv_specs(rt, tc, 2)
    dcur, dprev, dnxt = _conv_specs(rt, tc, 0)
    return _call(
        body, (z3, z3, z3, dy1, dy1, dy1, dg, conv_w), name=name, grid=(e // tc, rt.nt),
        in_specs=[cur, prev, nxt, dcur, dprev, dnxt, pl.BlockSpec((tr, tc), lambda j, i: (i, j)),
                  pl.BlockSpec((CONV_W, tc), lambda j, i: (0, j))],
        out_specs=[pl.BlockSpec((3, tr, tc), lambda j, i: (0, i, j)), pl.BlockSpec((CONV_W, tc), lambda j, i: (0, j)),
                   pl.BlockSpec((1, tc), lambda j, i: (0, j))],
        out_shape=[jax.ShapeDtypeStruct((3, rows, e), MXU_DTYPE), jax.ShapeDtypeStruct((CONV_W, e), F32),
                   jax.ShapeDtypeStruct((1, e), F32)],
        scratch_shapes=[_phase_scratch(tr, tc), _phase_scratch(tr, tc)],
        compiler_params=_params(("parallel", "arbitrary"), 48 * _nbytes((tr, tc), F32)), comm=comm)


def _conv_mid(y1, g, ln_g, ln_b):
    return _silu(_ln(y1, ln_g, ln_b)) * _silu(g)


def _conv_mid_fwd(y1, z3, ln_g, ln_b, rt, name):
    e = y1.shape[1]
    tr = CHUNK

    def body(y_ref, g_ref, lg_ref, lb_ref, t_ref):
        t_ref[...] = _conv_mid(y_ref[...], g_ref[...], lg_ref[...], lb_ref[...]).astype(t_ref.dtype)

    return pl.pallas_call(
        body, name=name, grid=(rt.rows // tr,),
        in_specs=[pl.BlockSpec((tr, e), lambda i: (i, 0)), pl.BlockSpec((None, tr, e), lambda i: (2, i, 0)), _vec(e), _vec(e)],
        out_specs=pl.BlockSpec((tr, e), lambda i: (i, 0)), out_shape=jax.ShapeDtypeStruct((rt.rows, e), MXU_DTYPE),
        compiler_params=_params(("parallel",), 12 * _nbytes((tr, e), F32)),
    )(y1, z3, ln_g, ln_b)


def _conv_mid_bwd(y1, z3, dt, ln_g, ln_b, rt, name):
    e = y1.shape[1]
    tr = CHUNK

    def body(y_ref, g_ref, dt_ref, lg_ref, lb_ref, dy_ref, dg_ref, dlg_ref, dlb_ref):
        first = pl.program_id(0) == 0
        _, vjp = jax.vjp(_conv_mid, y_ref[...], g_ref[...], lg_ref[...], lb_ref[...])
        dy, dg, dlg, dlb = vjp(dt_ref[...])
        dy_ref[...] = dy
        dg_ref[...] = dg.astype(dg_ref.dtype)
        _acc(dlg_ref, first, dlg)
        _acc(dlb_ref, first, dlb)

    row = pl.BlockSpec((tr, e), lambda i: (i, 0))
    return pl.pallas_call(
        body, name=name, grid=(rt.rows // tr,),
        in_specs=[row, pl.BlockSpec((None, tr, e), lambda i: (2, i, 0)), row, _vec(e), _vec(e)],
        out_specs=[row, row, _vec(e), _vec(e)],
        out_shape=[jax.ShapeDtypeStruct((rt.rows, e), F32), jax.ShapeDtypeStruct((rt.rows, e), MXU_DTYPE),
                   jax.ShapeDtypeStruct((1, e), F32), jax.ShapeDtypeStruct((1, e), F32)],
        compiler_params=_params(("arbitrary",), 20 * _nbytes((tr, e), F32)),
    )(y1, z3, dt, ln_g, ln_b)


def _rms(x, g):
    return x * lax.rsqrt(jnp.mean(x * x, axis=-1, keepdims=True) + LN_EPS) * g


def _pair_swap(x):
    lane = lax.broadcasted_iota(jnp.int32, x.shape, x.ndim - 1)
    return jnp.where(lane % 2 == 0, pltpu.roll(x, x.shape[-1] - 1, x.ndim - 1), pltpu.roll(x, 1, x.ndim - 1))


def _rope(x, cos, sin):
    return x * cos + _pair_swap(x) * sin


def _rope_t(dy, cos, sin):
    return dy * cos + _pair_swap(dy * sin)


def _rope_tables(n_x, n_ctx):
    t = jnp.arange(n_x)
    row = (t // GRID_W).astype(F32)
    col = (t % GRID_W).astype(F32)
    axis_dim = HEAD_DIM // 2
    inv = 1.0 / (ROPE_THETA ** (jnp.arange(0, axis_dim, 2, dtype=F32) / axis_dim))
    ang = jnp.concatenate([row[:, None] * inv, col[:, None] * inv], axis=-1)
    cos, sin = jnp.cos(ang), jnp.sin(ang)
    cos2 = jnp.repeat(cos, 2, axis=-1)
    sin2 = jnp.stack([-sin, sin], axis=-1).reshape(n_x, HEAD_DIM)
    cos2 = jnp.concatenate([cos2, jnp.ones((n_ctx, HEAD_DIM), F32)], axis=0)
    sin2 = jnp.concatenate([sin2, jnp.zeros((n_ctx, HEAD_DIM), F32)], axis=0)
    return cos2, sin2


def _qkv_prep(z4, q_g, k_g, cos, sin, d, kvw, rt, name):
    hd = HEAD_DIM
    kb = d // kvw

    def body(q_ref, k_ref, v_ref, qg_ref, kg_ref, cos_ref, sin_ref, qo_ref, ko_ref, vo_ref):
        cos, sin = cos_ref[...], sin_ref[...]
        for h in range(d // hd):
            cols = slice(h * hd, (h + 1) * hd)
            qo_ref[:, cols] = _rope(_rms(q_ref[:, cols], qg_ref[...]), cos, sin).astype(qo_ref.dtype)
        for h in range(kvw // hd):
            cols = slice(h * hd, (h + 1) * hd)
            ko_ref[:, cols] = _rope(_rms(k_ref[:, cols], kg_ref[...]), cos, sin).astype(ko_ref.dtype)
        vo_ref[...] = v_ref[...].astype(vo_ref.dtype)

    tr = rt.tile
    return pl.pallas_call(
        body, name=name, grid=(rt.nt,),
        in_specs=[pl.BlockSpec((tr, d), lambda i: (i, 0)), pl.BlockSpec((tr, kvw), lambda i: (i, kb)),
                  pl.BlockSpec((tr, kvw), lambda i: (i, kb + 1)), _vec(hd), _vec(hd), rt.full(hd), rt.full(hd)],
        out_specs=[rt.full(d), rt.full(kvw), rt.full(kvw)],
        out_shape=[jax.ShapeDtypeStruct((rt.rows, d), MXU_DTYPE), jax.ShapeDtypeStruct((rt.rows, kvw), MXU_DTYPE),
                   jax.ShapeDtypeStruct((rt.rows, kvw), MXU_DTYPE)],
        compiler_params=_params(("parallel",), 8 * _nbytes((tr, d), F32)),
    )(z4, z4, z4, q_g, k_g, cos, sin)


ATTN_Q_TILE = 256
LOG2_E = math.log2(math.e)


def _attn_fwd(qh, kh, vh, n_x, name, comm=None):
    rows, d = qh.shape
    kvw = kh.shape[1]
    hd = HEAD_DIM
    n_kv = kvw // hd
    gqw = d // n_kv
    grp = gqw // hd
    tq = _pick(n_x, (ATTN_Q_TILE, 128))
    scale = hd ** -0.5

    def body(q_ref, k_ref, v_ref, o_ref, lse_ref):
        k, v = k_ref[...], v_ref[...]
        for g in range(grp):
            cols = slice(g * hd, (g + 1) * hd)
            s = _dot(q_ref[:, cols], k, NT)
            m = jnp.max(s, axis=-1, keepdims=True)
            p = jnp.exp2((s - m) * (scale * LOG2_E))
            l = jnp.sum(p, axis=-1, keepdims=True)
            o_ref[:, cols] = _dot(p / l, v, NN)
            lse_ref[:, g:g + 1] = m * scale + jnp.log(l)

    vmem = 4 * _nbytes((rows, hd), MXU_DTYPE) + 4 * _nbytes((tq, rows), F32) + 6 * _nbytes((tq, gqw), F32)
    return _call(
        body, (qh, kh, vh), name=name, grid=(n_kv, n_x // tq),
        in_specs=[pl.BlockSpec((tq, gqw), lambda h, i: (i, h)), pl.BlockSpec((rows, hd), lambda h, i: (0, h)),
                  pl.BlockSpec((rows, hd), lambda h, i: (0, h))],
        out_specs=[pl.BlockSpec((tq, gqw), lambda h, i: (i, h)), pl.BlockSpec((None, tq, grp), lambda h, i: (h, i, 0))],
        out_shape=[jax.ShapeDtypeStruct((n_x, d), F32), jax.ShapeDtypeStruct((n_kv, n_x, grp), F32)],
        compiler_params=_params(("parallel", "arbitrary"), vmem), comm=comm)


def _attn_bwd(qh, kh, vh, do, lse, n_x, name, comm=None):
    rows, d = qh.shape
    kvw = kh.shape[1]
    hd = HEAD_DIM
    n_kv = kvw // hd
    gqw = d // n_kv
    grp = gqw // hd
    tq = _pick(n_x, (ATTN_Q_TILE, 128))
    scale = hd ** -0.5

    def body(q_ref, k_ref, v_ref, do_ref, lse_ref, dq_ref, dk_ref, dv_ref):
        @pl.when(pl.program_id(1) == 0)
        def _():
            dk_ref[...] = jnp.zeros_like(dk_ref)
            dv_ref[...] = jnp.zeros_like(dv_ref)

        k, v = k_ref[...], v_ref[...]
        for g in range(grp):
            cols = slice(g * hd, (g + 1) * hd)
            q, dog = q_ref[:, cols], do_ref[:, cols]
            p = jnp.exp2(_dot(q, k, NT) * (scale * LOG2_E) - lse_ref[:, g:g + 1] * LOG2_E)
            dp = _dot(dog, v, NT)
            ds = (p * (dp - jnp.sum(dp * p, axis=-1, keepdims=True)) * scale).astype(MXU_DTYPE)
            dq_ref[:, cols] = _dot(ds, k, NN)
            dk_ref[...] += _dot(ds, q, TN)
            dv_ref[...] += _dot(p, dog, TN)

    vmem = 4 * _nbytes((rows, hd), MXU_DTYPE) + 4 * _nbytes((rows, hd), F32) + 6 * _nbytes((tq, rows), F32) + 8 * _nbytes((tq, gqw), F32)
    qspec = pl.BlockSpec((tq, gqw), lambda h, i: (i, h))
    kspec = pl.BlockSpec((rows, hd), lambda h, i: (0, h))
    return _call(
        body, (qh, kh, vh, do, lse), name=name, grid=(n_kv, n_x // tq),
        in_specs=[qspec, kspec, kspec, qspec, pl.BlockSpec((None, tq, grp), lambda h, i: (h, i, 0))],
        out_specs=[qspec, kspec, kspec],
        out_shape=[jax.ShapeDtypeStruct((n_x, d), F32), jax.ShapeDtypeStruct((rows, kvw), F32),
                   jax.ShapeDtypeStruct((rows, kvw), F32)],
        compiler_params=_params(("parallel", "arbitrary"), vmem), comm=comm)


def _attn_gate(o, z4, d, kvw, rt, name):
    g0 = (d + 2 * kvw) // kvw
    tr = rt.tile

    def body(o_ref, g_ref, t_ref):
        t_ref[...] = (o_ref[...] * _silu(g_ref[...])).astype(t_ref.dtype)

    tile = pl.BlockSpec((tr, kvw), lambda i, j: (i, j))
    return pl.pallas_call(
        body, name=name, grid=(rt.nt, d // kvw),
        in_specs=[tile, pl.BlockSpec((tr, kvw), lambda i, j: (i, g0 + j))],
        out_specs=tile, out_shape=jax.ShapeDtypeStruct((rt.rows, d), MXU_DTYPE),
        compiler_params=_params(("parallel", "parallel"), 8 * _nbytes((tr, kvw), F32)),
    )(o, z4)


def _attn_gate_bwd(dt, o, z4, d, kvw, rt, name):
    g0 = (d + 2 * kvw) // kvw
    tr = rt.tile

    def body(dt_ref, o_ref, g_ref, do_ref, dg_ref):
        dt_v, g = dt_ref[...], g_ref[...]
        do_ref[...] = (dt_v * _silu(g)).astype(do_ref.dtype)
        dg_ref[...] = (dt_v * o_ref[...] * _dsilu(g)).astype(dg_ref.dtype)

    tile = pl.BlockSpec((tr, kvw), lambda i, j: (i, j))
    return pl.pallas_call(
        body, name=name, grid=(rt.nt, d // kvw),
        in_specs=[tile, tile, pl.BlockSpec((tr, kvw), lambda i, j: (i, g0 + j))],
        out_specs=[tile, tile],
        out_shape=[jax.ShapeDtypeStruct((rt.rows, d), MXU_DTYPE), jax.ShapeDtypeStruct((rt.rows, d), MXU_DTYPE)],
        compiler_params=_params(("parallel", "parallel"), 12 * _nbytes((tr, kvw), F32)),
    )(dt, o, z4)


def _prep_bwd(dxh, z4, col_block, gain, cos, sin, rt, name):
    w = dxh.shape[1]
    hd = HEAD_DIM

    def body(dxh_ref, x_ref, g_ref, cos_ref, sin_ref, dx_ref, dg_ref):
        cos, sin = cos_ref[...], sin_ref[...]
        dg = jnp.zeros((1, hd), F32)
        for h in range(w // hd):
            cols = slice(h * hd, (h + 1) * hd)
            _, vjp = jax.vjp(_rms, x_ref[:, cols], g_ref[...])
            dx, dgh = vjp(_rope_t(dxh_ref[:, cols], cos, sin))
            dx_ref[:, cols] = dx.astype(dx_ref.dtype)
            dg = dg + dgh
        _acc(dg_ref, pl.program_id(0) == 0, dg)

    tr = rt.tile
    return pl.pallas_call(
        body, name=name, grid=(rt.nt,),
        in_specs=[rt.full(w), pl.BlockSpec((tr, w), lambda i: (i, col_block)), _vec(hd), rt.full(hd), rt.full(hd)],
        out_specs=[rt.full(w), _vec(hd)],
        out_shape=[jax.ShapeDtypeStruct((rt.rows, w), MXU_DTYPE), jax.ShapeDtypeStruct((1, hd), F32)],
        compiler_params=_params(("arbitrary",), 12 * _nbytes((tr, w), F32)),
    )(dxh, z4, gain, cos, sin)


def _loss_head(x, target, rt, name):
    d = x.shape[1]

    def body(x_ref, t_ref, dx_ref, l_ref):
        err = x_ref[...] - t_ref[...]
        dx_ref[...] = err / d
        row = jnp.mean(err * err, axis=-1, keepdims=True)
        _acc(l_ref, pl.program_id(0) == 0, jnp.sum(row, axis=0, keepdims=True))

    return pl.pallas_call(
        body, name=name, grid=(rt.nt,),
        in_specs=[rt.full(d), rt.full(d)],
        out_specs=[rt.full(d), pl.BlockSpec((1, 1), lambda i: (0, 0))],
        out_shape=[jax.ShapeDtypeStruct(x.shape, F32), jax.ShapeDtypeStruct((1, 1), F32)],
        compiler_params=_params(("arbitrary",), 8 * _nbytes((rt.tile, d), F32)),
    )(x, target)


def _adamw(w, g, m, v):
    m = ADAM_B1 * m + (1.0 - ADAM_B1) * g
    v = ADAM_B2 * v + (1.0 - ADAM_B2) * (g * g)
    m_hat = m / (1.0 - ADAM_B1 ** ADAM_STEP)
    v_hat = v / (1.0 - ADAM_B2 ** ADAM_STEP)
    delta = -ADAM_LR * (m_hat / (jnp.sqrt(v_hat) + ADAM_EPS) + ADAM_WD * w)
    return delta, m, v


ADAM_TILE_BYTES = 1 << 20


def _adam_tile(rows, cols):
    tr = rows
    while tr % 16 == 0 and tr * cols * 4 > ADAM_TILE_BYTES:
        tr //= 2
    return tr


def _adam_reduce(parts, w, m, v, name, comm=None):
    slots, rows, cols = w.shape
    tr = _adam_tile(rows, cols)
    nt = rows // tr

    def body(*refs):
        p_refs = refs[:slots]
        w_ref, m_ref, v_ref, g_ref, d_ref, mo_ref, vo_ref = refs[slots:]
        for k in range(slots):
            @pl.when(pl.program_id(0) == k)
            def _(p_ref=p_refs[k]):
                g = p_ref[0].astype(F32)
                for part in range(1, p_ref.shape[0]):
                    g = g + p_ref[part].astype(F32)
                g_ref[...] = g
                d_ref[...], mo_ref[...], vo_ref[...] = _adamw(w_ref[...], g, m_ref[...], v_ref[...])

    def part_spec(k):
        return pl.BlockSpec((parts[k].shape[0], tr, cols),
                            lambda s, i: (0, jnp.where(s < k, 0, jnp.where(s == k, i, nt - 1)), 0))

    row = pl.BlockSpec((None, tr, cols), lambda s, i: (s, i, 0))
    sds = jax.ShapeDtypeStruct((slots, rows, cols), F32)
    return _call(
        body, (*parts, w, m, v), name=name, grid=(slots, nt),
        in_specs=[part_spec(k) for k in range(slots)] + [row, row, row],
        out_specs=[row] * 4, out_shape=[sds] * 4,
        compiler_params=_params(("arbitrary", "arbitrary"), 40 * _nbytes((tr, cols), F32)), comm=comm)


def _adam_plain(g, w, m, v, name):
    rows, cols = w.shape
    tr = _adam_tile(rows, cols)

    def body(g_ref, w_ref, m_ref, v_ref, d_ref, mo_ref, vo_ref):
        d_ref[...], mo_ref[...], vo_ref[...] = _adamw(w_ref[...], g_ref[...], m_ref[...], v_ref[...])

    row = pl.BlockSpec((tr, cols), lambda i: (i, 0))
    sds = jax.ShapeDtypeStruct((rows, cols), F32)
    return pl.pallas_call(
        body, name=name, grid=(rows // tr,),
        in_specs=[row] * 4, out_specs=[row] * 3, out_shape=[sds] * 3,
        compiler_params=_params(("parallel",), 32 * _nbytes((tr, cols), F32)),
    )(g, w, m, v)


def _sum_devices(parts, name):
    _, rows, cols = parts.shape
    tr = _adam_tile(rows, cols)

    def body(p_ref, o_ref):
        g = p_ref[0]
        for k in range(1, N_DEV):
            g = g + p_ref[k]
        o_ref[...] = g

    return pl.pallas_call(
        body, name=name, grid=(rows // tr,),
        in_specs=[pl.BlockSpec((N_DEV, tr, cols), lambda i: (0, i, 0))],
        out_specs=pl.BlockSpec((tr, cols), lambda i: (i, 0)), out_shape=jax.ShapeDtypeStruct((rows, cols), F32),
        compiler_params=_params(("parallel",), 24 * _nbytes((tr, cols), F32)),
    )(parts)


COND_ROWS = 16


def _mod_fwd_mm(cond, mod_w, mod_b, name):
    layers, d, w = mod_w.shape

    def body(c_ref, w_ref, b_ref, o_ref):
        o_ref[...] = _dot(_silu(c_ref[...]), w_ref[...], NN) + b_ref[...]

    return pl.pallas_call(
        body, name=name, grid=(layers,),
        in_specs=[pl.BlockSpec((COND_ROWS, d), lambda l: (0, 0)), pl.BlockSpec((None, d, w), lambda l: (l, 0, 0)),
                  pl.BlockSpec((None, 1, w), lambda l: (l, 0, 0))],
        out_specs=pl.BlockSpec((None, COND_ROWS, w), lambda l: (l, 0, 0)),
        out_shape=jax.ShapeDtypeStruct((layers, COND_ROWS, w), F32),
        compiler_params=_params(("parallel",), 4 * _nbytes((d, w), F32)),
    )(cond, mod_w, mod_b)


def _mod_bwd_mm(cond, dm, mod_w, name):
    layers, d, w = mod_w.shape

    def body(c_ref, dm_ref, w_ref, dw_ref, dc_ref):
        dmv = dm_ref[...]
        dw_ref[...] = _dot(_silu(c_ref[...]), dmv, TN)
        _acc(dc_ref, pl.program_id(0) == 0, _dot(dmv, w_ref[...], NT))

    return pl.pallas_call(
        body, name=name, grid=(layers,),
        in_specs=[pl.BlockSpec((COND_ROWS, d), lambda l: (0, 0)), pl.BlockSpec((None, COND_ROWS, w), lambda l: (l, 0, 0)),
                  pl.BlockSpec((None, d, w), lambda l: (l, 0, 0))],
        out_specs=[pl.BlockSpec((None, d, w), lambda l: (l, 0, 0)), pl.BlockSpec((COND_ROWS, d), lambda l: (0, 0))],
        out_shape=[jax.ShapeDtypeStruct((layers, d, w), F32), jax.ShapeDtypeStruct((COND_ROWS, d), F32)],
        compiler_params=_params(("arbitrary",), 6 * _nbytes((d, w), F32)),
    )(cond, dm, mod_w)


PACK_ROWS = 256


def _pack(arrs):
    flat = jnp.concatenate([a.reshape(-1).astype(F32) for a in arrs])
    pad = (-flat.shape[0]) % (PACK_ROWS * LANES)
    return jnp.pad(flat, (0, pad)).reshape(-1, LANES)


def _unpack(flat2d, shapes):
    flat = flat2d.reshape(-1)
    out, off = [], 0
    for s in shapes:
        n = math.prod(s)
        out.append(flat[off:off + n].reshape(s))
        off += n
    return out


def _unpack_dev(g2d, shapes):
    flat = g2d.reshape(N_DEV, -1)
    out, off = [], 0
    for s in shapes:
        n = math.prod(s)
        out.append(flat[:, off:off + n].reshape((N_DEV, *s)))
        off += n
    return out


def kernel(x, c, ctx, c_ctx, mod_w, mod_b, post_g, post_b, a_w_in, a_ln_g, a_ln_b, a_w_s, a_b_s, a_w_out, b_w_in, b_conv_w, b_conv_b, b_ln_g, b_ln_b, b_w_out, c_w_in, c_q_g, c_k_g, c_w_out, loss_target, m_c_ctx, m_mod_w, m_mod_b, m_post_g, m_post_b, m_a_w_in, m_a_ln_g, m_a_ln_b, m_a_w_s, m_a_b_s, m_a_w_out, m_b_w_in, m_b_conv_w, m_b_conv_b, m_b_ln_g, m_b_ln_b, m_b_w_out, m_c_w_in, m_c_q_g, m_c_k_g, m_c_w_out, v_c_ctx, v_mod_w, v_mod_b, v_post_g, v_post_b, v_a_w_in, v_a_ln_g, v_a_ln_b, v_a_w_s, v_a_b_s, v_a_w_out, v_b_w_in, v_b_conv_w, v_b_conv_b, v_b_ln_g, v_b_ln_b, v_b_w_out, v_c_w_in, v_c_q_g, v_c_k_g, v_c_w_out):
    n_x, d = x.shape[1], x.shape[2]
    n_ctx = ctx.shape[1]
    e = a_w_out.shape[1] * N_DEV
    kvw = N_KV_HEADS * HEAD_DIM
    me = _dev_index(_mesh_pos())
    rt_all = _Rows(n_x, n_ctx)
    rt_x = _Rows(n_x, 0)

    small_in = [c[0], a_ln_g, a_ln_b, b_conv_w[0]]
    (g_small,) = _all_gather([_pack(small_in)], "ag_small_params")
    conds, ln_g_all, ln_b_all, conv_w_all = _unpack_dev(g_small, [a.shape for a in small_in])
    a_ln_g_f = jnp.moveaxis(ln_g_all, 0, 1).reshape(a_ln_g.shape[0], 1, e)
    a_ln_b_f = jnp.moveaxis(ln_b_all, 0, 1).reshape(a_ln_b.shape[0], 1, e)
    conv_w_f = jnp.moveaxis(conv_w_all, 0, 1).reshape(CONV_W, e)
    cond = jnp.zeros((COND_ROWS, d), F32).at[:N_DEV].set(conds).at[N_DEV].set(c_ctx)

    wm = mod_w.shape[2]
    mod_b_mine = lax.dynamic_slice_in_dim(mod_b, me * wm, wm, axis=1).reshape(DEPTH, 1, wm)
    (mods_g,) = _all_gather([_mod_fwd_mm(cond, mod_w, mod_b_mine, "mod_fwd")], "ag_mod")
    mods = jnp.moveaxis(mods_g, 0, 2).reshape(DEPTH, COND_ROWS, 3 * d)
    mine = lax.dynamic_index_in_dim(mods, me, axis=1, keepdims=False)
    modv = jnp.stack([mine, mods[:, N_DEV]], axis=1).reshape(DEPTH * 2 * 3, 1, d)

    def gather_of(wt):
        return _gather_comm([wt.astype(MXU_DTYPE)])

    def exchange_of(*gs):
        return _exchange_comm([g if g.ndim == 3 else g.reshape(N_DEV, -1, g.shape[-1]) for g in gs])

    def chip_sums_comm(g, theirs, tag):
        mine = lax.dynamic_index_in_dim(g.reshape(N_CHIPS, 2, *g.shape[1:]), lax.axis_index("c"), axis=1, keepdims=False)
        return _chip_exchange_comm([_pair_add(mine, theirs, f"pair_add_{tag}")])

    def chip_sums_of(g, tag):
        (theirs,) = _comm_call(_pair_exchange_comm([g]), f"pair_{tag}")
        return chip_sums_comm(g, theirs, tag)

    (wa_in0,) = _all_gather([a_w_in[0].astype(MXU_DTYPE)], "ag_w_l0")

    ws_op = a_w_s.astype(MXU_DTYPE)
    bs_t = jnp.swapaxes(a_b_s, 1, 2)
    pg = post_g.reshape(DEPTH, 1, d)
    pb = post_b.reshape(DEPTH, 1, d)

    xs0 = jnp.concatenate([x[0], ctx[0]], axis=0)
    h0 = _modulate(xs0, modv, 0, rt_all, "mod0")
    z0, (wb_in,) = _mm_nn(h0, wa_in0, planes=3, name="l0_in", comm=gather_of(b_w_in[0]))
    t0, (wa_out0,) = _cm_mid_fwd(z0, a_ln_g_f[0], a_ln_b_f[0], ws_op[0], bs_t[0], "l0_mid", comm=gather_of(a_w_out[0]))
    wa_out0 = wa_out0.reshape(-1, d)
    y0, (wb_out,) = _mm_nn(t0, wa_out0, name="l0_out", comm=gather_of(b_w_out[0]))
    xs1 = _post_fwd(xs0, y0, modv, 0, pg[0], pb[0], rt_all, "l0_post")
    h1 = _modulate(xs1, modv, 1, rt_all, "mod1")
    z1, (wc_in,) = _mm_nn(h1, wb_in, planes=3, name="l1_in", comm=gather_of(c_w_in[0]))
    cy1, (wa_in1,) = _conv_fwd(z1, conv_w_f, b_conv_b, rt_all, "l1_conv", comm=gather_of(a_w_in[1]))
    t1 = _conv_mid_fwd(cy1, z1, b_ln_g, b_ln_b, rt_all, "l1_mid")
    y1, (wc_out,) = _mm_nn(t1, wb_out.reshape(-1, d), name="l1_out", comm=gather_of(c_w_out[0]))
    xs2 = _post_fwd(xs1, y1, modv, 1, pg[1], pb[1], rt_all, "l1_post")
    cos, sin = _rope_tables(n_x, n_ctx)
    h2 = _modulate(xs2, modv, 2, rt_all, "mod2")
    z2 = _mm_nn(h2, wc_in, name="l2_in")
    qh, kh, vh = _qkv_prep(z2, c_q_g, c_k_g, cos, sin, d, kvw, rt_all, "l2_prep")
    (o2, lse), (wa_out1,) = _attn_fwd(qh, kh, vh, n_x, "l2_attn", comm=gather_of(a_w_out[1]))
    wb_out, wc_out, wa_out1 = [wt.reshape(-1, d) for wt in (wb_out, wc_out, wa_out1)]
    t2 = _attn_gate(o2, z2, d, kvw, rt_x, "l2_gate")
    y2 = _mm_nn(t2, wc_out, name="l2_out")
    x2 = xs2
    x3 = _post_fwd(x2, y2, modv, 2, pg[2], pb[2], rt_x, "l2_post")
    h3 = _modulate(x3, modv, 3, rt_x, "mod3")
    z3 = _mm_nn(h3, wa_in1, planes=3, name="l3_in")
    t3 = _cm_mid_fwd(z3, a_ln_g_f[1], a_ln_b_f[1], ws_op[1], bs_t[1], "l3_mid")
    y3 = _mm_nn(t3, wa_out1, name="l3_out")
    x4 = _post_fwd(x3, y3, modv, 3, pg[3], pb[3], rt_x, "l3_post")

    dx4, loss_sum = _loss_head(x4, loss_target[0], rt_x, "loss")
    loss = lax.psum(0.5 * loss_sum[0, 0], ("x", "y", "c"))

    gdt = MXU_DTYPE
    dres3, dy3, dpg3, dpb3, dgate3 = _post_bwd(x3, y3, dx4, modv, 3, pg[3], pb[3], rt_x, "l3_post_b")
    dt3 = _mm_nt(dy3, wa_out1, name="l3_dt")
    gw_a_out1 = _mm_tn(t3, dy3, blocked=False, out_dtype=gdt, name="l3_dwout")
    dz3, dlg3, dlb3, dws3, dbs3 = _cm_mid_bwd(z3, dt3, a_ln_g_f[1], a_ln_b_f[1], ws_op[1], bs_t[1], "l3_mid_b")
    gw_a_in1 = _mm_tn(h3, dz3, blocked=True, out_dtype=gdt, name="l3_dwin")
    dh3, (r_a_out1,) = _mm_nt_blocked(dz3, wa_in1, name="l3_dh", comm=exchange_of(gw_a_out1))
    dx3, dshift3, dscale3 = _mod_bwd(dres3, dh3, x3, modv, 3, rt_x, "l3_mod_b")
    dres2, dy2, dpg2, dpb2, dgate2 = _post_bwd(x2, y2, dx3, modv, 2, pg[2], pb[2], rt_x, "l2_post_b")
    dt2 = _mm_nt(dy2, wc_out, name="l2_dt")
    gw_c_out = _mm_tn(t2, dy2, blocked=False, out_dtype=gdt, name="l2_dwout")
    do2, dg2 = _attn_gate_bwd(dt2, o2, z2, d, kvw, rt_x, "l2_gate_b")
    (dqh, dkh, dvh), (r_a_in1,) = _attn_bwd(qh, kh, vh, do2, lse, n_x, "l2_attn_b", comm=exchange_of(gw_a_in1))
    dq2, dqg = _prep_bwd(dqh, z2, 0, c_q_g, cos, sin, rt_x, "l2_qprep_b")
    dk2, dkg = _prep_bwd(dkh, z2, d // kvw, c_k_g, cos, sin, rt_all, "l2_kprep_b")
    zpad = jnp.zeros((n_ctx, d), MXU_DTYPE)
    dz2 = jnp.concatenate([jnp.concatenate([dq2, zpad], axis=0), dk2, dvh.astype(MXU_DTYPE),
                           jnp.concatenate([dg2, zpad], axis=0)], axis=1)
    gw_c_in = _mm_tn(h2, dz2, blocked=True, out_dtype=gdt, name="l2_dwin")
    dh2, (r_c_out,) = _mm_nt_blocked(dz2, wc_in, name="l2_dh", comm=exchange_of(gw_c_out))
    dxs2, dshift2, dscale2 = _mod_bwd(dres2, dh2, xs2, modv, 2, rt_all, "l2_mod_b")
    dres1, dy1, dpg1, dpb1, dgate1 = _post_bwd(xs1, y1, dxs2, modv, 1, pg[1], pb[1], rt_all, "l1_post_b")
    dt1 = _mm_nt(dy1, wb_out, name="l1_dt")
    gw_b_out = _mm_tn(t1, dy1, blocked=False, out_dtype=gdt, name="l1_dwout")
    dcy1, dgc1, dblg, dblb = _conv_mid_bwd(cy1, z1, dt1, b_ln_g, b_ln_b, rt_all, "l1_mid_b")
    (dz1, dconv_w, dconv_b), (r_c_in, r_b_out) = _conv_bwd(z1, dcy1, dgc1, conv_w_f, rt_all, "l1_conv_b",
                                                           comm=exchange_of(gw_c_in, gw_b_out))
    gw_b_in = _mm_tn(h1, dz1, blocked=True, out_dtype=gdt, name="l1_dwin")
    dh1, (pair_b_in,) = _mm_nt_blocked(dz1, wb_in, name="l1_dh", comm=_pair_exchange_comm([gw_b_in]))
    dxs1, dshift1, dscale1 = _mod_bwd(dres1, dh1, xs1, modv, 1, rt_all, "l1_mod_b")
    dres0, dy0, dpg0, dpb0, dgate0 = _post_bwd(xs0, y0, dxs1, modv, 0, pg[0], pb[0], rt_all, "l0_post_b")
    dt0 = _mm_nt(dy0, wa_out0, name="l0_dt")
    gw_a_out0 = _mm_tn(t0, dy0, blocked=False, out_dtype=gdt, name="l0_dwout")
    (dz0, dlg0, dlb0, dws0, dbs0), (r_a_out0,) = _cm_mid_bwd(z0, dt0, a_ln_g_f[0], a_ln_b_f[0], ws_op[0], bs_t[0], "l0_mid_b",
                                                             comm=exchange_of(gw_a_out0))
    gw_a_in0, (r_b_in,) = _mm_tn(h0, dz0, blocked=True, out_dtype=gdt, name="l0_dwin",
                                 comm=chip_sums_comm(gw_b_in, pair_b_in, "b_in"))
    dh0, (r_a_in0,) = _mm_nt_blocked(dz0, wa_in0, name="l0_dh", comm=chip_sums_of(gw_a_in0, "a_in0"))
    dx0, dshift0, dscale0 = _mod_bwd(dres0, dh0, xs0, modv, 0, rt_all, "l0_mod_b", dx_rows=n_x)
    grad_x = dx0[None]

    def seg2(a):
        a = a[:, 0]
        return a if a.shape[0] == 2 else jnp.concatenate([a, jnp.zeros_like(a)], axis=0)

    gate2 = jnp.concatenate([dgate2[:, 0], jnp.zeros((1, d), F32)], axis=0)
    dmod = jnp.stack([
        jnp.concatenate([seg2(dshift0), seg2(dscale0), seg2(dgate0)], axis=1),
        jnp.concatenate([seg2(dshift1), seg2(dscale1), seg2(dgate1)], axis=1),
        jnp.concatenate([seg2(dshift2), seg2(dscale2), gate2], axis=1),
        jnp.concatenate([seg2(dshift3), seg2(dscale3), seg2(dgate3)], axis=1)])

    g_post_g = jnp.concatenate([dpg0, dpg1, dpg2, dpg3], axis=0)
    g_post_b = jnp.concatenate([dpb0, dpb1, dpb2, dpb3], axis=0)
    g_a_ln_g = jnp.concatenate([dlg0, dlg3], axis=0)
    g_a_ln_b = jnp.concatenate([dlb0, dlb3], axis=0)
    g_a_w_s = jnp.stack([dws0, dws3])
    g_a_b_s = jnp.swapaxes(jnp.stack([dbs0, dbs3]), 1, 2)
    small_g = [g_post_g, g_post_b, g_a_w_s, g_a_b_s, dconv_b, dblg, dblb, dqg, dkg, g_a_ln_g, g_a_ln_b, dconv_w,
               dmod[:, 0], dmod[:, 1]]
    small_shapes = [a.shape for a in small_g]
    (gs_all,) = _all_gather([_pack(small_g)], "ag_small_grads")
    o_a_w_in = _adam_reduce([r_a_in0, r_a_in1], a_w_in, m_a_w_in, v_a_w_in, "adam_a_in")
    sums = _unpack(_sum_devices(gs_all, "sum_small"), small_shapes)
    (s_post_g, s_post_b, s_a_w_s, s_a_b_s, s_conv_b, s_b_ln_g, s_b_ln_b, s_q_g, s_k_g, s_a_ln_g, s_a_ln_b, s_conv_w,
     s_dmod_own, s_dmod_ctx) = sums
    grad_mod_b = s_dmod_own + s_dmod_ctx
    wl = a_ln_g.shape[1]
    wcv = b_conv_w.shape[2]
    grad_a_ln_g = lax.dynamic_slice_in_dim(s_a_ln_g, me * wl, wl, axis=1)
    grad_a_ln_b = lax.dynamic_slice_in_dim(s_a_ln_b, me * wl, wl, axis=1)
    grad_b_conv_w = lax.dynamic_slice_in_dim(s_conv_w, me * wcv, wcv, axis=1)[None]

    dmod_dev = _unpack_dev(gs_all, small_shapes)[12]
    dm_rows = jnp.concatenate([jnp.moveaxis(dmod_dev, 0, 1), s_dmod_ctx[:, None],
                               jnp.zeros((DEPTH, COND_ROWS - N_DEV - 1, 3 * d), F32)], axis=1)
    dm_mine = lax.dynamic_slice_in_dim(dm_rows, me * wm, wm, axis=2)
    grad_mod_w, dcond_part = _mod_bwd_mm(cond, dm_mine, mod_w, "mod_bwd")
    (dcond_all,) = _all_gather([dcond_part], "ag_dcond")
    dcond = _sum_devices(dcond_all, "sum_dcond")
    grad_c_ctx = dcond[N_DEV] * _dsilu(c_ctx)

    o_a_w_out = _adam_reduce([r_a_out0, r_a_out1], a_w_out, m_a_w_out, v_a_w_out, "adam_a_out")
    o_b_w_in = _adam_reduce([r_b_in], b_w_in, m_b_w_in, v_b_w_in, "adam_b_in")
    o_b_w_out = _adam_reduce([r_b_out], b_w_out, m_b_w_out, v_b_w_out, "adam_b_out")
    o_c_w_in = _adam_reduce([r_c_in], c_w_in, m_c_w_in, v_c_w_in, "adam_c_in")
    o_c_w_out = _adam_reduce([r_c_out], c_w_out, m_c_w_out, v_c_w_out, "adam_c_out")
    mw_shape = mod_w.shape
    o_mod_w = [grad_mod_w] + [a.reshape(mw_shape) for a in _adam_plain(
        grad_mod_w.reshape(-1, wm), mod_w.reshape(-1, wm), m_mod_w.reshape(-1, wm), v_mod_w.reshape(-1, wm), "adam_mod_w")]

    sg = [grad_c_ctx, grad_mod_b, s_post_g, s_post_b, grad_a_ln_g, grad_a_ln_b, s_a_w_s, s_a_b_s, grad_b_conv_w, s_conv_b,
          s_b_ln_g, s_b_ln_b, s_q_g, s_k_g]
    sw = [c_ctx, mod_b, post_g, post_b, a_ln_g, a_ln_b, a_w_s, a_b_s, b_conv_w, b_conv_b, b_ln_g, b_ln_b, c_q_g, c_k_g]
    sm = [m_c_ctx, m_mod_b, m_post_g, m_post_b, m_a_ln_g, m_a_ln_b, m_a_w_s, m_a_b_s, m_b_conv_w, m_b_conv_b, m_b_ln_g,
          m_b_ln_b, m_c_q_g, m_c_k_g]
    sv = [v_c_ctx, v_mod_b, v_post_g, v_post_b, v_a_ln_g, v_a_ln_b, v_a_w_s, v_a_b_s, v_b_conv_w, v_b_conv_b, v_b_ln_g,
          v_b_ln_b, v_c_q_g, v_c_k_g]
    shapes = [a.shape for a in sw]
    sg = [g.reshape(s) for g, s in zip(sg, shapes)]
    sd, snm, snv = [_unpack(a, shapes) for a in _adam_plain(_pack(sg), _pack(sw), _pack(sm), _pack(sv), "adam_small")]

    def small(k):
        return [sg[k], sd[k], snm[k], snv[k]]

    per_weight = [small(0), o_mod_w, small(1), small(2), small(3), o_a_w_in, small(4), small(5), small(6), small(7),
                  o_a_w_out, o_b_w_in, small(8), small(9), small(10), small(11), o_b_w_out, o_c_w_in, small(12), small(13),
                  o_c_w_out]
    outs = [loss, grad_x]
    for kind in range(4):
        outs += [pw[kind] for pw in per_weight]
    return tuple(outs)
```

```python
import functools
import math

import jax
import jax.numpy as jnp
from jax import lax
from jax.experimental import pallas as pl
from jax.experimental.pallas import tpu as pltpu

F32 = jnp.float32
BF16 = jnp.bfloat16
MXU_DTYPE = jnp.bfloat16

DEPTH = 4
GRID_W = 64
CHUNK = 128
SGU_GROUPS = 16
CONV_W = 31
CONV_HALO = 16
HEAD_DIM = 128
N_KV_HEADS = 4
ROPE_THETA = 10000.0
DEEPNORM_ALPHA = (2 * DEPTH) ** 0.25
LN_EPS = 1e-6
ADAM_LR, ADAM_B1, ADAM_B2, ADAM_EPS, ADAM_WD, ADAM_STEP = 0.001, 0.9, 0.999, 1e-08, 0.01, 10

N_DEV = 8
V7X_VMEM_BYTES = 64 * 1024 * 1024
V7X_VMEM_CLAIM = V7X_VMEM_BYTES * 7 // 8
LANES = 128

NN = ((1,), (0,))
NT = ((1,), (1,))
TN = ((0,), (0,))


def _dot(a, b, dims):
    return lax.dot_general(a.astype(MXU_DTYPE), b.astype(MXU_DTYPE), (dims, ((), ())), preferred_element_type=F32)


def _pick(n, prefs):
    for p in prefs:
        if n % p == 0:
            return p
    raise ValueError(f"no tile for {n} among {prefs}")


def _params(sem, vmem_bytes):
    assert vmem_bytes <= V7X_VMEM_CLAIM, (vmem_bytes, V7X_VMEM_CLAIM)
    return pltpu.CompilerParams(dimension_semantics=sem, vmem_limit_bytes=V7X_VMEM_CLAIM)


def _nbytes(shape, dtype):
    return math.prod(shape) * jnp.dtype(dtype).itemsize


def _sigmoid(x):
    return jax.nn.sigmoid(x)


def _silu(x):
    return x * jax.nn.sigmoid(x)


def _dsilu(x):
    s = jax.nn.sigmoid(x)
    return s * (1.0 + x * (1.0 - s))


def _ln_stats(x):
    mu = jnp.mean(x, axis=-1, keepdims=True)
    xc = x - mu
    var = jnp.mean(xc * xc, axis=-1, keepdims=True)
    return xc, lax.rsqrt(var + LN_EPS)


def _ln(x, g, b):
    xc, rstd = _ln_stats(x)
    return xc * rstd * g + b


def _mesh_pos():
    return lax.axis_index("x"), lax.axis_index("y"), lax.axis_index("c")


def _dev_index(p):
    return 4 * p[0] + 2 * p[1] + p[2]


class _Comm:
    def __init__(self, inputs, out_shapes, sems, start, finish):
        self.inputs, self.out_shapes, self.sems, self.start, self.finish = inputs, out_shapes, sems, start, finish


def _gather_comm(xs):
    n = len(xs)

    def place():
        x, y, c = _mesh_pos()
        return (x, y, c), (x, y, 1 - c), [(1 - x, y), (x, 1 - y), (1 - x, 1 - y)], c

    def copier(x_refs, o_refs, sems):
        send_sems, recv_sems, _ = sems

        def copy(t, k, block, to, from_input=False):
            dst = o_refs[t].at[_dev_index(block)]
            return pltpu.make_async_remote_copy(
                src_ref=x_refs[t] if from_input else dst, dst_ref=dst,
                send_sem=send_sems.at[t, k], recv_sem=recv_sems.at[t, k],
                device_id=to, device_id_type=pl.DeviceIdType.MESH)

        return copy

    def own(x_refs, o_refs, sems, t, me):
        return pltpu.make_async_copy(x_refs[t], o_refs[t].at[_dev_index(me)], sems[2].at[t])

    def first_copies(copy, t, me, sibling, chips, c):
        return [copy(t, 0, me, sibling, True)] + [copy(t, 1 + j, me, (*chip, c), True) for j, chip in enumerate(chips)]

    def start(x_refs, o_refs, sems):
        me, sibling, chips, c = place()
        copy = copier(x_refs, o_refs, sems)
        for t in range(n):
            own(x_refs, o_refs, sems, t, me).start()
            for cp in first_copies(copy, t, me, sibling, chips, c):
                cp.start()

    def finish(x_refs, o_refs, sems):
        me, sibling, chips, c = place()
        copy = copier(x_refs, o_refs, sems)
        passed = []
        for t in range(n):
            for j, chip in enumerate(chips):
                copy(t, 1 + j, (*chip, c), me).wait_recv()
                cp = copy(t, 4 + j, (*chip, c), sibling)
                cp.start()
                passed.append(cp)
        for t in range(n):
            copy(t, 0, sibling, me).wait_recv()
            for j, chip in enumerate(chips):
                copy(t, 4 + j, (*chip, 1 - c), me).wait_recv()
        for t in range(n):
            for cp in first_copies(copy, t, me, sibling, chips, c):
                cp.wait_send()
        for cp in passed:
            cp.wait_send()
        for t in range(n):
            own(x_refs, o_refs, sems, t, me).wait()

    sems = [pltpu.SemaphoreType.DMA((n, 7)), pltpu.SemaphoreType.DMA((n, 7)), pltpu.SemaphoreType.DMA((n,))]
    return _Comm(list(xs), [jax.ShapeDtypeStruct((N_DEV, *a.shape), a.dtype) for a in xs], sems, start, finish)


def _exchange_comm(gs):
    n = len(gs)

    def copies(g_refs, r_refs, sems):
        send_sems, recv_sems, local_sems = sems
        x, y, c = _mesh_pos()
        me = _dev_index((x, y, c))
        out = []
        for t in range(n):
            out.append(pltpu.make_async_copy(g_refs[t].at[me], r_refs[t].at[me], local_sems.at[t]))
            for k in range(1, N_DEV):
                fx, fy, fc = (k >> 2) & 1, (k >> 1) & 1, k & 1
                peer = (1 - x if fx else x, 1 - y if fy else y, 1 - c if fc else c)
                out.append(pltpu.make_async_remote_copy(
                    src_ref=g_refs[t].at[_dev_index(peer)], dst_ref=r_refs[t].at[me],
                    send_sem=send_sems.at[t, k - 1], recv_sem=recv_sems.at[t, k - 1],
                    device_id=peer, device_id_type=pl.DeviceIdType.MESH))
        return out

    def start(g_refs, r_refs, sems):
        for cp in copies(g_refs, r_refs, sems):
            cp.start()

    def finish(g_refs, r_refs, sems):
        for cp in copies(g_refs, r_refs, sems):
            cp.wait()

    sems = [pltpu.SemaphoreType.DMA((n, 7)), pltpu.SemaphoreType.DMA((n, 7)), pltpu.SemaphoreType.DMA((n,))]
    return _Comm(list(gs), [jax.ShapeDtypeStruct(g.shape, g.dtype) for g in gs], sems, start, finish)


N_CHIPS = N_DEV // 2


def _started_and_waited(copies):
    def start(in_refs, out_refs, sems):
        for cp in copies(in_refs, out_refs, sems):
            cp.start()

    def finish(in_refs, out_refs, sems):
        for cp in copies(in_refs, out_refs, sems):
            cp.wait()

    return start, finish


def _pair_exchange_comm(gs):
    n = len(gs)

    def copies(g_refs, r_refs, sems):
        send_sems, recv_sems = sems
        x, y, c = _mesh_pos()
        return [pltpu.make_async_remote_copy(
            src_ref=g_refs[t].at[2 * q + 1 - c], dst_ref=r_refs[t].at[q],
            send_sem=send_sems.at[t, q], recv_sem=recv_sems.at[t, q],
            device_id=(x, y, 1 - c), device_id_type=pl.DeviceIdType.MESH) for t in range(n) for q in range(N_CHIPS)]

    sems = [pltpu.SemaphoreType.DMA((n, N_CHIPS)), pltpu.SemaphoreType.DMA((n, N_CHIPS))]
    return _Comm(list(gs), [jax.ShapeDtypeStruct((N_CHIPS, *g.shape[1:]), g.dtype) for g in gs], sems,
                 *_started_and_waited(copies))


def _chip_exchange_comm(hs):
    n = len(hs)

    def copies(h_refs, r_refs, sems):
        send_sems, recv_sems, local_sems = sems
        x, y, c = _mesh_pos()
        mine = 2 * x + y
        out = []
        for t in range(n):
            out.append(pltpu.make_async_copy(h_refs[t].at[mine], r_refs[t].at[mine], local_sems.at[t]))
            for k in range(1, N_CHIPS):
                px, py = (1 - x if (k >> 1) & 1 else x), (1 - y if k & 1 else y)
                out.append(pltpu.make_async_remote_copy(
                    src_ref=h_refs[t].at[2 * px + py], dst_ref=r_refs[t].at[mine],
                    send_sem=send_sems.at[t, k - 1], recv_sem=recv_sems.at[t, k - 1],
                    device_id=(px, py, c), device_id_type=pl.DeviceIdType.MESH))
        return out

    sems = [pltpu.SemaphoreType.DMA((n, N_CHIPS - 1)), pltpu.SemaphoreType.DMA((n, N_CHIPS - 1)), pltpu.SemaphoreType.DMA((n,))]
    return _Comm(list(hs), [jax.ShapeDtypeStruct(h.shape, h.dtype) for h in hs], sems, *_started_and_waited(copies))


def _pair_add(mine, theirs, name):
    q, rows, cols = mine.shape
    tr = _pick(rows, (512, 256, 128, 64, 32, 16))

    def body(a_ref, b_ref, o_ref):
        o_ref[...] = (a_ref[...].astype(F32) + b_ref[...].astype(F32)).astype(o_ref.dtype)

    blk = pl.BlockSpec((None, tr, cols), lambda s, i: (s, i, 0))
    return pl.pallas_call(
        body, name=name, grid=(q, rows // tr), in_specs=[blk, blk], out_specs=blk,
        out_shape=jax.ShapeDtypeStruct(mine.shape, mine.dtype),
        compiler_params=_params(("parallel", "parallel"), 16 * _nbytes((tr, cols), F32)),
    )(mine, theirs)


def _comm_call(comm, name):
    n_in, n_out = len(comm.inputs), len(comm.out_shapes)

    def body(*refs):
        ins, outs, sems = refs[:n_in], refs[n_in:n_in + n_out], refs[n_in + n_out:]
        comm.start(ins, outs, sems)
        comm.finish(ins, outs, sems)

    hbm = pl.BlockSpec(memory_space=pl.ANY)
    return pl.pallas_call(
        body, name=name, out_shape=comm.out_shapes, in_specs=[hbm] * n_in, out_specs=[hbm] * n_out,
        scratch_shapes=comm.sems)(*comm.inputs)


def _call(body, operands, *, name, grid, in_specs, out_specs, out_shape, scratch_shapes=(), compiler_params, comm=None):
    single = not isinstance(out_shape, (list, tuple))
    out_shape = [out_shape] if single else list(out_shape)
    out_specs = [out_specs] if single else list(out_specs)
    scratch_shapes = list(scratch_shapes)
    if comm is None:
        res = pl.pallas_call(
            body, name=name, grid=grid, in_specs=list(in_specs), out_specs=out_specs, out_shape=out_shape,
            scratch_shapes=scratch_shapes, compiler_params=compiler_params)(*operands)
        return res[0] if single else res
    n_in, n_out, n_scr = len(in_specs), len(out_specs), len(scratch_shapes)
    c_in, c_out = len(comm.inputs), len(comm.out_shapes)

    def with_comm(*refs):
        ins, c_ins = refs[:n_in], refs[n_in:n_in + c_in]
        o0 = n_in + c_in
        outs, c_outs = refs[o0:o0 + n_out], refs[o0 + n_out:o0 + n_out + c_out]
        s0 = o0 + n_out + c_out
        scr, sems = refs[s0:s0 + n_scr], refs[s0 + n_scr:]
        ids = [pl.program_id(a) for a in range(len(grid))]
        first = functools.reduce(jnp.logical_and, [i == 0 for i in ids])
        last = functools.reduce(jnp.logical_and, [i == g - 1 for i, g in zip(ids, grid)])

        @pl.when(first)
        def _():
            comm.start(c_ins, c_outs, sems)

        body(*ins, *outs, *scr)

        @pl.when(last)
        def _():
            comm.finish(c_ins, c_outs, sems)

    hbm = pl.BlockSpec(memory_space=pl.ANY)
    params = pltpu.CompilerParams(dimension_semantics=("arbitrary",) * len(grid),
                                  vmem_limit_bytes=compiler_params.vmem_limit_bytes)
    res = pl.pallas_call(
        with_comm, name=name, grid=grid, in_specs=list(in_specs) + [hbm] * c_in, out_specs=out_specs + [hbm] * c_out,
        out_shape=out_shape + list(comm.out_shapes), scratch_shapes=scratch_shapes + list(comm.sems),
        compiler_params=params)(*operands, *comm.inputs)
    return (res[0] if single else res[:n_out]), res[n_out:]


def _all_gather(xs, name):
    return _comm_call(_gather_comm(xs), name)


ROW_TILES = (1088, 1024, 768, 544, 512, 384, 272, 256, 128)
TOKEN_K_TILES = (2176, 2048, 1088, 1024, 768, 512, 384, 256, 128)
COL_TILES = (1024, 768, 640, 512, 384, 256, 128)
DEEP_K = 2048


def _mm_nn(a, w, *, planes=1, name, comm=None):
    m, k = a.shape
    if w.ndim == 3:
        nd_w = w.shape[2]
        n = w.shape[0] * nd_w
    else:
        nd_w = n = w.shape[1]
    npl = n // planes
    tm = _pick(m, ROW_TILES)
    tn = _pick(math.gcd(nd_w, npl), COL_TILES if k <= DEEP_K else COL_TILES[3:])
    r, rp = nd_w // tn, npl // tn
    if w.ndim == 3:
        w_spec = pl.BlockSpec((None, k, tn), lambda i, j: (j // r, 0, j % r))
    else:
        w_spec = pl.BlockSpec((k, tn), lambda i, j: (0, j))
    if planes > 1:
        o_spec = pl.BlockSpec((None, tm, tn), lambda i, j: (j // rp, i, j % rp))
        out_shape = jax.ShapeDtypeStruct((planes, m, npl), F32)
    else:
        o_spec = pl.BlockSpec((tm, tn), lambda i, j: (i, j))
        out_shape = jax.ShapeDtypeStruct((m, n), F32)

    def body(a_ref, w_ref, o_ref):
        o_ref[...] = _dot(a_ref[...], w_ref[...], NN)

    vmem = 2 * (_nbytes((tm, k), a.dtype) + _nbytes((k, tn), w.dtype) + _nbytes((tm, tn), F32)) + _nbytes((tm, tn), F32)
    return _call(
        body, (a, w), name=name, grid=(m // tm, n // tn),
        in_specs=[pl.BlockSpec((tm, k), lambda i, j: (i, 0)), w_spec], out_specs=o_spec, out_shape=out_shape,
        compiler_params=_params(("parallel", "arbitrary"), vmem), comm=comm)


def _mm_nt(a, w, *, name):
    m, k = a.shape
    n = w.shape[0]
    tm = _pick(m, ROW_TILES)
    tn = _pick(n, COL_TILES)

    def body(a_ref, w_ref, o_ref):
        o_ref[...] = _dot(a_ref[...], w_ref[...], NT)

    vmem = 2 * (_nbytes((tm, k), a.dtype) + _nbytes((tn, k), w.dtype) + _nbytes((tm, tn), F32)) + _nbytes((tm, tn), F32)
    return pl.pallas_call(
        body, name=name, grid=(m // tm, n // tn),
        in_specs=[pl.BlockSpec((tm, k), lambda i, j: (i, 0)), pl.BlockSpec((tn, k), lambda i, j: (j, 0))],
        out_specs=pl.BlockSpec((tm, tn), lambda i, j: (i, j)), out_shape=jax.ShapeDtypeStruct((m, n), F32),
        compiler_params=_params(("parallel", "arbitrary"), vmem),
    )(a, w)


def _mm_nt_blocked(a, w, *, name, comm=None):
    nd, n, kd = w.shape
    if a.ndim == 3:
        p, m, kp = a.shape
    else:
        (m, kp), p = a.shape, 1
    tk = _pick(math.gcd(kd, kp), COL_TILES)
    ra, rw = kp // tk, kd // tk
    nk = nd * rw
    tm = _pick(m, ROW_TILES)
    if a.ndim == 3:
        a_spec = pl.BlockSpec((None, tm, tk), lambda i, kk: (kk // ra, i, kk % ra))
    else:
        a_spec = pl.BlockSpec((tm, tk), lambda i, kk: (i, kk))

    def body(a_ref, w_ref, o_ref, acc_ref):
        kk = pl.program_id(1)

        @pl.when(kk == 0)
        def _():
            acc_ref[...] = jnp.zeros_like(acc_ref)

        acc_ref[...] += _dot(a_ref[...], w_ref[...], NT)

        @pl.when(kk == nk - 1)
        def _():
            o_ref[...] = acc_ref[...]

    vmem = 2 * (_nbytes((tm, tk), a.dtype) + _nbytes((n, tk), w.dtype) + _nbytes((tm, n), F32)) + 2 * _nbytes((tm, n), F32)
    return _call(
        body, (a, w), name=name, grid=(m // tm, nk),
        in_specs=[a_spec, pl.BlockSpec((None, n, tk), lambda i, kk: (kk // rw, 0, kk % rw))],
        out_specs=pl.BlockSpec((tm, n), lambda i, kk: (i, 0)), out_shape=jax.ShapeDtypeStruct((m, n), F32),
        scratch_shapes=[pltpu.VMEM((tm, n), F32)],
        compiler_params=_params(("parallel", "arbitrary"), vmem), comm=comm)


def _mm_tn(a, b, *, blocked, out_dtype, name, comm=None):
    rows, da = a.shape
    if b.ndim == 3:
        p, _, npl = b.shape
    else:
        p, npl = 1, b.shape[1]
    n = p * npl
    nd_w = n // N_DEV if blocked else n
    tk = _pick(rows, TOKEN_K_TILES)
    tm = _pick(da, COL_TILES)
    tn = _pick(math.gcd(nd_w, npl), COL_TILES)
    rb, ro = npl // tn, nd_w // tn
    nk = rows // tk
    if b.ndim == 3:
        b_spec = pl.BlockSpec((None, tk, tn), lambda i, j, kk: (j // rb, kk, j % rb))
    else:
        b_spec = pl.BlockSpec((tk, tn), lambda i, j, kk: (kk, j))
    if blocked:
        o_spec = pl.BlockSpec((None, tm, tn), lambda i, j, kk: (j // ro, i, j % ro))
        out_shape = jax.ShapeDtypeStruct((N_DEV, da, nd_w), out_dtype)
    else:
        o_spec = pl.BlockSpec((tm, tn), lambda i, j, kk: (i, j))
        out_shape = jax.ShapeDtypeStruct((da, n), out_dtype)

    def body(a_ref, b_ref, o_ref, acc_ref):
        kk = pl.program_id(2)

        @pl.when(kk == 0)
        def _():
            acc_ref[...] = jnp.zeros_like(acc_ref)

        acc_ref[...] += _dot(a_ref[...], b_ref[...], TN)

        @pl.when(kk == nk - 1)
        def _():
            o_ref[...] = acc_ref[...].astype(o_ref.dtype)

    vmem = (2 * (_nbytes((tk, tm), a.dtype) + _nbytes((tk, tn), b.dtype) + _nbytes((tm, tn), out_dtype))
            + 3 * _nbytes((tm, tn), F32) + _nbytes((tk, tm), F32))
    return _call(
        body, (a, b), name=name, grid=(da // tm, n // tn, nk),
        in_specs=[pl.BlockSpec((tk, tm), lambda i, j, kk: (kk, i)), b_spec], out_specs=o_spec, out_shape=out_shape,
        scratch_shapes=[pltpu.VMEM((tm, tn), F32)],
        compiler_params=_params(("parallel", "parallel", "arbitrary"), vmem), comm=comm)


ROW_TILE = 256


class _Rows:
    def __init__(self, n_x, n_ctx, tile=ROW_TILE):
        assert n_x % tile == 0 and n_ctx % tile == 0
        self.n_x, self.n_ctx, self.tile = n_x, n_ctx, tile
        self.rows = n_x + n_ctx
        self.nt_x = n_x // tile
        self.nt = self.rows // tile
        self.n_seg = 2 if n_ctx else 1

    def seg(self, i):
        return jnp.where(i >= self.nt_x, 1, 0) if self.n_ctx else 0

    def first_of_seg(self, i):
        return (i == 0) | (i == self.nt_x) if self.n_ctx else i == 0

    def full(self, width):
        return pl.BlockSpec((self.tile, width), lambda i: (i, 0))

    def plane(self, p, width):
        return pl.BlockSpec((None, self.tile, width), lambda i: (p, i, 0))

    def modvec(self, layer, which, width):
        return pl.BlockSpec((None, 1, width), lambda i: ((layer * 2 + self.seg(i)) * 3 + which, 0, 0))

    def seg_acc(self, width):
        return pl.BlockSpec((None, 1, width), lambda i: (self.seg(i), 0, 0))


def _vec(width):
    return pl.BlockSpec((1, width), lambda i: (0, 0))


def _acc(ref, first, val):
    @pl.when(first)
    def _():
        ref[...] = jnp.zeros_like(ref)

    ref[...] += val


def _modulate(xs, modv, layer, rt, name):
    d = xs.shape[1]

    def body(x_ref, sh_ref, sc_ref, o_ref):
        o_ref[...] = (x_ref[...] * (1.0 + sc_ref[...]) + sh_ref[...]).astype(o_ref.dtype)

    return pl.pallas_call(
        body, name=name, grid=(rt.nt,),
        in_specs=[rt.full(d), rt.modvec(layer, 0, d), rt.modvec(layer, 1, d)],
        out_specs=rt.full(d), out_shape=jax.ShapeDtypeStruct(xs.shape, MXU_DTYPE),
        compiler_params=_params(("parallel",), 6 * _nbytes((rt.tile, d), F32)),
    )(xs, modv, modv)


def _post(x, y, gate, pg, pb):
    return _ln(DEEPNORM_ALPHA * x + gate * y, pg, pb)


def _post_fwd(xs, y, modv, layer, pg, pb, rt, name, modulate_next=True):
    d = xs.shape[1]

    def body(x_ref, y_ref, gate_ref, pg_ref, pb_ref, *rest):
        out = _post(x_ref[...], y_ref[...], gate_ref[...], pg_ref[...], pb_ref[...])
        if modulate_next:
            sh_ref, sc_ref, o_ref, h_ref = rest
            h_ref[...] = (out * (1.0 + sc_ref[...]) + sh_ref[...]).astype(h_ref.dtype)
        else:
            (o_ref,) = rest
        o_ref[...] = out

    nxt = [rt.modvec(layer + 1, 0, d), rt.modvec(layer + 1, 1, d)] if modulate_next else []
    res = pl.pallas_call(
        body, name=name, grid=(rt.nt,),
        in_specs=[rt.full(d), rt.full(d), rt.modvec(layer, 2, d), _vec(d), _vec(d)] + nxt,
        out_specs=[rt.full(d)] * (2 if modulate_next else 1),
        out_shape=[jax.ShapeDtypeStruct((rt.rows, d), F32)] + ([jax.ShapeDtypeStruct((rt.rows, d), MXU_DTYPE)] if modulate_next else []),
        compiler_params=_params(("parallel",), 12 * _nbytes((rt.tile, d), F32)),
    )(xs, y, modv, pg, pb, *([modv, modv] if modulate_next else []))
    return res if modulate_next else res[0]


def _post_bwd(xs, y, dout, modv, layer, pg, pb, rt, name):
    d = xs.shape[1]

    def body(x_ref, y_ref, do_ref, gate_ref, pg_ref, pb_ref, dres_ref, dy_ref, dpg_ref, dpb_ref, dgate_ref):
        i = pl.program_id(0)
        _, vjp = jax.vjp(_post, x_ref[...], y_ref[...], gate_ref[...], pg_ref[...], pb_ref[...])
        dx, dy, dgate, dpg, dpb = vjp(do_ref[...])
        dres_ref[...] = dx
        dy_ref[...] = dy.astype(dy_ref.dtype)
        _acc(dpg_ref, i == 0, dpg)
        _acc(dpb_ref, i == 0, dpb)
        _acc(dgate_ref, rt.first_of_seg(i), dgate)

    return pl.pallas_call(
        body, name=name, grid=(rt.nt,),
        in_specs=[rt.full(d), rt.full(d), rt.full(d), rt.modvec(layer, 2, d), _vec(d), _vec(d)],
        out_specs=[rt.full(d), rt.full(d), _vec(d), _vec(d), rt.seg_acc(d)],
        out_shape=[jax.ShapeDtypeStruct((rt.rows, d), F32), jax.ShapeDtypeStruct((rt.rows, d), MXU_DTYPE),
                   jax.ShapeDtypeStruct((1, d), F32), jax.ShapeDtypeStruct((1, d), F32),
                   jax.ShapeDtypeStruct((rt.n_seg, 1, d), F32)],
        compiler_params=_params(("arbitrary",), 16 * _nbytes((rt.tile, d), F32)),
    )(xs, y, dout, modv, pg, pb)


def _mod_bwd(dres, dh, xs, modv, layer, rt, name, dx_rows=None):
    d = xs.shape[1]
    nt_res = dres.shape[0] // rt.tile
    nt_dx = rt.nt if dx_rows is None else dx_rows // rt.tile

    def body(dres_ref, dh_ref, x_ref, sc_ref, dx_ref, dshift_ref, dscale_ref):
        i = pl.program_id(0)
        dh = dh_ref[...]

        @pl.when(i < nt_dx)
        def _():
            dx_ref[...] = jnp.where(i < nt_res, dres_ref[...], 0.0) + dh * (1.0 + sc_ref[...])

        first = rt.first_of_seg(i)
        _acc(dshift_ref, first, jnp.sum(dh, axis=0, keepdims=True))
        _acc(dscale_ref, first, jnp.sum(dh * x_ref[...], axis=0, keepdims=True))

    def clamped(nt):
        return pl.BlockSpec((rt.tile, d), lambda i: (jnp.minimum(i, nt - 1), 0))

    return pl.pallas_call(
        body, name=name, grid=(rt.nt,),
        in_specs=[clamped(nt_res), rt.full(d), rt.full(d), rt.modvec(layer, 1, d)],
        out_specs=[clamped(nt_dx), rt.seg_acc(d), rt.seg_acc(d)],
        out_shape=[jax.ShapeDtypeStruct((nt_dx * rt.tile, d), F32), jax.ShapeDtypeStruct((rt.n_seg, 1, d), F32),
                   jax.ShapeDtypeStruct((rt.n_seg, 1, d), F32)],
        compiler_params=_params(("arbitrary",), 10 * _nbytes((rt.tile, d), F32)),
    )(dres, dh, xs, modv)


def _cm_mid_fwd(z3, ln_g, ln_b, w_s, b_s_t, name, comm=None):
    _, rows, e = z3.shape
    groups = w_s.shape[0]
    gw = e // groups

    def body(z_ref, lg_ref, lb_ref, ws_ref, bs_ref, t_ref):
        vn = _ln(z_ref[1], lg_ref[...], lb_ref[...])
        for h in range(groups):
            cols = slice(h * gw, (h + 1) * gw)
            s = _dot(ws_ref[h], vn[:, cols], NN) + bs_ref[:, h:h + 1]
            t_ref[:, cols] = (z_ref[0, :, cols] * s * _silu(z_ref[2, :, cols])).astype(t_ref.dtype)

    return _call(
        body, (z3, ln_g, ln_b, w_s, b_s_t), name=name, grid=(rows // CHUNK,),
        in_specs=[pl.BlockSpec((3, CHUNK, e), lambda i: (0, i, 0)), _vec(e), _vec(e),
                  pl.BlockSpec(w_s.shape, lambda i: (0, 0, 0)), pl.BlockSpec(b_s_t.shape, lambda i: (0, 0))],
        out_specs=pl.BlockSpec((CHUNK, e), lambda i: (i, 0)), out_shape=jax.ShapeDtypeStruct((rows, e), MXU_DTYPE),
        compiler_params=_params(("parallel",), 12 * _nbytes((CHUNK, e), F32)), comm=comm)


def _cm_mid_bwd(z3, dt, ln_g, ln_b, w_s, b_s_t, name, comm=None):
    _, rows, e = z3.shape
    groups = w_s.shape[0]
    gw = e // groups

    def body(z_ref, dt_ref, lg_ref, lb_ref, ws_ref, bs_ref, dz_ref, dlg_ref, dlb_ref, dws_ref, dbs_ref, dvn_ref):
        i = pl.program_id(0)
        first = i == 0
        v = z_ref[1]
        vn, ln_vjp = jax.vjp(_ln, v, lg_ref[...], lb_ref[...])

        @pl.when(first)
        def _():
            dws_ref[...] = jnp.zeros_like(dws_ref)
            dbs_ref[...] = jnp.zeros_like(dbs_ref)

        for h in range(groups):
            cols = slice(h * gw, (h + 1) * gw)
            vn_h = vn[:, cols]
            s = _dot(ws_ref[h], vn_h, NN) + bs_ref[:, h:h + 1]
            u, g, dth = z_ref[0, :, cols], z_ref[2, :, cols], dt_ref[:, cols]
            sg = _silu(g)
            dz_ref[0, :, cols] = (dth * s * sg).astype(dz_ref.dtype)
            dz_ref[2, :, cols] = (dth * u * s * _dsilu(g)).astype(dz_ref.dtype)
            ds = dth * u * sg
            dvn_ref[:, cols] = _dot(ws_ref[h], ds, TN)
            dws_ref[h] += _dot(ds, vn_h, NT)
            dbs_ref[:, h:h + 1] += jnp.sum(ds, axis=1, keepdims=True)
        dv, dlg, dlb = ln_vjp(dvn_ref[...])
        dz_ref[1] = dv.astype(dz_ref.dtype)
        _acc(dlg_ref, first, dlg)
        _acc(dlb_ref, first, dlb)

    return _call(
        body, (z3, dt, ln_g, ln_b, w_s, b_s_t), name=name, grid=(rows // CHUNK,),
        in_specs=[pl.BlockSpec((3, CHUNK, e), lambda i: (0, i, 0)), pl.BlockSpec((CHUNK, e), lambda i: (i, 0)), _vec(e), _vec(e),
                  pl.BlockSpec(w_s.shape, lambda i: (0, 0, 0)), pl.BlockSpec(b_s_t.shape, lambda i: (0, 0))],
        out_specs=[pl.BlockSpec((3, CHUNK, e), lambda i: (0, i, 0)), _vec(e), _vec(e),
                   pl.BlockSpec(w_s.shape, lambda i: (0, 0, 0)), pl.BlockSpec(b_s_t.shape, lambda i: (0, 0))],
        out_shape=[jax.ShapeDtypeStruct((3, rows, e), MXU_DTYPE), jax.ShapeDtypeStruct((1, e), F32),
                   jax.ShapeDtypeStruct((1, e), F32), jax.ShapeDtypeStruct(w_s.shape, F32),
                   jax.ShapeDtypeStruct(b_s_t.shape, F32)],
        scratch_shapes=[pltpu.VMEM((CHUNK, e), F32)],
        compiler_params=_params(("arbitrary",), 20 * _nbytes((CHUNK, e), F32)), comm=comm)


CONV_COL_TILE = 512


def _conv_specs(rt, tc, planes):
    per = rt.tile // CONV_HALO
    last = rt.rows // CONV_HALO - 1
    if planes:
        cur = pl.BlockSpec((planes, rt.tile, tc), lambda j, i: (0, i, j))
        prev = pl.BlockSpec((planes, CONV_HALO, tc), lambda j, i: (0, jnp.maximum(i * per - 1, 0), j))
        nxt = pl.BlockSpec((planes, CONV_HALO, tc), lambda j, i: (0, jnp.minimum((i + 1) * per, last), j))
    else:
        cur = pl.BlockSpec((rt.tile, tc), lambda j, i: (i, j))
        prev = pl.BlockSpec((CONV_HALO, tc), lambda j, i: (jnp.maximum(i * per - 1, 0), j))
        nxt = pl.BlockSpec((CONV_HALO, tc), lambda j, i: (jnp.minimum((i + 1) * per, last), j))
    return cur, prev, nxt


def _halo_ok(rt, i):
    prev_ok = (i != 0) & (i != rt.nt_x)
    next_ok = (i != rt.nt_x - 1) & (i != rt.nt - 1)
    return prev_ok, next_ok


def _glu(ref):
    return ref[0] * _sigmoid(ref[1])


def _padded(cur, prev, nxt, prev_ok, next_ok):
    return jnp.concatenate([jnp.where(prev_ok, prev, 0.0), cur, jnp.where(next_ok, nxt, 0.0)], axis=0)


SUBLANES = 8
CONV_ROW_BLOCK = 16
CONV_DW_ROWS, CONV_DW_TAPS = 16, 4


def _phase_scratch(tr, tc):
    return pltpu.VMEM((SUBLANES, tr + 2 * CONV_HALO - SUBLANES, tc), F32)


def _store_phases(rot_ref, pad):
    rows = rot_ref.shape[1]
    for b in range(SUBLANES):
        rot_ref[b] = pad[b:b + rows, :]


def _tap_rows(rot_ref, r0, off, g):
    a, b = divmod(off, SUBLANES)
    return rot_ref[b, pl.ds(r0 + SUBLANES * (a + g), SUBLANES), :]


def _conv_rows(rot_ref, w_ref, r0, offs, init):
    tc = rot_ref.shape[2]
    accs = [init] * (CONV_ROW_BLOCK // SUBLANES)
    for k, off in enumerate(offs):
        wk = jnp.broadcast_to(w_ref[k:k + 1, :], (SUBLANES, tc))
        accs = [acc + wk * _tap_rows(rot_ref, r0, off, g) for g, acc in enumerate(accs)]
    return jnp.concatenate(accs, axis=0)


def _conv_fwd(z3, conv_w, conv_b, rt, name, comm=None):
    _, rows, e = z3.shape
    tc = _pick(e, (CONV_COL_TILE, 256, 128))
    tr = rt.tile

    def body(cur_ref, prev_ref, next_ref, w_ref, b_ref, o_ref, rot_ref):
        prev_ok, next_ok = _halo_ok(rt, pl.program_id(1))
        _store_phases(rot_ref, _padded(_glu(cur_ref), _glu(prev_ref), _glu(next_ref), prev_ok, next_ok))
        bias = jnp.broadcast_to(b_ref[...], (SUBLANES, tc))
        offs = [CONV_HALO - CONV_W // 2 + k for k in range(CONV_W)]

        def rows_block(rb, carry):
            r0 = pl.multiple_of(rb * CONV_ROW_BLOCK, CONV_ROW_BLOCK)
            o_ref[pl.ds(r0, CONV_ROW_BLOCK), :] = _conv_rows(rot_ref, w_ref, r0, offs, bias)
            return carry

        lax.fori_loop(0, tr // CONV_ROW_BLOCK, rows_block, 0)

    cur, prev, nxt = _conv_specs(rt, tc, 2)
    return _call(
        body, (z3, z3, z3, conv_w, conv_b), name=name, grid=(e // tc, rt.nt),
        in_specs=[cur, prev, nxt, pl.BlockSpec((CONV_W, tc), lambda j, i: (0, j)), pl.BlockSpec((1, tc), lambda j, i: (0, j))],
        out_specs=pl.BlockSpec((tr, tc), lambda j, i: (i, j)), out_shape=jax.ShapeDtypeStruct((rows, e), F32),
        scratch_shapes=[_phase_scratch(tr, tc)],
        compiler_params=_params(("parallel", "arbitrary"), 32 * _nbytes((tr, tc), F32)), comm=comm)


def _conv_bwd(z3, dy1, dg, conv_w, rt, name, comm=None):
    _, rows, e = z3.shape
    tc = _pick(e, (CONV_COL_TILE, 256, 128))
    tr = rt.tile

    def body(cur_ref, prev_ref, next_ref, dcur_ref, dprev_ref, dnext_ref, dg_ref, w_ref, dz_ref, dw_ref, db_ref, rot_ref, drot_ref):
        i = pl.program_id(1)
        prev_ok, next_ok = _halo_ok(rt, i)
        _store_phases(rot_ref, _padded(_glu(cur_ref), _glu(prev_ref), _glu(next_ref), prev_ok, next_ok))
        _store_phases(drot_ref, _padded(dcur_ref[...], dprev_ref[...], dnext_ref[...], prev_ok, next_ok))
        n_blocks = tr // CONV_ROW_BLOCK

        @pl.when(i == 0)
        def _():
            dw_ref[...] = jnp.zeros_like(dw_ref)
            db_ref[...] = jnp.zeros_like(db_ref)

        roffs = [CONV_HALO + CONV_W // 2 - k for k in range(CONV_W)]
        zero = jnp.zeros((SUBLANES, tc), F32)

        def dgate_block(rb, carry):
            r0 = pl.multiple_of(rb * CONV_ROW_BLOCK, CONV_ROW_BLOCK)
            dy0 = _conv_rows(drot_ref, w_ref, r0, roffs, zero)
            rws = pl.ds(r0, CONV_ROW_BLOCK)
            a, sb = cur_ref[0, rws, :], _sigmoid(cur_ref[1, rws, :])
            dz_ref[0, rws, :] = (dy0 * sb).astype(dz_ref.dtype)
            dz_ref[1, rws, :] = (dy0 * a * sb * (1.0 - sb)).astype(dz_ref.dtype)
            return carry

        lax.fori_loop(0, n_blocks, dgate_block, 0)
        dz_ref[2] = dg_ref[...]

        groups = CONV_DW_ROWS // SUBLANES
        for k0 in range(0, CONV_W, CONV_DW_TAPS):
            taps = list(range(k0, min(k0 + CONV_DW_TAPS, CONV_W)))

            accs = [zero] * (len(taps) * groups)
            for r0 in range(0, tr, CONV_DW_ROWS):
                dy = [dcur_ref[r0 + SUBLANES * g:r0 + SUBLANES * (g + 1), :] for g in range(groups)]
                accs = [accs[t * groups + g] + dy[g] * _tap_rows(rot_ref, r0, CONV_HALO - CONV_W // 2 + k, g)
                        for t, k in enumerate(taps) for g in range(groups)]
            for t, k in enumerate(taps):
                tot = functools.reduce(jnp.add, accs[t * groups:(t + 1) * groups])
                dw_ref[k:k + 1, :] += jnp.sum(tot, axis=0, keepdims=True)
        db_ref[...] += jnp.sum(dcur_ref[...], axis=0, keepdims=True)

    cur, prev, nxt = _conv_specs(rt, tc, 2)
    dcur, dprev, dnxt = _conv_specs(rt, tc, 0)
    return _call(
        body, (z3, z3, z3, dy1, dy1, dy1, dg, conv_w), name=name, grid=(e // tc, rt.nt),
        in_specs=[cur, prev, nxt, dcur, dprev, dnxt, pl.BlockSpec((tr, tc), lambda j, i: (i, j)),
                  pl.BlockSpec((CONV_W, tc), lambda j, i: (0, j))],
        out_specs=[pl.BlockSpec((3, tr, tc), lambda j, i: (0, i, j)), pl.BlockSpec((CONV_W, tc), lambda j, i: (0, j)),
                   pl.BlockSpec((1, tc), lambda j, i: (0, j))],
        out_shape=[jax.ShapeDtypeStruct((3, rows, e), MXU_DTYPE), jax.ShapeDtypeStruct((CONV_W, e), F32),
                   jax.ShapeDtypeStruct((1, e), F32)],
        scratch_shapes=[_phase_scratch(tr, tc), _phase_scratch(tr, tc)],
        compiler_params=_params(("parallel", "arbitrary"), 48 * _nbytes((tr, tc), F32)), comm=comm)


def _conv_mid(y1, g, ln_g, ln_b):
    return _silu(_ln(y1, ln_g, ln_b)) * _silu(g)


def _conv_mid_fwd(y1, z3, ln_g, ln_b, rt, name):
    e = y1.shape[1]
    tr = CHUNK

    def body(y_ref, g_ref, lg_ref, lb_ref, t_ref):
        t_ref[...] = _conv_mid(y_ref[...], g_ref[...], lg_ref[...], lb_ref[...]).astype(t_ref.dtype)

    return pl.pallas_call(
        body, name=name, grid=(rt.rows // tr,),
        in_specs=[pl.BlockSpec((tr, e), lambda i: (i, 0)), pl.BlockSpec((None, tr, e), lambda i: (2, i, 0)), _vec(e), _vec(e)],
        out_specs=pl.BlockSpec((tr, e), lambda i: (i, 0)), out_shape=jax.ShapeDtypeStruct((rt.rows, e), MXU_DTYPE),
        compiler_params=_params(("parallel",), 12 * _nbytes((tr, e), F32)),
    )(y1, z3, ln_g, ln_b)


def _conv_mid_bwd(y1, z3, dt, ln_g, ln_b, rt, name):
    e = y1.shape[1]
    tr = CHUNK

    def body(y_ref, g_ref, dt_ref, lg_ref, lb_ref, dy_ref, dg_ref, dlg_ref, dlb_ref):
        first = pl.program_id(0) == 0
        _, vjp = jax.vjp(_conv_mid, y_ref[...], g_ref[...], lg_ref[...], lb_ref[...])
        dy, dg, dlg, dlb = vjp(dt_ref[...])
        dy_ref[...] = dy
        dg_ref[...] = dg.astype(dg_ref.dtype)
        _acc(dlg_ref, first, dlg)
        _acc(dlb_ref, first, dlb)

    row = pl.BlockSpec((tr, e), lambda i: (i, 0))
    return pl.pallas_call(
        body, name=name, grid=(rt.rows // tr,),
        in_specs=[row, pl.BlockSpec((None, tr, e), lambda i: (2, i, 0)), row, _vec(e), _vec(e)],
        out_specs=[row, row, _vec(e), _vec(e)],
        out_shape=[jax.ShapeDtypeStruct((rt.rows, e), F32), jax.ShapeDtypeStruct((rt.rows, e), MXU_DTYPE),
                   jax.ShapeDtypeStruct((1, e), F32), jax.ShapeDtypeStruct((1, e), F32)],
        compiler_params=_params(("arbitrary",), 20 * _nbytes((tr, e), F32)),
    )(y1, z3, dt, ln_g, ln_b)


def _rms(x, g):
    return x * lax.rsqrt(jnp.mean(x * x, axis=-1, keepdims=True) + LN_EPS) * g


def _pair_swap(x):
    lane = lax.broadcasted_iota(jnp.int32, x.shape, x.ndim - 1)
    return jnp.where(lane % 2 == 0, pltpu.roll(x, x.shape[-1] - 1, x.ndim - 1), pltpu.roll(x, 1, x.ndim - 1))


def _rope(x, cos, sin):
    return x * cos + _pair_swap(x) * sin


def _rope_t(dy, cos, sin):
    return dy * cos + _pair_swap(dy * sin)


def _rope_tables(n_x, n_ctx):
    t = jnp.arange(n_x)
    row = (t // GRID_W).astype(F32)
    col = (t % GRID_W).astype(F32)
    axis_dim = HEAD_DIM // 2
    inv = 1.0 / (ROPE_THETA ** (jnp.arange(0, axis_dim, 2, dtype=F32) / axis_dim))
    ang = jnp.concatenate([row[:, None] * inv, col[:, None] * inv], axis=-1)
    cos, sin = jnp.cos(ang), jnp.sin(ang)
    cos2 = jnp.repeat(cos, 2, axis=-1)
    sin2 = jnp.stack([-sin, sin], axis=-1).reshape(n_x, HEAD_DIM)
    cos2 = jnp.concatenate([cos2, jnp.ones((n_ctx, HEAD_DIM), F32)], axis=0)
    sin2 = jnp.concatenate([sin2, jnp.zeros((n_ctx, HEAD_DIM), F32)], axis=0)
    return cos2, sin2


def _qkv_prep(z4, q_g, k_g, cos, sin, d, kvw, rt, name):
    hd = HEAD_DIM
    kb = d // kvw

    def body(q_ref, k_ref, v_ref, qg_ref, kg_ref, cos_ref, sin_ref, qo_ref, ko_ref, vo_ref):
        cos, sin = cos_ref[...], sin_ref[...]
        for h in range(d // hd):
            cols = slice(h * hd, (h + 1) * hd)
            qo_ref[:, cols] = _rope(_rms(q_ref[:, cols], qg_ref[...]), cos, sin).astype(qo_ref.dtype)
        for h in range(kvw // hd):
            cols = slice(h * hd, (h + 1) * hd)
            ko_ref[:, cols] = _rope(_rms(k_ref[:, cols], kg_ref[...]), cos, sin).astype(ko_ref.dtype)
        vo_ref[...] = v_ref[...].astype(vo_ref.dtype)

    tr = rt.tile
    return pl.pallas_call(
        body, name=name, grid=(rt.nt,),
        in_specs=[pl.BlockSpec((tr, d), lambda i: (i, 0)), pl.BlockSpec((tr, kvw), lambda i: (i, kb)),
                  pl.BlockSpec((tr, kvw), lambda i: (i, kb + 1)), _vec(hd), _vec(hd), rt.full(hd), rt.full(hd)],
        out_specs=[rt.full(d), rt.full(kvw), rt.full(kvw)],
        out_shape=[jax.ShapeDtypeStruct((rt.rows, d), MXU_DTYPE), jax.ShapeDtypeStruct((rt.rows, kvw), MXU_DTYPE),
                   jax.ShapeDtypeStruct((rt.rows, kvw), MXU_DTYPE)],
        compiler_params=_params(("parallel",), 8 * _nbytes((tr, d), F32)),
    )(z4, z4, z4, q_g, k_g, cos, sin)


ATTN_Q_TILE = 256
ATTN_HEADS_PER_PASS = 2
LOG2_E = math.log2(math.e)


def _attn_fwd(qh, kh, vh, n_x, name, comm=None):
    rows, d = qh.shape
    kvw = kh.shape[1]
    hd = HEAD_DIM
    n_kv = kvw // hd
    gqw = d // n_kv
    grp = gqw // hd
    tq = _pick(n_x, (ATTN_Q_TILE, 128))
    scale = hd ** -0.5

    def body(q_ref, k_ref, v_ref, o_ref, lse_ref):
        k, v = k_ref[...], v_ref[...]
        for g0 in range(0, grp, ATTN_HEADS_PER_PASS):
            heads = range(g0, min(g0 + ATTN_HEADS_PER_PASS, grp))
            q = jnp.concatenate([q_ref[:, g * hd:(g + 1) * hd] for g in heads], axis=0)
            s = _dot(q, k, NT)
            m = jnp.max(s, axis=-1, keepdims=True)
            p = jnp.exp2((s - m) * (scale * LOG2_E))
            l = jnp.sum(p, axis=-1, keepdims=True)
            o = _dot(p / l, v, NN)
            lse = m * scale + jnp.log(l)
            for n, g in enumerate(heads):
                o_ref[:, g * hd:(g + 1) * hd] = o[n * tq:(n + 1) * tq]
                lse_ref[:, g:g + 1] = lse[n * tq:(n + 1) * tq]

    vmem = 4 * _nbytes((rows, hd), MXU_DTYPE) + 4 * _nbytes((tq, rows), F32) + 6 * _nbytes((tq, gqw), F32)
    return _call(
        body, (qh, kh, vh), name=name, grid=(n_kv, n_x // tq),
        in_specs=[pl.BlockSpec((tq, gqw), lambda h, i: (i, h)), pl.BlockSpec((rows, hd), lambda h, i: (0, h)),
                  pl.BlockSpec((rows, hd), lambda h, i: (0, h))],
        out_specs=[pl.BlockSpec((tq, gqw), lambda h, i: (i, h)), pl.BlockSpec((None, tq, grp), lambda h, i: (h, i, 0))],
        out_shape=[jax.ShapeDtypeStruct((n_x, d), F32), jax.ShapeDtypeStruct((n_kv, n_x, grp), F32)],
        compiler_params=_params(("parallel", "arbitrary"), vmem), comm=comm)


def _attn_bwd(qh, kh, vh, do, lse, n_x, name, comm=None):
    rows, d = qh.shape
    kvw = kh.shape[1]
    hd = HEAD_DIM
    n_kv = kvw // hd
    gqw = d // n_kv
    grp = gqw // hd
    tq = _pick(n_x, (ATTN_Q_TILE, 128))
    scale = hd ** -0.5

    def body(q_ref, k_ref, v_ref, do_ref, lse_ref, dq_ref, dk_ref, dv_ref):
        @pl.when(pl.program_id(1) == 0)
        def _():
            dk_ref[...] = jnp.zeros_like(dk_ref)
            dv_ref[...] = jnp.zeros_like(dv_ref)

        k, v = k_ref[...], v_ref[...]
        for g0 in range(0, grp, ATTN_HEADS_PER_PASS):
            heads = range(g0, min(g0 + ATTN_HEADS_PER_PASS, grp))
            q = jnp.concatenate([q_ref[:, g * hd:(g + 1) * hd] for g in heads], axis=0)
            dog = jnp.concatenate([do_ref[:, g * hd:(g + 1) * hd] for g in heads], axis=0)
            lse = jnp.concatenate([lse_ref[:, g:g + 1] for g in heads], axis=0)
            p = jnp.exp2(_dot(q, k, NT) * (scale * LOG2_E) - lse * LOG2_E)
            dp = _dot(dog, v, NT)
            ds = (p * (dp - jnp.sum(dp * p, axis=-1, keepdims=True)) * scale).astype(MXU_DTYPE)
            dq = _dot(ds, k, NN)
            for n, g in enumerate(heads):
                dq_ref[:, g * hd:(g + 1) * hd] = dq[n * tq:(n + 1) * tq]
            dk_ref[...] += _dot(ds, q, TN)
            dv_ref[...] += _dot(p, dog, TN)

    vmem = 4 * _nbytes((rows, hd), MXU_DTYPE) + 4 * _nbytes((rows, hd), F32) + 6 * _nbytes((tq, rows), F32) + 8 * _nbytes((tq, gqw), F32)
    qspec = pl.BlockSpec((tq, gqw), lambda h, i: (i, h))
    kspec = pl.BlockSpec((rows, hd), lambda h, i: (0, h))
    return _call(
        body, (qh, kh, vh, do, lse), name=name, grid=(n_kv, n_x // tq),
        in_specs=[qspec, kspec, kspec, qspec, pl.BlockSpec((None, tq, grp), lambda h, i: (h, i, 0))],
        out_specs=[qspec, kspec, kspec],
        out_shape=[jax.ShapeDtypeStruct((n_x, d), F32), jax.ShapeDtypeStruct((rows, kvw), F32),
                   jax.ShapeDtypeStruct((rows, kvw), F32)],
        compiler_params=_params(("parallel", "arbitrary"), vmem), comm=comm)


def _attn_gate(o, z4, d, kvw, rt, name):
    g0 = (d + 2 * kvw) // kvw
    tr = rt.tile

    def body(o_ref, g_ref, t_ref):
        t_ref[...] = (o_ref[...] * _silu(g_ref[...])).astype(t_ref.dtype)

    tile = pl.BlockSpec((tr, kvw), lambda i, j: (i, j))
    return pl.pallas_call(
        body, name=name, grid=(rt.nt, d // kvw),
        in_specs=[tile, pl.BlockSpec((tr, kvw), lambda i, j: (i, g0 + j))],
        out_specs=tile, out_shape=jax.ShapeDtypeStruct((rt.rows, d), MXU_DTYPE),
        compiler_params=_params(("parallel", "parallel"), 8 * _nbytes((tr, kvw), F32)),
    )(o, z4)


def _attn_gate_bwd(dt, o, z4, d, kvw, rt, name):
    g0 = (d + 2 * kvw) // kvw
    tr = rt.tile

    def body(dt_ref, o_ref, g_ref, do_ref, dg_ref):
        dt_v, g = dt_ref[...], g_ref[...]
        do_ref[...] = (dt_v * _silu(g)).astype(do_ref.dtype)
        dg_ref[...] = (dt_v * o_ref[...] * _dsilu(g)).astype(dg_ref.dtype)

    tile = pl.BlockSpec((tr, kvw), lambda i, j: (i, j))
    return pl.pallas_call(
        body, name=name, grid=(rt.nt, d // kvw),
        in_specs=[tile, tile, pl.BlockSpec((tr, kvw), lambda i, j: (i, g0 + j))],
        out_specs=[tile, tile],
        out_shape=[jax.ShapeDtypeStruct((rt.rows, d), MXU_DTYPE), jax.ShapeDtypeStruct((rt.rows, d), MXU_DTYPE)],
        compiler_params=_params(("parallel", "parallel"), 12 * _nbytes((tr, kvw), F32)),
    )(dt, o, z4)


def _prep_bwd(dxh, z4, col_block, gain, cos, sin, rt, name):
    w = dxh.shape[1]
    hd = HEAD_DIM

    def body(dxh_ref, x_ref, g_ref, cos_ref, sin_ref, dx_ref, dg_ref):
        cos, sin = cos_ref[...], sin_ref[...]
        dg = jnp.zeros((1, hd), F32)
        for h in range(w // hd):
            cols = slice(h * hd, (h + 1) * hd)
            _, vjp = jax.vjp(_rms, x_ref[:, cols], g_ref[...])
            dx, dgh = vjp(_rope_t(dxh_ref[:, cols], cos, sin))
            dx_ref[:, cols] = dx.astype(dx_ref.dtype)
            dg = dg + dgh
        _acc(dg_ref, pl.program_id(0) == 0, dg)

    tr = rt.tile
    return pl.pallas_call(
        body, name=name, grid=(rt.nt,),
        in_specs=[rt.full(w), pl.BlockSpec((tr, w), lambda i: (i, col_block)), _vec(hd), rt.full(hd), rt.full(hd)],
        out_specs=[rt.full(w), _vec(hd)],
        out_shape=[jax.ShapeDtypeStruct((rt.rows, w), MXU_DTYPE), jax.ShapeDtypeStruct((1, hd), F32)],
        compiler_params=_params(("arbitrary",), 12 * _nbytes((tr, w), F32)),
    )(dxh, z4, gain, cos, sin)


def _loss_head(x, target, rt, name):
    d = x.shape[1]

    def body(x_ref, t_ref, dx_ref, l_ref):
        err = x_ref[...] - t_ref[...]
        dx_ref[...] = err / d
        row = jnp.mean(err * err, axis=-1, keepdims=True)
        _acc(l_ref, pl.program_id(0) == 0, jnp.sum(row, axis=0, keepdims=True))

    return pl.pallas_call(
        body, name=name, grid=(rt.nt,),
        in_specs=[rt.full(d), rt.full(d)],
        out_specs=[rt.full(d), pl.BlockSpec((1, 1), lambda i: (0, 0))],
        out_shape=[jax.ShapeDtypeStruct(x.shape, F32), jax.ShapeDtypeStruct((1, 1), F32)],
        compiler_params=_params(("arbitrary",), 8 * _nbytes((rt.tile, d), F32)),
    )(x, target)


def _adamw(w, g, m, v):
    m = ADAM_B1 * m + (1.0 - ADAM_B1) * g
    v = ADAM_B2 * v + (1.0 - ADAM_B2) * (g * g)
    m_hat = m / (1.0 - ADAM_B1 ** ADAM_STEP)
    v_hat = v / (1.0 - ADAM_B2 ** ADAM_STEP)
    delta = -ADAM_LR * (m_hat / (jnp.sqrt(v_hat) + ADAM_EPS) + ADAM_WD * w)
    return delta, m, v


ADAM_TILE_BYTES = 1 << 20


def _adam_tile(rows, cols):
    tr = rows
    while tr % 16 == 0 and tr * cols * 4 > ADAM_TILE_BYTES:
        tr //= 2
    return tr


def _adam_reduce(parts, w, m, v, name, comm=None):
    slots, rows, cols = w.shape
    tr = _adam_tile(rows, cols)
    nt = rows // tr

    def body(*refs):
        p_refs = refs[:slots]
        w_ref, m_ref, v_ref, g_ref, d_ref, mo_ref, vo_ref = refs[slots:]
        for k in range(slots):
            @pl.when(pl.program_id(0) == k)
            def _(p_ref=p_refs[k]):
                g = p_ref[0].astype(F32)
                for part in range(1, p_ref.shape[0]):
                    g = g + p_ref[part].astype(F32)
                g_ref[...] = g
                d_ref[...], mo_ref[...], vo_ref[...] = _adamw(w_ref[...], g, m_ref[...], v_ref[...])

    def part_spec(k):
        return pl.BlockSpec((parts[k].shape[0], tr, cols),
                            lambda s, i: (0, jnp.where(s < k, 0, jnp.where(s == k, i, nt - 1)), 0))

    row = pl.BlockSpec((None, tr, cols), lambda s, i: (s, i, 0))
    sds = jax.ShapeDtypeStruct((slots, rows, cols), F32)
    return _call(
        body, (*parts, w, m, v), name=name, grid=(slots, nt),
        in_specs=[part_spec(k) for k in range(slots)] + [row, row, row],
        out_specs=[row] * 4, out_shape=[sds] * 4,
        compiler_params=_params(("arbitrary", "arbitrary"), 40 * _nbytes((tr, cols), F32)), comm=comm)


def _adam_plain(g, w, m, v, name):
    rows, cols = w.shape
    tr = _adam_tile(rows, cols)

    def body(g_ref, w_ref, m_ref, v_ref, d_ref, mo_ref, vo_ref):
        d_ref[...], mo_ref[...], vo_ref[...] = _adamw(w_ref[...], g_ref[...], m_ref[...], v_ref[...])

    row = pl.BlockSpec((tr, cols), lambda i: (i, 0))
    sds = jax.ShapeDtypeStruct((rows, cols), F32)
    return pl.pallas_call(
        body, name=name, grid=(rows // tr,),
        in_specs=[row] * 4, out_specs=[row] * 3, out_shape=[sds] * 3,
        compiler_params=_params(("parallel",), 32 * _nbytes((tr, cols), F32)),
    )(g, w, m, v)


def _sum_devices(parts, name):
    _, rows, cols = parts.shape
    tr = _adam_tile(rows, cols)

    def body(p_ref, o_ref):
        g = p_ref[0]
        for k in range(1, N_DEV):
            g = g + p_ref[k]
        o_ref[...] = g

    return pl.pallas_call(
        body, name=name, grid=(rows // tr,),
        in_specs=[pl.BlockSpec((N_DEV, tr, cols), lambda i: (0, i, 0))],
        out_specs=pl.BlockSpec((tr, cols), lambda i: (i, 0)), out_shape=jax.ShapeDtypeStruct((rows, cols), F32),
        compiler_params=_params(("parallel",), 24 * _nbytes((tr, cols), F32)),
    )(parts)


COND_ROWS = 16


def _mod_fwd_mm(cond, mod_w, mod_b, name):
    layers, d, w = mod_w.shape

    def body(c_ref, w_ref, b_ref, o_ref):
        o_ref[...] = _dot(_silu(c_ref[...]), w_ref[...], NN) + b_ref[...]

    return pl.pallas_call(
        body, name=name, grid=(layers,),
        in_specs=[pl.BlockSpec((COND_ROWS, d), lambda l: (0, 0)), pl.BlockSpec((None, d, w), lambda l: (l, 0, 0)),
                  pl.BlockSpec((None, 1, w), lambda l: (l, 0, 0))],
        out_specs=pl.BlockSpec((None, COND_ROWS, w), lambda l: (l, 0, 0)),
        out_shape=jax.ShapeDtypeStruct((layers, COND_ROWS, w), F32),
        compiler_params=_params(("parallel",), 4 * _nbytes((d, w), F32)),
    )(cond, mod_w, mod_b)


def _mod_bwd_mm(cond, dm, mod_w, name):
    layers, d, w = mod_w.shape

    def body(c_ref, dm_ref, w_ref, dw_ref, dc_ref):
        dmv = dm_ref[...]
        dw_ref[...] = _dot(_silu(c_ref[...]), dmv, TN)
        _acc(dc_ref, pl.program_id(0) == 0, _dot(dmv, w_ref[...], NT))

    return pl.pallas_call(
        body, name=name, grid=(layers,),
        in_specs=[pl.BlockSpec((COND_ROWS, d), lambda l: (0, 0)), pl.BlockSpec((None, COND_ROWS, w), lambda l: (l, 0, 0)),
                  pl.BlockSpec((None, d, w), lambda l: (l, 0, 0))],
        out_specs=[pl.BlockSpec((None, d, w), lambda l: (l, 0, 0)), pl.BlockSpec((COND_ROWS, d), lambda l: (0, 0))],
        out_shape=[jax.ShapeDtypeStruct((layers, d, w), F32), jax.ShapeDtypeStruct((COND_ROWS, d), F32)],
        compiler_params=_params(("arbitrary",), 6 * _nbytes((d, w), F32)),
    )(cond, dm, mod_w)


PACK_ROWS = 256


def _pack(arrs):
    flat = jnp.concatenate([a.reshape(-1).astype(F32) for a in arrs])
    pad = (-flat.shape[0]) % (PACK_ROWS * LANES)
    return jnp.pad(flat, (0, pad)).reshape(-1, LANES)


def _unpack(flat2d, shapes):
    flat = flat2d.reshape(-1)
    out, off = [], 0
    for s in shapes:
        n = math.prod(s)
        out.append(flat[off:off + n].reshape(s))
        off += n
    return out


def _unpack_dev(g2d, shapes):
    flat = g2d.reshape(N_DEV, -1)
    out, off = [], 0
    for s in shapes:
        n = math.prod(s)
        out.append(flat[:, off:off + n].reshape((N_DEV, *s)))
        off += n
    return out


def kernel(x, c, ctx, c_ctx, mod_w, mod_b, post_g, post_b, a_w_in, a_ln_g, a_ln_b, a_w_s, a_b_s, a_w_out, b_w_in, b_conv_w, b_conv_b, b_ln_g, b_ln_b, b_w_out, c_w_in, c_q_g, c_k_g, c_w_out, loss_target, m_c_ctx, m_mod_w, m_mod_b, m_post_g, m_post_b, m_a_w_in, m_a_ln_g, m_a_ln_b, m_a_w_s, m_a_b_s, m_a_w_out, m_b_w_in, m_b_conv_w, m_b_conv_b, m_b_ln_g, m_b_ln_b, m_b_w_out, m_c_w_in, m_c_q_g, m_c_k_g, m_c_w_out, v_c_ctx, v_mod_w, v_mod_b, v_post_g, v_post_b, v_a_w_in, v_a_ln_g, v_a_ln_b, v_a_w_s, v_a_b_s, v_a_w_out, v_b_w_in, v_b_conv_w, v_b_conv_b, v_b_ln_g, v_b_ln_b, v_b_w_out, v_c_w_in, v_c_q_g, v_c_k_g, v_c_w_out):
    n_x, d = x.shape[1], x.shape[2]
    n_ctx = ctx.shape[1]
    e = a_w_out.shape[1] * N_DEV
    kvw = N_KV_HEADS * HEAD_DIM
    me = _dev_index(_mesh_pos())
    rt_all = _Rows(n_x, n_ctx)
    rt_x = _Rows(n_x, 0)

    small_in = [c[0], a_ln_g, a_ln_b, b_conv_w[0]]
    (g_small,) = _all_gather([_pack(small_in)], "ag_small_params")
    conds, ln_g_all, ln_b_all, conv_w_all = _unpack_dev(g_small, [a.shape for a in small_in])
    a_ln_g_f = jnp.moveaxis(ln_g_all, 0, 1).reshape(a_ln_g.shape[0], 1, e)
    a_ln_b_f = jnp.moveaxis(ln_b_all, 0, 1).reshape(a_ln_b.shape[0], 1, e)
    conv_w_f = jnp.moveaxis(conv_w_all, 0, 1).reshape(CONV_W, e)
    cond = jnp.zeros((COND_ROWS, d), F32).at[:N_DEV].set(conds).at[N_DEV].set(c_ctx)

    wm = mod_w.shape[2]
    mod_b_mine = lax.dynamic_slice_in_dim(mod_b, me * wm, wm, axis=1).reshape(DEPTH, 1, wm)
    (mods_g,) = _all_gather([_mod_fwd_mm(cond, mod_w, mod_b_mine, "mod_fwd")], "ag_mod")
    mods = jnp.moveaxis(mods_g, 0, 2).reshape(DEPTH, COND_ROWS, 3 * d)
    mine = lax.dynamic_index_in_dim(mods, me, axis=1, keepdims=False)
    modv = jnp.stack([mine, mods[:, N_DEV]], axis=1).reshape(DEPTH * 2 * 3, 1, d)

    def gather_of(wt):
        return _gather_comm([wt.astype(MXU_DTYPE)])

    def exchange_of(*gs):
        return _exchange_comm([g if g.ndim == 3 else g.reshape(N_DEV, -1, g.shape[-1]) for g in gs])

    def chip_sums_comm(g, theirs, tag):
        mine = lax.dynamic_index_in_dim(g.reshape(N_CHIPS, 2, *g.shape[1:]), lax.axis_index("c"), axis=1, keepdims=False)
        return _chip_exchange_comm([_pair_add(mine, theirs, f"pair_add_{tag}")])

    def chip_sums_of(g, tag):
        (theirs,) = _comm_call(_pair_exchange_comm([g]), f"pair_{tag}")
        return chip_sums_comm(g, theirs, tag)

    (wa_in0,) = _all_gather([a_w_in[0].astype(MXU_DTYPE)], "ag_w_l0")

    ws_op = a_w_s.astype(MXU_DTYPE)
    bs_t = jnp.swapaxes(a_b_s, 1, 2)
    pg = post_g.reshape(DEPTH, 1, d)
    pb = post_b.reshape(DEPTH, 1, d)

    xs0 = jnp.concatenate([x[0], ctx[0]], axis=0)
    h0 = _modulate(xs0, modv, 0, rt_all, "mod0")
    z0, (wb_in,) = _mm_nn(h0, wa_in0, planes=3, name="l0_in", comm=gather_of(b_w_in[0]))
    t0, (wa_out0,) = _cm_mid_fwd(z0, a_ln_g_f[0], a_ln_b_f[0], ws_op[0], bs_t[0], "l0_mid", comm=gather_of(a_w_out[0]))
    wa_out0 = wa_out0.reshape(-1, d)
    y0, (wb_out,) = _mm_nn(t0, wa_out0, name="l0_out", comm=gather_of(b_w_out[0]))
    xs1, h1 = _post_fwd(xs0, y0, modv, 0, pg[0], pb[0], rt_all, "l0_post")
    z1, (wc_in,) = _mm_nn(h1, wb_in, planes=3, name="l1_in", comm=gather_of(c_w_in[0]))
    cy1, (wa_in1,) = _conv_fwd(z1, conv_w_f, b_conv_b, rt_all, "l1_conv", comm=gather_of(a_w_in[1]))
    t1 = _conv_mid_fwd(cy1, z1, b_ln_g, b_ln_b, rt_all, "l1_mid")
    y1, (wc_out,) = _mm_nn(t1, wb_out.reshape(-1, d), name="l1_out", comm=gather_of(c_w_out[0]))
    xs2, h2 = _post_fwd(xs1, y1, modv, 1, pg[1], pb[1], rt_all, "l1_post")
    cos, sin = _rope_tables(n_x, n_ctx)
    z2 =_mm_nn(h2, wc_in, name="l2_in")
    qh, kh, vh = _qkv_prep(z2, c_q_g, c_k_g, cos, sin, d, kvw, rt_all, "l2_prep")
    (o2, lse), (wa_out1,) = _attn_fwd(qh, kh, vh, n_x, "l2_attn", comm=gather_of(a_w_out[1]))
    wb_out, wc_out, wa_out1 = [wt.reshape(-1, d) for wt in (wb_out, wc_out, wa_out1)]
    t2 = _attn_gate(o2, z2, d, kvw, rt_x, "l2_gate")
    y2 = _mm_nn(t2, wc_out, name="l2_out")
    x2 = xs2
    x3, h3 = _post_fwd(x2, y2, modv, 2, pg[2], pb[2], rt_x, "l2_post")
    z3 =_mm_nn(h3, wa_in1, planes=3, name="l3_in")
    t3 = _cm_mid_fwd(z3, a_ln_g_f[1], a_ln_b_f[1], ws_op[1], bs_t[1], "l3_mid")
    y3 = _mm_nn(t3, wa_out1, name="l3_out")
    x4 = _post_fwd(x3, y3, modv, 3, pg[3], pb[3], rt_x, "l3_post", modulate_next=False)

    dx4, loss_sum = _loss_head(x4, loss_target[0], rt_x, "loss")
    loss = lax.psum(0.5 * loss_sum[0, 0], ("x", "y", "c"))

    gdt = MXU_DTYPE
    dres3, dy3, dpg3, dpb3, dgate3 = _post_bwd(x3, y3, dx4, modv, 3, pg[3], pb[3], rt_x, "l3_post_b")
    dt3 = _mm_nt(dy3, wa_out1, name="l3_dt")
    gw_a_out1 = _mm_tn(t3, dy3, blocked=False, out_dtype=gdt, name="l3_dwout")
    dz3, dlg3, dlb3, dws3, dbs3 = _cm_mid_bwd(z3, dt3, a_ln_g_f[1], a_ln_b_f[1], ws_op[1], bs_t[1], "l3_mid_b")
    gw_a_in1 = _mm_tn(h3, dz3, blocked=True, out_dtype=gdt, name="l3_dwin")
    dh3, (r_a_out1,) = _mm_nt_blocked(dz3, wa_in1, name="l3_dh", comm=exchange_of(gw_a_out1))
    dx3, dshift3, dscale3 = _mod_bwd(dres3, dh3, x3, modv, 3, rt_x, "l3_mod_b")
    dres2, dy2, dpg2, dpb2, dgate2 = _post_bwd(x2, y2, dx3, modv, 2, pg[2], pb[2], rt_x, "l2_post_b")
    dt2 = _mm_nt(dy2, wc_out, name="l2_dt")
    gw_c_out = _mm_tn(t2, dy2, blocked=False, out_dtype=gdt, name="l2_dwout")
    do2, dg2 = _attn_gate_bwd(dt2, o2, z2, d, kvw, rt_x, "l2_gate_b")
    (dqh, dkh, dvh), (r_a_in1,) = _attn_bwd(qh, kh, vh, do2, lse, n_x, "l2_attn_b", comm=exchange_of(gw_a_in1))
    dq2, dqg = _prep_bwd(dqh, z2, 0, c_q_g, cos, sin, rt_x, "l2_qprep_b")
    dk2, dkg = _prep_bwd(dkh, z2, d // kvw, c_k_g, cos, sin, rt_all, "l2_kprep_b")
    zpad = jnp.zeros((n_ctx, d), MXU_DTYPE)
    dz2 = jnp.concatenate([jnp.concatenate([dq2, zpad], axis=0), dk2, dvh.astype(MXU_DTYPE),
                           jnp.concatenate([dg2, zpad], axis=0)], axis=1)
    gw_c_in = _mm_tn(h2, dz2, blocked=True, out_dtype=gdt, name="l2_dwin")
    dh2, (r_c_out,) = _mm_nt_blocked(dz2, wc_in, name="l2_dh", comm=exchange_of(gw_c_out))
    dxs2, dshift2, dscale2 = _mod_bwd(dres2, dh2, xs2, modv, 2, rt_all, "l2_mod_b")
    dres1, dy1, dpg1, dpb1, dgate1 = _post_bwd(xs1, y1, dxs2, modv, 1, pg[1], pb[1], rt_all, "l1_post_b")
    dt1 = _mm_nt(dy1, wb_out, name="l1_dt")
    gw_b_out = _mm_tn(t1, dy1, blocked=False, out_dtype=gdt, name="l1_dwout")
    dcy1, dgc1, dblg, dblb = _conv_mid_bwd(cy1, z1, dt1, b_ln_g, b_ln_b, rt_all, "l1_mid_b")
    (dz1, dconv_w, dconv_b), (r_c_in, r_b_out) = _conv_bwd(z1, dcy1, dgc1, conv_w_f, rt_all, "l1_conv_b",
                                                           comm=exchange_of(gw_c_in, gw_b_out))
    gw_b_in = _mm_tn(h1, dz1, blocked=True, out_dtype=gdt, name="l1_dwin")
    dh1, (pair_b_in,) = _mm_nt_blocked(dz1, wb_in, name="l1_dh", comm=_pair_exchange_comm([gw_b_in]))
    dxs1, dshift1, dscale1 = _mod_bwd(dres1, dh1, xs1, modv, 1, rt_all, "l1_mod_b")
    dres0, dy0, dpg0, dpb0, dgate0 = _post_bwd(xs0, y0, dxs1, modv, 0, pg[0], pb[0], rt_all, "l0_post_b")
    dt0 = _mm_nt(dy0, wa_out0, name="l0_dt")
    gw_a_out0 = _mm_tn(t0, dy0, blocked=False, out_dtype=gdt, name="l0_dwout")
    (dz0, dlg0, dlb0, dws0, dbs0), (r_a_out0,) = _cm_mid_bwd(z0, dt0, a_ln_g_f[0], a_ln_b_f[0], ws_op[0], bs_t[0], "l0_mid_b",
                                                             comm=exchange_of(gw_a_out0))
    gw_a_in0, (r_b_in,) = _mm_tn(h0, dz0, blocked=True, out_dtype=gdt, name="l0_dwin",
                                 comm=chip_sums_comm(gw_b_in, pair_b_in, "b_in"))
    dh0, (r_a_in0,) = _mm_nt_blocked(dz0, wa_in0, name="l0_dh", comm=chip_sums_of(gw_a_in0, "a_in0"))
    dx0, dshift0, dscale0 = _mod_bwd(dres0, dh0, xs0, modv, 0, rt_all, "l0_mod_b", dx_rows=n_x)
    grad_x = dx0[None]

    def seg2(a):
        a = a[:, 0]
        return a if a.shape[0] == 2 else jnp.concatenate([a, jnp.zeros_like(a)], axis=0)

    gate2 = jnp.concatenate([dgate2[:, 0], jnp.zeros((1, d), F32)], axis=0)
    dmod = jnp.stack([
        jnp.concatenate([seg2(dshift0), seg2(dscale0), seg2(dgate0)], axis=1),
        jnp.concatenate([seg2(dshift1), seg2(dscale1), seg2(dgate1)], axis=1),
        jnp.concatenate([seg2(dshift2), seg2(dscale2), gate2], axis=1),
        jnp.concatenate([seg2(dshift3), seg2(dscale3), seg2(dgate3)], axis=1)])

    g_post_g = jnp.concatenate([dpg0, dpg1, dpg2, dpg3], axis=0)
    g_post_b = jnp.concatenate([dpb0, dpb1, dpb2, dpb3], axis=0)
    g_a_ln_g = jnp.concatenate([dlg0, dlg3], axis=0)
    g_a_ln_b = jnp.concatenate([dlb0, dlb3], axis=0)
    g_a_w_s = jnp.stack([dws0, dws3])
    g_a_b_s = jnp.swapaxes(jnp.stack([dbs0, dbs3]), 1, 2)
    small_g = [g_post_g, g_post_b, g_a_w_s, g_a_b_s, dconv_b, dblg, dblb, dqg, dkg, g_a_ln_g, g_a_ln_b, dconv_w,
               dmod[:, 0], dmod[:, 1]]
    small_shapes = [a.shape for a in small_g]
    (gs_all,) = _all_gather([_pack(small_g)], "ag_small_grads")
    o_a_w_in = _adam_reduce([r_a_in0, r_a_in1], a_w_in, m_a_w_in, v_a_w_in, "adam_a_in")
    sums = _unpack(_sum_devices(gs_all, "sum_small"), small_shapes)
    (s_post_g, s_post_b, s_a_w_s, s_a_b_s, s_conv_b, s_b_ln_g, s_b_ln_b, s_q_g, s_k_g, s_a_ln_g, s_a_ln_b, s_conv_w,
     s_dmod_own, s_dmod_ctx) = sums
    grad_mod_b = s_dmod_own + s_dmod_ctx
    wl = a_ln_g.shape[1]
    wcv = b_conv_w.shape[2]
    grad_a_ln_g = lax.dynamic_slice_in_dim(s_a_ln_g, me * wl, wl, axis=1)
    grad_a_ln_b = lax.dynamic_slice_in_dim(s_a_ln_b, me * wl, wl, axis=1)
    grad_b_conv_w = lax.dynamic_slice_in_dim(s_conv_w, me * wcv, wcv, axis=1)[None]

    dmod_dev = _unpack_dev(gs_all, small_shapes)[12]
    dm_rows = jnp.concatenate([jnp.moveaxis(dmod_dev, 0, 1), s_dmod_ctx[:, None],
                               jnp.zeros((DEPTH, COND_ROWS - N_DEV - 1, 3 * d), F32)], axis=1)
    dm_mine = lax.dynamic_slice_in_dim(dm_rows, me * wm, wm, axis=2)
    grad_mod_w, dcond_part = _mod_bwd_mm(cond, dm_mine, mod_w, "mod_bwd")
    (dcond_all,) = _all_gather([dcond_part], "ag_dcond")
    dcond = _sum_devices(dcond_all, "sum_dcond")
    grad_c_ctx = dcond[N_DEV] * _dsilu(c_ctx)

    o_a_w_out = _adam_reduce([r_a_out0, r_a_out1], a_w_out, m_a_w_out, v_a_w_out, "adam_a_out")
    o_b_w_in = _adam_reduce([r_b_in], b_w_in, m_b_w_in, v_b_w_in, "adam_b_in")
    o_b_w_out = _adam_reduce([r_b_out], b_w_out, m_b_w_out, v_b_w_out, "adam_b_out")
    o_c_w_in = _adam_reduce([r_c_in], c_w_in, m_c_w_in, v_c_w_in, "adam_c_in")
    o_c_w_out = _adam_reduce([r_c_out], c_w_out, m_c_w_out, v_c_w_out, "adam_c_out")
    mw_shape = mod_w.shape
    o_mod_w = [grad_mod_w] + [a.reshape(mw_shape) for a in _adam_plain(
        grad_mod_w.reshape(-1, wm), mod_w.reshape(-1, wm), m_mod_w.reshape(-1, wm), v_mod_w.reshape(-1, wm), "adam_mod_w")]

    sg = [grad_c_ctx, grad_mod_b, s_post_g, s_post_b, grad_a_ln_g, grad_a_ln_b, s_a_w_s, s_a_b_s, grad_b_conv_w, s_conv_b,
          s_b_ln_g, s_b_ln_b, s_q_g, s_k_g]
    sw = [c_ctx, mod_b, post_g, post_b, a_ln_g, a_ln_b, a_w_s, a_b_s, b_conv_w, b_conv_b, b_ln_g, b_ln_b, c_q_g, c_k_g]
    sm = [m_c_ctx, m_mod_b, m_post_g, m_post_b, m_a_ln_g, m_a_ln_b, m_a_w_s, m_a_b_s, m_b_conv_w, m_b_conv_b, m_b_ln_g,
          m_b_ln_b, m_c_q_g, m_c_k_g]
    sv = [v_c_ctx, v_mod_b, v_post_g, v_post_b, v_a_ln_g, v_a_ln_b, v_a_w_s, v_a_b_s, v_b_conv_w, v_b_conv_b, v_b_ln_g,
          v_b_ln_b, v_c_q_g, v_c_k_g]
    shapes = [a.shape for a in sw]
    sg = [g.reshape(s) for g, s in zip(sg, shapes)]
    sd, snm, snv = [_unpack(a, shapes) for a in _adam_plain(_pack(sg), _pack(sw), _pack(sm), _pack(sv), "adam_small")]

    def small(k):
        return [sg[k], sd[k], snm[k], snv[k]]

    per_weight = [small(0), o_mod_w, small(1), small(2), small(3), o_a_w_in, small(4), small(5), small(6), small(7),
                  o_a_w_out, o_b_w_in, small(8), small(9), small(10), small(11), o_b_w_out, o_c_w_in, small(12), small(13),
                  o_c_w_out]
    outs = [loss, grad_x]
    for kind in range(4):
        outs += [pw[kind] for pw in per_weight]
    return tuple(outs)
```

```python
import functools
import math

import jax
import jax.numpy as jnp
from jax import lax
from jax.experimental import pallas as pl
from jax.experimental.pallas import tpu as pltpu

F32 = jnp.float32
BF16 = jnp.bfloat16
MXU_DTYPE = jnp.bfloat16

DEPTH = 4
GRID_W = 64
CHUNK = 128
SGU_GROUPS = 16
CONV_W = 31
CONV_HALO = 16
HEAD_DIM = 128
N_KV_HEADS = 4
ROPE_THETA = 10000.0
DEEPNORM_ALPHA = (2 * DEPTH) ** 0.25
LN_EPS = 1e-6
ADAM_LR, ADAM_B1, ADAM_B2, ADAM_EPS, ADAM_WD, ADAM_STEP = 0.001, 0.9, 0.999, 1e-08, 0.01, 10

N_DEV = 8
V7X_VMEM_BYTES = 64 * 1024 * 1024
V7X_VMEM_CLAIM = V7X_VMEM_BYTES * 7 // 8
LANES = 128

NN = ((1,), (0,))
NT = ((1,), (1,))
TN = ((0,), (0,))


def _dot(a, b, dims):
    return lax.dot_general(a.astype(MXU_DTYPE), b.astype(MXU_DTYPE), (dims, ((), ())), preferred_element_type=F32)


def _pick(n, prefs):
    for p in prefs:
        if n % p == 0:
            return p
    raise ValueError(f"no tile for {n} among {prefs}")


def _params(sem, vmem_bytes):
    assert vmem_bytes <= V7X_VMEM_CLAIM, (vmem_bytes, V7X_VMEM_CLAIM)
    return pltpu.CompilerParams(dimension_semantics=sem, vmem_limit_bytes=V7X_VMEM_CLAIM)


def _nbytes(shape, dtype):
    return math.prod(shape) * jnp.dtype(dtype).itemsize


def _sigmoid(x):
    return jax.nn.sigmoid(x)


def _silu(x):
    return x * jax.nn.sigmoid(x)


def _dsilu(x):
    s = jax.nn.sigmoid(x)
    return s * (1.0 + x * (1.0 - s))


def _ln_stats(x):
    mu = jnp.mean(x, axis=-1, keepdims=True)
    xc = x - mu
    var = jnp.mean(xc * xc, axis=-1, keepdims=True)
    return xc, lax.rsqrt(var + LN_EPS)


def _ln(x, g, b):
    xc, rstd = _ln_stats(x)
    return xc * rstd * g + b


def _mesh_pos():
    return lax.axis_index("x"), lax.axis_index("y"), lax.axis_index("c")


def _dev_index(p):
    return 4 * p[0] + 2 * p[1] + p[2]


class _Comm:
    def __init__(self, inputs, out_shapes, sems, start, finish):
        self.inputs, self.out_shapes, self.sems, self.start, self.finish = inputs, out_shapes, sems, start, finish


def _gather_comm(xs):
    n = len(xs)

    def place():
        x, y, c = _mesh_pos()
        return (x, y, c), (x, y, 1 - c), [(1 - x, y), (x, 1 - y), (1 - x, 1 - y)], c

    def copier(x_refs, o_refs, sems):
        send_sems, recv_sems, _ = sems

        def copy(t, k, block, to, from_input=False):
            dst = o_refs[t].at[_dev_index(block)]
            return pltpu.make_async_remote_copy(
                src_ref=x_refs[t] if from_input else dst, dst_ref=dst,
                send_sem=send_sems.at[t, k], recv_sem=recv_sems.at[t, k],
                device_id=to, device_id_type=pl.DeviceIdType.MESH)

        return copy

    def own(x_refs, o_refs, sems, t, me):
        return pltpu.make_async_copy(x_refs[t], o_refs[t].at[_dev_index(me)], sems[2].at[t])

    def first_copies(copy, t, me, sibling, chips, c):
        return [copy(t, 0, me, sibling, True)] + [copy(t, 1 + j, me, (*chip, c), True) for j, chip in enumerate(chips)]

    def start(x_refs, o_refs, sems):
        me, sibling, chips, c = place()
        copy = copier(x_refs, o_refs, sems)
        for t in range(n):
            own(x_refs, o_refs, sems, t, me).start()
            for cp in first_copies(copy, t, me, sibling, chips, c):
                cp.start()

    def finish(x_refs, o_refs, sems):
        me, sibling, chips, c = place()
        copy = copier(x_refs, o_refs, sems)
        passed = []
        for t in range(n):
            for j, chip in enumerate(chips):
                copy(t, 1 + j, (*chip, c), me).wait_recv()
                cp = copy(t, 4 + j, (*chip, c), sibling)
                cp.start()
                passed.append(cp)
        for t in range(n):
            copy(t, 0, sibling, me).wait_recv()
            for j, chip in enumerate(chips):
                copy(t, 4 + j, (*chip, 1 - c), me).wait_recv()
        for t in range(n):
            for cp in first_copies(copy, t, me, sibling, chips, c):
                cp.wait_send()
        for cp in passed:
            cp.wait_send()
        for t in range(n):
            own(x_refs, o_refs, sems, t, me).wait()

    sems = [pltpu.SemaphoreType.DMA((n, 7)), pltpu.SemaphoreType.DMA((n, 7)), pltpu.SemaphoreType.DMA((n,))]
    return _Comm(list(xs), [jax.ShapeDtypeStruct((N_DEV, *a.shape), a.dtype) for a in xs], sems, start, finish)


def _exchange_comm(gs):
    n = len(gs)

    def copies(g_refs, r_refs, sems):
        send_sems, recv_sems, local_sems = sems
        x, y, c = _mesh_pos()
        me = _dev_index((x, y, c))
        out = []
        for t in range(n):
            out.append(pltpu.make_async_copy(g_refs[t].at[me], r_refs[t].at[me], local_sems.at[t]))
            for k in range(1, N_DEV):
                fx, fy, fc = (k >> 2) & 1, (k >> 1) & 1, k & 1
                peer = (1 - x if fx else x, 1 - y if fy else y, 1 - c if fc else c)
                out.append(pltpu.make_async_remote_copy(
                    src_ref=g_refs[t].at[_dev_index(peer)], dst_ref=r_refs[t].at[me],
                    send_sem=send_sems.at[t, k - 1], recv_sem=recv_sems.at[t, k - 1],
                    device_id=peer, device_id_type=pl.DeviceIdType.MESH))
        return out

    def start(g_refs, r_refs, sems):
        for cp in copies(g_refs, r_refs, sems):
            cp.start()

    def finish(g_refs, r_refs, sems):
        for cp in copies(g_refs, r_refs, sems):
            cp.wait()

    sems = [pltpu.SemaphoreType.DMA((n, 7)), pltpu.SemaphoreType.DMA((n, 7)), pltpu.SemaphoreType.DMA((n,))]
    return _Comm(list(gs), [jax.ShapeDtypeStruct(g.shape, g.dtype) for g in gs], sems, start, finish)


N_CHIPS = N_DEV // 2


def _started_and_waited(copies):
    def start(in_refs, out_refs, sems):
        for cp in copies(in_refs, out_refs, sems):
            cp.start()

    def finish(in_refs, out_refs, sems):
        for cp in copies(in_refs, out_refs, sems):
            cp.wait()

    return start, finish


def _pair_exchange_comm(gs):
    n = len(gs)

    def copies(g_refs, r_refs, sems):
        send_sems, recv_sems = sems
        x, y, c = _mesh_pos()
        return [pltpu.make_async_remote_copy(
            src_ref=g_refs[t].at[2 * q + 1 - c], dst_ref=r_refs[t].at[q],
            send_sem=send_sems.at[t, q], recv_sem=recv_sems.at[t, q],
            device_id=(x, y, 1 - c), device_id_type=pl.DeviceIdType.MESH) for t in range(n) for q in range(N_CHIPS)]

    sems = [pltpu.SemaphoreType.DMA((n, N_CHIPS)), pltpu.SemaphoreType.DMA((n, N_CHIPS))]
    return _Comm(list(gs), [jax.ShapeDtypeStruct((N_CHIPS, *g.shape[1:]), g.dtype) for g in gs], sems,
                 *_started_and_waited(copies))


def _chip_exchange_comm(hs):
    n = len(hs)

    def copies(h_refs, r_refs, sems):
        send_sems, recv_sems, local_sems = sems
        x, y, c = _mesh_pos()
        mine = 2 * x + y
        out = []
        for t in range(n):
            out.append(pltpu.make_async_copy(h_refs[t].at[mine], r_refs[t].at[mine], local_sems.at[t]))
            for k in range(1, N_CHIPS):
                px, py = (1 - x if (k >> 1) & 1 else x), (1 - y if k & 1 else y)
                out.append(pltpu.make_async_remote_copy(
                    src_ref=h_refs[t].at[2 * px + py], dst_ref=r_refs[t].at[mine],
                    send_sem=send_sems.at[t, k - 1], recv_sem=recv_sems.at[t, k - 1],
                    device_id=(px, py, c), device_id_type=pl.DeviceIdType.MESH))
        return out

    sems = [pltpu.SemaphoreType.DMA((n, N_CHIPS - 1)), pltpu.SemaphoreType.DMA((n, N_CHIPS - 1)), pltpu.SemaphoreType.DMA((n,))]
    return _Comm(list(hs), [jax.ShapeDtypeStruct(h.shape, h.dtype) for h in hs], sems, *_started_and_waited(copies))


def _pair_add(mine, theirs, name):
    q, rows, cols = mine.shape
    tr = _pick(rows, (512, 256, 128, 64, 32, 16))

    def body(a_ref, b_ref, o_ref):
        o_ref[...] = (a_ref[...].astype(F32) + b_ref[...].astype(F32)).astype(o_ref.dtype)

    blk = pl.BlockSpec((None, tr, cols), lambda s, i: (s, i, 0))
    return pl.pallas_call(
        body, name=name, grid=(q, rows // tr), in_specs=[blk, blk], out_specs=blk,
        out_shape=jax.ShapeDtypeStruct(mine.shape, mine.dtype),
        compiler_params=_params(("parallel", "parallel"), 16 * _nbytes((tr, cols), F32)),
    )(mine, theirs)


def _comm_call(comm, name):
    n_in, n_out = len(comm.inputs), len(comm.out_shapes)

    def body(*refs):
        ins, outs, sems = refs[:n_in], refs[n_in:n_in + n_out], refs[n_in + n_out:]
        comm.start(ins, outs, sems)
        comm.finish(ins, outs, sems)

    hbm = pl.BlockSpec(memory_space=pl.ANY)
    return pl.pallas_call(
        body, name=name, out_shape=comm.out_shapes, in_specs=[hbm] * n_in, out_specs=[hbm] * n_out,
        scratch_shapes=comm.sems)(*comm.inputs)


def _call(body, operands, *, name, grid, in_specs, out_specs, out_shape, scratch_shapes=(), compiler_params, comm=None):
    single = not isinstance(out_shape, (list, tuple))
    out_shape = [out_shape] if single else list(out_shape)
    out_specs = [out_specs] if single else list(out_specs)
    scratch_shapes = list(scratch_shapes)
    if comm is None:
        res = pl.pallas_call(
            body, name=name, grid=grid, in_specs=list(in_specs), out_specs=out_specs, out_shape=out_shape,
            scratch_shapes=scratch_shapes, compiler_params=compiler_params)(*operands)
        return res[0] if single else res
    n_in, n_out, n_scr = len(in_specs), len(out_specs), len(scratch_shapes)
    c_in, c_out = len(comm.inputs), len(comm.out_shapes)

    def with_comm(*refs):
        ins, c_ins = refs[:n_in], refs[n_in:n_in + c_in]
        o0 = n_in + c_in
        outs, c_outs = refs[o0:o0 + n_out], refs[o0 + n_out:o0 + n_out + c_out]
        s0 = o0 + n_out + c_out
        scr, sems = refs[s0:s0 + n_scr], refs[s0 + n_scr:]
        ids = [pl.program_id(a) for a in range(len(grid))]
        first = functools.reduce(jnp.logical_and, [i == 0 for i in ids])
        last = functools.reduce(jnp.logical_and, [i == g - 1 for i, g in zip(ids, grid)])

        @pl.when(first)
        def _():
            comm.start(c_ins, c_outs, sems)

        body(*ins, *outs, *scr)

        @pl.when(last)
        def _():
            comm.finish(c_ins, c_outs, sems)

    hbm = pl.BlockSpec(memory_space=pl.ANY)
    params = pltpu.CompilerParams(dimension_semantics=("arbitrary",) * len(grid),
                                  vmem_limit_bytes=compiler_params.vmem_limit_bytes)
    res = pl.pallas_call(
        with_comm, name=name, grid=grid, in_specs=list(in_specs) + [hbm] * c_in, out_specs=out_specs + [hbm] * c_out,
        out_shape=out_shape + list(comm.out_shapes), scratch_shapes=scratch_shapes + list(comm.sems),
        compiler_params=params)(*operands, *comm.inputs)
    return (res[0] if single else res[:n_out]), res[n_out:]


def _all_gather(xs, name):
    return _comm_call(_gather_comm(xs), name)


ROW_TILES = (1088, 1024, 768, 544, 512, 384, 272, 256, 128)
TOKEN_K_TILES = (2176, 2048, 1088, 1024, 768, 512, 384, 256, 128)
COL_TILES = (1024, 768, 640, 512, 384, 256, 128)
DEEP_K = 2048


def _mm_nn(a, w, *, planes=1, name, comm=None):
    m, k = a.shape
    if w.ndim == 3:
        nd_w = w.shape[2]
        n = w.shape[0] * nd_w
    else:
        nd_w = n = w.shape[1]
    npl = n // planes
    tm = _pick(m, ROW_TILES)
    tn = _pick(math.gcd(nd_w, npl), COL_TILES if k <= DEEP_K else COL_TILES[3:])
    r, rp = nd_w // tn, npl // tn
    if w.ndim == 3:
        w_spec = pl.BlockSpec((None, k, tn), lambda i, j: (j // r, 0, j % r))
    else:
        w_spec = pl.BlockSpec((k, tn), lambda i, j: (0, j))
    if planes > 1:
        o_spec = pl.BlockSpec((None, tm, tn), lambda i, j: (j // rp, i, j % rp))
        out_shape = jax.ShapeDtypeStruct((planes, m, npl), F32)
    else:
        o_spec = pl.BlockSpec((tm, tn), lambda i, j: (i, j))
        out_shape = jax.ShapeDtypeStruct((m, n), F32)

    def body(a_ref, w_ref, o_ref):
        o_ref[...] = _dot(a_ref[...], w_ref[...], NN)

    vmem = 2 * (_nbytes((tm, k), a.dtype) + _nbytes((k, tn), w.dtype) + _nbytes((tm, tn), F32)) + _nbytes((tm, tn), F32)
    return _call(
        body, (a, w), name=name, grid=(m // tm, n // tn),
        in_specs=[pl.BlockSpec((tm, k), lambda i, j: (i, 0)), w_spec], out_specs=o_spec, out_shape=out_shape,
        compiler_params=_params(("parallel", "arbitrary"), vmem), comm=comm)


def _mm_nt(a, w, *, name):
    m, k = a.shape
    n = w.shape[0]
    tm = _pick(m, ROW_TILES)
    tn = _pick(n, COL_TILES)

    def body(a_ref, w_ref, o_ref):
        o_ref[...] = _dot(a_ref[...], w_ref[...], NT)

    vmem = 2 * (_nbytes((tm, k), a.dtype) + _nbytes((tn, k), w.dtype) + _nbytes((tm, tn), F32)) + _nbytes((tm, tn), F32)
    return pl.pallas_call(
        body, name=name, grid=(m // tm, n // tn),
        in_specs=[pl.BlockSpec((tm, k), lambda i, j: (i, 0)), pl.BlockSpec((tn, k), lambda i, j: (j, 0))],
        out_specs=pl.BlockSpec((tm, tn), lambda i, j: (i, j)), out_shape=jax.ShapeDtypeStruct((m, n), F32),
        compiler_params=_params(("parallel", "arbitrary"), vmem),
    )(a, w)


def _mm_nt_blocked(a, w, *, name, comm=None):
    nd, n, kd = w.shape
    if a.ndim == 3:
        p, m, kp = a.shape
    else:
        (m, kp), p = a.shape, 1
    tk = _pick(math.gcd(kd, kp), COL_TILES)
    ra, rw = kp // tk, kd // tk
    nk = nd * rw
    tm = _pick(m, ROW_TILES)
    if a.ndim == 3:
        a_spec = pl.BlockSpec((None, tm, tk), lambda i, kk: (kk // ra, i, kk % ra))
    else:
        a_spec = pl.BlockSpec((tm, tk), lambda i, kk: (i, kk))

    def body(a_ref, w_ref, o_ref, acc_ref):
        kk = pl.program_id(1)

        @pl.when(kk == 0)
        def _():
            acc_ref[...] = jnp.zeros_like(acc_ref)

        acc_ref[...] += _dot(a_ref[...], w_ref[...], NT)

        @pl.when(kk == nk - 1)
        def _():
            o_ref[...] = acc_ref[...]

    vmem = 2 * (_nbytes((tm, tk), a.dtype) + _nbytes((n, tk), w.dtype) + _nbytes((tm, n), F32)) + 2 * _nbytes((tm, n), F32)
    return _call(
        body, (a, w), name=name, grid=(m // tm, nk),
        in_specs=[a_spec, pl.BlockSpec((None, n, tk), lambda i, kk: (kk // rw, 0, kk % rw))],
        out_specs=pl.BlockSpec((tm, n), lambda i, kk: (i, 0)), out_shape=jax.ShapeDtypeStruct((m, n), F32),
        scratch_shapes=[pltpu.VMEM((tm, n), F32)],
        compiler_params=_params(("parallel", "arbitrary"), vmem), comm=comm)


def _mm_tn(a, b, *, blocked, out_dtype, name, comm=None):
    rows, da = a.shape
    if b.ndim == 3:
        p, _, npl = b.shape
    else:
        p, npl = 1, b.shape[1]
    n = p * npl
    nd_w = n // N_DEV if blocked else n
    tk = _pick(rows, TOKEN_K_TILES)
    tm = _pick(da, COL_TILES)
    tn = _pick(math.gcd(nd_w, npl), COL_TILES)
    rb, ro = npl // tn, nd_w // tn
    nk = rows // tk
    if b.ndim == 3:
        b_spec = pl.BlockSpec((None, tk, tn), lambda i, j, kk: (j // rb, kk, j % rb))
    else:
        b_spec = pl.BlockSpec((tk, tn), lambda i, j, kk: (kk, j))
    if blocked:
        o_spec = pl.BlockSpec((None, tm, tn), lambda i, j, kk: (j // ro, i, j % ro))
        out_shape = jax.ShapeDtypeStruct((N_DEV, da, nd_w), out_dtype)
    else:
        o_spec = pl.BlockSpec((tm, tn), lambda i, j, kk: (i, j))
        out_shape = jax.ShapeDtypeStruct((da, n), out_dtype)

    def body(a_ref, b_ref, o_ref, acc_ref):
        kk = pl.program_id(2)

        @pl.when(kk == 0)
        def _():
            acc_ref[...] = jnp.zeros_like(acc_ref)

        acc_ref[...] += _dot(a_ref[...], b_ref[...], TN)

        @pl.when(kk == nk - 1)
        def _():
            o_ref[...] = acc_ref[...].astype(o_ref.dtype)

    vmem = (2 * (_nbytes((tk, tm), a.dtype) + _nbytes((tk, tn), b.dtype) + _nbytes((tm, tn), out_dtype))
            + 3 * _nbytes((tm, tn), F32) + _nbytes((tk, tm), F32))
    return _call(
        body, (a, b), name=name, grid=(da // tm, n // tn, nk),
        in_specs=[pl.BlockSpec((tk, tm), lambda i, j, kk: (kk, i)), b_spec], out_specs=o_spec, out_shape=out_shape,
        scratch_shapes=[pltpu.VMEM((tm, tn), F32)],
        compiler_params=_params(("parallel", "parallel", "arbitrary"), vmem), comm=comm)


ROW_TILE = 256


class _Rows:
    def __init__(self, n_x, n_ctx, tile=ROW_TILE):
        assert n_x % tile == 0 and n_ctx % tile == 0
        self.n_x, self.n_ctx, self.tile = n_x, n_ctx, tile
        self.rows = n_x + n_ctx
        self.nt_x = n_x // tile
        self.nt = self.rows // tile
        self.n_seg = 2 if n_ctx else 1

    def seg(self, i):
        return jnp.where(i >= self.nt_x, 1, 0) if self.n_ctx else 0

    def first_of_seg(self, i):
        return (i == 0) | (i == self.nt_x) if self.n_ctx else i == 0

    def full(self, width):
        return pl.BlockSpec((self.tile, width), lambda i: (i, 0))

    def plane(self, p, width):
        return pl.BlockSpec((None, self.tile, width), lambda i: (p, i, 0))

    def modvec(self, layer, which, width):
        return pl.BlockSpec((None, 1, width), lambda i: ((layer * 2 + self.seg(i)) * 3 + which, 0, 0))

    def seg_acc(self, width):
        return pl.BlockSpec((None, 1, width), lambda i: (self.seg(i), 0, 0))


def _vec(width):
    return pl.BlockSpec((1, width), lambda i: (0, 0))


def _acc(ref, first, val):
    @pl.when(first)
    def _():
        ref[...] = jnp.zeros_like(ref)

    ref[...] += val


def _modulate(xs, modv, layer, rt, name):
    d = xs.shape[1]

    def body(x_ref, sh_ref, sc_ref, o_ref):
        o_ref[...] = (x_ref[...] * (1.0 + sc_ref[...]) + sh_ref[...]).astype(o_ref.dtype)

    return pl.pallas_call(
        body, name=name, grid=(rt.nt,),
        in_specs=[rt.full(d), rt.modvec(layer, 0, d), rt.modvec(layer, 1, d)],
        out_specs=rt.full(d), out_shape=jax.ShapeDtypeStruct(xs.shape, MXU_DTYPE),
        compiler_params=_params(("parallel",), 6 * _nbytes((rt.tile, d), F32)),
    )(xs, modv, modv)


def _post(x, y, gate, pg, pb):
    return _ln(DEEPNORM_ALPHA * x + gate * y, pg, pb)


def _post_fwd(xs, y, modv, layer, pg, pb, rt, name, modulate_next=True):
    d = xs.shape[1]

    def body(x_ref, y_ref, gate_ref, pg_ref, pb_ref, *rest):
        out = _post(x_ref[...], y_ref[...], gate_ref[...], pg_ref[...], pb_ref[...])
        if modulate_next:
            sh_ref, sc_ref, o_ref, h_ref = rest
            h_ref[...] = (out * (1.0 + sc_ref[...]) + sh_ref[...]).astype(h_ref.dtype)
        else:
            (o_ref,) = rest
        o_ref[...] = out

    nxt = [rt.modvec(layer + 1, 0, d), rt.modvec(layer + 1, 1, d)] if modulate_next else []
    res = pl.pallas_call(
        body, name=name, grid=(rt.nt,),
        in_specs=[rt.full(d), rt.full(d), rt.modvec(layer, 2, d), _vec(d), _vec(d)] + nxt,
        out_specs=[rt.full(d)] * (2 if modulate_next else 1),
        out_shape=[jax.ShapeDtypeStruct((rt.rows, d), F32)] + ([jax.ShapeDtypeStruct((rt.rows, d), MXU_DTYPE)] if modulate_next else []),
        compiler_params=_params(("parallel",), 12 * _nbytes((rt.tile, d), F32)),
    )(xs, y, modv, pg, pb, *([modv, modv] if modulate_next else []))
    return res if modulate_next else res[0]


def _post_bwd(xs, y, dout, modv, layer, pg, pb, rt, name):
    d = xs.shape[1]

    def body(x_ref, y_ref, do_ref, gate_ref, pg_ref, pb_ref, dres_ref, dy_ref, dpg_ref, dpb_ref, dgate_ref):
        i = pl.program_id(0)
        _, vjp = jax.vjp(_post, x_ref[...], y_ref[...], gate_ref[...], pg_ref[...], pb_ref[...])
        dx, dy, dgate, dpg, dpb = vjp(do_ref[...])
        dres_ref[...] = dx
        dy_ref[...] = dy.astype(dy_ref.dtype)
        _acc(dpg_ref, i == 0, dpg)
        _acc(dpb_ref, i == 0, dpb)
        _acc(dgate_ref, rt.first_of_seg(i), dgate)

    return pl.pallas_call(
        body, name=name, grid=(rt.nt,),
        in_specs=[rt.full(d), rt.full(d), rt.full(d), rt.modvec(layer, 2, d), _vec(d), _vec(d)],
        out_specs=[rt.full(d), rt.full(d), _vec(d), _vec(d), rt.seg_acc(d)],
        out_shape=[jax.ShapeDtypeStruct((rt.rows, d), F32), jax.ShapeDtypeStruct((rt.rows, d), MXU_DTYPE),
                   jax.ShapeDtypeStruct((1, d), F32), jax.ShapeDtypeStruct((1, d), F32),
                   jax.ShapeDtypeStruct((rt.n_seg, 1, d), F32)],
        compiler_params=_params(("arbitrary",), 16 * _nbytes((rt.tile, d), F32)),
    )(xs, y, dout, modv, pg, pb)


def _mod_bwd(dres, dh, xs, modv, layer, rt, name, dx_rows=None):
    d = xs.shape[1]
    nt_res = dres.shape[0] // rt.tile
    nt_dx = rt.nt if dx_rows is None else dx_rows // rt.tile

    def body(dres_ref, dh_ref, x_ref, sc_ref, dx_ref, dshift_ref, dscale_ref):
        i = pl.program_id(0)
        dh = dh_ref[...]

        @pl.when(i < nt_dx)
        def _():
            dx_ref[...] = jnp.where(i < nt_res, dres_ref[...], 0.0) + dh * (1.0 + sc_ref[...])

        first = rt.first_of_seg(i)
        _acc(dshift_ref, first, jnp.sum(dh, axis=0, keepdims=True))
        _acc(dscale_ref, first, jnp.sum(dh * x_ref[...], axis=0, keepdims=True))

    def clamped(nt):
        return pl.BlockSpec((rt.tile, d), lambda i: (jnp.minimum(i, nt - 1), 0))

    return pl.pallas_call(
        body, name=name, grid=(rt.nt,),
        in_specs=[clamped(nt_res), rt.full(d), rt.full(d), rt.modvec(layer, 1, d)],
        out_specs=[clamped(nt_dx), rt.seg_acc(d), rt.seg_acc(d)],
        out_shape=[jax.ShapeDtypeStruct((nt_dx * rt.tile, d), F32), jax.ShapeDtypeStruct((rt.n_seg, 1, d), F32),
                   jax.ShapeDtypeStruct((rt.n_seg, 1, d), F32)],
        compiler_params=_params(("arbitrary",), 10 * _nbytes((rt.tile, d), F32)),
    )(dres, dh, xs, modv)


def _cm_mid_fwd(z3, ln_g, ln_b, w_s, b_s_t, name, comm=None):
    _, rows, e = z3.shape
    groups = w_s.shape[0]
    gw = e // groups

    def body(z_ref, lg_ref, lb_ref, ws_ref, bs_ref, t_ref):
        vn = _ln(z_ref[1], lg_ref[...], lb_ref[...])
        for h in range(groups):
            cols = slice(h * gw, (h + 1) * gw)
            s = _dot(ws_ref[h], vn[:, cols], NN) + bs_ref[:, h:h + 1]
            t_ref[:, cols] = (z_ref[0, :, cols] * s * _silu(z_ref[2, :, cols])).astype(t_ref.dtype)

    return _call(
        body, (z3, ln_g, ln_b, w_s, b_s_t), name=name, grid=(rows // CHUNK,),
        in_specs=[pl.BlockSpec((3, CHUNK, e), lambda i: (0, i, 0)), _vec(e), _vec(e),
                  pl.BlockSpec(w_s.shape, lambda i: (0, 0, 0)), pl.BlockSpec(b_s_t.shape, lambda i: (0, 0))],
        out_specs=pl.BlockSpec((CHUNK, e), lambda i: (i, 0)), out_shape=jax.ShapeDtypeStruct((rows, e), MXU_DTYPE),
        compiler_params=_params(("parallel",), 12 * _nbytes((CHUNK, e), F32)), comm=comm)


def _cm_mid_bwd(z3, dt, ln_g, ln_b, w_s, b_s_t, name, comm=None):
    _, rows, e = z3.shape
    groups = w_s.shape[0]
    gw = e // groups

    def body(z_ref, dt_ref, lg_ref, lb_ref, ws_ref, bs_ref, dz_ref, dlg_ref, dlb_ref, dws_ref, dbs_ref, dvn_ref):
        i = pl.program_id(0)
        first = i == 0
        v = z_ref[1]
        vn, ln_vjp = jax.vjp(_ln, v, lg_ref[...], lb_ref[...])

        @pl.when(first)
        def _():
            dws_ref[...] = jnp.zeros_like(dws_ref)
            dbs_ref[...] = jnp.zeros_like(dbs_ref)

        for h in range(groups):
            cols = slice(h * gw, (h + 1) * gw)
            vn_h = vn[:, cols]
            s = _dot(ws_ref[h], vn_h, NN) + bs_ref[:, h:h + 1]
            u, g, dth = z_ref[0, :, cols], z_ref[2, :, cols], dt_ref[:, cols]
            sg = _silu(g)
            dz_ref[0, :, cols] = (dth * s * sg).astype(dz_ref.dtype)
            dz_ref[2, :, cols] = (dth * u * s * _dsilu(g)).astype(dz_ref.dtype)
            ds = dth * u * sg
            dvn_ref[:, cols] = _dot(ws_ref[h], ds, TN)
            dws_ref[h] += _dot(ds, vn_h, NT)
            dbs_ref[:, h:h + 1] += jnp.sum(ds, axis=1, keepdims=True)
        dv, dlg, dlb = ln_vjp(dvn_ref[...])
        dz_ref[1] = dv.astype(dz_ref.dtype)
        _acc(dlg_ref, first, dlg)
        _acc(dlb_ref, first, dlb)

    return _call(
        body, (z3, dt, ln_g, ln_b, w_s, b_s_t), name=name, grid=(rows // CHUNK,),
        in_specs=[pl.BlockSpec((3, CHUNK, e), lambda i: (0, i, 0)), pl.BlockSpec((CHUNK, e), lambda i: (i, 0)), _vec(e), _vec(e),
                  pl.BlockSpec(w_s.shape, lambda i: (0, 0, 0)), pl.BlockSpec(b_s_t.shape, lambda i: (0, 0))],
        out_specs=[pl.BlockSpec((3, CHUNK, e), lambda i: (0, i, 0)), _vec(e), _vec(e),
                   pl.BlockSpec(w_s.shape, lambda i: (0, 0, 0)), pl.BlockSpec(b_s_t.shape, lambda i: (0, 0))],
        out_shape=[jax.ShapeDtypeStruct((3, rows, e), MXU_DTYPE), jax.ShapeDtypeStruct((1, e), F32),
                   jax.ShapeDtypeStruct((1, e), F32), jax.ShapeDtypeStruct(w_s.shape, F32),
                   jax.ShapeDtypeStruct(b_s_t.shape, F32)],
        scratch_shapes=[pltpu.VMEM((CHUNK, e), F32)],
        compiler_params=_params(("arbitrary",), 20 * _nbytes((CHUNK, e), F32)), comm=comm)


CONV_COL_TILE = 512


def _conv_specs(rt, tc, planes):
    per = rt.tile // CONV_HALO
    last = rt.rows // CONV_HALO - 1
    if planes:
        cur = pl.BlockSpec((planes, rt.tile, tc), lambda j, i: (0, i, j))
        prev = pl.BlockSpec((planes, CONV_HALO, tc), lambda j, i: (0, jnp.maximum(i * per - 1, 0), j))
        nxt = pl.BlockSpec((planes, CONV_HALO, tc), lambda j, i: (0, jnp.minimum((i + 1) * per, last), j))
    else:
        cur = pl.BlockSpec((rt.tile, tc), lambda j, i: (i, j))
        prev = pl.BlockSpec((CONV_HALO, tc), lambda j, i: (jnp.maximum(i * per - 1, 0), j))
        nxt = pl.BlockSpec((CONV_HALO, tc), lambda j, i: (jnp.minimum((i + 1) * per, last), j))
    return cur, prev, nxt


def _halo_ok(rt, i):
    prev_ok = (i != 0) & (i != rt.nt_x)
    next_ok = (i != rt.nt_x - 1) & (i != rt.nt - 1)
    return prev_ok, next_ok


def _glu(ref):
    return ref[0] * _sigmoid(ref[1])


def _padded(cur, prev, nxt, prev_ok, next_ok):
    return jnp.concatenate([jnp.where(prev_ok, prev, 0.0), cur, jnp.where(next_ok, nxt, 0.0)], axis=0)


SUBLANES = 8
CONV_ROW_BLOCK = 16
CONV_DW_ROWS, CONV_DW_TAPS = 16, 4


def _phase_scratch(tr, tc):
    return pltpu.VMEM((SUBLANES, tr + 2 * CONV_HALO - SUBLANES, tc), F32)


def _store_phases(rot_ref, pad):
    rows = rot_ref.shape[1]
    for b in range(SUBLANES):
        rot_ref[b] = pad[b:b + rows, :]


def _tap_rows(rot_ref, r0, off, g):
    a, b = divmod(off, SUBLANES)
    return rot_ref[b, pl.ds(r0 + SUBLANES * (a + g), SUBLANES), :]


def _conv_rows(rot_ref, w_ref, r0, offs, init):
    tc = rot_ref.shape[2]
    accs = [init] * (CONV_ROW_BLOCK // SUBLANES)
    for k, off in enumerate(offs):
        wk = jnp.broadcast_to(w_ref[k:k + 1, :], (SUBLANES, tc))
        accs = [acc + wk * _tap_rows(rot_ref, r0, off, g) for g, acc in enumerate(accs)]
    return jnp.concatenate(accs, axis=0)


def _conv_fwd(z3, conv_w, conv_b, rt, name, comm=None):
    _, rows, e = z3.shape
    tc = _pick(e, (CONV_COL_TILE, 256, 128))
    tr = rt.tile

    def body(cur_ref, prev_ref, next_ref, w_ref, b_ref, o_ref, rot_ref):
        prev_ok, next_ok = _halo_ok(rt, pl.program_id(1))
        _store_phases(rot_ref, _padded(_glu(cur_ref), _glu(prev_ref), _glu(next_ref), prev_ok, next_ok))
        bias = jnp.broadcast_to(b_ref[...], (SUBLANES, tc))
        offs = [CONV_HALO - CONV_W // 2 + k for k in range(CONV_W)]

        def rows_block(rb, carry):
            r0 = pl.multiple_of(rb * CONV_ROW_BLOCK, CONV_ROW_BLOCK)
            o_ref[pl.ds(r0, CONV_ROW_BLOCK), :] = _conv_rows(rot_ref, w_ref, r0, offs, bias)
            return carry

        lax.fori_loop(0, tr // CONV_ROW_BLOCK, rows_block, 0)

    cur, prev, nxt = _conv_specs(rt, tc, 2)
    return _call(
        body, (z3, z3, z3, conv_w, conv_b), name=name, grid=(e // tc, rt.nt),
        in_specs=[cur, prev, nxt, pl.BlockSpec((CONV_W, tc), lambda j, i: (0, j)), pl.BlockSpec((1, tc), lambda j, i: (0, j))],
        out_specs=pl.BlockSpec((tr, tc), lambda j, i: (i, j)), out_shape=jax.ShapeDtypeStruct((rows, e), F32),
        scratch_shapes=[_phase_scratch(tr, tc)],
        compiler_params=_params(("parallel", "arbitrary"), 32 * _nbytes((tr, tc), F32)), comm=comm)


def _conv_bwd(z3, dy1, dg, conv_w, rt, name, comm=None):
    _, rows, e = z3.shape
    tc = _pick(e, (CONV_COL_TILE, 256, 128))
    tr = rt.tile

    def body(cur_ref, prev_ref, next_ref, dcur_ref, dprev_ref, dnext_ref, dg_ref, w_ref, dz_ref, dw_ref, db_ref, rot_ref, drot_ref):
        i = pl.program_id(1)
        prev_ok, next_ok = _halo_ok(rt, i)
        _store_phases(rot_ref, _padded(_glu(cur_ref), _glu(prev_ref), _glu(next_ref), prev_ok, next_ok))
        _store_phases(drot_ref, _padded(dcur_ref[...], dprev_ref[...], dnext_ref[...], prev_ok, next_ok))
        n_blocks = tr // CONV_ROW_BLOCK

        @pl.when(i == 0)
        def _():
            dw_ref[...] = jnp.zeros_like(dw_ref)
            db_ref[...] = jnp.zeros_like(db_ref)

        roffs = [CONV_HALO + CONV_W // 2 - k for k in range(CONV_W)]
        zero = jnp.zeros((SUBLANES, tc), F32)

        def dgate_block(rb, carry):
            r0 = pl.multiple_of(rb * CONV_ROW_BLOCK, CONV_ROW_BLOCK)
            dy0 = _conv_rows(drot_ref, w_ref, r0, roffs, zero)
            rws = pl.ds(r0, CONV_ROW_BLOCK)
            a, sb = cur_ref[0, rws, :], _sigmoid(cur_ref[1, rws, :])
            dz_ref[0, rws, :] = (dy0 * sb).astype(dz_ref.dtype)
            dz_ref[1, rws, :] = (dy0 * a * sb * (1.0 - sb)).astype(dz_ref.dtype)
            return carry

        lax.fori_loop(0, n_blocks, dgate_block, 0)
        dz_ref[2] = dg_ref[...]

        groups = CONV_DW_ROWS // SUBLANES
        for k0 in range(0, CONV_W, CONV_DW_TAPS):
            taps = list(range(k0, min(k0 + CONV_DW_TAPS, CONV_W)))

            accs = [zero] * (len(taps) * groups)
            for r0 in range(0, tr, CONV_DW_ROWS):
                dy = [dcur_ref[r0 + SUBLANES * g:r0 + SUBLANES * (g + 1), :] for g in range(groups)]
                accs = [accs[t * groups + g] + dy[g] * _tap_rows(rot_ref, r0, CONV_HALO - CONV_W // 2 + k, g)
                        for t, k in enumerate(taps) for g in range(groups)]
            for t, k in enumerate(taps):
                tot = functools.reduce(jnp.add, accs[t * groups:(t + 1) * groups])
                dw_ref[k:k + 1, :] += jnp.sum(tot, axis=0, keepdims=True)
        db_ref[...] += jnp.sum(dcur_ref[...], axis=0, keepdims=True)

    cur, prev, nxt = _conv_specs(rt, tc, 2)
    dcur, dprev, dnxt = _conv_specs(rt, tc, 0)
    return _call(
        body, (z3, z3, z3, dy1, dy1, dy1, dg, conv_w), name=name, grid=(e // tc, rt.nt),
        in_specs=[cur, prev, nxt, dcur, dprev, dnxt, pl.BlockSpec((tr, tc), lambda j, i: (i, j)),
                  pl.BlockSpec((CONV_W, tc), lambda j, i: (0, j))],
        out_specs=[pl.BlockSpec((3, tr, tc), lambda j, i: (0, i, j)), pl.BlockSpec((CONV_W, tc), lambda j, i: (0, j)),
                   pl.BlockSpec((1, tc), lambda j, i: (0, j))],
        out_shape=[jax.ShapeDtypeStruct((3, rows, e), MXU_DTYPE), jax.ShapeDtypeStruct((CONV_W, e), F32),
                   jax.ShapeDtypeStruct((1, e), F32)],
        scratch_shapes=[_phase_scratch(tr, tc), _phase_scratch(tr, tc)],
        compiler_params=_params(("parallel", "arbitrary"), 48 * _nbytes((tr, tc), F32)), comm=comm)


def _conv_mid(y1, g, ln_g, ln_b):
    return _silu(_ln(y1, ln_g, ln_b)) * _silu(g)


def _conv_mid_fwd(y1, z3, ln_g, ln_b, rt, name):
    e = y1.shape[1]
    tr = CHUNK

    def body(y_ref, g_ref, lg_ref, lb_ref, t_ref):
        t_ref[...] = _conv_mid(y_ref[...], g_ref[...], lg_ref[...], lb_ref[...]).astype(t_ref.dtype)

    return pl.pallas_call(
        body, name=name, grid=(rt.rows // tr,),
        in_specs=[pl.BlockSpec((tr, e), lambda i: (i, 0)), pl.BlockSpec((None, tr, e), lambda i: (2, i, 0)), _vec(e), _vec(e)],
        out_specs=pl.BlockSpec((tr, e), lambda i: (i, 0)), out_shape=jax.ShapeDtypeStruct((rt.rows, e), MXU_DTYPE),
        compiler_params=_params(("parallel",), 12 * _nbytes((tr, e), F32)),
    )(y1, z3, ln_g, ln_b)


def _conv_mid_bwd(y1, z3, dt, ln_g, ln_b, rt, name):
    e = y1.shape[1]
    tr = CHUNK

    def body(y_ref, g_ref, dt_ref, lg_ref, lb_ref, dy_ref, dg_ref, dlg_ref, dlb_ref):
        first = pl.program_id(0) == 0
        _, vjp = jax.vjp(_conv_mid, y_ref[...], g_ref[...], lg_ref[...], lb_ref[...])
        dy, dg, dlg, dlb = vjp(dt_ref[...])
        dy_ref[...] = dy
        dg_ref[...] = dg.astype(dg_ref.dtype)
        _acc(dlg_ref, first, dlg)
        _acc(dlb_ref, first, dlb)

    row = pl.BlockSpec((tr, e), lambda i: (i, 0))
    return pl.pallas_call(
        body, name=name, grid=(rt.rows // tr,),
        in_specs=[row, pl.BlockSpec((None, tr, e), lambda i: (2, i, 0)), row, _vec(e), _vec(e)],
        out_specs=[row, row, _vec(e), _vec(e)],
        out_shape=[jax.ShapeDtypeStruct((rt.rows, e), F32), jax.ShapeDtypeStruct((rt.rows, e), MXU_DTYPE),
                   jax.ShapeDtypeStruct((1, e), F32), jax.ShapeDtypeStruct((1, e), F32)],
        compiler_params=_params(("arbitrary",), 20 * _nbytes((tr, e), F32)),
    )(y1, z3, dt, ln_g, ln_b)


def _rms(x, g):
    return x * lax.rsqrt(jnp.mean(x * x, axis=-1, keepdims=True) + LN_EPS) * g


def _pair_swap(x):
    lane = lax.broadcasted_iota(jnp.int32, x.shape, x.ndim - 1)
    return jnp.where(lane % 2 == 0, pltpu.roll(x, x.shape[-1] - 1, x.ndim - 1), pltpu.roll(x, 1, x.ndim - 1))


def _rope(x, cos, sin):
    return x * cos + _pair_swap(x) * sin


def _rope_t(dy, cos, sin):
    return dy * cos + _pair_swap(dy * sin)


def _rope_tables(n_x, n_ctx):
    t = jnp.arange(n_x)
    row = (t // GRID_W).astype(F32)
    col = (t % GRID_W).astype(F32)
    axis_dim = HEAD_DIM // 2
    inv = 1.0 / (ROPE_THETA ** (jnp.arange(0, axis_dim, 2, dtype=F32) / axis_dim))
    ang = jnp.concatenate([row[:, None] * inv, col[:, None] * inv], axis=-1)
    cos, sin = jnp.cos(ang), jnp.sin(ang)
    cos2 = jnp.repeat(cos, 2, axis=-1)
    sin2 = jnp.stack([-sin, sin], axis=-1).reshape(n_x, HEAD_DIM)
    cos2 = jnp.concatenate([cos2, jnp.ones((n_ctx, HEAD_DIM), F32)], axis=0)
    sin2 = jnp.concatenate([sin2, jnp.zeros((n_ctx, HEAD_DIM), F32)], axis=0)
    return cos2, sin2


def _qkv_prep(z4, q_g, k_g, cos, sin, d, kvw, rt, name):
    hd = HEAD_DIM
    kb = d // kvw

    def body(q_ref, k_ref, v_ref, qg_ref, kg_ref, cos_ref, sin_ref, qo_ref, ko_ref, vo_ref):
        cos, sin = cos_ref[...], sin_ref[...]
        for h in range(d // hd):
            cols = slice(h * hd, (h + 1) * hd)
            qo_ref[:, cols] = _rope(_rms(q_ref[:, cols], qg_ref[...]), cos, sin).astype(qo_ref.dtype)
        for h in range(kvw // hd):
            cols = slice(h * hd, (h + 1) * hd)
            ko_ref[:, cols] = _rope(_rms(k_ref[:, cols], kg_ref[...]), cos, sin).astype(ko_ref.dtype)
        vo_ref[...] = v_ref[...].astype(vo_ref.dtype)

    tr = rt.tile
    return pl.pallas_call(
        body, name=name, grid=(rt.nt,),
        in_specs=[pl.BlockSpec((tr, d), lambda i: (i, 0)), pl.BlockSpec((tr, kvw), lambda i: (i, kb)),
                  pl.BlockSpec((tr, kvw), lambda i: (i, kb + 1)), _vec(hd), _vec(hd), rt.full(hd), rt.full(hd)],
        out_specs=[rt.full(d), rt.full(kvw), rt.full(kvw)],
        out_shape=[jax.ShapeDtypeStruct((rt.rows, d), MXU_DTYPE), jax.ShapeDtypeStruct((rt.rows, kvw), MXU_DTYPE),
                   jax.ShapeDtypeStruct((rt.rows, kvw), MXU_DTYPE)],
        compiler_params=_params(("parallel",), 8 * _nbytes((tr, d), F32)),
    )(z4, z4, z4, q_g, k_g, cos, sin)


ATTN_Q_TILE = 256
ATTN_HEADS_PER_PASS = 1
LOG2_E = math.log2(math.e)


def _attn_fwd(qh, kh, vh, n_x, name, comm=None):
    rows, d = qh.shape
    kvw = kh.shape[1]
    hd = HEAD_DIM
    n_kv = kvw // hd
    gqw = d // n_kv
    grp = gqw // hd
    tq = _pick(n_x, (ATTN_Q_TILE, 128))
    scale = hd ** -0.5

    def body(q_ref, k_ref, v_ref, o_ref, lse_ref):
        k, v = k_ref[...], v_ref[...]
        for g0 in range(0, grp, ATTN_HEADS_PER_PASS):
            heads = range(g0, min(g0 + ATTN_HEADS_PER_PASS, grp))
            q = jnp.concatenate([q_ref[:, g * hd:(g + 1) * hd] for g in heads], axis=0)
            s = _dot(q, k, NT)
            m = jnp.max(s, axis=-1, keepdims=True)
            p = jnp.exp2((s - m) * (scale * LOG2_E))
            l = jnp.sum(p, axis=-1, keepdims=True)
            o = _dot(p / l, v, NN)
            lse = m * scale + jnp.log(l)
            for n, g in enumerate(heads):
                o_ref[:, g * hd:(g + 1) * hd] = o[n * tq:(n + 1) * tq]
                lse_ref[:, g:g + 1] = lse[n * tq:(n + 1) * tq]

    vmem = 4 * _nbytes((rows, hd), MXU_DTYPE) + 4 * _nbytes((tq, rows), F32) + 6 * _nbytes((tq, gqw), F32)
    return _call(
        body, (qh, kh, vh), name=name, grid=(n_kv, n_x // tq),
        in_specs=[pl.BlockSpec((tq, gqw), lambda h, i: (i, h)), pl.BlockSpec((rows, hd), lambda h, i: (0, h)),
                  pl.BlockSpec((rows, hd), lambda h, i: (0, h))],
        out_specs=[pl.BlockSpec((tq, gqw), lambda h, i: (i, h)), pl.BlockSpec((None, tq, grp), lambda h, i: (h, i, 0))],
        out_shape=[jax.ShapeDtypeStruct((n_x, d), F32), jax.ShapeDtypeStruct((n_kv, n_x, grp), F32)],
        compiler_params=_params(("parallel", "arbitrary"), vmem), comm=comm)


def _attn_bwd(qh, kh, vh, do, lse, n_x, name, comm=None):
    rows, d = qh.shape
    kvw = kh.shape[1]
    hd = HEAD_DIM
    n_kv = kvw // hd
    gqw = d // n_kv
    grp = gqw // hd
    tq = _pick(n_x, (ATTN_Q_TILE, 128))
    scale = hd ** -0.5

    def body(q_ref, k_ref, v_ref, do_ref, lse_ref, dq_ref, dk_ref, dv_ref):
        @pl.when(pl.program_id(1) == 0)
        def _():
            dk_ref[...] = jnp.zeros_like(dk_ref)
            dv_ref[...] = jnp.zeros_like(dv_ref)

        k, v = k_ref[...], v_ref[...]
        for g0 in range(0, grp, ATTN_HEADS_PER_PASS):
            heads = range(g0, min(g0 + ATTN_HEADS_PER_PASS, grp))
            q = jnp.concatenate([q_ref[:, g * hd:(g + 1) * hd] for g in heads], axis=0)
            dog = jnp.concatenate([do_ref[:, g * hd:(g + 1) * hd] for g in heads], axis=0)
            lse = jnp.concatenate([lse_ref[:, g:g + 1] for g in heads], axis=0)
            p = jnp.exp2(_dot(q, k, NT) * (scale * LOG2_E) - lse * LOG2_E)
            dp = _dot(dog, v, NT)
            ds = (p * (dp - jnp.sum(dp * p, axis=-1, keepdims=True)) * scale).astype(MXU_DTYPE)
            dq = _dot(ds, k, NN)
            for n, g in enumerate(heads):
                dq_ref[:, g * hd:(g + 1) * hd] = dq[n * tq:(n + 1) * tq]
            dk_ref[...] += _dot(ds, q, TN)
            dv_ref[...] += _dot(p, dog, TN)

    vmem = 4 * _nbytes((rows, hd), MXU_DTYPE) + 4 * _nbytes((rows, hd), F32) + 6 * _nbytes((tq, rows), F32) + 8 * _nbytes((tq, gqw), F32)
    qspec = pl.BlockSpec((tq, gqw), lambda h, i: (i, h))
    kspec = pl.BlockSpec((rows, hd), lambda h, i: (0, h))
    return _call(
        body, (qh, kh, vh, do, lse), name=name, grid=(n_kv, n_x // tq),
        in_specs=[qspec, kspec, kspec, qspec, pl.BlockSpec((None, tq, grp), lambda h, i: (h, i, 0))],
        out_specs=[qspec, kspec, kspec],
        out_shape=[jax.ShapeDtypeStruct((n_x, d), F32), jax.ShapeDtypeStruct((rows, kvw), F32),
                   jax.ShapeDtypeStruct((rows, kvw), F32)],
        compiler_params=_params(("parallel", "arbitrary"), vmem), comm=comm)


def _attn_gate(o, z4, d, kvw, rt, name):
    g0 = (d + 2 * kvw) // kvw
    tr = rt.tile

    def body(o_ref, g_ref, t_ref):
        t_ref[...] = (o_ref[...] * _silu(g_ref[...])).astype(t_ref.dtype)

    tile = pl.BlockSpec((tr, kvw), lambda i, j: (i, j))
    return pl.pallas_call(
        body, name=name, grid=(rt.nt, d // kvw),
        in_specs=[tile, pl.BlockSpec((tr, kvw), lambda i, j: (i, g0 + j))],
        out_specs=tile, out_shape=jax.ShapeDtypeStruct((rt.rows, d), MXU_DTYPE),
        compiler_params=_params(("parallel", "parallel"), 8 * _nbytes((tr, kvw), F32)),
    )(o, z4)


def _attn_gate_bwd(dt, o, z4, d, kvw, rt, name):
    g0 = (d + 2 * kvw) // kvw
    tr = rt.tile

    def body(dt_ref, o_ref, g_ref, do_ref, dg_ref):
        dt_v, g = dt_ref[...], g_ref[...]
        do_ref[...] = (dt_v * _silu(g)).astype(do_ref.dtype)
        dg_ref[...] = (dt_v * o_ref[...] * _dsilu(g)).astype(dg_ref.dtype)

    tile = pl.BlockSpec((tr, kvw), lambda i, j: (i, j))
    return pl.pallas_call(
        body, name=name, grid=(rt.nt, d // kvw),
        in_specs=[tile, tile, pl.BlockSpec((tr, kvw), lambda i, j: (i, g0 + j))],
        out_specs=[tile, tile],
        out_shape=[jax.ShapeDtypeStruct((rt.rows, d), MXU_DTYPE), jax.ShapeDtypeStruct((rt.rows, d), MXU_DTYPE)],
        compiler_params=_params(("parallel", "parallel"), 12 * _nbytes((tr, kvw), F32)),
    )(dt, o, z4)


def _prep_bwd(dxh, z4, col_block, gain, cos, sin, rt, name):
    w = dxh.shape[1]
    hd = HEAD_DIM

    def body(dxh_ref, x_ref, g_ref, cos_ref, sin_ref, dx_ref, dg_ref):
        cos, sin = cos_ref[...], sin_ref[...]
        dg = jnp.zeros((1, hd), F32)
        for h in range(w // hd):
            cols = slice(h * hd, (h + 1) * hd)
            _, vjp = jax.vjp(_rms, x_ref[:, cols], g_ref[...])
            dx, dgh = vjp(_rope_t(dxh_ref[:, cols], cos, sin))
            dx_ref[:, cols] = dx.astype(dx_ref.dtype)
            dg = dg + dgh
        _acc(dg_ref, pl.program_id(0) == 0, dg)

    tr = rt.tile
    return pl.pallas_call(
        body, name=name, grid=(rt.nt,),
        in_specs=[rt.full(w), pl.BlockSpec((tr, w), lambda i: (i, col_block)), _vec(hd), rt.full(hd), rt.full(hd)],
        out_specs=[rt.full(w), _vec(hd)],
        out_shape=[jax.ShapeDtypeStruct((rt.rows, w), MXU_DTYPE), jax.ShapeDtypeStruct((1, hd), F32)],
        compiler_params=_params(("arbitrary",), 12 * _nbytes((tr, w), F32)),
    )(dxh, z4, gain, cos, sin)


def _loss_head(x, target, rt, name):
    d = x.shape[1]

    def body(x_ref, t_ref, dx_ref, l_ref):
        err = x_ref[...] - t_ref[...]
        dx_ref[...] = err / d
        row = jnp.mean(err * err, axis=-1, keepdims=True)
        _acc(l_ref, pl.program_id(0) == 0, jnp.sum(row, axis=0, keepdims=True))

    return pl.pallas_call(
        body, name=name, grid=(rt.nt,),
        in_specs=[rt.full(d), rt.full(d)],
        out_specs=[rt.full(d), pl.BlockSpec((1, 1), lambda i: (0, 0))],
        out_shape=[jax.ShapeDtypeStruct(x.shape, F32), jax.ShapeDtypeStruct((1, 1), F32)],
        compiler_params=_params(("arbitrary",), 8 * _nbytes((rt.tile, d), F32)),
    )(x, target)


def _adamw(w, g, m, v):
    m = ADAM_B1 * m + (1.0 - ADAM_B1) * g
    v = ADAM_B2 * v + (1.0 - ADAM_B2) * (g * g)
    m_hat = m / (1.0 - ADAM_B1 ** ADAM_STEP)
    v_hat = v / (1.0 - ADAM_B2 ** ADAM_STEP)
    delta = -ADAM_LR * (m_hat / (jnp.sqrt(v_hat) + ADAM_EPS) + ADAM_WD * w)
    return delta, m, v


ADAM_TILE_BYTES = 1 << 20


def _adam_tile(rows, cols):
    tr = rows
    while tr % 16 == 0 and tr * cols * 4 > ADAM_TILE_BYTES:
        tr //= 2
    return tr


def _adam_reduce(parts, w, m, v, name, comm=None):
    slots, rows, cols = w.shape
    tr = _adam_tile(rows, cols)
    nt = rows // tr

    def body(*refs):
        p_refs = refs[:slots]
        w_ref, m_ref, v_ref, g_ref, d_ref, mo_ref, vo_ref = refs[slots:]
        for k in range(slots):
            @pl.when(pl.program_id(0) == k)
            def _(p_ref=p_refs[k]):
                g = p_ref[0].astype(F32)
                for part in range(1, p_ref.shape[0]):
                    g = g + p_ref[part].astype(F32)
                g_ref[...] = g
                d_ref[...], mo_ref[...], vo_ref[...] = _adamw(w_ref[...], g, m_ref[...], v_ref[...])

    def part_spec(k):
        return pl.BlockSpec((parts[k].shape[0], tr, cols),
                            lambda s, i: (0, jnp.where(s < k, 0, jnp.where(s == k, i, nt - 1)), 0))

    row = pl.BlockSpec((None, tr, cols), lambda s, i: (s, i, 0))
    sds = jax.ShapeDtypeStruct((slots, rows, cols), F32)
    return _call(
        body, (*parts, w, m, v), name=name, grid=(slots, nt),
        in_specs=[part_spec(k) for k in range(slots)] + [row, row, row],
        out_specs=[row] * 4, out_shape=[sds] * 4,
        compiler_params=_params(("arbitrary", "arbitrary"), 40 * _nbytes((tr, cols), F32)), comm=comm)


def _adam_plain(g, w, m, v, name):
    rows, cols = w.shape
    tr = _adam_tile(rows, cols)

    def body(g_ref, w_ref, m_ref, v_ref, d_ref, mo_ref, vo_ref):
        d_ref[...], mo_ref[...], vo_ref[...] = _adamw(w_ref[...], g_ref[...], m_ref[...], v_ref[...])

    row = pl.BlockSpec((tr, cols), lambda i: (i, 0))
    sds = jax.ShapeDtypeStruct((rows, cols), F32)
    return pl.pallas_call(
        body, name=name, grid=(rows // tr,),
        in_specs=[row] * 4, out_specs=[row] * 3, out_shape=[sds] * 3,
        compiler_params=_params(("parallel",), 32 * _nbytes((tr, cols), F32)),
    )(g, w, m, v)


def _sum_devices(parts, name):
    _, rows, cols = parts.shape
    tr = _adam_tile(rows, cols)

    def body(p_ref, o_ref):
        g = p_ref[0]
        for k in range(1, N_DEV):
            g = g + p_ref[k]
        o_ref[...] = g

    return pl.pallas_call(
        body, name=name, grid=(rows // tr,),
        in_specs=[pl.BlockSpec((N_DEV, tr, cols), lambda i: (0, i, 0))],
        out_specs=pl.BlockSpec((tr, cols), lambda i: (i, 0)), out_shape=jax.ShapeDtypeStruct((rows, cols), F32),
        compiler_params=_params(("parallel",), 24 * _nbytes((tr, cols), F32)),
    )(parts)


COND_ROWS = 16


def _mod_fwd_mm(cond, mod_w, mod_b, name):
    layers, d, w = mod_w.shape

    def body(c_ref, w_ref, b_ref, o_ref):
        o_ref[...] = _dot(_silu(c_ref[...]), w_ref[...], NN) + b_ref[...]

    return pl.pallas_call(
        body, name=name, grid=(layers,),
        in_specs=[pl.BlockSpec((COND_ROWS, d), lambda l: (0, 0)), pl.BlockSpec((None, d, w), lambda l: (l, 0, 0)),
                  pl.BlockSpec((None, 1, w), lambda l: (l, 0, 0))],
        out_specs=pl.BlockSpec((None, COND_ROWS, w), lambda l: (l, 0, 0)),
        out_shape=jax.ShapeDtypeStruct((layers, COND_ROWS, w), F32),
        compiler_params=_params(("parallel",), 4 * _nbytes((d, w), F32)),
    )(cond, mod_w, mod_b)


def _mod_bwd_mm(cond, dm, mod_w, name):
    layers, d, w = mod_w.shape

    def body(c_ref, dm_ref, w_ref, dw_ref, dc_ref):
        dmv = dm_ref[...]
        dw_ref[...] = _dot(_silu(c_ref[...]), dmv, TN)
        _acc(dc_ref, pl.program_id(0) == 0, _dot(dmv, w_ref[...], NT))

    return pl.pallas_call(
        body, name=name, grid=(layers,),
        in_specs=[pl.BlockSpec((COND_ROWS, d), lambda l: (0, 0)), pl.BlockSpec((None, COND_ROWS, w), lambda l: (l, 0, 0)),
                  pl.BlockSpec((None, d, w), lambda l: (l, 0, 0))],
        out_specs=[pl.BlockSpec((None, d, w), lambda l: (l, 0, 0)), pl.BlockSpec((COND_ROWS, d), lambda l: (0, 0))],
        out_shape=[jax.ShapeDtypeStruct((layers, d, w), F32), jax.ShapeDtypeStruct((COND_ROWS, d), F32)],
        compiler_params=_params(("arbitrary",), 6 * _nbytes((d, w), F32)),
    )(cond, dm, mod_w)


PACK_ROWS = 256


def _pack(arrs):
    flat = jnp.concatenate([a.reshape(-1).astype(F32) for a in arrs])
    pad = (-flat.shape[0]) % (PACK_ROWS * LANES)
    return jnp.pad(flat, (0, pad)).reshape(-1, LANES)


def _unpack(flat2d, shapes):
    flat = flat2d.reshape(-1)
    out, off = [], 0
    for s in shapes:
        n = math.prod(s)
        out.append(flat[off:off + n].reshape(s))
        off += n
    return out


def _unpack_dev(g2d, shapes):
    flat = g2d.reshape(N_DEV, -1)
    out, off = [], 0
    for s in shapes:
        n = math.prod(s)
        out.append(flat[:, off:off + n].reshape((N_DEV, *s)))
        off += n
    return out


def kernel(x, c, ctx, c_ctx, mod_w, mod_b, post_g, post_b, a_w_in, a_ln_g, a_ln_b, a_w_s, a_b_s, a_w_out, b_w_in, b_conv_w, b_conv_b, b_ln_g, b_ln_b, b_w_out, c_w_in, c_q_g, c_k_g, c_w_out, loss_target, m_c_ctx, m_mod_w, m_mod_b, m_post_g, m_post_b, m_a_w_in, m_a_ln_g, m_a_ln_b, m_a_w_s, m_a_b_s, m_a_w_out, m_b_w_in, m_b_conv_w, m_b_conv_b, m_b_ln_g, m_b_ln_b, m_b_w_out, m_c_w_in, m_c_q_g, m_c_k_g, m_c_w_out, v_c_ctx, v_mod_w, v_mod_b, v_post_g, v_post_b, v_a_w_in, v_a_ln_g, v_a_ln_b, v_a_w_s, v_a_b_s, v_a_w_out, v_b_w_in, v_b_conv_w, v_b_conv_b, v_b_ln_g, v_b_ln_b, v_b_w_out, v_c_w_in, v_c_q_g, v_c_k_g, v_c_w_out):
    n_x, d = x.shape[1], x.shape[2]
    n_ctx = ctx.shape[1]
    e = a_w_out.shape[1] * N_DEV
    kvw = N_KV_HEADS * HEAD_DIM
    me = _dev_index(_mesh_pos())
    rt_all = _Rows(n_x, n_ctx)
    rt_x = _Rows(n_x, 0)

    small_in = [c[0], a_ln_g, a_ln_b, b_conv_w[0]]
    (g_small,) = _all_gather([_pack(small_in)], "ag_small_params")
    conds, ln_g_all, ln_b_all, conv_w_all = _unpack_dev(g_small, [a.shape for a in small_in])
    a_ln_g_f = jnp.moveaxis(ln_g_all, 0, 1).reshape(a_ln_g.shape[0], 1, e)
    a_ln_b_f = jnp.moveaxis(ln_b_all, 0, 1).reshape(a_ln_b.shape[0], 1, e)
    conv_w_f = jnp.moveaxis(conv_w_all, 0, 1).reshape(CONV_W, e)
    cond = jnp.zeros((COND_ROWS, d), F32).at[:N_DEV].set(conds).at[N_DEV].set(c_ctx)

    wm = mod_w.shape[2]
    mod_b_mine = lax.dynamic_slice_in_dim(mod_b, me * wm, wm, axis=1).reshape(DEPTH, 1, wm)
    (mods_g,) = _all_gather([_mod_fwd_mm(cond, mod_w, mod_b_mine, "mod_fwd")], "ag_mod")
    mods = jnp.moveaxis(mods_g, 0, 2).reshape(DEPTH, COND_ROWS, 3 * d)
    mine = lax.dynamic_index_in_dim(mods, me, axis=1, keepdims=False)
    modv = jnp.stack([mine, mods[:, N_DEV]], axis=1).reshape(DEPTH * 2 * 3, 1, d)

    def gather_of(wt):
        return _gather_comm([wt.astype(MXU_DTYPE)])

    def exchange_of(*gs):
        return _exchange_comm([g if g.ndim == 3 else g.reshape(N_DEV, -1, g.shape[-1]) for g in gs])

    def chip_sums_comm(g, theirs, tag):
        mine = lax.dynamic_index_in_dim(g.reshape(N_CHIPS, 2, *g.shape[1:]), lax.axis_index("c"), axis=1, keepdims=False)
        return _chip_exchange_comm([_pair_add(mine, theirs, f"pair_add_{tag}")])

    def chip_sums_of(g, tag):
        (theirs,) = _comm_call(_pair_exchange_comm([g]), f"pair_{tag}")
        return chip_sums_comm(g, theirs, tag)

    (wa_in0,) = _all_gather([a_w_in[0].astype(MXU_DTYPE)], "ag_w_l0")

    ws_op = a_w_s.astype(MXU_DTYPE)
    bs_t = jnp.swapaxes(a_b_s, 1, 2)
    pg = post_g.reshape(DEPTH, 1, d)
    pb = post_b.reshape(DEPTH, 1, d)

    xs0 = jnp.concatenate([x[0], ctx[0]], axis=0)
    h0 = _modulate(xs0, modv, 0, rt_all, "mod0")
    z0, (wb_in,) = _mm_nn(h0, wa_in0, planes=3, name="l0_in", comm=gather_of(b_w_in[0]))
    t0, (wa_out0,) = _cm_mid_fwd(z0, a_ln_g_f[0], a_ln_b_f[0], ws_op[0], bs_t[0], "l0_mid", comm=gather_of(a_w_out[0]))
    wa_out0 = wa_out0.reshape(-1, d)
    y0, (wb_out,) = _mm_nn(t0, wa_out0, name="l0_out", comm=gather_of(b_w_out[0]))
    xs1, h1 = _post_fwd(xs0, y0, modv, 0, pg[0], pb[0], rt_all, "l0_post")
    z1, (wc_in,) = _mm_nn(h1, wb_in, planes=3, name="l1_in", comm=gather_of(c_w_in[0]))
    cy1, (wa_in1,) = _conv_fwd(z1, conv_w_f, b_conv_b, rt_all, "l1_conv", comm=gather_of(a_w_in[1]))
    t1 = _conv_mid_fwd(cy1, z1, b_ln_g, b_ln_b, rt_all, "l1_mid")
    y1, (wc_out,) = _mm_nn(t1, wb_out.reshape(-1, d), name="l1_out", comm=gather_of(c_w_out[0]))
    xs2, h2 = _post_fwd(xs1, y1, modv, 1, pg[1], pb[1], rt_all, "l1_post")
    cos, sin = _rope_tables(n_x, n_ctx)
    z2 =_mm_nn(h2, wc_in, name="l2_in")
    qh, kh, vh = _qkv_prep(z2, c_q_g, c_k_g, cos, sin, d, kvw, rt_all, "l2_prep")
    (o2, lse), (wa_out1,) = _attn_fwd(qh, kh, vh, n_x, "l2_attn", comm=gather_of(a_w_out[1]))
    wb_out, wc_out, wa_out1 = [wt.reshape(-1, d) for wt in (wb_out, wc_out, wa_out1)]
    t2 = _attn_gate(o2, z2, d, kvw, rt_x, "l2_gate")
    y2 = _mm_nn(t2, wc_out, name="l2_out")
    x2 = xs2
    x3, h3 = _post_fwd(x2, y2, modv, 2, pg[2], pb[2], rt_x, "l2_post")
    z3 =_mm_nn(h3, wa_in1, planes=3, name="l3_in")
    t3 = _cm_mid_fwd(z3, a_ln_g_f[1], a_ln_b_f[1], ws_op[1], bs_t[1], "l3_mid")
    y3 = _mm_nn(t3, wa_out1, name="l3_out")
    x4 = _post_fwd(x3, y3, modv, 3, pg[3], pb[3], rt_x, "l3_post", modulate_next=False)

    dx4, loss_sum = _loss_head(x4, loss_target[0], rt_x, "loss")
    loss = lax.psum(0.5 * loss_sum[0, 0], ("x", "y", "c"))

    gdt = MXU_DTYPE
    dres3, dy3, dpg3, dpb3, dgate3 = _post_bwd(x3, y3, dx4, modv, 3, pg[3], pb[3], rt_x, "l3_post_b")
    dt3 = _mm_nt(dy3, wa_out1, name="l3_dt")
    gw_a_out1 = _mm_tn(t3, dy3, blocked=False, out_dtype=gdt, name="l3_dwout")
    dz3, dlg3, dlb3, dws3, dbs3 = _cm_mid_bwd(z3, dt3, a_ln_g_f[1], a_ln_b_f[1], ws_op[1], bs_t[1], "l3_mid_b")
    gw_a_in1 = _mm_tn(h3, dz3, blocked=True, out_dtype=gdt, name="l3_dwin")
    dh3, (r_a_out1,) = _mm_nt_blocked(dz3, wa_in1, name="l3_dh", comm=exchange_of(gw_a_out1))
    dx3, dshift3, dscale3 = _mod_bwd(dres3, dh3, x3, modv, 3, rt_x, "l3_mod_b")
    dres2, dy2, dpg2, dpb2, dgate2 = _post_bwd(x2, y2, dx3, modv, 2, pg[2], pb[2], rt_x, "l2_post_b")
    dt2 = _mm_nt(dy2, wc_out, name="l2_dt")
    gw_c_out = _mm_tn(t2, dy2, blocked=False, out_dtype=gdt, name="l2_dwout")
    do2, dg2 = _attn_gate_bwd(dt2, o2, z2, d, kvw, rt_x, "l2_gate_b")
    (dqh, dkh, dvh), (r_a_in1,) = _attn_bwd(qh, kh, vh, do2, lse, n_x, "l2_attn_b", comm=exchange_of(gw_a_in1))
    dq2, dqg = _prep_bwd(dqh, z2, 0, c_q_g, cos, sin, rt_x, "l2_qprep_b")
    dk2, dkg = _prep_bwd(dkh, z2, d // kvw, c_k_g, cos, sin, rt_all, "l2_kprep_b")
    zpad = jnp.zeros((n_ctx, d), MXU_DTYPE)
    dz2 = jnp.concatenate([jnp.concatenate([dq2, zpad], axis=0), dk2, dvh.astype(MXU_DTYPE),
                           jnp.concatenate([dg2, zpad], axis=0)], axis=1)
    gw_c_in = _mm_tn(h2, dz2, blocked=True, out_dtype=gdt, name="l2_dwin")
    dh2, (r_c_out,) = _mm_nt_blocked(dz2, wc_in, name="l2_dh", comm=exchange_of(gw_c_out))
    dxs2, dshift2, dscale2 = _mod_bwd(dres2, dh2, xs2, modv, 2, rt_all, "l2_mod_b")
    dres1, dy1, dpg1, dpb1, dgate1 = _post_bwd(xs1, y1, dxs2, modv, 1, pg[1], pb[1], rt_all, "l1_post_b")
    dt1 = _mm_nt(dy1, wb_out, name="l1_dt")
    gw_b_out = _mm_tn(t1, dy1, blocked=False, out_dtype=gdt, name="l1_dwout")
    dcy1, dgc1, dblg, dblb = _conv_mid_bwd(cy1, z1, dt1, b_ln_g, b_ln_b, rt_all, "l1_mid_b")
    (dz1, dconv_w, dconv_b), (r_c_in, r_b_out) = _conv_bwd(z1, dcy1, dgc1, conv_w_f, rt_all, "l1_conv_b",
                                                           comm=exchange_of(gw_c_in, gw_b_out))
    gw_b_in = _mm_tn(h1, dz1, blocked=True, out_dtype=gdt, name="l1_dwin")
    dh1, (pair_b_in,) = _mm_nt_blocked(dz1, wb_in, name="l1_dh", comm=_pair_exchange_comm([gw_b_in]))
    dxs1, dshift1, dscale1 = _mod_bwd(dres1, dh1, xs1, modv, 1, rt_all, "l1_mod_b")
    dres0, dy0, dpg0, dpb0, dgate0 = _post_bwd(xs0, y0, dxs1, modv, 0, pg[0], pb[0], rt_all, "l0_post_b")
    dt0 = _mm_nt(dy0, wa_out0, name="l0_dt")
    gw_a_out0 = _mm_tn(t0, dy0, blocked=False, out_dtype=gdt, name="l0_dwout")
    (dz0, dlg0, dlb0, dws0, dbs0), (r_a_out0,) = _cm_mid_bwd(z0, dt0, a_ln_g_f[0], a_ln_b_f[0], ws_op[0], bs_t[0], "l0_mid_b",
                                                             comm=exchange_of(gw_a_out0))
    gw_a_in0, (r_b_in,) = _mm_tn(h0, dz0, blocked=True, out_dtype=gdt, name="l0_dwin",
                                 comm=chip_sums_comm(gw_b_in, pair_b_in, "b_in"))
    dh0, (r_a_in0,) = _mm_nt_blocked(dz0, wa_in0, name="l0_dh", comm=chip_sums_of(gw_a_in0, "a_in0"))
    dx0, dshift0, dscale0 = _mod_bwd(dres0, dh0, xs0, modv, 0, rt_all, "l0_mod_b", dx_rows=n_x)
    grad_x = dx0[None]

    def seg2(a):
        a = a[:, 0]
        return a if a.shape[0] == 2 else jnp.concatenate([a, jnp.zeros_like(a)], axis=0)

    gate2 = jnp.concatenate([dgate2[:, 0], jnp.zeros((1, d), F32)], axis=0)
    dmod = jnp.stack([
        jnp.concatenate([seg2(dshift0), seg2(dscale0), seg2(dgate0)], axis=1),
        jnp.concatenate([seg2(dshift1), seg2(dscale1), seg2(dgate1)], axis=1),
        jnp.concatenate([seg2(dshift2), seg2(dscale2), gate2], axis=1),
        jnp.concatenate([seg2(dshift3), seg2(dscale3), seg2(dgate3)], axis=1)])

    g_post_g = jnp.concatenate([dpg0, dpg1, dpg2, dpg3], axis=0)
    g_post_b = jnp.concatenate([dpb0, dpb1, dpb2, dpb3], axis=0)
    g_a_ln_g = jnp.concatenate([dlg0, dlg3], axis=0)
    g_a_ln_b = jnp.concatenate([dlb0, dlb3], axis=0)
    g_a_w_s = jnp.stack([dws0, dws3])
    g_a_b_s = jnp.swapaxes(jnp.stack([dbs0, dbs3]), 1, 2)
    small_g = [g_post_g, g_post_b, g_a_w_s, g_a_b_s, dconv_b, dblg, dblb, dqg, dkg, g_a_ln_g, g_a_ln_b, dconv_w,
               dmod[:, 0], dmod[:, 1]]
    small_shapes = [a.shape for a in small_g]
    (gs_all,) = _all_gather([_pack(small_g)], "ag_small_grads")
    o_a_w_in = _adam_reduce([r_a_in0, r_a_in1], a_w_in, m_a_w_in, v_a_w_in, "adam_a_in")
    sums = _unpack(_sum_devices(gs_all, "sum_small"), small_shapes)
    (s_post_g, s_post_b, s_a_w_s, s_a_b_s, s_conv_b, s_b_ln_g, s_b_ln_b, s_q_g, s_k_g, s_a_ln_g, s_a_ln_b, s_conv_w,
     s_dmod_own, s_dmod_ctx) = sums
    grad_mod_b = s_dmod_own + s_dmod_ctx
    wl = a_ln_g.shape[1]
    wcv = b_conv_w.shape[2]
    grad_a_ln_g = lax.dynamic_slice_in_dim(s_a_ln_g, me * wl, wl, axis=1)
    grad_a_ln_b = lax.dynamic_slice_in_dim(s_a_ln_b, me * wl, wl, axis=1)
    grad_b_conv_w = lax.dynamic_slice_in_dim(s_conv_w, me * wcv, wcv, axis=1)[None]

    dmod_dev = _unpack_dev(gs_all, small_shapes)[12]
    dm_rows = jnp.concatenate([jnp.moveaxis(dmod_dev, 0, 1), s_dmod_ctx[:, None],
                               jnp.zeros((DEPTH, COND_ROWS - N_DEV - 1, 3 * d), F32)], axis=1)
    dm_mine = lax.dynamic_slice_in_dim(dm_rows, me * wm, wm, axis=2)
    grad_mod_w, dcond_part = _mod_bwd_mm(cond, dm_mine, mod_w, "mod_bwd")
    (dcond_all,) = _all_gather([dcond_part], "ag_dcond")
    dcond = _sum_devices(dcond_all, "sum_dcond")
    grad_c_ctx = dcond[N_DEV] * _dsilu(c_ctx)

    o_a_w_out = _adam_reduce([r_a_out0, r_a_out1], a_w_out, m_a_w_out, v_a_w_out, "adam_a_out")
    o_b_w_in = _adam_reduce([r_b_in], b_w_in, m_b_w_in, v_b_w_in, "adam_b_in")
    o_b_w_out = _adam_reduce([r_b_out], b_w_out, m_b_w_out, v_b_w_out, "adam_b_out")
    o_c_w_in = _adam_reduce([r_c_in], c_w_in, m_c_w_in, v_c_w_in, "adam_c_in")
    o_c_w_out = _adam_reduce([r_c_out], c_w_out, m_c_w_out, v_c_w_out, "adam_c_out")
    mw_shape = mod_w.shape
    o_mod_w = [grad_mod_w] + [a.reshape(mw_shape) for a in _adam_plain(
        grad_mod_w.reshape(-1, wm), mod_w.reshape(-1, wm), m_mod_w.reshape(-1, wm), v_mod_w.reshape(-1, wm), "adam_mod_w")]

    sg = [grad_c_ctx, grad_mod_b, s_post_g, s_post_b, grad_a_ln_g, grad_a_ln_b, s_a_w_s, s_a_b_s, grad_b_conv_w, s_conv_b,
          s_b_ln_g, s_b_ln_b, s_q_g, s_k_g]
    sw = [c_ctx, mod_b, post_g, post_b, a_ln_g, a_ln_b, a_w_s, a_b_s, b_conv_w, b_conv_b, b_ln_g, b_ln_b, c_q_g, c_k_g]
    sm = [m_c_ctx, m_mod_b, m_post_g, m_post_b, m_a_ln_g, m_a_ln_b, m_a_w_s, m_a_b_s, m_b_conv_w, m_b_conv_b, m_b_ln_g,
          m_b_ln_b, m_c_q_g, m_c_k_g]
    sv = [v_c_ctx, v_mod_b, v_post_g, v_post_b, v_a_ln_g, v_a_ln_b, v_a_w_s, v_a_b_s, v_b_conv_w, v_b_conv_b, v_b_ln_g,
          v_b_ln_b, v_c_q_g, v_c_k_g]
    shapes = [a.shape for a in sw]
    sg = [g.reshape(s) for g, s in zip(sg, shapes)]
    sd, snm, snv = [_unpack(a, shapes) for a in _adam_plain(_pack(sg), _pack(sw), _pack(sm), _pack(sv), "adam_small")]

    def small(k):
        return [sg[k], sd[k], snm[k], snv[k]]

    per_weight = [small(0), o_mod_w, small(1), small(2), small(3), o_a_w_in, small(4), small(5), small(6), small(7),
                  o_a_w_out, o_b_w_in, small(8), small(9), small(10), small(11), o_b_w_out, o_c_w_in, small(12), small(13),
                  o_c_w_out]
    outs = [loss, grad_x]
    for kind in range(4):
        outs += [pw[kind] for pw in per_weight]
    return tuple(outs)
```

```python
import functools
import math

import jax
import jax.numpy as jnp
from jax import lax
from jax.experimental import pallas as pl
from jax.experimental.pallas import tpu as pltpu

F32 = jnp.float32
BF16 = jnp.bfloat16
MXU_DTYPE = jnp.bfloat16

DEPTH = 4
GRID_W = 64
CHUNK = 128
SGU_GROUPS = 16
CONV_W = 31
CONV_HALO = 16
HEAD_DIM = 128
N_KV_HEADS = 4
ROPE_THETA = 10000.0
DEEPNORM_ALPHA = (2 * DEPTH) ** 0.25
LN_EPS = 1e-6
ADAM_LR, ADAM_B1, ADAM_B2, ADAM_EPS, ADAM_WD, ADAM_STEP = 0.001, 0.9, 0.999, 1e-08, 0.01, 10

N_DEV = 8
V7X_VMEM_BYTES = 64 * 1024 * 1024
V7X_VMEM_CLAIM = V7X_VMEM_BYTES * 7 // 8
LANES = 128

NN = ((1,), (0,))
NT = ((1,), (1,))
TN = ((0,), (0,))


def _dot(a, b, dims):
    return lax.dot_general(a.astype(MXU_DTYPE), b.astype(MXU_DTYPE), (dims, ((), ())), preferred_element_type=F32)


def _pick(n, prefs):
    for p in prefs:
        if n % p == 0:
            return p
    raise ValueError(f"no tile for {n} among {prefs}")


def _params(sem, vmem_bytes):
    assert vmem_bytes <= V7X_VMEM_CLAIM, (vmem_bytes, V7X_VMEM_CLAIM)
    return pltpu.CompilerParams(dimension_semantics=sem, vmem_limit_bytes=V7X_VMEM_CLAIM)


def _nbytes(shape, dtype):
    return math.prod(shape) * jnp.dtype(dtype).itemsize


def _sigmoid(x):
    return jax.nn.sigmoid(x)


def _silu(x):
    return x * jax.nn.sigmoid(x)


def _dsilu(x):
    s = jax.nn.sigmoid(x)
    return s * (1.0 + x * (1.0 - s))


def _ln_stats(x):
    mu = jnp.mean(x, axis=-1, keepdims=True)
    xc = x - mu
    var = jnp.mean(xc * xc, axis=-1, keepdims=True)
    return xc, lax.rsqrt(var + LN_EPS)


def _ln(x, g, b):
    xc, rstd = _ln_stats(x)
    return xc * rstd * g + b


def _mesh_pos():
    return lax.axis_index("x"), lax.axis_index("y"), lax.axis_index("c")


def _dev_index(p):
    return 4 * p[0] + 2 * p[1] + p[2]


class _Comm:
    def __init__(self, inputs, out_shapes, sems, start, finish):
        self.inputs, self.out_shapes, self.sems, self.start, self.finish = inputs, out_shapes, sems, start, finish


def _gather_comm(xs):
    n = len(xs)

    def place():
        x, y, c = _mesh_pos()
        return (x, y, c), (x, y, 1 - c), [(1 - x, y), (x, 1 - y), (1 - x, 1 - y)], c

    def copier(x_refs, o_refs, sems):
        send_sems, recv_sems, _ = sems

        def copy(t, k, block, to, from_input=False):
            dst = o_refs[t].at[_dev_index(block)]
            return pltpu.make_async_remote_copy(
                src_ref=x_refs[t] if from_input else dst, dst_ref=dst,
                send_sem=send_sems.at[t, k], recv_sem=recv_sems.at[t, k],
                device_id=to, device_id_type=pl.DeviceIdType.MESH)

        return copy

    def own(x_refs, o_refs, sems, t, me):
        return pltpu.make_async_copy(x_refs[t], o_refs[t].at[_dev_index(me)], sems[2].at[t])

    def first_copies(copy, t, me, sibling, chips, c):
        return [copy(t, 0, me, sibling, True)] + [copy(t, 1 + j, me, (*chip, c), True) for j, chip in enumerate(chips)]

    def start(x_refs, o_refs, sems):
        me, sibling, chips, c = place()
        copy = copier(x_refs, o_refs, sems)
        for t in range(n):
            own(x_refs, o_refs, sems, t, me).start()
            for cp in first_copies(copy, t, me, sibling, chips, c):
                cp.start()

    def finish(x_refs, o_refs, sems):
        me, sibling, chips, c = place()
        copy = copier(x_refs, o_refs, sems)
        passed = []
        for t in range(n):
            for j, chip in enumerate(chips):
                copy(t, 1 + j, (*chip, c), me).wait_recv()
                cp = copy(t, 4 + j, (*chip, c), sibling)
                cp.start()
                passed.append(cp)
        for t in range(n):
            copy(t, 0, sibling, me).wait_recv()
            for j, chip in enumerate(chips):
                copy(t, 4 + j, (*chip, 1 - c), me).wait_recv()
        for t in range(n):
            for cp in first_copies(copy, t, me, sibling, chips, c):
                cp.wait_send()
        for cp in passed:
            cp.wait_send()
        for t in range(n):
            own(x_refs, o_refs, sems, t, me).wait()

    sems = [pltpu.SemaphoreType.DMA((n, 7)), pltpu.SemaphoreType.DMA((n, 7)), pltpu.SemaphoreType.DMA((n,))]
    return _Comm(list(xs), [jax.ShapeDtypeStruct((N_DEV, *a.shape), a.dtype) for a in xs], sems, start, finish)


def _exchange_comm(gs):
    n = len(gs)

    def copies(g_refs, r_refs, sems):
        send_sems, recv_sems, local_sems = sems
        x, y, c = _mesh_pos()
        me = _dev_index((x, y, c))
        out = []
        for t in range(n):
            out.append(pltpu.make_async_copy(g_refs[t].at[me], r_refs[t].at[me], local_sems.at[t]))
            for k in range(1, N_DEV):
                fx, fy, fc = (k >> 2) & 1, (k >> 1) & 1, k & 1
                peer = (1 - x if fx else x, 1 - y if fy else y, 1 - c if fc else c)
                out.append(pltpu.make_async_remote_copy(
                    src_ref=g_refs[t].at[_dev_index(peer)], dst_ref=r_refs[t].at[me],
                    send_sem=send_sems.at[t, k - 1], recv_sem=recv_sems.at[t, k - 1],
                    device_id=peer, device_id_type=pl.DeviceIdType.MESH))
        return out

    def start(g_refs, r_refs, sems):
        for cp in copies(g_refs, r_refs, sems):
            cp.start()

    def finish(g_refs, r_refs, sems):
        for cp in copies(g_refs, r_refs, sems):
            cp.wait()

    sems = [pltpu.SemaphoreType.DMA((n, 7)), pltpu.SemaphoreType.DMA((n, 7)), pltpu.SemaphoreType.DMA((n,))]
    return _Comm(list(gs), [jax.ShapeDtypeStruct(g.shape, g.dtype) for g in gs], sems, start, finish)


N_CHIPS = N_DEV // 2


def _started_and_waited(copies):
    def start(in_refs, out_refs, sems):
        for cp in copies(in_refs, out_refs, sems):
            cp.start()

    def finish(in_refs, out_refs, sems):
        for cp in copies(in_refs, out_refs, sems):
            cp.wait()

    return start, finish


def _pair_exchange_comm(gs):
    n = len(gs)

    def copies(g_refs, r_refs, sems):
        send_sems, recv_sems = sems
        x, y, c = _mesh_pos()
        return [pltpu.make_async_remote_copy(
            src_ref=g_refs[t].at[2 * q + 1 - c], dst_ref=r_refs[t].at[q],
            send_sem=send_sems.at[t, q], recv_sem=recv_sems.at[t, q],
            device_id=(x, y, 1 - c), device_id_type=pl.DeviceIdType.MESH) for t in range(n) for q in range(N_CHIPS)]

    sems = [pltpu.SemaphoreType.DMA((n, N_CHIPS)), pltpu.SemaphoreType.DMA((n, N_CHIPS))]
    return _Comm(list(gs), [jax.ShapeDtypeStruct((N_CHIPS, *g.shape[1:]), g.dtype) for g in gs], sems,
                 *_started_and_waited(copies))


def _chip_exchange_comm(hs):
    n = len(hs)

    def copies(h_refs, r_refs, sems):
        send_sems, recv_sems, local_sems = sems
        x, y, c = _mesh_pos()
        mine = 2 * x + y
        out = []
        for t in range(n):
            out.append(pltpu.make_async_copy(h_refs[t].at[mine], r_refs[t].at[mine], local_sems.at[t]))
            for k in range(1, N_CHIPS):
                px, py = (1 - x if (k >> 1) & 1 else x), (1 - y if k & 1 else y)
                out.append(pltpu.make_async_remote_copy(
                    src_ref=h_refs[t].at[2 * px + py], dst_ref=r_refs[t].at[mine],
                    send_sem=send_sems.at[t, k - 1], recv_sem=recv_sems.at[t, k - 1],
                    device_id=(px, py, c), device_id_type=pl.DeviceIdType.MESH))
        return out

    sems = [pltpu.SemaphoreType.DMA((n, N_CHIPS - 1)), pltpu.SemaphoreType.DMA((n, N_CHIPS - 1)), pltpu.SemaphoreType.DMA((n,))]
    return _Comm(list(hs), [jax.ShapeDtypeStruct(h.shape, h.dtype) for h in hs], sems, *_started_and_waited(copies))


def _pair_add(mine, theirs, name):
    q, rows, cols = mine.shape
    tr = _pick(rows, (512, 256, 128, 64, 32, 16))

    def body(a_ref, b_ref, o_ref):
        o_ref[...] = (a_ref[...].astype(F32) + b_ref[...].astype(F32)).astype(o_ref.dtype)

    blk = pl.BlockSpec((None, tr, cols), lambda s, i: (s, i, 0))
    return pl.pallas_call(
        body, name=name, grid=(q, rows // tr), in_specs=[blk, blk], out_specs=blk,
        out_shape=jax.ShapeDtypeStruct(mine.shape, mine.dtype),
        compiler_params=_params(("parallel", "parallel"), 16 * _nbytes((tr, cols), F32)),
    )(mine, theirs)


def _comm_call(comm, name):
    n_in, n_out = len(comm.inputs), len(comm.out_shapes)

    def body(*refs):
        ins, outs, sems = refs[:n_in], refs[n_in:n_in + n_out], refs[n_in + n_out:]
        comm.start(ins, outs, sems)
        comm.finish(ins, outs, sems)

    hbm = pl.BlockSpec(memory_space=pl.ANY)
    return pl.pallas_call(
        body, name=name, out_shape=comm.out_shapes, in_specs=[hbm] * n_in, out_specs=[hbm] * n_out,
        scratch_shapes=comm.sems)(*comm.inputs)


def _call(body, operands, *, name, grid, in_specs, out_specs, out_shape, scratch_shapes=(), compiler_params, comm=None):
    single = not isinstance(out_shape, (list, tuple))
    out_shape = [out_shape] if single else list(out_shape)
    out_specs = [out_specs] if single else list(out_specs)
    scratch_shapes = list(scratch_shapes)
    if comm is None:
        res = pl.pallas_call(
            body, name=name, grid=grid, in_specs=list(in_specs), out_specs=out_specs, out_shape=out_shape,
            scratch_shapes=scratch_shapes, compiler_params=compiler_params)(*operands)
        return res[0] if single else res
    n_in, n_out, n_scr = len(in_specs), len(out_specs), len(scratch_shapes)
    c_in, c_out = len(comm.inputs), len(comm.out_shapes)

    def with_comm(*refs):
        ins, c_ins = refs[:n_in], refs[n_in:n_in + c_in]
        o0 = n_in + c_in
        outs, c_outs = refs[o0:o0 + n_out], refs[o0 + n_out:o0 + n_out + c_out]
        s0 = o0 + n_out + c_out
        scr, sems = refs[s0:s0 + n_scr], refs[s0 + n_scr:]
        ids = [pl.program_id(a) for a in range(len(grid))]
        first = functools.reduce(jnp.logical_and, [i == 0 for i in ids])
        last = functools.reduce(jnp.logical_and, [i == g - 1 for i, g in zip(ids, grid)])

        @pl.when(first)
        def _():
            comm.start(c_ins, c_outs, sems)

        body(*ins, *outs, *scr)

        @pl.when(last)
        def _():
            comm.finish(c_ins, c_outs, sems)

    hbm = pl.BlockSpec(memory_space=pl.ANY)
    params = pltpu.CompilerParams(dimension_semantics=("arbitrary",) * len(grid),
                                  vmem_limit_bytes=compiler_params.vmem_limit_bytes)
    res = pl.pallas_call(
        with_comm, name=name, grid=grid, in_specs=list(in_specs) + [hbm] * c_in, out_specs=out_specs + [hbm] * c_out,
        out_shape=out_shape + list(comm.out_shapes), scratch_shapes=scratch_shapes + list(comm.sems),
        compiler_params=params)(*operands, *comm.inputs)
    return (res[0] if single else res[:n_out]), res[n_out:]


def _all_gather(xs, name):
    return _comm_call(_gather_comm(xs), name)


ROW_TILES = (1088, 1024, 768, 544, 512, 384, 272, 256, 128)
TOKEN_K_TILES = (2176, 2048, 1088, 1024, 768, 512, 384, 256, 128)
COL_TILES = (1024, 768, 640, 512, 384, 256, 128)
DEEP_K = 2048


def _mm_nn(a, w, *, planes=1, name, comm=None):
    m, k = a.shape
    if w.ndim == 3:
        nd_w = w.shape[2]
        n = w.shape[0] * nd_w
    else:
        nd_w = n = w.shape[1]
    npl = n // planes
    tm = _pick(m, ROW_TILES)
    tn = _pick(math.gcd(nd_w, npl), COL_TILES if k <= DEEP_K else COL_TILES[3:])
    r, rp = nd_w // tn, npl // tn
    if w.ndim == 3:
        w_spec = pl.BlockSpec((None, k, tn), lambda i, j: (j // r, 0, j % r))
    else:
        w_spec = pl.BlockSpec((k, tn), lambda i, j: (0, j))
    if planes > 1:
        o_spec = pl.BlockSpec((None, tm, tn), lambda i, j: (j // rp, i, j % rp))
        out_shape = jax.ShapeDtypeStruct((planes, m, npl), F32)
    else:
        o_spec = pl.BlockSpec((tm, tn), lambda i, j: (i, j))
        out_shape = jax.ShapeDtypeStruct((m, n), F32)

    def body(a_ref, w_ref, o_ref):
        o_ref[...] = _dot(a_ref[...], w_ref[...], NN)

    vmem = 2 * (_nbytes((tm, k), a.dtype) + _nbytes((k, tn), w.dtype) + _nbytes((tm, tn), F32)) + _nbytes((tm, tn), F32)
    return _call(
        body, (a, w), name=name, grid=(m // tm, n // tn),
        in_specs=[pl.BlockSpec((tm, k), lambda i, j: (i, 0)), w_spec], out_specs=o_spec, out_shape=out_shape,
        compiler_params=_params(("parallel", "arbitrary"), vmem), comm=comm)


def _mm_nt(a, w, *, name, comm=None):
    m, k = a.shape
    n = w.shape[0]
    tm = _pick(m, ROW_TILES)
    tn = _pick(n, COL_TILES)

    def body(a_ref, w_ref, o_ref):
        o_ref[...] = _dot(a_ref[...], w_ref[...], NT)

    vmem = 2 * (_nbytes((tm, k), a.dtype) + _nbytes((tn, k), w.dtype) + _nbytes((tm, tn), F32)) + _nbytes((tm, tn), F32)
    return _call(
        body, (a, w), name=name, grid=(m // tm, n // tn),
        in_specs=[pl.BlockSpec((tm, k), lambda i, j: (i, 0)), pl.BlockSpec((tn, k), lambda i, j: (j, 0))],
        out_specs=pl.BlockSpec((tm, tn), lambda i, j: (i, j)), out_shape=jax.ShapeDtypeStruct((m, n), F32),
        compiler_params=_params(("parallel", "arbitrary"), vmem), comm=comm)


def _mm_nt_blocked(a, w, *, name, comm=None):
    nd, n, kd = w.shape
    if a.ndim == 3:
        p, m, kp = a.shape
    else:
        (m, kp), p = a.shape, 1
    tk = _pick(math.gcd(kd, kp), COL_TILES)
    ra, rw = kp // tk, kd // tk
    nk = nd * rw
    tm = _pick(m, ROW_TILES)
    if a.ndim == 3:
        a_spec = pl.BlockSpec((None, tm, tk), lambda i, kk: (kk // ra, i, kk % ra))
    else:
        a_spec = pl.BlockSpec((tm, tk), lambda i, kk: (i, kk))

    def body(a_ref, w_ref, o_ref, acc_ref):
        kk = pl.program_id(1)

        @pl.when(kk == 0)
        def _():
            acc_ref[...] = jnp.zeros_like(acc_ref)

        acc_ref[...] += _dot(a_ref[...], w_ref[...], NT)

        @pl.when(kk == nk - 1)
        def _():
            o_ref[...] = acc_ref[...]

    vmem = 2 * (_nbytes((tm, tk), a.dtype) + _nbytes((n, tk), w.dtype) + _nbytes((tm, n), F32)) + 2 * _nbytes((tm, n), F32)
    return _call(
        body, (a, w), name=name, grid=(m // tm, nk),
        in_specs=[a_spec, pl.BlockSpec((None, n, tk), lambda i, kk: (kk // rw, 0, kk % rw))],
        out_specs=pl.BlockSpec((tm, n), lambda i, kk: (i, 0)), out_shape=jax.ShapeDtypeStruct((m, n), F32),
        scratch_shapes=[pltpu.VMEM((tm, n), F32)],
        compiler_params=_params(("parallel", "arbitrary"), vmem), comm=comm)


def _mm_tn(a, b, *, blocked, out_dtype, name, comm=None):
    rows, da = a.shape
    if b.ndim == 3:
        p, _, npl = b.shape
    else:
        p, npl = 1, b.shape[1]
    n = p * npl
    nd_w = n // N_DEV if blocked else n
    tk = _pick(rows, TOKEN_K_TILES)
    tm = _pick(da, COL_TILES)
    tn = _pick(math.gcd(nd_w, npl), COL_TILES)
    rb, ro = npl // tn, nd_w // tn
    nk = rows // tk
    if b.ndim == 3:
        b_spec = pl.BlockSpec((None, tk, tn), lambda i, j, kk: (j // rb, kk, j % rb))
    else:
        b_spec = pl.BlockSpec((tk, tn), lambda i, j, kk: (kk, j))
    if blocked:
        o_spec = pl.BlockSpec((None, tm, tn), lambda i, j, kk: (j // ro, i, j % ro))
        out_shape = jax.ShapeDtypeStruct((N_DEV, da, nd_w), out_dtype)
    else:
        o_spec = pl.BlockSpec((tm, tn), lambda i, j, kk: (i, j))
        out_shape = jax.ShapeDtypeStruct((da, n), out_dtype)

    def body(a_ref, b_ref, o_ref, acc_ref):
        kk = pl.program_id(2)

        @pl.when(kk == 0)
        def _():
            acc_ref[...] = jnp.zeros_like(acc_ref)

        acc_ref[...] += _dot(a_ref[...], b_ref[...], TN)

        @pl.when(kk == nk - 1)
        def _():
            o_ref[...] = acc_ref[...].astype(o_ref.dtype)

    vmem = (2 * (_nbytes((tk, tm), a.dtype) + _nbytes((tk, tn), b.dtype) + _nbytes((tm, tn), out_dtype))
            + 3 * _nbytes((tm, tn), F32) + _nbytes((tk, tm), F32))
    return _call(
        body, (a, b), name=name, grid=(da // tm, n // tn, nk),
        in_specs=[pl.BlockSpec((tk, tm), lambda i, j, kk: (kk, i)), b_spec], out_specs=o_spec, out_shape=out_shape,
        scratch_shapes=[pltpu.VMEM((tm, tn), F32)],
        compiler_params=_params(("parallel", "parallel", "arbitrary"), vmem), comm=comm)


ROW_TILE = 256


class _Rows:
    def __init__(self, n_x, n_ctx, tile=ROW_TILE):
        assert n_x % tile == 0 and n_ctx % tile == 0
        self.n_x, self.n_ctx, self.tile = n_x, n_ctx, tile
        self.rows = n_x + n_ctx
        self.nt_x = n_x // tile
        self.nt = self.rows // tile
        self.n_seg = 2 if n_ctx else 1

    def seg(self, i):
        return jnp.where(i >= self.nt_x, 1, 0) if self.n_ctx else 0

    def first_of_seg(self, i):
        return (i == 0) | (i == self.nt_x) if self.n_ctx else i == 0

    def full(self, width):
        return pl.BlockSpec((self.tile, width), lambda i: (i, 0))

    def plane(self, p, width):
        return pl.BlockSpec((None, self.tile, width), lambda i: (p, i, 0))

    def modvec(self, layer, which, width):
        return pl.BlockSpec((None, 1, width), lambda i: ((layer * 2 + self.seg(i)) * 3 + which, 0, 0))

    def seg_acc(self, width):
        return pl.BlockSpec((None, 1, width), lambda i: (self.seg(i), 0, 0))


def _vec(width):
    return pl.BlockSpec((1, width), lambda i: (0, 0))


def _acc(ref, first, val):
    @pl.when(first)
    def _():
        ref[...] = jnp.zeros_like(ref)

    ref[...] += val


def _modulate(xs, modv, layer, rt, name):
    d = xs.shape[1]

    def body(x_ref, sh_ref, sc_ref, o_ref):
        o_ref[...] = (x_ref[...] * (1.0 + sc_ref[...]) + sh_ref[...]).astype(o_ref.dtype)

    return pl.pallas_call(
        body, name=name, grid=(rt.nt,),
        in_specs=[rt.full(d), rt.modvec(layer, 0, d), rt.modvec(layer, 1, d)],
        out_specs=rt.full(d), out_shape=jax.ShapeDtypeStruct(xs.shape, MXU_DTYPE),
        compiler_params=_params(("parallel",), 6 * _nbytes((rt.tile, d), F32)),
    )(xs, modv, modv)


def _post(x, y, gate, pg, pb):
    return _ln(DEEPNORM_ALPHA * x + gate * y, pg, pb)


def _post_fwd(xs, y, modv, layer, pg, pb, rt, name, modulate_next=True):
    d = xs.shape[1]

    def body(x_ref, y_ref, gate_ref, pg_ref, pb_ref, *rest):
        out = _post(x_ref[...], y_ref[...], gate_ref[...], pg_ref[...], pb_ref[...])
        if modulate_next:
            sh_ref, sc_ref, o_ref, h_ref = rest
            h_ref[...] = (out * (1.0 + sc_ref[...]) + sh_ref[...]).astype(h_ref.dtype)
        else:
            (o_ref,) = rest
        o_ref[...] = out

    nxt = [rt.modvec(layer + 1, 0, d), rt.modvec(layer + 1, 1, d)] if modulate_next else []
    res = pl.pallas_call(
        body, name=name, grid=(rt.nt,),
        in_specs=[rt.full(d), rt.full(d), rt.modvec(layer, 2, d), _vec(d), _vec(d)] + nxt,
        out_specs=[rt.full(d)] * (2 if modulate_next else 1),
        out_shape=[jax.ShapeDtypeStruct((rt.rows, d), F32)] + ([jax.ShapeDtypeStruct((rt.rows, d), MXU_DTYPE)] if modulate_next else []),
        compiler_params=_params(("parallel",), 12 * _nbytes((rt.tile, d), F32)),
    )(xs, y, modv, pg, pb, *([modv, modv] if modulate_next else []))
    return res if modulate_next else res[0]


def _post_bwd(xs, y, dout, modv, layer, pg, pb, rt, name):
    d = xs.shape[1]

    def body(x_ref, y_ref, do_ref, gate_ref, pg_ref, pb_ref, dres_ref, dy_ref, dpg_ref, dpb_ref, dgate_ref):
        i = pl.program_id(0)
        _, vjp = jax.vjp(_post, x_ref[...], y_ref[...], gate_ref[...], pg_ref[...], pb_ref[...])
        dx, dy, dgate, dpg, dpb = vjp(do_ref[...])
        dres_ref[...] = dx
        dy_ref[...] = dy.astype(dy_ref.dtype)
        _acc(dpg_ref, i == 0, dpg)
        _acc(dpb_ref, i == 0, dpb)
        _acc(dgate_ref, rt.first_of_seg(i), dgate)

    return pl.pallas_call(
        body, name=name, grid=(rt.nt,),
        in_specs=[rt.full(d), rt.full(d), rt.full(d), rt.modvec(layer, 2, d), _vec(d), _vec(d)],
        out_specs=[rt.full(d), rt.full(d), _vec(d), _vec(d), rt.seg_acc(d)],
        out_shape=[jax.ShapeDtypeStruct((rt.rows, d), F32), jax.ShapeDtypeStruct((rt.rows, d), MXU_DTYPE),
                   jax.ShapeDtypeStruct((1, d), F32), jax.ShapeDtypeStruct((1, d), F32),
                   jax.ShapeDtypeStruct((rt.n_seg, 1, d), F32)],
        compiler_params=_params(("arbitrary",), 16 * _nbytes((rt.tile, d), F32)),
    )(xs, y, dout, modv, pg, pb)


def _mod_bwd(dres, dh, xs, modv, layer, rt, name, dx_rows=None):
    d = xs.shape[1]
    nt_res = dres.shape[0] // rt.tile
    nt_dx = rt.nt if dx_rows is None else dx_rows // rt.tile

    def body(dres_ref, dh_ref, x_ref, sc_ref, dx_ref, dshift_ref, dscale_ref):
        i = pl.program_id(0)
        dh = dh_ref[...]

        @pl.when(i < nt_dx)
        def _():
            dx_ref[...] = jnp.where(i < nt_res, dres_ref[...], 0.0) + dh * (1.0 + sc_ref[...])

        first = rt.first_of_seg(i)
        _acc(dshift_ref, first, jnp.sum(dh, axis=0, keepdims=True))
        _acc(dscale_ref, first, jnp.sum(dh * x_ref[...], axis=0, keepdims=True))

    def clamped(nt):
        return pl.BlockSpec((rt.tile, d), lambda i: (jnp.minimum(i, nt - 1), 0))

    return pl.pallas_call(
        body, name=name, grid=(rt.nt,),
        in_specs=[clamped(nt_res), rt.full(d), rt.full(d), rt.modvec(layer, 1, d)],
        out_specs=[clamped(nt_dx), rt.seg_acc(d), rt.seg_acc(d)],
        out_shape=[jax.ShapeDtypeStruct((nt_dx * rt.tile, d), F32), jax.ShapeDtypeStruct((rt.n_seg, 1, d), F32),
                   jax.ShapeDtypeStruct((rt.n_seg, 1, d), F32)],
        compiler_params=_params(("arbitrary",), 10 * _nbytes((rt.tile, d), F32)),
    )(dres, dh, xs, modv)


def _cm_mid_fwd(z3, ln_g, ln_b, w_s, b_s_t, name, comm=None):
    _, rows, e = z3.shape
    groups = w_s.shape[0]
    gw = e // groups

    def body(z_ref, lg_ref, lb_ref, ws_ref, bs_ref, t_ref):
        vn = _ln(z_ref[1], lg_ref[...], lb_ref[...])
        for h in range(groups):
            cols = slice(h * gw, (h + 1) * gw)
            s = _dot(ws_ref[h], vn[:, cols], NN) + bs_ref[:, h:h + 1]
            t_ref[:, cols] = (z_ref[0, :, cols] * s * _silu(z_ref[2, :, cols])).astype(t_ref.dtype)

    return _call(
        body, (z3, ln_g, ln_b, w_s, b_s_t), name=name, grid=(rows // CHUNK,),
        in_specs=[pl.BlockSpec((3, CHUNK, e), lambda i: (0, i, 0)), _vec(e), _vec(e),
                  pl.BlockSpec(w_s.shape, lambda i: (0, 0, 0)), pl.BlockSpec(b_s_t.shape, lambda i: (0, 0))],
        out_specs=pl.BlockSpec((CHUNK, e), lambda i: (i, 0)), out_shape=jax.ShapeDtypeStruct((rows, e), MXU_DTYPE),
        compiler_params=_params(("parallel",), 12 * _nbytes((CHUNK, e), F32)), comm=comm)


def _cm_mid_bwd(z3, dt, ln_g, ln_b, w_s, b_s_t, name, comm=None):
    _, rows, e = z3.shape
    groups = w_s.shape[0]
    gw = e // groups

    def body(z_ref, dt_ref, lg_ref, lb_ref, ws_ref, bs_ref, dz_ref, dlg_ref, dlb_ref, dws_ref, dbs_ref, dvn_ref):
        i = pl.program_id(0)
        first = i == 0
        v = z_ref[1]
        vn, ln_vjp = jax.vjp(_ln, v, lg_ref[...], lb_ref[...])

        @pl.when(first)
        def _():
            dws_ref[...] = jnp.zeros_like(dws_ref)
            dbs_ref[...] = jnp.zeros_like(dbs_ref)

        for h in range(groups):
            cols = slice(h * gw, (h + 1) * gw)
            vn_h = vn[:, cols]
            s = _dot(ws_ref[h], vn_h, NN) + bs_ref[:, h:h + 1]
            u, g, dth = z_ref[0, :, cols], z_ref[2, :, cols], dt_ref[:, cols]
            sg = _silu(g)
            dz_ref[0, :, cols] = (dth * s * sg).astype(dz_ref.dtype)
            dz_ref[2, :, cols] = (dth * u * s * _dsilu(g)).astype(dz_ref.dtype)
            ds = dth * u * sg
            dvn_ref[:, cols] = _dot(ws_ref[h], ds, TN)
            dws_ref[h] += _dot(ds, vn_h, NT)
            dbs_ref[:, h:h + 1] += jnp.sum(ds, axis=1, keepdims=True)
        dv, dlg, dlb = ln_vjp(dvn_ref[...])
        dz_ref[1] = dv.astype(dz_ref.dtype)
        _acc(dlg_ref, first, dlg)
        _acc(dlb_ref, first, dlb)

    return _call(
        body, (z3, dt, ln_g, ln_b, w_s, b_s_t), name=name, grid=(rows // CHUNK,),
        in_specs=[pl.BlockSpec((3, CHUNK, e), lambda i: (0, i, 0)), pl.BlockSpec((CHUNK, e), lambda i: (i, 0)), _vec(e), _vec(e),
                  pl.BlockSpec(w_s.shape, lambda i: (0, 0, 0)), pl.BlockSpec(b_s_t.shape, lambda i: (0, 0))],
        out_specs=[pl.BlockSpec((3, CHUNK, e), lambda i: (0, i, 0)), _vec(e), _vec(e),
                   pl.BlockSpec(w_s.shape, lambda i: (0, 0, 0)), pl.BlockSpec(b_s_t.shape, lambda i: (0, 0))],
        out_shape=[jax.ShapeDtypeStruct((3, rows, e), MXU_DTYPE), jax.ShapeDtypeStruct((1, e), F32),
                   jax.ShapeDtypeStruct((1, e), F32), jax.ShapeDtypeStruct(w_s.shape, F32),
                   jax.ShapeDtypeStruct(b_s_t.shape, F32)],
        scratch_shapes=[pltpu.VMEM((CHUNK, e), F32)],
        compiler_params=_params(("arbitrary",), 20 * _nbytes((CHUNK, e), F32)), comm=comm)


CONV_COL_TILE = 512


def _conv_specs(rt, tc, planes):
    per = rt.tile // CONV_HALO
    last = rt.rows // CONV_HALO - 1
    if planes:
        cur = pl.BlockSpec((planes, rt.tile, tc), lambda j, i: (0, i, j))
        prev = pl.BlockSpec((planes, CONV_HALO, tc), lambda j, i: (0, jnp.maximum(i * per - 1, 0), j))
        nxt = pl.BlockSpec((planes, CONV_HALO, tc), lambda j, i: (0, jnp.minimum((i + 1) * per, last), j))
    else:
        cur = pl.BlockSpec((rt.tile, tc), lambda j, i: (i, j))
        prev = pl.BlockSpec((CONV_HALO, tc), lambda j, i: (jnp.maximum(i * per - 1, 0), j))
        nxt = pl.BlockSpec((CONV_HALO, tc), lambda j, i: (jnp.minimum((i + 1) * per, last), j))
    return cur, prev, nxt


def _halo_ok(rt, i):
    prev_ok = (i != 0) & (i != rt.nt_x)
    next_ok = (i != rt.nt_x - 1) & (i != rt.nt - 1)
    return prev_ok, next_ok


def _glu(ref):
    return ref[0] * _sigmoid(ref[1])


def _padded(cur, prev, nxt, prev_ok, next_ok):
    return jnp.concatenate([jnp.where(prev_ok, prev, 0.0), cur, jnp.where(next_ok, nxt, 0.0)], axis=0)


SUBLANES = 8
CONV_ROW_BLOCK = 16
CONV_DW_ROWS, CONV_DW_TAPS = 16, 4


def _phase_scratch(tr, tc):
    return pltpu.VMEM((SUBLANES, tr + 2 * CONV_HALO - SUBLANES, tc), F32)


def _store_phases(rot_ref, pad):
    rows = rot_ref.shape[1]
    for b in range(SUBLANES):
        rot_ref[b] = pad[b:b + rows, :]


def _tap_rows(rot_ref, r0, off, g):
    a, b = divmod(off, SUBLANES)
    return rot_ref[b, pl.ds(r0 + SUBLANES * (a + g), SUBLANES), :]


def _conv_rows(rot_ref, w_ref, r0, offs, init):
    tc = rot_ref.shape[2]
    accs = [init] * (CONV_ROW_BLOCK // SUBLANES)
    for k, off in enumerate(offs):
        wk = jnp.broadcast_to(w_ref[k:k + 1, :], (SUBLANES, tc))
        accs = [acc + wk * _tap_rows(rot_ref, r0, off, g) for g, acc in enumerate(accs)]
    return jnp.concatenate(accs, axis=0)


def _conv_fwd(z3, conv_w, conv_b, rt, name, comm=None):
    _, rows, e = z3.shape
    tc = _pick(e, (CONV_COL_TILE, 256, 128))
    tr = rt.tile

    def body(cur_ref, prev_ref, next_ref, w_ref, b_ref, o_ref, rot_ref):
        prev_ok, next_ok = _halo_ok(rt, pl.program_id(1))
        _store_phases(rot_ref, _padded(_glu(cur_ref), _glu(prev_ref), _glu(next_ref), prev_ok, next_ok))
        bias = jnp.broadcast_to(b_ref[...], (SUBLANES, tc))
        offs = [CONV_HALO - CONV_W // 2 + k for k in range(CONV_W)]

        def rows_block(rb, carry):
            r0 = pl.multiple_of(rb * CONV_ROW_BLOCK, CONV_ROW_BLOCK)
            o_ref[pl.ds(r0, CONV_ROW_BLOCK), :] = _conv_rows(rot_ref, w_ref, r0, offs, bias)
            return carry

        lax.fori_loop(0, tr // CONV_ROW_BLOCK, rows_block, 0)

    cur, prev, nxt = _conv_specs(rt, tc, 2)
    return _call(
        body, (z3, z3, z3, conv_w, conv_b), name=name, grid=(e // tc, rt.nt),
        in_specs=[cur, prev, nxt, pl.BlockSpec((CONV_W, tc), lambda j, i: (0, j)), pl.BlockSpec((1, tc), lambda j, i: (0, j))],
        out_specs=pl.BlockSpec((tr, tc), lambda j, i: (i, j)), out_shape=jax.ShapeDtypeStruct((rows, e), F32),
        scratch_shapes=[_phase_scratch(tr, tc)],
        compiler_params=_params(("parallel", "arbitrary"), 32 * _nbytes((tr, tc), F32)), comm=comm)


def _conv_bwd(z3, dy1, dg, conv_w, rt, name, comm=None):
    _, rows, e = z3.shape
    tc = _pick(e, (CONV_COL_TILE, 256, 128))
    tr = rt.tile

    def body(cur_ref, prev_ref, next_ref, dcur_ref, dprev_ref, dnext_ref, dg_ref, w_ref, dz_ref, dw_ref, db_ref, rot_ref, drot_ref):
        i = pl.program_id(1)
        prev_ok, next_ok = _halo_ok(rt, i)
        _store_phases(rot_ref, _padded(_glu(cur_ref), _glu(prev_ref), _glu(next_ref), prev_ok, next_ok))
        _store_phases(drot_ref, _padded(dcur_ref[...], dprev_ref[...], dnext_ref[...], prev_ok, next_ok))
        n_blocks = tr // CONV_ROW_BLOCK

        @pl.when(i == 0)
        def _():
            dw_ref[...] = jnp.zeros_like(dw_ref)
            db_ref[...] = jnp.zeros_like(db_ref)

        roffs = [CONV_HALO + CONV_W // 2 - k for k in range(CONV_W)]
        zero = jnp.zeros((SUBLANES, tc), F32)

        def dgate_block(rb, carry):
            r0 = pl.multiple_of(rb * CONV_ROW_BLOCK, CONV_ROW_BLOCK)
            dy0 = _conv_rows(drot_ref, w_ref, r0, roffs, zero)
            rws = pl.ds(r0, CONV_ROW_BLOCK)
            a, sb = cur_ref[0, rws, :], _sigmoid(cur_ref[1, rws, :])
            dz_ref[0, rws, :] = (dy0 * sb).astype(dz_ref.dtype)
            dz_ref[1, rws, :] = (dy0 * a * sb * (1.0 - sb)).astype(dz_ref.dtype)
            return carry

        lax.fori_loop(0, n_blocks, dgate_block, 0)
        dz_ref[2] = dg_ref[...]

        groups = CONV_DW_ROWS // SUBLANES
        for k0 in range(0, CONV_W, CONV_DW_TAPS):
            taps = list(range(k0, min(k0 + CONV_DW_TAPS, CONV_W)))

            accs = [zero] * (len(taps) * groups)
            for r0 in range(0, tr, CONV_DW_ROWS):
                dy = [dcur_ref[r0 + SUBLANES * g:r0 + SUBLANES * (g + 1), :] for g in range(groups)]
                accs = [accs[t * groups + g] + dy[g] * _tap_rows(rot_ref, r0, CONV_HALO - CONV_W // 2 + k, g)
                        for t, k in enumerate(taps) for g in range(groups)]
            for t, k in enumerate(taps):
                tot = functools.reduce(jnp.add, accs[t * groups:(t + 1) * groups])
                dw_ref[k:k + 1, :] += jnp.sum(tot, axis=0, keepdims=True)
        db_ref[...] += jnp.sum(dcur_ref[...], axis=0, keepdims=True)

    cur, prev, nxt = _conv_specs(rt, tc, 2)
    dcur, dprev, dnxt = _conv_specs(rt, tc, 0)
    return _call(
        body, (z3, z3, z3, dy1, dy1, dy1, dg, conv_w), name=name, grid=(e // tc, rt.nt),
        in_specs=[cur, prev, nxt, dcur, dprev, dnxt, pl.BlockSpec((tr, tc), lambda j, i: (i, j)),
                  pl.BlockSpec((CONV_W, tc), lambda j, i: (0, j))],
        out_specs=[pl.BlockSpec((3, tr, tc), lambda j, i: (0, i, j)), pl.BlockSpec((CONV_W, tc), lambda j, i: (0, j)),
                   pl.BlockSpec((1, tc), lambda j, i: (0, j))],
        out_shape=[jax.ShapeDtypeStruct((3, rows, e), MXU_DTYPE), jax.ShapeDtypeStruct((CONV_W, e), F32),
                   jax.ShapeDtypeStruct((1, e), F32)],
        scratch_shapes=[_phase_scratch(tr, tc), _phase_scratch(tr, tc)],
        compiler_params=_params(("parallel", "arbitrary"), 48 * _nbytes((tr, tc), F32)), comm=comm)


def _conv_mid(y1, g, ln_g, ln_b):
    return _silu(_ln(y1, ln_g, ln_b)) * _silu(g)


def _conv_mid_fwd(y1, z3, ln_g, ln_b, rt, name):
    e = y1.shape[1]
    tr = CHUNK

    def body(y_ref, g_ref, lg_ref, lb_ref, t_ref):
        t_ref[...] = _conv_mid(y_ref[...], g_ref[...], lg_ref[...], lb_ref[...]).astype(t_ref.dtype)

    return pl.pallas_call(
        body, name=name, grid=(rt.rows // tr,),
        in_specs=[pl.BlockSpec((tr, e), lambda i: (i, 0)), pl.BlockSpec((None, tr, e), lambda i: (2, i, 0)), _vec(e), _vec(e)],
        out_specs=pl.BlockSpec((tr, e), lambda i: (i, 0)), out_shape=jax.ShapeDtypeStruct((rt.rows, e), MXU_DTYPE),
        compiler_params=_params(("parallel",), 12 * _nbytes((tr, e), F32)),
    )(y1, z3, ln_g, ln_b)


def _conv_mid_bwd(y1, z3, dt, ln_g, ln_b, rt, name):
    e = y1.shape[1]
    tr = CHUNK

    def body(y_ref, g_ref, dt_ref, lg_ref, lb_ref, dy_ref, dg_ref, dlg_ref, dlb_ref):
        first = pl.program_id(0) == 0
        _, vjp = jax.vjp(_conv_mid, y_ref[...], g_ref[...], lg_ref[...], lb_ref[...])
        dy, dg, dlg, dlb = vjp(dt_ref[...])
        dy_ref[...] = dy
        dg_ref[...] = dg.astype(dg_ref.dtype)
        _acc(dlg_ref, first, dlg)
        _acc(dlb_ref, first, dlb)

    row = pl.BlockSpec((tr, e), lambda i: (i, 0))
    return pl.pallas_call(
        body, name=name, grid=(rt.rows // tr,),
        in_specs=[row, pl.BlockSpec((None, tr, e), lambda i: (2, i, 0)), row, _vec(e), _vec(e)],
        out_specs=[row, row, _vec(e), _vec(e)],
        out_shape=[jax.ShapeDtypeStruct((rt.rows, e), F32), jax.ShapeDtypeStruct((rt.rows, e), MXU_DTYPE),
                   jax.ShapeDtypeStruct((1, e), F32), jax.ShapeDtypeStruct((1, e), F32)],
        compiler_params=_params(("arbitrary",), 20 * _nbytes((tr, e), F32)),
    )(y1, z3, dt, ln_g, ln_b)


def _rms(x, g):
    return x * lax.rsqrt(jnp.mean(x * x, axis=-1, keepdims=True) + LN_EPS) * g


def _pair_swap(x):
    lane = lax.broadcasted_iota(jnp.int32, x.shape, x.ndim - 1)
    return jnp.where(lane % 2 == 0, pltpu.roll(x, x.shape[-1] - 1, x.ndim - 1), pltpu.roll(x, 1, x.ndim - 1))


def _rope(x, cos, sin):
    return x * cos + _pair_swap(x) * sin


def _rope_t(dy, cos, sin):
    return dy * cos + _pair_swap(dy * sin)


def _rope_tables(n_x, n_ctx):
    t = jnp.arange(n_x)
    row = (t // GRID_W).astype(F32)
    col = (t % GRID_W).astype(F32)
    axis_dim = HEAD_DIM // 2
    inv = 1.0 / (ROPE_THETA ** (jnp.arange(0, axis_dim, 2, dtype=F32) / axis_dim))
    ang = jnp.concatenate([row[:, None] * inv, col[:, None] * inv], axis=-1)
    cos, sin = jnp.cos(ang), jnp.sin(ang)
    cos2 = jnp.repeat(cos, 2, axis=-1)
    sin2 = jnp.stack([-sin, sin], axis=-1).reshape(n_x, HEAD_DIM)
    cos2 = jnp.concatenate([cos2, jnp.ones((n_ctx, HEAD_DIM), F32)], axis=0)
    sin2 = jnp.concatenate([sin2, jnp.zeros((n_ctx, HEAD_DIM), F32)], axis=0)
    return cos2, sin2


def _qkv_prep(z4, q_g, k_g, cos, sin, d, kvw, rt, name):
    hd = HEAD_DIM
    kb = d // kvw

    def body(q_ref, k_ref, v_ref, qg_ref, kg_ref, cos_ref, sin_ref, qo_ref, ko_ref, vo_ref):
        cos, sin = cos_ref[...], sin_ref[...]
        for h in range(d // hd):
            cols = slice(h * hd, (h + 1) * hd)
            qo_ref[:, cols] = _rope(_rms(q_ref[:, cols], qg_ref[...]), cos, sin).astype(qo_ref.dtype)
        for h in range(kvw // hd):
            cols = slice(h * hd, (h + 1) * hd)
            ko_ref[:, cols] = _rope(_rms(k_ref[:, cols], kg_ref[...]), cos, sin).astype(ko_ref.dtype)
        vo_ref[...] = v_ref[...].astype(vo_ref.dtype)

    tr = rt.tile
    return pl.pallas_call(
        body, name=name, grid=(rt.nt,),
        in_specs=[pl.BlockSpec((tr, d), lambda i: (i, 0)), pl.BlockSpec((tr, kvw), lambda i: (i, kb)),
                  pl.BlockSpec((tr, kvw), lambda i: (i, kb + 1)), _vec(hd), _vec(hd), rt.full(hd), rt.full(hd)],
        out_specs=[rt.full(d), rt.full(kvw), rt.full(kvw)],
        out_shape=[jax.ShapeDtypeStruct((rt.rows, d), MXU_DTYPE), jax.ShapeDtypeStruct((rt.rows, kvw), MXU_DTYPE),
                   jax.ShapeDtypeStruct((rt.rows, kvw), MXU_DTYPE)],
        compiler_params=_params(("parallel",), 8 * _nbytes((tr, d), F32)),
    )(z4, z4, z4, q_g, k_g, cos, sin)


ATTN_Q_TILE = 256
ATTN_HEADS_PER_PASS = 1
LOG2_E = math.log2(math.e)


def _attn_fwd(qh, kh, vh, n_x, name, comm=None):
    rows, d = qh.shape
    kvw = kh.shape[1]
    hd = HEAD_DIM
    n_kv = kvw // hd
    gqw = d // n_kv
    grp = gqw // hd
    tq = _pick(n_x, (ATTN_Q_TILE, 128))
    scale = hd ** -0.5

    def body(q_ref, k_ref, v_ref, o_ref, lse_ref):
        k, v = k_ref[...], v_ref[...]
        for g0 in range(0, grp, ATTN_HEADS_PER_PASS):
            heads = range(g0, min(g0 + ATTN_HEADS_PER_PASS, grp))
            q = jnp.concatenate([q_ref[:, g * hd:(g + 1) * hd] for g in heads], axis=0)
            s = _dot(q, k, NT)
            m = jnp.max(s, axis=-1, keepdims=True)
            p = jnp.exp2((s - m) * (scale * LOG2_E))
            l = jnp.sum(p, axis=-1, keepdims=True)
            o = _dot(p / l, v, NN)
            lse = m * scale + jnp.log(l)
            for n, g in enumerate(heads):
                o_ref[:, g * hd:(g + 1) * hd] = o[n * tq:(n + 1) * tq]
                lse_ref[:, g:g + 1] = lse[n * tq:(n + 1) * tq]

    vmem = 4 * _nbytes((rows, hd), MXU_DTYPE) + 4 * _nbytes((tq, rows), F32) + 6 * _nbytes((tq, gqw), F32)
    return _call(
        body, (qh, kh, vh), name=name, grid=(n_kv, n_x // tq),
        in_specs=[pl.BlockSpec((tq, gqw), lambda h, i: (i, h)), pl.BlockSpec((rows, hd), lambda h, i: (0, h)),
                  pl.BlockSpec((rows, hd), lambda h, i: (0, h))],
        out_specs=[pl.BlockSpec((tq, gqw), lambda h, i: (i, h)), pl.BlockSpec((None, tq, grp), lambda h, i: (h, i, 0))],
        out_shape=[jax.ShapeDtypeStruct((n_x, d), F32), jax.ShapeDtypeStruct((n_kv, n_x, grp), F32)],
        compiler_params=_params(("parallel", "arbitrary"), vmem), comm=comm)


def _attn_bwd(qh, kh, vh, do, lse, n_x, name, comm=None):
    rows, d = qh.shape
    kvw = kh.shape[1]
    hd = HEAD_DIM
    n_kv = kvw // hd
    gqw = d // n_kv
    grp = gqw // hd
    tq = _pick(n_x, (ATTN_Q_TILE, 128))
    scale = hd ** -0.5

    def body(q_ref, k_ref, v_ref, do_ref, lse_ref, dq_ref, dk_ref, dv_ref):
        @pl.when(pl.program_id(1) == 0)
        def _():
            dk_ref[...] = jnp.zeros_like(dk_ref)
            dv_ref[...] = jnp.zeros_like(dv_ref)

        k, v = k_ref[...], v_ref[...]
        for g0 in range(0, grp, ATTN_HEADS_PER_PASS):
            heads = range(g0, min(g0 + ATTN_HEADS_PER_PASS, grp))
            q = jnp.concatenate([q_ref[:, g * hd:(g + 1) * hd] for g in heads], axis=0)
            dog = jnp.concatenate([do_ref[:, g * hd:(g + 1) * hd] for g in heads], axis=0)
            lse = jnp.concatenate([lse_ref[:, g:g + 1] for g in heads], axis=0)
            p = jnp.exp2(_dot(q, k, NT) * (scale * LOG2_E) - lse * LOG2_E)
            dp = _dot(dog, v, NT)
            ds = (p * (dp - jnp.sum(dp * p, axis=-1, keepdims=True)) * scale).astype(MXU_DTYPE)
            dq = _dot(ds, k, NN)
            for n, g in enumerate(heads):
                dq_ref[:, g * hd:(g + 1) * hd] = dq[n * tq:(n + 1) * tq]
            dk_ref[...] += _dot(ds, q, TN)
            dv_ref[...] += _dot(p, dog, TN)

    vmem = 4 * _nbytes((rows, hd), MXU_DTYPE) + 4 * _nbytes((rows, hd), F32) + 6 * _nbytes((tq, rows), F32) + 8 * _nbytes((tq, gqw), F32)
    qspec = pl.BlockSpec((tq, gqw), lambda h, i: (i, h))
    kspec = pl.BlockSpec((rows, hd), lambda h, i: (0, h))
    return _call(
        body, (qh, kh, vh, do, lse), name=name, grid=(n_kv, n_x // tq),
        in_specs=[qspec, kspec, kspec, qspec, pl.BlockSpec((None, tq, grp), lambda h, i: (h, i, 0))],
        out_specs=[qspec, kspec, kspec],
        out_shape=[jax.ShapeDtypeStruct((n_x, d), F32), jax.ShapeDtypeStruct((rows, kvw), F32),
                   jax.ShapeDtypeStruct((rows, kvw), F32)],
        compiler_params=_params(("parallel", "arbitrary"), vmem), comm=comm)


def _attn_gate(o, z4, d, kvw, rt, name):
    g0 = (d + 2 * kvw) // kvw
    tr = rt.tile

    def body(o_ref, g_ref, t_ref):
        t_ref[...] = (o_ref[...] * _silu(g_ref[...])).astype(t_ref.dtype)

    tile = pl.BlockSpec((tr, kvw), lambda i, j: (i, j))
    return pl.pallas_call(
        body, name=name, grid=(rt.nt, d // kvw),
        in_specs=[tile, pl.BlockSpec((tr, kvw), lambda i, j: (i, g0 + j))],
        out_specs=tile, out_shape=jax.ShapeDtypeStruct((rt.rows, d), MXU_DTYPE),
        compiler_params=_params(("parallel", "parallel"), 8 * _nbytes((tr, kvw), F32)),
    )(o, z4)


def _attn_gate_bwd(dt, o, z4, d, kvw, rt, name):
    g0 = (d + 2 * kvw) // kvw
    tr = rt.tile

    def body(dt_ref, o_ref, g_ref, do_ref, dg_ref):
        dt_v, g = dt_ref[...], g_ref[...]
        do_ref[...] = (dt_v * _silu(g)).astype(do_ref.dtype)
        dg_ref[...] = (dt_v * o_ref[...] * _dsilu(g)).astype(dg_ref.dtype)

    tile = pl.BlockSpec((tr, kvw), lambda i, j: (i, j))
    return pl.pallas_call(
        body, name=name, grid=(rt.nt, d // kvw),
        in_specs=[tile, tile, pl.BlockSpec((tr, kvw), lambda i, j: (i, g0 + j))],
        out_specs=[tile, tile],
        out_shape=[jax.ShapeDtypeStruct((rt.rows, d), MXU_DTYPE), jax.ShapeDtypeStruct((rt.rows, d), MXU_DTYPE)],
        compiler_params=_params(("parallel", "parallel"), 12 * _nbytes((tr, kvw), F32)),
    )(dt, o, z4)


def _prep_bwd(dxh, z4, col_block, gain, cos, sin, rt, name):
    w = dxh.shape[1]
    hd = HEAD_DIM

    def body(dxh_ref, x_ref, g_ref, cos_ref, sin_ref, dx_ref, dg_ref):
        cos, sin = cos_ref[...], sin_ref[...]
        dg = jnp.zeros((1, hd), F32)
        for h in range(w // hd):
            cols = slice(h * hd, (h + 1) * hd)
            _, vjp = jax.vjp(_rms, x_ref[:, cols], g_ref[...])
            dx, dgh = vjp(_rope_t(dxh_ref[:, cols], cos, sin))
            dx_ref[:, cols] = dx.astype(dx_ref.dtype)
            dg = dg + dgh
        _acc(dg_ref, pl.program_id(0) == 0, dg)

    tr = rt.tile
    return pl.pallas_call(
        body, name=name, grid=(rt.nt,),
        in_specs=[rt.full(w), pl.BlockSpec((tr, w), lambda i: (i, col_block)), _vec(hd), rt.full(hd), rt.full(hd)],
        out_specs=[rt.full(w), _vec(hd)],
        out_shape=[jax.ShapeDtypeStruct((rt.rows, w), MXU_DTYPE), jax.ShapeDtypeStruct((1, hd), F32)],
        compiler_params=_params(("arbitrary",), 12 * _nbytes((tr, w), F32)),
    )(dxh, z4, gain, cos, sin)


def _loss_head(x, target, rt, name):
    d = x.shape[1]

    def body(x_ref, t_ref, dx_ref, l_ref):
        err = x_ref[...] - t_ref[...]
        dx_ref[...] = err / d
        row = jnp.mean(err * err, axis=-1, keepdims=True)
        _acc(l_ref, pl.program_id(0) == 0, jnp.sum(row, axis=0, keepdims=True))

    return pl.pallas_call(
        body, name=name, grid=(rt.nt,),
        in_specs=[rt.full(d), rt.full(d)],
        out_specs=[rt.full(d), pl.BlockSpec((1, 1), lambda i: (0, 0))],
        out_shape=[jax.ShapeDtypeStruct(x.shape, F32), jax.ShapeDtypeStruct((1, 1), F32)],
        compiler_params=_params(("arbitrary",), 8 * _nbytes((rt.tile, d), F32)),
    )(x, target)


def _adamw(w, g, m, v):
    m = ADAM_B1 * m + (1.0 - ADAM_B1) * g
    v = ADAM_B2 * v + (1.0 - ADAM_B2) * (g * g)
    m_hat = m / (1.0 - ADAM_B1 ** ADAM_STEP)
    v_hat = v / (1.0 - ADAM_B2 ** ADAM_STEP)
    delta = -ADAM_LR * (m_hat / (jnp.sqrt(v_hat) + ADAM_EPS) + ADAM_WD * w)
    return delta, m, v


ADAM_TILE_BYTES = 1 << 20


def _adam_tile(rows, cols):
    tr = rows
    while tr % 16 == 0 and tr * cols * 4 > ADAM_TILE_BYTES:
        tr //= 2
    return tr


def _adam_reduce(parts, w, m, v, name, comm=None):
    slots, rows, cols = w.shape
    tr = _adam_tile(rows, cols)
    nt = rows // tr

    def body(*refs):
        p_refs = refs[:slots]
        w_ref, m_ref, v_ref, g_ref, d_ref, mo_ref, vo_ref = refs[slots:]
        for k in range(slots):
            @pl.when(pl.program_id(0) == k)
            def _(p_ref=p_refs[k]):
                g = p_ref[0].astype(F32)
                for part in range(1, p_ref.shape[0]):
                    g = g + p_ref[part].astype(F32)
                g_ref[...] = g
                d_ref[...], mo_ref[...], vo_ref[...] = _adamw(w_ref[...], g, m_ref[...], v_ref[...])

    def part_spec(k):
        return pl.BlockSpec((parts[k].shape[0], tr, cols),
                            lambda s, i: (0, jnp.where(s < k, 0, jnp.where(s == k, i, nt - 1)), 0))

    row = pl.BlockSpec((None, tr, cols), lambda s, i: (s, i, 0))
    sds = jax.ShapeDtypeStruct((slots, rows, cols), F32)
    return _call(
        body, (*parts, w, m, v), name=name, grid=(slots, nt),
        in_specs=[part_spec(k) for k in range(slots)] + [row, row, row],
        out_specs=[row] * 4, out_shape=[sds] * 4,
        compiler_params=_params(("arbitrary", "arbitrary"), 40 * _nbytes((tr, cols), F32)), comm=comm)


def _adam_plain(g, w, m, v, name):
    rows, cols = w.shape
    tr = _adam_tile(rows, cols)

    def body(g_ref, w_ref, m_ref, v_ref, d_ref, mo_ref, vo_ref):
        d_ref[...], mo_ref[...], vo_ref[...] = _adamw(w_ref[...], g_ref[...], m_ref[...], v_ref[...])

    row = pl.BlockSpec((tr, cols), lambda i: (i, 0))
    sds = jax.ShapeDtypeStruct((rows, cols), F32)
    return pl.pallas_call(
        body, name=name, grid=(rows // tr,),
        in_specs=[row] * 4, out_specs=[row] * 3, out_shape=[sds] * 3,
        compiler_params=_params(("parallel",), 32 * _nbytes((tr, cols), F32)),
    )(g, w, m, v)


def _sum_devices(parts, name):
    _, rows, cols = parts.shape
    tr = _adam_tile(rows, cols)

    def body(p_ref, o_ref):
        g = p_ref[0]
        for k in range(1, N_DEV):
            g = g + p_ref[k]
        o_ref[...] = g

    return pl.pallas_call(
        body, name=name, grid=(rows // tr,),
        in_specs=[pl.BlockSpec((N_DEV, tr, cols), lambda i: (0, i, 0))],
        out_specs=pl.BlockSpec((tr, cols), lambda i: (i, 0)), out_shape=jax.ShapeDtypeStruct((rows, cols), F32),
        compiler_params=_params(("parallel",), 24 * _nbytes((tr, cols), F32)),
    )(parts)


COND_ROWS = 16


def _mod_fwd_mm(cond, mod_w, mod_b, name):
    layers, d, w = mod_w.shape

    def body(c_ref, w_ref, b_ref, o_ref):
        o_ref[...] = _dot(_silu(c_ref[...]), w_ref[...], NN) + b_ref[...]

    return pl.pallas_call(
        body, name=name, grid=(layers,),
        in_specs=[pl.BlockSpec((COND_ROWS, d), lambda l: (0, 0)), pl.BlockSpec((None, d, w), lambda l: (l, 0, 0)),
                  pl.BlockSpec((None, 1, w), lambda l: (l, 0, 0))],
        out_specs=pl.BlockSpec((None, COND_ROWS, w), lambda l: (l, 0, 0)),
        out_shape=jax.ShapeDtypeStruct((layers, COND_ROWS, w), F32),
        compiler_params=_params(("parallel",), 4 * _nbytes((d, w), F32)),
    )(cond, mod_w, mod_b)


def _mod_bwd_mm(cond, dm, mod_w, name):
    layers, d, w = mod_w.shape

    def body(c_ref, dm_ref, w_ref, dw_ref, dc_ref):
        dmv = dm_ref[...]
        dw_ref[...] = _dot(_silu(c_ref[...]), dmv, TN)
        _acc(dc_ref, pl.program_id(0) == 0, _dot(dmv, w_ref[...], NT))

    return pl.pallas_call(
        body, name=name, grid=(layers,),
        in_specs=[pl.BlockSpec((COND_ROWS, d), lambda l: (0, 0)), pl.BlockSpec((None, COND_ROWS, w), lambda l: (l, 0, 0)),
                  pl.BlockSpec((None, d, w), lambda l: (l, 0, 0))],
        out_specs=[pl.BlockSpec((None, d, w), lambda l: (l, 0, 0)), pl.BlockSpec((COND_ROWS, d), lambda l: (0, 0))],
        out_shape=[jax.ShapeDtypeStruct((layers, d, w), F32), jax.ShapeDtypeStruct((COND_ROWS, d), F32)],
        compiler_params=_params(("arbitrary",), 6 * _nbytes((d, w), F32)),
    )(cond, dm, mod_w)


PACK_ROWS = 256


def _pack(arrs):
    flat = jnp.concatenate([a.reshape(-1).astype(F32) for a in arrs])
    pad = (-flat.shape[0]) % (PACK_ROWS * LANES)
    return jnp.pad(flat, (0, pad)).reshape(-1, LANES)


def _unpack(flat2d, shapes):
    flat = flat2d.reshape(-1)
    out, off = [], 0
    for s in shapes:
        n = math.prod(s)
        out.append(flat[off:off + n].reshape(s))
        off += n
    return out


def _unpack_dev(g2d, shapes):
    flat = g2d.reshape(N_DEV, -1)
    out, off = [], 0
    for s in shapes:
        n = math.prod(s)
        out.append(flat[:, off:off + n].reshape((N_DEV, *s)))
        off += n
    return out


def kernel(x, c, ctx, c_ctx, mod_w, mod_b, post_g, post_b, a_w_in, a_ln_g, a_ln_b, a_w_s, a_b_s, a_w_out, b_w_in, b_conv_w, b_conv_b, b_ln_g, b_ln_b, b_w_out, c_w_in, c_q_g, c_k_g, c_w_out, loss_target, m_c_ctx, m_mod_w, m_mod_b, m_post_g, m_post_b, m_a_w_in, m_a_ln_g, m_a_ln_b, m_a_w_s, m_a_b_s, m_a_w_out, m_b_w_in, m_b_conv_w, m_b_conv_b, m_b_ln_g, m_b_ln_b, m_b_w_out, m_c_w_in, m_c_q_g, m_c_k_g, m_c_w_out, v_c_ctx, v_mod_w, v_mod_b, v_post_g, v_post_b, v_a_w_in, v_a_ln_g, v_a_ln_b, v_a_w_s, v_a_b_s, v_a_w_out, v_b_w_in, v_b_conv_w, v_b_conv_b, v_b_ln_g, v_b_ln_b, v_b_w_out, v_c_w_in, v_c_q_g, v_c_k_g, v_c_w_out):
    n_x, d = x.shape[1], x.shape[2]
    n_ctx = ctx.shape[1]
    e = a_w_out.shape[1] * N_DEV
    kvw = N_KV_HEADS * HEAD_DIM
    me = _dev_index(_mesh_pos())
    rt_all = _Rows(n_x, n_ctx)
    rt_x = _Rows(n_x, 0)

    small_in = [c[0], a_ln_g, a_ln_b, b_conv_w[0]]
    (g_small,) = _all_gather([_pack(small_in)], "ag_small_params")
    conds, ln_g_all, ln_b_all, conv_w_all = _unpack_dev(g_small, [a.shape for a in small_in])
    a_ln_g_f = jnp.moveaxis(ln_g_all, 0, 1).reshape(a_ln_g.shape[0], 1, e)
    a_ln_b_f = jnp.moveaxis(ln_b_all, 0, 1).reshape(a_ln_b.shape[0], 1, e)
    conv_w_f = jnp.moveaxis(conv_w_all, 0, 1).reshape(CONV_W, e)
    cond = jnp.zeros((COND_ROWS, d), F32).at[:N_DEV].set(conds).at[N_DEV].set(c_ctx)

    wm = mod_w.shape[2]
    mod_b_mine = lax.dynamic_slice_in_dim(mod_b, me * wm, wm, axis=1).reshape(DEPTH, 1, wm)
    (mods_g,) = _all_gather([_mod_fwd_mm(cond, mod_w, mod_b_mine, "mod_fwd")], "ag_mod")
    mods = jnp.moveaxis(mods_g, 0, 2).reshape(DEPTH, COND_ROWS, 3 * d)
    mine = lax.dynamic_index_in_dim(mods, me, axis=1, keepdims=False)
    modv = jnp.stack([mine, mods[:, N_DEV]], axis=1).reshape(DEPTH * 2 * 3, 1, d)

    def gather_of(wt):
        return _gather_comm([wt.astype(MXU_DTYPE)])

    def exchange_of(*gs):
        return _exchange_comm([g if g.ndim == 3 else g.reshape(N_DEV, -1, g.shape[-1]) for g in gs])

    def chip_sums_comm(g, theirs, tag):
        mine = lax.dynamic_index_in_dim(g.reshape(N_CHIPS, 2, *g.shape[1:]), lax.axis_index("c"), axis=1, keepdims=False)
        return _chip_exchange_comm([_pair_add(mine, theirs, f"pair_add_{tag}")])

    def chip_sums_of(g, tag):
        (theirs,) = _comm_call(_pair_exchange_comm([g]), f"pair_{tag}")
        return chip_sums_comm(g, theirs, tag)

    (wa_in0,) = _all_gather([a_w_in[0].astype(MXU_DTYPE)], "ag_w_l0")

    ws_op = a_w_s.astype(MXU_DTYPE)
    bs_t = jnp.swapaxes(a_b_s, 1, 2)
    pg = post_g.reshape(DEPTH, 1, d)
    pb = post_b.reshape(DEPTH, 1, d)

    xs0 = jnp.concatenate([x[0], ctx[0]], axis=0)
    h0 = _modulate(xs0, modv, 0, rt_all, "mod0")
    z0, (wb_in,) = _mm_nn(h0, wa_in0, planes=3, name="l0_in", comm=gather_of(b_w_in[0]))
    t0, (wa_out0,) = _cm_mid_fwd(z0, a_ln_g_f[0], a_ln_b_f[0], ws_op[0], bs_t[0], "l0_mid", comm=gather_of(a_w_out[0]))
    wa_out0 = wa_out0.reshape(-1, d)
    y0, (wb_out,) = _mm_nn(t0, wa_out0, name="l0_out", comm=gather_of(b_w_out[0]))
    xs1, h1 = _post_fwd(xs0, y0, modv, 0, pg[0], pb[0], rt_all, "l0_post")
    z1, (wc_in,) = _mm_nn(h1, wb_in, planes=3, name="l1_in", comm=gather_of(c_w_in[0]))
    cy1, (wa_in1,) = _conv_fwd(z1, conv_w_f, b_conv_b, rt_all, "l1_conv", comm=gather_of(a_w_in[1]))
    t1 = _conv_mid_fwd(cy1, z1, b_ln_g, b_ln_b, rt_all, "l1_mid")
    y1, (wc_out,) = _mm_nn(t1, wb_out.reshape(-1, d), name="l1_out", comm=gather_of(c_w_out[0]))
    xs2, h2 = _post_fwd(xs1, y1, modv, 1, pg[1], pb[1], rt_all, "l1_post")
    cos, sin = _rope_tables(n_x, n_ctx)
    z2 =_mm_nn(h2, wc_in, name="l2_in")
    qh, kh, vh = _qkv_prep(z2, c_q_g, c_k_g, cos, sin, d, kvw, rt_all, "l2_prep")
    (o2, lse), (wa_out1,) = _attn_fwd(qh, kh, vh, n_x, "l2_attn", comm=gather_of(a_w_out[1]))
    wb_out, wc_out, wa_out1 = [wt.reshape(-1, d) for wt in (wb_out, wc_out, wa_out1)]
    t2 = _attn_gate(o2, z2, d, kvw, rt_x, "l2_gate")
    y2 = _mm_nn(t2, wc_out, name="l2_out")
    x2 = xs2
    x3, h3 = _post_fwd(x2, y2, modv, 2, pg[2], pb[2], rt_x, "l2_post")
    z3 =_mm_nn(h3, wa_in1, planes=3, name="l3_in")
    t3 = _cm_mid_fwd(z3, a_ln_g_f[1], a_ln_b_f[1], ws_op[1], bs_t[1], "l3_mid")
    y3 = _mm_nn(t3, wa_out1, name="l3_out")
    x4 = _post_fwd(x3, y3, modv, 3, pg[3], pb[3], rt_x, "l3_post", modulate_next=False)

    dx4, loss_sum = _loss_head(x4, loss_target[0], rt_x, "loss")
    loss = lax.psum(0.5 * loss_sum[0, 0], ("x", "y", "c"))

    gdt = MXU_DTYPE
    dres3, dy3, dpg3, dpb3, dgate3 = _post_bwd(x3, y3, dx4, modv, 3, pg[3], pb[3], rt_x, "l3_post_b")
    dt3 = _mm_nt(dy3, wa_out1, name="l3_dt")
    gw_a_out1 = _mm_tn(t3, dy3, blocked=False, out_dtype=gdt, name="l3_dwout")
    dz3, dlg3, dlb3, dws3, dbs3 = _cm_mid_bwd(z3, dt3, a_ln_g_f[1], a_ln_b_f[1], ws_op[1], bs_t[1], "l3_mid_b")
    gw_a_in1 = _mm_tn(h3, dz3, blocked=True, out_dtype=gdt, name="l3_dwin")
    dh3, (r_a_out1,) = _mm_nt_blocked(dz3, wa_in1, name="l3_dh", comm=exchange_of(gw_a_out1))
    dx3, dshift3, dscale3 = _mod_bwd(dres3, dh3, x3, modv, 3, rt_x, "l3_mod_b")
    dres2, dy2, dpg2, dpb2, dgate2 = _post_bwd(x2, y2, dx3, modv, 2, pg[2], pb[2], rt_x, "l2_post_b")
    dt2 = _mm_nt(dy2, wc_out, name="l2_dt")
    gw_c_out = _mm_tn(t2, dy2, blocked=False, out_dtype=gdt, name="l2_dwout")
    do2, dg2 = _attn_gate_bwd(dt2, o2, z2, d, kvw, rt_x, "l2_gate_b")
    (dqh, dkh, dvh), (r_a_in1,) = _attn_bwd(qh, kh, vh, do2, lse, n_x, "l2_attn_b", comm=exchange_of(gw_a_in1))
    dq2, dqg = _prep_bwd(dqh, z2, 0, c_q_g, cos, sin, rt_x, "l2_qprep_b")
    dk2, dkg = _prep_bwd(dkh, z2, d // kvw, c_k_g, cos, sin, rt_all, "l2_kprep_b")
    zpad = jnp.zeros((n_ctx, d), MXU_DTYPE)
    dz2 = jnp.concatenate([jnp.concatenate([dq2, zpad], axis=0), dk2, dvh.astype(MXU_DTYPE),
                           jnp.concatenate([dg2, zpad], axis=0)], axis=1)
    gw_c_in = _mm_tn(h2, dz2, blocked=True, out_dtype=gdt, name="l2_dwin")
    dh2, (r_c_out,) = _mm_nt_blocked(dz2, wc_in, name="l2_dh", comm=exchange_of(gw_c_out))
    dxs2, dshift2, dscale2 = _mod_bwd(dres2, dh2, xs2, modv, 2, rt_all, "l2_mod_b")
    dres1, dy1, dpg1, dpb1, dgate1 = _post_bwd(xs1, y1, dxs2, modv, 1, pg[1], pb[1], rt_all, "l1_post_b")
    dt1 = _mm_nt(dy1, wb_out, name="l1_dt")
    gw_b_out = _mm_tn(t1, dy1, blocked=False, out_dtype=gdt, name="l1_dwout")
    dcy1, dgc1, dblg, dblb = _conv_mid_bwd(cy1, z1, dt1, b_ln_g, b_ln_b, rt_all, "l1_mid_b")
    (dz1, dconv_w, dconv_b), (r_c_in, r_b_out) = _conv_bwd(z1, dcy1, dgc1, conv_w_f, rt_all, "l1_conv_b",
                                                           comm=exchange_of(gw_c_in, gw_b_out))
    gw_b_in = _mm_tn(h1, dz1, blocked=True, out_dtype=gdt, name="l1_dwin")
    dh1, (pair_b_in,) = _mm_nt_blocked(dz1, wb_in, name="l1_dh", comm=_pair_exchange_comm([gw_b_in]))
    dxs1, dshift1, dscale1 = _mod_bwd(dres1, dh1, xs1, modv, 1, rt_all, "l1_mod_b")
    def seg2(a):
        a = a[:, 0]
        return a if a.shape[0] == 2 else jnp.concatenate([a, jnp.zeros_like(a)], axis=0)

    def mod_rows(dshift, dscale, dgate):
        return jnp.concatenate([seg2(dshift), seg2(dscale), seg2(dgate)], axis=1)

    dmod_hi = jnp.stack([mod_rows(dshift1, dscale1, dgate1), mod_rows(dshift2, dscale2, dgate2),
                         mod_rows(dshift3, dscale3, dgate3)])
    early_g = [jnp.concatenate([dpg1, dpg2, dpg3], axis=0), jnp.concatenate([dpb1, dpb2, dpb3], axis=0), dws3, dbs3,
               dconv_b, dblg, dblb, dqg, dkg, dlg3, dlb3, dconv_w, dmod_hi[:, 0], dmod_hi[:, 1]]
    early_shapes = [a.shape for a in early_g]
    dres0, dy0, dpg0, dpb0, dgate0 = _post_bwd(xs0, y0, dxs1, modv, 0, pg[0], pb[0], rt_all, "l0_post_b")
    dt0, (early_all,) = _mm_nt(dy0, wa_out0, name="l0_dt", comm=_gather_comm([_pack(early_g)]))
    gw_a_out0 = _mm_tn(t0, dy0, blocked=False, out_dtype=gdt, name="l0_dwout")
    (dz0, dlg0, dlb0, dws0, dbs0), (r_a_out0,) = _cm_mid_bwd(z0, dt0, a_ln_g_f[0], a_ln_b_f[0], ws_op[0], bs_t[0], "l0_mid_b",
                                                             comm=exchange_of(gw_a_out0))
    gw_a_in0, (r_b_in,) = _mm_tn(h0, dz0, blocked=True, out_dtype=gdt, name="l0_dwin",
                                 comm=chip_sums_comm(gw_b_in, pair_b_in, "b_in"))
    dh0, (r_a_in0,) = _mm_nt_blocked(dz0, wa_in0, name="l0_dh", comm=chip_sums_of(gw_a_in0, "a_in0"))
    dx0, dshift0, dscale0 = _mod_bwd(dres0, dh0, xs0, modv, 0, rt_all, "l0_mod_b", dx_rows=n_x)
    grad_x = dx0[None]

    dmod_lo = mod_rows(dshift0, dscale0, dgate0)
    late_g = [dpg0, dpb0, dws0, dbs0, dlg0, dlb0, dmod_lo[0], dmod_lo[1]]
    late_shapes = [a.shape for a in late_g]
    (late_all,) = _all_gather([_pack(late_g)], "ag_small_grads")
    o_a_w_in = _adam_reduce([r_a_in0, r_a_in1], a_w_in, m_a_w_in, v_a_w_in, "adam_a_in")
    (e_post_g, e_post_b, e_ws, e_bs, s_conv_b, s_b_ln_g, s_b_ln_b, s_q_g, s_k_g, e_lg, e_lb, s_conv_w, e_dmod_own,
     e_dmod_ctx) = _unpack(_sum_devices(early_all, "sum_small_hi"), early_shapes)
    l_pg, l_pb, l_ws, l_bs, l_lg, l_lb, l_dmod_own, l_dmod_ctx = _unpack(_sum_devices(late_all, "sum_small_lo"), late_shapes)
    s_post_g = jnp.concatenate([l_pg, e_post_g], axis=0)
    s_post_b = jnp.concatenate([l_pb, e_post_b], axis=0)
    s_a_w_s = jnp.stack([l_ws, e_ws])
    s_a_b_s = jnp.swapaxes(jnp.stack([l_bs, e_bs]), 1, 2)
    s_a_ln_g = jnp.concatenate([l_lg, e_lg], axis=0)
    s_a_ln_b = jnp.concatenate([l_lb, e_lb], axis=0)
    s_dmod_own = jnp.concatenate([l_dmod_own[None], e_dmod_own], axis=0)
    s_dmod_ctx = jnp.concatenate([l_dmod_ctx[None], e_dmod_ctx], axis=0)
    grad_mod_b = s_dmod_own + s_dmod_ctx
    wl = a_ln_g.shape[1]
    wcv = b_conv_w.shape[2]
    grad_a_ln_g = lax.dynamic_slice_in_dim(s_a_ln_g, me * wl, wl, axis=1)
    grad_a_ln_b = lax.dynamic_slice_in_dim(s_a_ln_b, me * wl, wl, axis=1)
    grad_b_conv_w = lax.dynamic_slice_in_dim(s_conv_w, me * wcv, wcv, axis=1)[None]

    dmod_dev = jnp.concatenate([_unpack_dev(late_all, late_shapes)[6][:, None], _unpack_dev(early_all, early_shapes)[12]],
                               axis=1)
    dm_rows = jnp.concatenate([jnp.moveaxis(dmod_dev, 0, 1), s_dmod_ctx[:, None],
                               jnp.zeros((DEPTH, COND_ROWS - N_DEV - 1, 3 * d), F32)], axis=1)
    dm_mine = lax.dynamic_slice_in_dim(dm_rows, me * wm, wm, axis=2)
    grad_mod_w, dcond_part = _mod_bwd_mm(cond, dm_mine, mod_w, "mod_bwd")
    (dcond_all,) = _all_gather([dcond_part], "ag_dcond")
    dcond = _sum_devices(dcond_all, "sum_dcond")
    grad_c_ctx = dcond[N_DEV] * _dsilu(c_ctx)

    o_a_w_out = _adam_reduce([r_a_out0, r_a_out1], a_w_out, m_a_w_out, v_a_w_out, "adam_a_out")
    o_b_w_in = _adam_reduce([r_b_in], b_w_in, m_b_w_in, v_b_w_in, "adam_b_in")
    o_b_w_out = _adam_reduce([r_b_out], b_w_out, m_b_w_out, v_b_w_out, "adam_b_out")
    o_c_w_in = _adam_reduce([r_c_in], c_w_in, m_c_w_in, v_c_w_in, "adam_c_in")
    o_c_w_out = _adam_reduce([r_c_out], c_w_out, m_c_w_out, v_c_w_out, "adam_c_out")
    mw_shape = mod_w.shape
    o_mod_w = [grad_mod_w] + [a.reshape(mw_shape) for a in _adam_plain(
        grad_mod_w.reshape(-1, wm), mod_w.reshape(-1, wm), m_mod_w.reshape(-1, wm), v_mod_w.reshape(-1, wm), "adam_mod_w")]

    sg = [grad_c_ctx, grad_mod_b, s_post_g, s_post_b, grad_a_ln_g, grad_a_ln_b, s_a_w_s, s_a_b_s, grad_b_conv_w, s_conv_b,
          s_b_ln_g, s_b_ln_b, s_q_g, s_k_g]
    sw = [c_ctx, mod_b, post_g, post_b, a_ln_g, a_ln_b, a_w_s, a_b_s, b_conv_w, b_conv_b, b_ln_g, b_ln_b, c_q_g, c_k_g]
    sm = [m_c_ctx, m_mod_b, m_post_g, m_post_b, m_a_ln_g, m_a_ln_b, m_a_w_s, m_a_b_s, m_b_conv_w, m_b_conv_b, m_b_ln_g,
          m_b_ln_b, m_c_q_g, m_c_k_g]
    sv = [v_c_ctx, v_mod_b, v_post_g, v_post_b, v_a_ln_g, v_a_ln_b, v_a_w_s, v_a_b_s, v_b_conv_w, v_b_conv_b, v_b_ln_g,
          v_b_ln_b, v_c_q_g, v_c_k_g]
    shapes = [a.shape for a in sw]
    sg = [g.reshape(s) for g, s in zip(sg, shapes)]
    sd, snm, snv = [_unpack(a, shapes) for a in _adam_plain(_pack(sg), _pack(sw), _pack(sm), _pack(sv), "adam_small")]

    def small(k):
        return [sg[k], sd[k], snm[k], snv[k]]

    per_weight = [small(0), o_mod_w, small(1), small(2), small(3), o_a_w_in, small(4), small(5), small(6), small(7),
                  o_a_w_out, o_b_w_in, small(8), small(9), small(10), small(11), o_b_w_out, o_c_w_in, small(12), small(13),
                  o_c_w_out]
    outs = [loss, grad_x]
    for kind in range(4):
        outs += [pw[kind] for pw in per_weight]
    return tuple(outs)
```
